```python
import jax, jax.numpy as jnp
from jax import lax
import numpy as np

D_MODEL = 2048
BATCH = 32
SEQ = 256
DEPTH = 2
DEC_BATCH = 2
DEC_SEQ = 4096
PAST_LEN = 512

GRID_W = 64
HEAD_DIM = 128
N_EVEN_LAYERS = (DEPTH + 1) // 2
N_ODD_LAYERS = DEPTH // 2
CONV_DIM = 512
CONV_WIDTH = 3
WIN_Q_HEADS = 12
WIN_KV_HEADS = 4
WIN_GROUP = WIN_Q_HEADS // WIN_KV_HEADS
WINDOW = 128
WIN_BLOCK = 128
EVEN_SPLITS = (CONV_DIM, CONV_DIM, CONV_DIM, WIN_Q_HEADS * HEAD_DIM, WIN_KV_HEADS * HEAD_DIM, WIN_KV_HEADS * HEAD_DIM)
EVEN_IN = sum(EVEN_SPLITS)
EVEN_MIX = CONV_DIM + WIN_Q_HEADS * HEAD_DIM
NAT_HEADS = 8
NAT_ROWS = 8
NAT_COLS = 16
NAT_QCOLS = 16
NAT_SLAB = 32
GLOB_Q_HEADS = 8
GLOB_KV_HEADS = 2
GLOB_GROUP = GLOB_Q_HEADS // GLOB_KV_HEADS
ODD_SPLITS = (NAT_HEADS * HEAD_DIM, NAT_HEADS * HEAD_DIM, NAT_HEADS * HEAD_DIM, GLOB_Q_HEADS * HEAD_DIM, GLOB_KV_HEADS * HEAD_DIM, GLOB_KV_HEADS * HEAD_DIM)
ODD_IN = sum(ODD_SPLITS)
ODD_MIX = NAT_HEADS * HEAD_DIM + GLOB_Q_HEADS * HEAD_DIM
Q_BLOCK = 128
ROPE_THETA = 10000.0
ROPE_FREQS = HEAD_DIM // 4
N_EXPERTS = 16
EXPERT_FF = 2048
EC_CAPACITY = 2
N_MOD = 6
NORM_EPS = 1e-6
NEG_INF = -1e30

kernel_name = 'hybrid_diffusion_ctx_prefix_step'


def rms_norm(x, g):
    xf = x.astype(jnp.float32)
    y = xf * lax.rsqrt(jnp.mean(xf * xf, axis=-1, keepdims=True) + NORM_EPS)
    return (y * g.astype(jnp.float32)).astype(x.dtype)


def split_cols(y, sizes):
    cuts = [int(s) for s in np.cumsum(sizes)[:-1]]
    return jnp.split(y, cuts, axis=-1)


def modulation(cond, w, b):
    m = jax.nn.silu(cond) @ w + b
    return jnp.split(m[..., None, :], N_MOD, axis=-1)


def modulated_norm(x, g, shift, scale):
    return rms_norm(x, g) * (1 + scale) + shift


def axial_rope_tables(n):
    t = jnp.arange(n)
    row = (t // GRID_W).astype(jnp.float32)
    col = (t % GRID_W).astype(jnp.float32)
    inv = ROPE_THETA ** (-jnp.arange(ROPE_FREQS, dtype=jnp.float32) / ROPE_FREQS)
    ang = jnp.concatenate([row[:, None] * inv, col[:, None] * inv], axis=-1)
    return jnp.cos(ang), jnp.sin(ang)


def apply_rope(x, cos, sin):
    half = x.shape[-1] // 2
    shape = (1, x.shape[1]) + (1,) * (x.ndim - 3) + (half,)
    cs = cos.reshape(shape).astype(x.dtype)
    sn = sin.reshape(shape).astype(x.dtype)
    x1, x2 = x[..., :half], x[..., half:]
    return jnp.concatenate([x1 * cs - x2 * sn, x1 * sn + x2 * cs], axis=-1)


def short_conv(u, w, b):
    up = jnp.pad(u, ((0, 0), (1, 1), (0, 0)))
    return up[:, :-2] * w[0] + up[:, 1:-1] * w[1] + up[:, 2:] * w[2] + b


def dense_attention(q, k, v, sink=None):
    b, nq, hkv, g, hd = q.shape
    nb = nq // Q_BLOCK
    scale = hd ** -0.5
    qb = jnp.moveaxis(q.reshape(b, nb, Q_BLOCK, hkv, g, hd), 1, 0)

    def block(qi):
        s = jnp.einsum('bqhgd,bkhd->bhgqk', qi, k).astype(jnp.float32) * scale
        if sink is None:
            p = jax.nn.softmax(s, axis=-1)
        else:
            sk = jnp.broadcast_to(sink.astype(jnp.float32).reshape(1, hkv, g, 1, 1), s.shape[:-1] + (1,))
            p = jax.nn.softmax(jnp.concatenate([s, sk], axis=-1), axis=-1)[..., :-1]
        return jnp.einsum('bhgqk,bkhd->bqhgd', p.astype(v.dtype), v)

    out = lax.map(block, qb)
    return jnp.moveaxis(out, 0, 1).reshape(b, nq, hkv, g, hd)


def window_attention_latent(q, k, v, ctx_k, ctx_v, sink):
    b, n, hkv, g, hd = q.shape
    nb = n // WIN_BLOCK
    scale = hd ** -0.5
    pad = ((0, 0), (WIN_BLOCK, WIN_BLOCK), (0, 0), (0, 0))
    kp = jnp.pad(k, pad).reshape(b, nb + 2, WIN_BLOCK, hkv, hd)
    vp = jnp.pad(v, pad).reshape(b, nb + 2, WIN_BLOCK, hkv, hd)
    k_band = jnp.concatenate([kp[:, :-2], kp[:, 1:-1], kp[:, 2:]], axis=2)
    v_band = jnp.concatenate([vp[:, :-2], vp[:, 1:-1], vp[:, 2:]], axis=2)
    qb = q.reshape(b, nb, WIN_BLOCK, hkv, g, hd)
    s_loc = jnp.einsum('bnqhgd,bnkhd->bhgnqk', qb, k_band).astype(jnp.float32) * scale
    blk = jnp.arange(nb)[:, None, None]
    q_pos = blk * WIN_BLOCK + jnp.arange(WIN_BLOCK)[None, :, None]
    k_pos = (blk - 1) * WIN_BLOCK + jnp.arange(3 * WIN_BLOCK)[None, None, :]
    valid = (jnp.abs(k_pos - q_pos) <= WINDOW) & (k_pos >= 0) & (k_pos < n)
    s_loc = jnp.where(valid, s_loc, NEG_INF)
    s_ctx = jnp.einsum('bnqhgd,bkhd->bhgnqk', qb, ctx_k).astype(jnp.float32) * scale
    sk = jnp.broadcast_to(sink.astype(jnp.float32).reshape(1, hkv, g, 1, 1, 1), s_loc.shape[:-1] + (1,))
    p = jax.nn.softmax(jnp.concatenate([s_loc, s_ctx, sk], axis=-1), axis=-1).astype(v.dtype)
    nloc = 3 * WIN_BLOCK
    nctx = ctx_k.shape[1]
    out = (jnp.einsum('bhgnqk,bnkhd->bnqhgd', p[..., :nloc], v_band)
           + jnp.einsum('bhgnqk,bkhd->bnqhgd', p[..., nloc:nloc + nctx], ctx_v))
    return out.reshape(b, n, hkv, g, hd)


def neighbourhood_attention_latent(q, k, v, ctx_k, ctx_v, rpb):
    b, n, h, hd = q.shape
    rows = n // GRID_W
    kr = min(NAT_ROWS, rows)
    ncb = GRID_W // NAT_QCOLS
    scale = hd ** -0.5
    kg = k.reshape(b, rows, GRID_W, h, hd)
    vg = v.reshape(b, rows, GRID_W, h, hd)
    q_rows = jnp.moveaxis(q.reshape(b, rows, ncb, NAT_QCOLS, h, hd), 1, 0)
    q_col = jnp.arange(GRID_W).reshape(ncb, NAT_QCOLS)
    win_start = jnp.clip(q_col - NAT_COLS // 2, 0, GRID_W - NAT_COLS)
    slab_start = jnp.clip(jnp.arange(ncb) * NAT_QCOLS - NAT_COLS // 2, 0, GRID_W - NAT_SLAB)
    slab_cols = slab_start[:, None] + jnp.arange(NAT_SLAB)[None, :]
    sc = slab_cols[:, None, :]
    col_in = (sc >= win_start[..., None]) & (sc < win_start[..., None] + NAT_COLS)
    col_idx = jnp.clip(sc - q_col[..., None] + NAT_COLS - 1, 0, 2 * NAT_COLS - 2)
    mask = col_in[None, None, :, :, None, :]

    def one_row(args):
        r, q_r = args
        rs = jnp.clip(r - kr // 2, 0, rows - kr)
        k_rows = lax.dynamic_slice_in_dim(kg, rs, kr, axis=1)
        v_rows = lax.dynamic_slice_in_dim(vg, rs, kr, axis=1)
        k_blk = k_rows[:, :, slab_cols]
        v_blk = v_rows[:, :, slab_cols]
        s_loc = jnp.einsum('bnqhd,bmnshd->bhnqms', q_r, k_blk).astype(jnp.float32) * scale
        row_idx = rs + jnp.arange(kr) - r + NAT_ROWS - 1
        bias = rpb[:, row_idx][:, :, col_idx].astype(jnp.float32)
        s_loc = jnp.where(mask, s_loc + jnp.transpose(bias, (0, 2, 3, 1, 4))[None], NEG_INF)
        s_loc = s_loc.reshape(b, h, ncb, NAT_QCOLS, kr * NAT_SLAB)
        s_ctx = jnp.einsum('bnqhd,bkhd->bhnqk', q_r, ctx_k).astype(jnp.float32) * scale
        p = jax.nn.softmax(jnp.concatenate([s_loc, s_ctx], axis=-1), axis=-1).astype(v.dtype)
        nloc = kr * NAT_SLAB
        p_loc = p[..., :nloc].reshape(b, h, ncb, NAT_QCOLS, kr, NAT_SLAB)
        return (jnp.einsum('bhnqms,bmnshd->bnqhd', p_loc, v_blk)
                + jnp.einsum('bhnqk,bkhd->bnqhd', p[..., nloc:], ctx_v))

    out = lax.map(one_row, (jnp.arange(rows), q_rows))
    return jnp.moveaxis(out, 0, 1).reshape(b, n, h, hd)


def expert_choice_moe(h, w_router, w_gate, w_up, w_down):
    b, n, _ = h.shape
    cap = (EC_CAPACITY * n) // N_EXPERTS
    aff = jax.nn.softmax((h @ w_router).astype(jnp.float32), axis=-1)
    gates, idx = lax.top_k(jnp.swapaxes(aff, 1, 2), cap)
    bidx = jnp.arange(b)[:, None, None]
    xg = h[bidx, idx]
    hid = jax.nn.silu(jnp.einsum('becd,edf->becf', xg, w_gate)) * jnp.einsum('becd,edf->becf', xg, w_up)
    y = jnp.einsum('becf,efd->becd', hid, w_down) * gates[..., None].astype(h.dtype)
    return jnp.zeros_like(h).at[bidx, idx].add(y)


def even_project(h, w_in, conv_w, conv_b, q_norm, k_norm):
    b, l, _ = h.shape
    a_b, a_c, a_h, q, k, v = split_cols(h @ w_in, EVEN_SPLITS)
    y_a = a_b * short_conv(a_c * a_h, conv_w, conv_b)
    q = rms_norm(q.reshape(b, l, WIN_KV_HEADS, WIN_GROUP, HEAD_DIM), q_norm)
    k = rms_norm(k.reshape(b, l, WIN_KV_HEADS, HEAD_DIM), k_norm)
    v = v.reshape(b, l, WIN_KV_HEADS, HEAD_DIM)
    return y_a, q, k, v


def even_context(h, w_in, w_out, conv_w, conv_b, sink, q_norm, k_norm):
    b, l, _ = h.shape
    y_a, q, k, v = even_project(h, w_in, conv_w, conv_b, q_norm, k_norm)
    y_b = dense_attention(q, k, v, sink.reshape(WIN_KV_HEADS, WIN_GROUP))
    out = jnp.concatenate([y_a, y_b.reshape(b, l, -1)], axis=-1) @ w_out
    return out, k, v


def even_latent(h, ctx_k, ctx_v, cos, sin, w_in, w_out, conv_w, conv_b, sink, q_norm, k_norm):
    b, l, _ = h.shape
    y_a, q, k, v = even_project(h, w_in, conv_w, conv_b, q_norm, k_norm)
    q = apply_rope(q, cos, sin)
    k = apply_rope(k, cos, sin)
    y_b = window_attention_latent(q, k, v, ctx_k, ctx_v, sink.reshape(WIN_KV_HEADS, WIN_GROUP))
    return jnp.concatenate([y_a, y_b.reshape(b, l, -1)], axis=-1) @ w_out


def odd_project(h, w_in, nq_norm, nk_norm, gq_norm, gk_norm):
    b, l, _ = h.shape
    qc, kc, vc, qd, kd, vd = split_cols(h @ w_in, ODD_SPLITS)
    qc = rms_norm(qc.reshape(b, l, NAT_HEADS, HEAD_DIM), nq_norm)
    kc = rms_norm(kc.reshape(b, l, NAT_HEADS, HEAD_DIM), nk_norm)
    vc = vc.reshape(b, l, NAT_HEADS, HEAD_DIM)
    qd = rms_norm(qd.reshape(b, l, GLOB_KV_HEADS, GLOB_GROUP, HEAD_DIM), gq_norm)
    kd = rms_norm(kd.reshape(b, l, GLOB_KV_HEADS, HEAD_DIM), gk_norm)
    vd = vd.reshape(b, l, GLOB_KV_HEADS, HEAD_DIM)
    return qc, kc, vc, qd, kd, vd


def odd_context(h, w_in, w_out, nq_norm, nk_norm, gq_norm, gk_norm):
    b, l, _ = h.shape
    qc, kc, vc, qd, kd, vd = odd_project(h, w_in, nq_norm, nk_norm, gq_norm, gk_norm)
    y_c = dense_attention(qc[:, :, :, None], kc, vc)
    y_d = dense_attention(qd, kd, vd)
    out = jnp.concatenate([y_c.reshape(b, l, -1), y_d.reshape(b, l, -1)], axis=-1) @ w_out
    return out, kc, vc, kd, vd


def odd_latent(h, ctx_kc, ctx_vc, ctx_kd, ctx_vd, cos, sin, w_in, w_out, rpb, nq_norm, nk_norm, gq_norm, gk_norm):
    b, l, _ = h.shape
    qc, kc, vc, qd, kd, vd = odd_project(h, w_in, nq_norm, nk_norm, gq_norm, gk_norm)
    y_c = neighbourhood_attention_latent(qc, kc, vc, ctx_kc, ctx_vc, rpb)
    qd = apply_rope(qd, cos, sin)
    kd = apply_rope(kd, cos, sin)
    y_d = dense_attention(qd, jnp.concatenate([kd, ctx_kd], axis=1), jnp.concatenate([vd, ctx_vd], axis=1))
    return jnp.concatenate([y_c.reshape(b, l, -1), y_d.reshape(b, l, -1)], axis=-1) @ w_out


def setup_inputs(seed: int = 0) -> dict:
    key = jax.random.key(seed)
    ks = iter(jax.random.split(key, 40))

    def nrm(shape, s):
        return jax.random.normal(next(ks), shape, jnp.float32) * s

    d = D_MODEL
    return {
        'x_prompt': nrm((BATCH, SEQ, d), 1.0),
        'x_sample': nrm((DEC_BATCH, DEC_SEQ, d), 1.0),
        'cache_win_k': nrm((DEC_BATCH, N_EVEN_LAYERS, PAST_LEN, WIN_KV_HEADS, HEAD_DIM), 1.0),
        'cache_win_v': nrm((DEC_BATCH, N_EVEN_LAYERS, PAST_LEN, WIN_KV_HEADS, HEAD_DIM), 1.0),
        'cache_nat_k': nrm((DEC_BATCH, N_ODD_LAYERS, PAST_LEN, NAT_HEADS, HEAD_DIM), 1.0),
        'cache_nat_v': nrm((DEC_BATCH, N_ODD_LAYERS, PAST_LEN, NAT_HEADS, HEAD_DIM), 1.0),
        'cache_glob_k': nrm((DEC_BATCH, N_ODD_LAYERS, PAST_LEN, GLOB_KV_HEADS, HEAD_DIM), 1.0),
        'cache_glob_v': nrm((DEC_BATCH, N_ODD_LAYERS, PAST_LEN, GLOB_KV_HEADS, HEAD_DIM), 1.0),
        'c': nrm((DEC_BATCH, d), 1.0),
        'c_ctx': nrm((d,), 1.0),
        'mod_w': nrm((DEPTH, d, N_MOD * d), d ** -0.5),
        'mod_b': nrm((DEPTH, N_MOD * d), 0.01),
        'norm_mix_w': 1.0 + nrm((DEPTH, d), 0.02),
        'norm_ffn_w': 1.0 + nrm((DEPTH, d), 0.02),
        'even_w_in': nrm((N_EVEN_LAYERS, d, EVEN_IN), d ** -0.5),
        'even_w_out': nrm((N_EVEN_LAYERS, EVEN_MIX, d), EVEN_MIX ** -0.5),
        'conv_w': nrm((N_EVEN_LAYERS, CONV_WIDTH, CONV_DIM), CONV_WIDTH ** -0.5),
        'conv_b': nrm((N_EVEN_LAYERS, CONV_DIM), 0.01),
        'win_sink': nrm((N_EVEN_LAYERS, WIN_Q_HEADS), 0.5),
        'win_q_norm': 1.0 + nrm((N_EVEN_LAYERS, HEAD_DIM), 0.02),
        'win_k_norm': 1.0 + nrm((N_EVEN_LAYERS, HEAD_DIM), 0.02),
        'odd_w_in': nrm((N_ODD_LAYERS, d, ODD_IN), d ** -0.5),
        'odd_w_out': nrm((N_ODD_LAYERS, ODD_MIX, d), ODD_MIX ** -0.5),
        'nat_rpb': nrm((N_ODD_LAYERS, NAT_HEADS, 2 * NAT_ROWS - 1, 2 * NAT_COLS - 1), 0.1),
        'nat_q_norm': 1.0 + nrm((N_ODD_LAYERS, HEAD_DIM), 0.02),
        'nat_k_norm': 1.0 + nrm((N_ODD_LAYERS, HEAD_DIM), 0.02),
        'glob_q_norm': 1.0 + nrm((N_ODD_LAYERS, HEAD_DIM), 0.02),
        'glob_k_norm': 1.0 + nrm((N_ODD_LAYERS, HEAD_DIM), 0.02),
        'router_w': nrm((DEPTH, d, N_EXPERTS), d ** -0.5),
        'expert_w_gate': nrm((DEPTH, N_EXPERTS, d, EXPERT_FF), d ** -0.5),
        'expert_w_up': nrm((DEPTH, N_EXPERTS, d, EXPERT_FF), d ** -0.5),
        'expert_w_down': nrm((DEPTH, N_EXPERTS, EXPERT_FF, d), EXPERT_FF ** -0.5),
    }


def reference(x_prompt, x_sample, cache_win_k, cache_win_v, cache_nat_k, cache_nat_v, cache_glob_k, cache_glob_v,
              c, c_ctx, mod_w, mod_b, norm_mix_w, norm_ffn_w, even_w_in, even_w_out, conv_w, conv_b, win_sink,
              win_q_norm, win_k_norm, odd_w_in, odd_w_out, nat_rpb, nat_q_norm, nat_k_norm, glob_q_norm, glob_k_norm,
              router_w, expert_w_gate, expert_w_up, expert_w_down):
    cos, sin = axial_rope_tables(x_sample.shape[1])
    xp, xs = x_prompt, x_sample
    win_k, win_v, nat_k, nat_v, glob_k, glob_v = [], [], [], [], [], []
    for layer in range(DEPTH):
        i = layer // 2
        mp = modulation(c_ctx, mod_w[layer], mod_b[layer])
        ms = modulation(c, mod_w[layer], mod_b[layer])
        hp = modulated_norm(xp, norm_mix_w[layer], mp[0], mp[1])
        hs = modulated_norm(xs, norm_mix_w[layer], ms[0], ms[1])
        if layer % 2 == 0:
            op, kp, vp = even_context(hp, even_w_in[i], even_w_out[i], conv_w[i], conv_b[i], win_sink[i],
                                      win_q_norm[i], win_k_norm[i])
            o_s = even_latent(hs, cache_win_k[:, i], cache_win_v[:, i], cos, sin, even_w_in[i], even_w_out[i],
                              conv_w[i], conv_b[i], win_sink[i], win_q_norm[i], win_k_norm[i])
            win_k.append(kp)
            win_v.append(vp)
        else:
            op, kcp, vcp, kdp, vdp = odd_context(hp, odd_w_in[i], odd_w_out[i], nat_q_norm[i], nat_k_norm[i],
                                                 glob_q_norm[i], glob_k_norm[i])
            o_s = odd_latent(hs, cache_nat_k[:, i], cache_nat_v[:, i], cache_glob_k[:, i], cache_glob_v[:, i],
                             cos, sin, odd_w_in[i], odd_w_out[i], nat_rpb[i], nat_q_norm[i], nat_k_norm[i],
                             glob_q_norm[i], glob_k_norm[i])
            nat_k.append(kcp)
            nat_v.append(vcp)
            glob_k.append(kdp)
            glob_v.append(vdp)
        xp = xp + mp[2] * op
        xs = xs + ms[2] * o_s
        xp = xp + mp[5] * expert_choice_moe(modulated_norm(xp, norm_ffn_w[layer], mp[3], mp[4]), router_w[layer],
                                            expert_w_gate[layer], expert_w_up[layer], expert_w_down[layer])
        xs = xs + ms[5] * expert_choice_moe(modulated_norm(xs, norm_ffn_w[layer], ms[3], ms[4]), router_w[layer],
                                            expert_w_gate[layer], expert_w_up[layer], expert_w_down[layer])
    return (xp, xs, jnp.stack(win_k, axis=1), jnp.stack(win_v, axis=1), jnp.stack(nat_k, axis=1),
            jnp.stack(nat_v, axis=1), jnp.stack(glob_k, axis=1), jnp.stack(glob_v, axis=1))
```

```python
import functools

import numpy as np
import jax
import jax.numpy as jnp
from jax import lax
from jax.experimental import pallas as pl
from jax.experimental.pallas import tpu as pltpu

D_MODEL = 2048
HEAD_DIM = 128
GRID_W = 64
CONV_DIM = 512
WIN_Q_HEADS = 12
WIN_KV_HEADS = 4
WIN_BLOCK = 128
WINDOW = 128
NAT_HEADS = 8
NAT_ROWS = 8
NAT_COLS = 16
GLOB_Q_HEADS = 8
GLOB_KV_HEADS = 2
ROPE_THETA = 10000.0
N_EXPERTS = 16
EXPERT_FF = 2048
EC_CAPACITY = 2
N_MOD = 6
NORM_EPS = 1e-6
NEG_INF = -1e30
ATTN_SCALE = HEAD_DIM ** -0.5

LANES = 128
MIB = 1024 * 1024
BF16 = jnp.bfloat16
F32 = jnp.float32


def _params(semantics, vmem_mib):
    return pltpu.CompilerParams(dimension_semantics=semantics, vmem_limit_bytes=vmem_mib * MIB)


def _silu(x):
    return x * (1.0 / (1.0 + jnp.exp(-x)))


def _modulation_kernel(cond_ref, w_ref, b_ref, o_ref):
    s = _silu(cond_ref[...])
    o_ref[0] = jnp.dot(s, w_ref[0], preferred_element_type=F32,
                       precision=lax.Precision.HIGHEST) + b_ref[0]


def modulation(cond8, mod_w, mod_b):
    depth, d, n = mod_w.shape
    tn = 512
    return pl.pallas_call(
        _modulation_kernel,
        out_shape=jax.ShapeDtypeStruct((depth, 8, n), F32),
        grid=(depth, n // tn),
        in_specs=[pl.BlockSpec((8, d), lambda l, j: (0, 0)),
                  pl.BlockSpec((1, d, tn), lambda l, j: (l, 0, j)),
                  pl.BlockSpec((1, 1, tn), lambda l, j: (l, 0, j))],
        out_specs=pl.BlockSpec((1, 8, tn), lambda l, j: (l, 0, j)),
        compiler_params=_params(("parallel", "parallel"), 48),
        name="modulation",
    )(cond8, mod_w, mod_b.reshape(depth, 1, n))


def _mod_norm(x, g, shift, scale):
    ms = jnp.mean(x * x, axis=-1, keepdims=True)
    y = x * lax.rsqrt(ms + NORM_EPS) * g
    return y * (1.0 + scale) + shift


def _norm_proj_kernel(x_ref, g_ref, shift_ref, scale_ref, w_ref, o_ref, h_ref):
    @pl.when(pl.program_id(1) == 0)
    def _():
        h_ref[...] = _mod_norm(x_ref[...], g_ref[...], shift_ref[0], scale_ref[0]).astype(BF16)

    o_ref[...] = jnp.dot(h_ref[...], w_ref[...], preferred_element_type=F32)


def norm_proj(x, g, shift, scale, w_bf16, n_tiles):
    r, d = x.shape
    n = w_bf16.shape[1]
    nseg = shift.shape[0]
    tm = 512
    tn = n // n_tiles
    tiles_per_seg = (r // nseg) // tm
    return pl.pallas_call(
        _norm_proj_kernel,
        out_shape=jax.ShapeDtypeStruct((r, n), F32),
        grid=(r // tm, n_tiles),
        in_specs=[pl.BlockSpec((tm, d), lambda i, j: (i, 0)),
                  pl.BlockSpec((1, d), lambda i, j: (0, 0)),
                  pl.BlockSpec((1, 1, d), lambda i, j: (i // tiles_per_seg, 0, 0)),
                  pl.BlockSpec((1, 1, d), lambda i, j: (i // tiles_per_seg, 0, 0)),
                  pl.BlockSpec((d, tn), lambda i, j: (0, j))],
        out_specs=pl.BlockSpec((tm, tn), lambda i, j: (i, j)),
        scratch_shapes=[pltpu.VMEM((tm, d), BF16)],
        compiler_params=_params(("parallel", "arbitrary"), 48),
        name="norm_proj",
    )(x, g.reshape(1, d), shift, scale, w_bf16)


def _conv_gate_kernel(ab_ref, ac_ref, ah_ref, w_ref, b_ref, o_ref, *, seq):
    u = ac_ref[...] * ah_ref[...]
    rows = u.shape[0]
    pos = lax.broadcasted_iota(jnp.int32, u.shape, 0) % seq
    prev = jnp.where(pos == 0, 0.0, pltpu.roll(u, 1, 0))
    nxt = jnp.where(pos == seq - 1, 0.0, pltpu.roll(u, rows - 1, 0))
    conv = prev * w_ref[0:1, :] + u * w_ref[1:2, :] + nxt * w_ref[2:3, :] + b_ref[...]
    o_ref[...] = (ab_ref[...] * conv).astype(o_ref.dtype)


def conv_gate(proj, conv_w, conv_b, seq):
    r = proj.shape[0]
    tr = 4096
    nc = CONV_DIM // LANES
    return pl.pallas_call(
        functools.partial(_conv_gate_kernel, seq=seq),
        out_shape=jax.ShapeDtypeStruct((r, CONV_DIM), BF16),
        grid=(r // tr, nc),
        in_specs=[pl.BlockSpec((tr, LANES), lambda i, c: (i, c)),
                  pl.BlockSpec((tr, LANES), lambda i, c: (i, nc + c)),
                  pl.BlockSpec((tr, LANES), lambda i, c: (i, 2 * nc + c)),
                  pl.BlockSpec((3, LANES), lambda i, c: (0, c)),
                  pl.BlockSpec((1, LANES), lambda i, c: (0, c))],
        out_specs=pl.BlockSpec((tr, LANES), lambda i, c: (i, c)),
        compiler_params=_params(("parallel", "parallel"), 48),
        name="conv_gate",
    )(proj, proj, proj, conv_w, conv_b.reshape(1, CONV_DIM))


def _head_norm_kernel(*refs, heads, rope, scale, want_f32):
    x_ref, g_ref = refs[0], refs[1]
    k = 2
    if rope:
        cos_ref, sin_ref = refs[2], refs[3]
        k = 4
    ob_ref = refs[k]
    g = g_ref[...]
    for h in range(heads):
        sl = slice(h * HEAD_DIM, (h + 1) * HEAD_DIM)
        x = x_ref[:, sl]
        y = x * lax.rsqrt(jnp.mean(x * x, axis=-1, keepdims=True) + NORM_EPS) * g
        if want_f32:
            refs[k + 1][:, sl] = y
        if rope:
            y = y * cos_ref[...] + pltpu.roll(y, HEAD_DIM // 2, 1) * sin_ref[...]
        ob_ref[:, sl] = (y * scale).astype(BF16)


def head_norm(proj, col_block, width, gain, scale=1.0, rope_tables=None, want_f32=False):
    r = proj.shape[0]
    tm = 512
    heads = width // HEAD_DIM
    rope = rope_tables is not None
    in_specs = [pl.BlockSpec((tm, width), lambda i: (i, col_block)),
                pl.BlockSpec((1, HEAD_DIM), lambda i: (0, 0))]
    args = [proj, gain.reshape(1, HEAD_DIM)]
    if rope:
        n_pos = rope_tables[0].shape[0]
        per = n_pos // tm
        in_specs += [pl.BlockSpec((tm, HEAD_DIM), lambda i: (i % per, 0))] * 2
        args += list(rope_tables)
    out_shape = [jax.ShapeDtypeStruct((r, width), BF16)]
    out_specs = [pl.BlockSpec((tm, width), lambda i: (i, 0))]
    if want_f32:
        out_shape.append(jax.ShapeDtypeStruct((r, width), F32))
        out_specs.append(pl.BlockSpec((tm, width), lambda i: (i, 0)))
    out = pl.pallas_call(
        functools.partial(_head_norm_kernel, heads=heads, rope=rope, scale=scale, want_f32=want_f32),
        out_shape=out_shape,
        grid=(r // tm,),
        in_specs=in_specs,
        out_specs=out_specs,
        compiler_params=_params(("parallel",), 32),
        name="head_norm",
    )(*args)
    return out if want_f32 else out[0]


def _dense_attn_kernel(*refs, group, has_sink):
    if has_sink:
        sink_ref, q_ref, k_ref, v_ref, o_ref = refs
    else:
        q_ref, k_ref, v_ref, o_ref = refs
    qb = q_ref.shape[1]
    q = jnp.concatenate([q_ref[0, :, g * HEAD_DIM:(g + 1) * HEAD_DIM] for g in range(group)], axis=0)
    s = lax.dot_general(q, k_ref[0], (((1,), (1,)), ((), ())), preferred_element_type=F32)
    m = jnp.max(s, axis=-1, keepdims=True)
    if has_sink:
        hkv = pl.program_id(2)
        sk = jnp.concatenate([jnp.full((qb, 1), sink_ref[hkv * group + g], F32) for g in range(group)], axis=0)
        m = jnp.maximum(m, sk)
    p = jnp.exp(s - m)
    den = jnp.sum(p, axis=-1, keepdims=True)
    if has_sink:
        den = den + jnp.exp(sk - m)
    o = jnp.dot(p.astype(BF16), v_ref[0], preferred_element_type=F32) / den
    for g in range(group):
        o_ref[0, :, g * HEAD_DIM:(g + 1) * HEAD_DIM] = o[g * qb:(g + 1) * qb].astype(o_ref.dtype)


def dense_attention(q, k, v, group, q_block, sink=None):
    b, nq, qw = q.shape
    nk = k.shape[1]
    hkv = k.shape[2] // HEAD_DIM
    gw = group * HEAD_DIM
    in_specs = [pl.BlockSpec((1, q_block, gw), lambda bi, qi, h: (bi, qi, h)),
                pl.BlockSpec((1, nk, HEAD_DIM), lambda bi, qi, h: (bi, 0, h)),
                pl.BlockSpec((1, nk, HEAD_DIM), lambda bi, qi, h: (bi, 0, h))]
    args = [q, k, v]
    if sink is not None:
        in_specs = [pl.BlockSpec(memory_space=pltpu.SMEM)] + in_specs
        args = [sink] + args
    return pl.pallas_call(
        functools.partial(_dense_attn_kernel, group=group, has_sink=sink is not None),
        out_shape=jax.ShapeDtypeStruct((b, nq, qw), BF16),
        grid=(b, nq // q_block, hkv),
        in_specs=in_specs,
        out_specs=pl.BlockSpec((1, q_block, gw), lambda bi, qi, h: (bi, qi, h)),
        compiler_params=_params(("parallel", "parallel", "parallel"), 56),
        name="dense_attention",
    )(*args)


def _window_attn_kernel(sink_ref, q_ref, kp_ref, kc_ref, kn_ref, vp_ref, vc_ref, vn_ref, ck_ref, cv_ref, o_ref,
                        *, group, n_tokens):
    blk = pl.program_id(1)
    hkv = pl.program_id(2)
    wb = WIN_BLOCK
    q = jnp.concatenate([q_ref[0, :, g * HEAD_DIM:(g + 1) * HEAD_DIM] for g in range(group)], axis=0)
    keys = jnp.concatenate([kp_ref[0], kc_ref[0], kn_ref[0], ck_ref[0]], axis=0)
    vals = jnp.concatenate([vp_ref[0], vc_ref[0], vn_ref[0], cv_ref[0]], axis=0)
    s = lax.dot_general(q, keys, (((1,), (1,)), ((), ())), preferred_element_type=F32)
    rows, cols = s.shape
    q_pos = blk * wb + lax.broadcasted_iota(jnp.int32, (rows, cols), 0) % wb
    col = lax.broadcasted_iota(jnp.int32, (rows, cols), 1)
    k_pos = (blk - 1) * wb + col
    local_ok = (jnp.abs(k_pos - q_pos) <= WINDOW) & (k_pos >= 0) & (k_pos < n_tokens)
    s = jnp.where((col >= 3 * wb) | local_ok, s, NEG_INF)
    sk = jnp.concatenate([jnp.full((wb, 1), sink_ref[hkv * group + g], F32) for g in range(group)], axis=0)
    m = jnp.maximum(jnp.max(s, axis=-1, keepdims=True), sk)
    p = jnp.exp(s - m)
    den = jnp.sum(p, axis=-1, keepdims=True) + jnp.exp(sk - m)
    o = jnp.dot(p.astype(BF16), vals, preferred_element_type=F32) / den
    for g in range(group):
        o_ref[0, :, g * HEAD_DIM:(g + 1) * HEAD_DIM] = o[g * wb:(g + 1) * wb].astype(o_ref.dtype)


def window_attention(q, k, v, ctx_k, ctx_v, sink, group):
    b, n, qw = q.shape
    hkv = k.shape[2] // HEAD_DIM
    nb = n // WIN_BLOCK
    nctx = ctx_k.shape[1]
    gw = group * HEAD_DIM
    prev_map = lambda bi, i, h: (bi, jnp.maximum(i - 1, 0), h)
    cur_map = lambda bi, i, h: (bi, i, h)
    next_map = lambda bi, i, h: (bi, jnp.minimum(i + 1, nb - 1), h)
    ctx_map = lambda bi, i, h: (bi, 0, h)
    kv_block = (1, WIN_BLOCK, HEAD_DIM)
    return pl.pallas_call(
        functools.partial(_window_attn_kernel, group=group, n_tokens=n),
        out_shape=jax.ShapeDtypeStruct((b, n, qw), BF16),
        grid=(b, nb, hkv),
        in_specs=[pl.BlockSpec(memory_space=pltpu.SMEM),
                  pl.BlockSpec((1, WIN_BLOCK, gw), cur_map),
                  pl.BlockSpec(kv_block, prev_map), pl.BlockSpec(kv_block, cur_map), pl.BlockSpec(kv_block, next_map),
                  pl.BlockSpec(kv_block, prev_map), pl.BlockSpec(kv_block, cur_map), pl.BlockSpec(kv_block, next_map),
                  pl.BlockSpec((1, nctx, HEAD_DIM), ctx_map), pl.BlockSpec((1, nctx, HEAD_DIM), ctx_map)],
        out_specs=pl.BlockSpec((1, WIN_BLOCK, gw), cur_map),
        compiler_params=_params(("parallel", "parallel", "parallel"), 32),
        name="window_attention",
    )(sink, q, k, k, k, v, v, v, ctx_k, ctx_v)


NAT_QROWS = 8
NAT_KROWS = 16


def _nat_bias_table(rpb):
    w = GRID_W
    c = np.arange(w)[:, None]
    kc = np.arange(w)[None, :]
    ws = np.clip(c - NAT_COLS // 2, 0, w - NAT_COLS)
    col_ok = (kc >= ws) & (kc < ws + NAT_COLS)
    col_idx = np.clip(kc - c + NAT_COLS - 1, 0, 2 * NAT_COLS - 2)
    t = jnp.where(col_ok[None, None], rpb[:, :, col_idx], NEG_INF)
    dead = jnp.full_like(t[:, :1], NEG_INF)
    t = jnp.concatenate([dead, t, dead], axis=1)
    return jnp.concatenate([t[:, :-1], t[:, 1:]], axis=-1)


def _nat_attn_kernel(q_ref, k_ref, v_ref, ck_ref, cv_ref, t_ref, o_ref, *, grid_rows):
    m = pl.program_id(2)
    w = GRID_W
    r0 = m * NAT_QROWS
    kr0 = jnp.clip(r0 - NAT_ROWS // 2, 0, grid_rows - NAT_KROWS)
    tok0 = pl.multiple_of(kr0 * w, 4 * w)
    keys = k_ref[0, pl.ds(tok0, NAT_KROWS * w), :]
    vals = v_ref[0, pl.ds(tok0, NAT_KROWS * w), :]
    q = q_ref[0]
    s_loc = lax.dot_general(q, keys, (((1,), (1,)), ((), ())), preferred_element_type=F32)
    lane_hi = lax.broadcasted_iota(jnp.int32, (w, 2 * w), 1) >= w
    bias_rows = []
    for i in range(NAT_QROWS):
        r = r0 + i
        rs = jnp.clip(r - NAT_ROWS // 2, 0, grid_rows - NAT_ROWS)
        pieces = []
        for jj in range(NAT_KROWS // 2):
            kr = kr0 + 2 * jj
            d = kr - r + NAT_ROWS - 1
            ok_lo = ((kr >= rs) & (kr < rs + NAT_ROWS)).astype(jnp.int32)
            ok_hi = ((kr + 1 >= rs) & (kr + 1 < rs + NAT_ROWS)).astype(jnp.int32)
            piece = t_ref[0, jnp.clip(d, -1, 2 * NAT_ROWS - 2) + 1]
            ok = jnp.where(lane_hi, ok_hi, ok_lo) > 0
            pieces.append(jnp.where(ok, piece, NEG_INF))
        bias_rows.append(jnp.concatenate(pieces, axis=1))
    s_loc = s_loc + jnp.concatenate(bias_rows, axis=0)
    s_ctx = lax.dot_general(q, ck_ref[0], (((1,), (1,)), ((), ())), preferred_element_type=F32)
    mx = jnp.maximum(jnp.max(s_loc, axis=-1, keepdims=True), jnp.max(s_ctx, axis=-1, keepdims=True))
    p_loc = jnp.exp(s_loc - mx)
    p_ctx = jnp.exp(s_ctx - mx)
    den = jnp.sum(p_loc, axis=-1, keepdims=True) + jnp.sum(p_ctx, axis=-1, keepdims=True)
    o = (jnp.dot(p_loc.astype(BF16), vals, preferred_element_type=F32)
         + jnp.dot(p_ctx.astype(BF16), cv_ref[0], preferred_element_type=F32))
    o_ref[0] = (o / den).astype(o_ref.dtype)


def neighbourhood_attention(q, k, v, ctx_k, ctx_v, bias_table):
    b, n, hw = q.shape
    heads = hw // HEAD_DIM
    grid_rows = n // GRID_W
    nctx = ctx_k.shape[1]
    tq = NAT_QROWS * GRID_W
    return pl.pallas_call(
        functools.partial(_nat_attn_kernel, grid_rows=grid_rows),
        out_shape=jax.ShapeDtypeStruct((b, n, hw), BF16),
        grid=(b, heads, grid_rows // NAT_QROWS),
        in_specs=[pl.BlockSpec((1, tq, HEAD_DIM), lambda bi, h, m: (bi, m, h)),
                  pl.BlockSpec((1, n, HEAD_DIM), lambda bi, h, m: (bi, 0, h)),
                  pl.BlockSpec((1, n, HEAD_DIM), lambda bi, h, m: (bi, 0, h)),
                  pl.BlockSpec((1, nctx, HEAD_DIM), lambda bi, h, m: (bi, 0, h)),
                  pl.BlockSpec((1, nctx, HEAD_DIM), lambda bi, h, m: (bi, 0, h)),
                  pl.BlockSpec((1, 2 * NAT_ROWS, GRID_W, 2 * GRID_W), lambda bi, h, m: (h, 0, 0, 0))],
        out_specs=pl.BlockSpec((1, tq, HEAD_DIM), lambda bi, h, m: (bi, m, h)),
        compiler_params=_params(("parallel", "parallel", "arbitrary"), 32),
        name="neighbourhood_attention",
    )(q, k, v, ctx_k, ctx_v, bias_table)


def _out_proj_kernel(a_ref, b_ref, w_ref, x_ref, gate_ref, o_ref):
    ka = a_ref.shape[1]
    acc = jnp.dot(a_ref[...], w_ref[:ka, :], preferred_element_type=F32)
    acc = acc + jnp.dot(b_ref[...], w_ref[ka:, :], preferred_element_type=F32)
    o_ref[...] = x_ref[...] + gate_ref[0] * acc


def out_proj_residual(a, b_, w_bf16, x, gate):
    r, d = x.shape
    ka, kb = a.shape[1], b_.shape[1]
    nseg = gate.shape[0]
    tm, tn = 1024, 1024
    tiles_per_seg = (r // nseg) // tm
    return pl.pallas_call(
        _out_proj_kernel,
        out_shape=jax.ShapeDtypeStruct((r, d), F32),
        grid=(r // tm, d // tn),
        in_specs=[pl.BlockSpec((tm, ka), lambda i, j: (i, 0)),
                  pl.BlockSpec((tm, kb), lambda i, j: (i, 0)),
                  pl.BlockSpec((ka + kb, tn), lambda i, j: (0, j)),
                  pl.BlockSpec((tm, tn), lambda i, j: (i, j)),
                  pl.BlockSpec((1, 1, tn), lambda i, j: (i // tiles_per_seg, 0, j))],
        out_specs=pl.BlockSpec((tm, tn), lambda i, j: (i, j)),
        compiler_params=_params(("parallel", "parallel"), 56),
        name="out_proj_residual",
    )(a, b_, w_bf16, x, gate)


def _norm_router_kernel(x_ref, g_ref, shift_ref, scale_ref, whi_ref, wlo_ref, h_ref, logit_ref):
    h = _mod_norm(x_ref[...], g_ref[...], shift_ref[0], scale_ref[0])
    h_hi = h.astype(BF16)
    h_lo = (h - h_hi.astype(F32)).astype(BF16)
    h_ref[...] = h_hi
    w_hi = whi_ref[...]
    logit_ref[...] = (jnp.dot(h_hi, w_hi, preferred_element_type=F32)
                      + jnp.dot(h_lo, w_hi, preferred_element_type=F32)
                      + jnp.dot(h_hi, wlo_ref[...], preferred_element_type=F32))


def norm_router(x, g, shift, scale, router_w):
    r, d = x.shape
    nseg = shift.shape[0]
    tm = 512
    tiles_per_seg = (r // nseg) // tm
    w_pad = jnp.zeros((d, LANES), F32).at[:, :N_EXPERTS].set(router_w)
    w_hi = w_pad.astype(BF16)
    w_lo = (w_pad - w_hi.astype(F32)).astype(BF16)
    return pl.pallas_call(
        _norm_router_kernel,
        out_shape=[jax.ShapeDtypeStruct((r, d), BF16), jax.ShapeDtypeStruct((r, LANES), F32)],
        grid=(r // tm,),
        in_specs=[pl.BlockSpec((tm, d), lambda i: (i, 0)),
                  pl.BlockSpec((1, d), lambda i: (0, 0)),
                  pl.BlockSpec((1, 1, d), lambda i: (i // tiles_per_seg, 0, 0)),
                  pl.BlockSpec((1, 1, d), lambda i: (i // tiles_per_seg, 0, 0)),
                  pl.BlockSpec((d, LANES), lambda i: (0, 0)),
                  pl.BlockSpec((d, LANES), lambda i: (0, 0))],
        out_specs=[pl.BlockSpec((tm, d), lambda i: (i, 0)), pl.BlockSpec((tm, LANES), lambda i: (i, 0))],
        compiler_params=_params(("parallel",), 32),
        name="norm_router",
    )(x, g.reshape(1, d), shift, scale, w_hi, w_lo)


CUMSUM_BLOCK = 256


def _exclusive_cumsum_lanes(flags):
    e, n = flags.shape
    cb = min(CUMSUM_BLOCK, n)
    tri = (lax.broadcasted_iota(jnp.int32, (cb, cb), 0) < lax.broadcasted_iota(jnp.int32, (cb, cb), 1)).astype(BF16)
    carry = jnp.zeros((e, 1), F32)
    out = []
    for j in range(n // cb):
        blk = flags[:, j * cb:(j + 1) * cb]
        out.append(jnp.dot(blk.astype(BF16), tri, preferred_element_type=F32) + carry)
        carry = carry + jnp.sum(blk, axis=-1, keepdims=True)
    return jnp.concatenate(out, axis=1) if len(out) > 1 else out[0]


def _select_kernel(logit_ref, pos_ref, gate_ref, *, cap):
    lg = logit_ref[0]
    e, n = lg.shape
    ex = jnp.exp(lg - jnp.max(lg, axis=0, keepdims=True))
    aff = ex / jnp.sum(ex, axis=0, keepdims=True)
    key = pltpu.bitcast(aff, jnp.int32)
    thr = jnp.zeros((e, 1), jnp.int32)
    for bit in range(30, -1, -1):
        cand = thr | (1 << bit)
        cnt = jnp.sum((key >= cand).astype(F32), axis=-1, keepdims=True)
        thr = jnp.where(cnt >= cap, cand, thr)
    above = (key > thr).astype(F32)
    tied = (key == thr).astype(F32)
    need = cap - jnp.sum(above, axis=-1, keepdims=True)
    take = above + tied * (_exclusive_cumsum_lanes(tied) < need).astype(F32)
    slot = _exclusive_cumsum_lanes(take)
    pos = jnp.where(take > 0, slot, -1.0).astype(jnp.int32)
    pos_ref[0] = pos
    s_iota = lax.broadcasted_iota(jnp.int32, (cap, n), 0)
    for ei in range(e):
        hit = pos[ei:ei + 1, :] == s_iota
        gate_ref[0, ei] = jnp.sum(jnp.where(hit, aff[ei:ei + 1, :], 0.0), axis=-1, keepdims=True)


def select_tokens(logits_t, cap):
    nseg, e, n = logits_t.shape
    return pl.pallas_call(
        functools.partial(_select_kernel, cap=cap),
        out_shape=[jax.ShapeDtypeStruct((nseg, e, n), jnp.int32),
                   jax.ShapeDtypeStruct((nseg, e, cap, 1), F32)],
        grid=(nseg,),
        in_specs=[pl.BlockSpec((1, e, n), lambda s: (s, 0, 0))],
        out_specs=[pl.BlockSpec((1, e, n), lambda s: (s, 0, 0)),
                   pl.BlockSpec((1, e, cap, 1), lambda s: (s, 0, 0, 0))],
        compiler_params=_params(("parallel",), 48),
        name="select_tokens",
    )(logits_t)


def _gather_kernel(pos_ref, h_ref, o_ref, acc_ref, *, cap):
    kt = pl.program_id(2)
    eb = pos_ref.shape[1]
    tk = pos_ref.shape[3]
    s_iota = lax.broadcasted_iota(jnp.int32, (cap, tk), 0)
    onehot = jnp.concatenate([(pos_ref[0, i] == s_iota).astype(BF16) for i in range(eb)], axis=0)
    part = jnp.dot(onehot, h_ref[0], preferred_element_type=F32)

    @pl.when(kt == 0)
    def _():
        acc_ref[...] = part

    @pl.when(kt > 0)
    def _():
        acc_ref[...] += part

    @pl.when(kt == pl.num_programs(2) - 1)
    def _():
        o_ref[0] = acc_ref[...].reshape(o_ref.shape[1:]).astype(o_ref.dtype)


def gather_tokens(pos, h, cap, experts_per_step, tk):
    nseg, e, n = pos.shape
    d = h.shape[2]
    eb = experts_per_step
    return pl.pallas_call(
        functools.partial(_gather_kernel, cap=cap),
        out_shape=jax.ShapeDtypeStruct((nseg, e, cap, d), BF16),
        grid=(nseg, e // eb, n // tk),
        in_specs=[pl.BlockSpec((1, eb, 1, tk), lambda s, ei, kt: (s, ei, 0, kt)),
                  pl.BlockSpec((1, tk, d), lambda s, ei, kt: (s, kt, 0))],
        out_specs=pl.BlockSpec((1, eb, cap, d), lambda s, ei, kt: (s, ei, 0, 0)),
        scratch_shapes=[pltpu.VMEM((eb * cap, d), F32)],
        compiler_params=_params(("parallel", "parallel", "arbitrary"), 48),
        name="gather_tokens",
    )(pos.reshape(nseg, e, 1, n), h)


def _expert_up_kernel(xc_ref, xl_ref, wg_ref, wu_ref, o_ref):
    d = xc_ref.shape[-1]
    wg = wg_ref[0].astype(BF16)
    wu = wu_ref[0].astype(BF16)
    row = 0
    for x_ref in (xc_ref, xl_ref):
        rows = x_ref.shape[0] * x_ref.shape[2]
        x = x_ref[...].reshape(rows, d)
        a = jnp.dot(x, wg, preferred_element_type=F32)
        b = jnp.dot(x, wu, preferred_element_type=F32)
        o_ref[0, row:row + rows, :] = (_silu(a) * b).astype(o_ref.dtype)
        row += rows


def expert_up(xg_ctx, xg_lat, w_gate, w_up):
    e, d, f = w_gate.shape
    tf = 256
    sc, _, cc, _ = xg_ctx.shape
    sl, _, cl, _ = xg_lat.shape
    rows = sc * cc + sl * cl
    return pl.pallas_call(
        _expert_up_kernel,
        out_shape=jax.ShapeDtypeStruct((e, rows, f), BF16),
        grid=(e, f // tf),
        in_specs=[pl.BlockSpec((sc, 1, cc, d), lambda ei, j: (0, ei, 0, 0)),
                  pl.BlockSpec((sl, 1, cl, d), lambda ei, j: (0, ei, 0, 0)),
                  pl.BlockSpec((1, d, tf), lambda ei, j: (ei, 0, j)),
                  pl.BlockSpec((1, d, tf), lambda ei, j: (ei, 0, j))],
        out_specs=pl.BlockSpec((1, rows, tf), lambda ei, j: (ei, 0, j)),
        compiler_params=_params(("parallel", "arbitrary"), 56),
        name="expert_up",
    )(xg_ctx, xg_lat, w_gate, w_up)


def _expert_down_kernel(h_ref, w_ref, gate_ref, o_ref):
    y = jnp.dot(h_ref[0], w_ref[0].astype(BF16), preferred_element_type=F32)
    o_ref[0] = (y * gate_ref[0]).astype(o_ref.dtype)


def expert_down(hid, w_down, gates):
    e, rows, f = hid.shape
    d = w_down.shape[2]
    td = 256
    return pl.pallas_call(
        _expert_down_kernel,
        out_shape=jax.ShapeDtypeStruct((e, rows, d), BF16),
        grid=(e, d // td),
        in_specs=[pl.BlockSpec((1, rows, f), lambda ei, j: (ei, 0, 0)),
                  pl.BlockSpec((1, f, td), lambda ei, j: (ei, 0, j)),
                  pl.BlockSpec((1, rows, 1), lambda ei, j: (ei, 0, 0))],
        out_specs=pl.BlockSpec((1, rows, td), lambda ei, j: (ei, 0, j)),
        compiler_params=_params(("parallel", "arbitrary"), 48),
        name="expert_down",
    )(hid, w_down, gates)


def _combine_kernel(post_ref, y_ref, x_ref, gate_ref, o_ref, acc_ref, *, cap):
    ei = pl.program_id(3)
    eb = y_ref.shape[0]
    tt = post_ref.shape[1]
    post = post_ref[0].astype(F32)
    e_iota = lax.broadcasted_iota(jnp.int32, post.shape, 1)
    lane = lax.broadcasted_iota(jnp.int32, (tt, eb * cap), 1)
    hit = None
    for i in range(eb):
        col = jnp.sum(jnp.where(e_iota == ei * eb + i, post, 0.0), axis=-1, keepdims=True).astype(jnp.int32)
        h = (jnp.where(col >= 0, col + i * cap, -1) == lane)
        hit = h if hit is None else (hit | h)
    y = y_ref[...].reshape(eb * cap, y_ref.shape[-1])
    part = jnp.dot(hit.astype(BF16), y, preferred_element_type=F32)

    @pl.when(ei == 0)
    def _():
        acc_ref[...] = part

    @pl.when(ei > 0)
    def _():
        acc_ref[...] += part

    @pl.when(ei == pl.num_programs(3) - 1)
    def _():
        o_ref[...] = x_ref[...] + gate_ref[0] * acc_ref[...]


def combine_tokens(pos_t, y, x, gate, cap, row_offset, tt, td, experts_per_step):
    nseg, n, e = pos_t.shape
    r, d = x.shape
    eb = experts_per_step
    nt = n // tt
    rb = row_offset // cap
    return pl.pallas_call(
        functools.partial(_combine_kernel, cap=cap),
        out_shape=jax.ShapeDtypeStruct((r, d), F32),
        grid=(nseg, nt, d // td, e // eb),
        in_specs=[pl.BlockSpec((1, tt, e), lambda s, t, j, ei: (s, t, 0)),
                  pl.BlockSpec((eb, cap, td), lambda s, t, j, ei: (ei, rb + s, j)),
                  pl.BlockSpec((tt, td), lambda s, t, j, ei: (s * nt + t, j)),
                  pl.BlockSpec((1, 1, td), lambda s, t, j, ei: (s * gate.shape[0] // nseg, 0, j))],
        out_specs=pl.BlockSpec((tt, td), lambda s, t, j, ei: (s * nt + t, j)),
        scratch_shapes=[pltpu.VMEM((tt, td), F32)],
        compiler_params=_params(("parallel", "parallel", "parallel", "arbitrary"), 48),
        name="combine_tokens",
    )(pos_t, y, x, gate)


def moe_layer(groups, g_ffn, router_w, w_gate, w_up, w_down):
    sel = []
    for grp in groups:
        nseg, n = grp["nseg"], grp["n"]
        cap = (EC_CAPACITY * n) // N_EXPERTS
        h, logits = norm_router(grp["x"], g_ffn, grp["shift"], grp["scale"], router_w)
        logits_t = jnp.swapaxes(logits[:, :N_EXPERTS].reshape(nseg, n, N_EXPERTS), 1, 2)
        pos, gates = select_tokens(logits_t, cap)
        eb = N_EXPERTS if n <= 512 else 1
        xg = gather_tokens(pos, h.reshape(nseg, n, D_MODEL), cap, eb, min(n, 1024))
        sel.append(dict(cap=cap, pos=pos, xg=xg,
                        gates=jnp.swapaxes(gates, 0, 1).reshape(N_EXPERTS, nseg * cap, 1)))
    hid = expert_up(sel[0]["xg"], sel[1]["xg"], w_gate, w_up)
    y = expert_down(hid, w_down, jnp.concatenate([s["gates"] for s in sel], axis=1))
    out = []
    row_offset = 0
    for grp, s in zip(groups, sel):
        n, cap = grp["n"], s["cap"]
        small = n <= 512
        out.append(combine_tokens(jnp.swapaxes(s["pos"], 1, 2), y, grp["x"], grp["gate"], cap, row_offset,
                                  n if small else 1024, D_MODEL if small else 1024,
                                  N_EXPERTS if small else 1))
        row_offset += grp["nseg"] * cap
    return out


def _rope_tables(n):
    t = jnp.arange(n)
    row = (t // GRID_W).astype(F32)
    col = (t % GRID_W).astype(F32)
    nf = HEAD_DIM // 4
    inv = ROPE_THETA ** (-jnp.arange(nf, dtype=F32) / nf)
    ang = jnp.concatenate([row[:, None] * inv, col[:, None] * inv], axis=-1)
    cos, sin = jnp.cos(ang), jnp.sin(ang)
    return jnp.concatenate([cos, cos], axis=-1), jnp.concatenate([-sin, sin], axis=-1)


def kernel(x_prompt, x_sample, cache_win_k, cache_win_v, cache_nat_k, cache_nat_v, cache_glob_k, cache_glob_v, c, c_ctx, mod_w, mod_b, norm_mix_w, norm_ffn_w, even_w_in, even_w_out, conv_w, conv_b, win_sink, win_q_norm, win_k_norm, odd_w_in, odd_w_out, nat_rpb, nat_q_norm, nat_k_norm, glob_q_norm, glob_k_norm, router_w, expert_w_gate, expert_w_up, expert_w_down):
    bp, sp, d = x_prompt.shape
    bs, ss, _ = x_sample.shape
    depth = mod_w.shape[0]
    rope = _rope_tables(ss)

    cond8 = jnp.zeros((8, d), F32).at[0].set(c_ctx).at[1:1 + bs].set(c)
    mods = modulation(cond8, mod_w, mod_b).reshape(depth, 8, N_MOD, d)

    xp = x_prompt.reshape(bp * sp, d)
    xs = x_sample.reshape(bs * ss, d)
    win_k, win_v, nat_k, nat_v, glob_k, glob_v = [], [], [], [], [], []

    def flat_cache(cache, i):
        return cache[:, i].reshape(bs, cache.shape[2], -1).astype(BF16)

    for layer in range(depth):
        i = layer // 2
        mp = [mods[layer, 0:1, k].reshape(1, 1, d) for k in range(N_MOD)]
        ms = [mods[layer, 1:1 + bs, k].reshape(bs, 1, d) for k in range(N_MOD)]
        if layer % 2 == 0:
            w_in = even_w_in[i].astype(BF16)
            w_out = even_w_out[i].astype(BF16)
            q0 = 3 * CONV_DIM
            qw = WIN_Q_HEADS * HEAD_DIM
            kw = WIN_KV_HEADS * HEAD_DIM
            group = WIN_Q_HEADS // WIN_KV_HEADS
            pp = norm_proj(xp, norm_mix_w[layer], mp[0], mp[1], w_in, 4)
            ya = conv_gate(pp, conv_w[i], conv_b[i], sp)
            q = head_norm(pp, q0 // qw, qw, win_q_norm[i], scale=ATTN_SCALE)
            k, k32 = head_norm(pp, (q0 + qw) // kw, kw, win_k_norm[i], want_f32=True)
            v32 = pp[:, q0 + qw + kw:]
            yb = dense_attention(q.reshape(bp, sp, qw), k.reshape(bp, sp, kw), v32.astype(BF16).reshape(bp, sp, kw),
                                 group, sp, sink=win_sink[i])
            xp = out_proj_residual(ya, yb.reshape(bp * sp, qw), w_out, xp, mp[2])
            win_k.append(k32.reshape(bp, sp, WIN_KV_HEADS, HEAD_DIM))
            win_v.append(v32.reshape(bp, sp, WIN_KV_HEADS, HEAD_DIM))
            ps = norm_proj(xs, norm_mix_w[layer], ms[0], ms[1], w_in, 4)
            ya = conv_gate(ps, conv_w[i], conv_b[i], ss)
            q = head_norm(ps, q0 // qw, qw, win_q_norm[i], scale=ATTN_SCALE, rope_tables=rope)
            k = head_norm(ps, (q0 + qw) // kw, kw, win_k_norm[i], rope_tables=rope)
            v = ps[:, q0 + qw + kw:].astype(BF16)
            yb = window_attention(q.reshape(bs, ss, qw), k.reshape(bs, ss, kw), v.reshape(bs, ss, kw),
                                  flat_cache(cache_win_k, i), flat_cache(cache_win_v, i), win_sink[i], group)
            xs = out_proj_residual(ya, yb.reshape(bs * ss, qw), w_out, xs, ms[2])
        else:
            w_in = odd_w_in[i].astype(BF16)
            w_out = odd_w_out[i].astype(BF16)
            nw = NAT_HEADS * HEAD_DIM
            gqw = GLOB_Q_HEADS * HEAD_DIM
            gkw = GLOB_KV_HEADS * HEAD_DIM
            group = GLOB_Q_HEADS // GLOB_KV_HEADS
            kd0 = 3 * nw + gqw
            pp = norm_proj(xp, norm_mix_w[layer], mp[0], mp[1], w_in, 4)
            qc = head_norm(pp, 0, nw, nat_q_norm[i], scale=ATTN_SCALE)
            kc, kc32 = head_norm(pp, 1, nw, nat_k_norm[i], want_f32=True)
            vc32 = pp[:, 2 * nw:3 * nw]
            qd = head_norm(pp, 3, gqw, glob_q_norm[i], scale=ATTN_SCALE)
            kd, kd32 = head_norm(pp, kd0 // gkw, gkw, glob_k_norm[i], want_f32=True)
            vd32 = pp[:, kd0 + gkw:]
            yc = dense_attention(qc.reshape(bp, sp, nw), kc.reshape(bp, sp, nw),
                                 vc32.astype(BF16).reshape(bp, sp, nw), 1, sp)
            yd = dense_attention(qd.reshape(bp, sp, gqw), kd.reshape(bp, sp, gkw),
                                 vd32.astype(BF16).reshape(bp, sp, gkw), group, sp)
            xp = out_proj_residual(yc.reshape(bp * sp, nw), yd.reshape(bp * sp, gqw), w_out, xp, mp[2])
            nat_k.append(kc32.reshape(bp, sp, NAT_HEADS, HEAD_DIM))
            nat_v.append(vc32.reshape(bp, sp, NAT_HEADS, HEAD_DIM))
            glob_k.append(kd32.reshape(bp, sp, GLOB_KV_HEADS, HEAD_DIM))
            glob_v.append(vd32.reshape(bp, sp, GLOB_KV_HEADS, HEAD_DIM))
            ps = norm_proj(xs, norm_mix_w[layer], ms[0], ms[1], w_in, 4)
            qc = head_norm(ps, 0, nw, nat_q_norm[i], scale=ATTN_SCALE)
            kc = head_norm(ps, 1, nw, nat_k_norm[i])
            vc = ps[:, 2 * nw:3 * nw].astype(BF16)
            qd = head_norm(ps, 3, gqw, glob_q_norm[i], scale=ATTN_SCALE, rope_tables=rope)
            kd = head_norm(ps, kd0 // gkw, gkw, glob_k_norm[i], rope_tables=rope)
            vd = ps[:, kd0 + gkw:].astype(BF16)
            yc = neighbourhood_attention(qc.reshape(bs, ss, nw), kc.reshape(bs, ss, nw), vc.reshape(bs, ss, nw),
                                         flat_cache(cache_nat_k, i), flat_cache(cache_nat_v, i),
                                         _nat_bias_table(nat_rpb[i]))
            k_all = jnp.concatenate([kd.reshape(bs, ss, gkw), flat_cache(cache_glob_k, i)], axis=1)
            v_all = jnp.concatenate([vd.reshape(bs, ss, gkw), flat_cache(cache_glob_v, i)], axis=1)
            yd = dense_attention(qd.reshape(bs, ss, gqw), k_all, v_all, group, 128)
            xs = out_proj_residual(yc.reshape(bs * ss, nw), yd.reshape(bs * ss, gqw), w_out, xs, ms[2])

        xp, xs = moe_layer(
            [dict(x=xp, shift=mp[3], scale=mp[4], gate=mp[5], nseg=bp, n=sp),
             dict(x=xs, shift=ms[3], scale=ms[4], gate=ms[5], nseg=bs, n=ss)],
            norm_ffn_w[layer], router_w[layer], expert_w_gate[layer], expert_w_up[layer], expert_w_down[layer])

    stack = lambda xs_: jnp.stack(xs_, axis=1)
    return (xp.reshape(bp, sp, d), xs.reshape(bs, ss, d), stack(win_k), stack(win_v), stack(nat_k),
            stack(nat_v), stack(glob_k), stack(glob_v))
```

```python
import functools

import numpy as np
import jax
import jax.numpy as jnp
from jax import lax
from jax.experimental import pallas as pl
from jax.experimental.pallas import tpu as pltpu

D_MODEL = 2048
HEAD_DIM = 128
GRID_W = 64
CONV_DIM = 512
WIN_Q_HEADS = 12
WIN_KV_HEADS = 4
WIN_BLOCK = 128
WINDOW = 128
NAT_HEADS = 8
NAT_ROWS = 8
NAT_COLS = 16
GLOB_Q_HEADS = 8
GLOB_KV_HEADS = 2
ROPE_THETA = 10000.0
N_EXPERTS = 16
EXPERT_FF = 2048
EC_CAPACITY = 2
N_MOD = 6
NORM_EPS = 1e-6
NEG_INF = -1e30
ATTN_SCALE = HEAD_DIM ** -0.5

LANES = 128
MIB = 1024 * 1024
BF16 = jnp.bfloat16
F32 = jnp.float32


def _params(semantics, vmem_mib):
    return pltpu.CompilerParams(dimension_semantics=semantics, vmem_limit_bytes=vmem_mib * MIB)


def _silu(x):
    return x * (1.0 / (1.0 + jnp.exp(-x)))


def _modulation_kernel(cond_ref, w_ref, b_ref, o_ref):
    s = _silu(cond_ref[...])
    o_ref[0] = jnp.dot(s, w_ref[0], preferred_element_type=F32,
                       precision=lax.Precision.HIGHEST) + b_ref[0]


def modulation(cond8, mod_w, mod_b):
    depth, d, n = mod_w.shape
    tn = 512
    return pl.pallas_call(
        _modulation_kernel,
        out_shape=jax.ShapeDtypeStruct((depth, 8, n), F32),
        grid=(depth, n // tn),
        in_specs=[pl.BlockSpec((8, d), lambda l, j: (0, 0)),
                  pl.BlockSpec((1, d, tn), lambda l, j: (l, 0, j)),
                  pl.BlockSpec((1, 1, tn), lambda l, j: (l, 0, j))],
        out_specs=pl.BlockSpec((1, 8, tn), lambda l, j: (l, 0, j)),
        compiler_params=_params(("parallel", "parallel"), 48),
        name="modulation",
    )(cond8, mod_w, mod_b.reshape(depth, 1, n))


def _mod_norm(x, g, shift, scale):
    ms = jnp.mean(x * x, axis=-1, keepdims=True)
    y = x * lax.rsqrt(ms + NORM_EPS) * g
    return y * (1.0 + scale) + shift


def _norm_proj_kernel(x_ref, g_ref, shift_ref, scale_ref, w_ref, o_ref, h_ref):
    @pl.when(pl.program_id(1) == 0)
    def _():
        h_ref[...] = _mod_norm(x_ref[...], g_ref[...], shift_ref[0], scale_ref[0]).astype(BF16)

    o_ref[...] = jnp.dot(h_ref[...], w_ref[...], preferred_element_type=F32)


def norm_proj(x, g, shift, scale, w_bf16, n_tiles):
    r, d = x.shape
    n = w_bf16.shape[1]
    nseg = shift.shape[0]
    tm = 512
    tn = n // n_tiles
    tiles_per_seg = (r // nseg) // tm
    return pl.pallas_call(
        _norm_proj_kernel,
        out_shape=jax.ShapeDtypeStruct((r, n), F32),
        grid=(r // tm, n_tiles),
        in_specs=[pl.BlockSpec((tm, d), lambda i, j: (i, 0)),
                  pl.BlockSpec((1, d), lambda i, j: (0, 0)),
                  pl.BlockSpec((1, 1, d), lambda i, j: (i // tiles_per_seg, 0, 0)),
                  pl.BlockSpec((1, 1, d), lambda i, j: (i // tiles_per_seg, 0, 0)),
                  pl.BlockSpec((d, tn), lambda i, j: (0, j))],
        out_specs=pl.BlockSpec((tm, tn), lambda i, j: (i, j)),
        scratch_shapes=[pltpu.VMEM((tm, d), BF16)],
        compiler_params=_params(("parallel", "arbitrary"), 48),
        name="norm_proj",
    )(x, g.reshape(1, d), shift, scale, w_bf16)


def _conv_gate_kernel(ab_ref, ac_ref, ah_ref, w_ref, b_ref, o_ref, *, seq):
    u = ac_ref[...] * ah_ref[...]
    rows = u.shape[0]
    pos = lax.broadcasted_iota(jnp.int32, u.shape, 0) % seq
    prev = jnp.where(pos == 0, 0.0, pltpu.roll(u, 1, 0))
    nxt = jnp.where(pos == seq - 1, 0.0, pltpu.roll(u, rows - 1, 0))
    conv = prev * w_ref[0:1, :] + u * w_ref[1:2, :] + nxt * w_ref[2:3, :] + b_ref[...]
    o_ref[...] = (ab_ref[...] * conv).astype(o_ref.dtype)


def conv_gate(proj, conv_w, conv_b, seq):
    r = proj.shape[0]
    tr = 4096
    nc = CONV_DIM // LANES
    return pl.pallas_call(
        functools.partial(_conv_gate_kernel, seq=seq),
        out_shape=jax.ShapeDtypeStruct((r, CONV_DIM), BF16),
        grid=(r // tr, nc),
        in_specs=[pl.BlockSpec((tr, LANES), lambda i, c: (i, c)),
                  pl.BlockSpec((tr, LANES), lambda i, c: (i, nc + c)),
                  pl.BlockSpec((tr, LANES), lambda i, c: (i, 2 * nc + c)),
                  pl.BlockSpec((3, LANES), lambda i, c: (0, c)),
                  pl.BlockSpec((1, LANES), lambda i, c: (0, c))],
        out_specs=pl.BlockSpec((tr, LANES), lambda i, c: (i, c)),
        compiler_params=_params(("parallel", "parallel"), 48),
        name="conv_gate",
    )(proj, proj, proj, conv_w, conv_b.reshape(1, CONV_DIM))


def _head_norm_kernel(*refs, heads, rope, scale, want_f32):
    x_ref, g_ref = refs[0], refs[1]
    k = 2
    if rope:
        cos_ref, sin_ref = refs[2], refs[3]
        k = 4
    ob_ref = refs[k]
    g = g_ref[...]
    for h in range(heads):
        sl = slice(h * HEAD_DIM, (h + 1) * HEAD_DIM)
        x = x_ref[:, sl]
        y = x * lax.rsqrt(jnp.mean(x * x, axis=-1, keepdims=True) + NORM_EPS) * g
        if want_f32:
            refs[k + 1][:, sl] = y
        if rope:
            y = y * cos_ref[...] + pltpu.roll(y, HEAD_DIM // 2, 1) * sin_ref[...]
        ob_ref[:, sl] = (y * scale).astype(BF16)


def head_norm(proj, col_block, width, gain, scale=1.0, rope_tables=None, want_f32=False):
    r = proj.shape[0]
    tm = 512
    heads = width // HEAD_DIM
    rope = rope_tables is not None
    in_specs = [pl.BlockSpec((tm, width), lambda i: (i, col_block)),
                pl.BlockSpec((1, HEAD_DIM), lambda i: (0, 0))]
    args = [proj, gain.reshape(1, HEAD_DIM)]
    if rope:
        n_pos = rope_tables[0].shape[0]
        per = n_pos // tm
        in_specs += [pl.BlockSpec((tm, HEAD_DIM), lambda i: (i % per, 0))] * 2
        args += list(rope_tables)
    out_shape = [jax.ShapeDtypeStruct((r, width), BF16)]
    out_specs = [pl.BlockSpec((tm, width), lambda i: (i, 0))]
    if want_f32:
        out_shape.append(jax.ShapeDtypeStruct((r, width), F32))
        out_specs.append(pl.BlockSpec((tm, width), lambda i: (i, 0)))
    out = pl.pallas_call(
        functools.partial(_head_norm_kernel, heads=heads, rope=rope, scale=scale, want_f32=want_f32),
        out_shape=out_shape,
        grid=(r // tm,),
        in_specs=in_specs,
        out_specs=out_specs,
        compiler_params=_params(("parallel",), 32),
        name="head_norm",
    )(*args)
    return out if want_f32 else out[0]


def _dense_attn_kernel(*refs, group, has_sink):
    if has_sink:
        sink_ref, q_ref, k_ref, v_ref, o_ref = refs
    else:
        q_ref, k_ref, v_ref, o_ref = refs
    qb = q_ref.shape[1]
    for hkv in range(k_ref.shape[2] // HEAD_DIM):
        head = lambda g: slice((hkv * group + g) * HEAD_DIM, (hkv * group + g + 1) * HEAD_DIM)
        kv = slice(hkv * HEAD_DIM, (hkv + 1) * HEAD_DIM)
        q = jnp.concatenate([q_ref[0, :, head(g)] for g in range(group)], axis=0)
        s = lax.dot_general(q, k_ref[0, :, kv], (((1,), (1,)), ((), ())), preferred_element_type=F32)
        m = jnp.max(s, axis=-1, keepdims=True)
        if has_sink:
            sk = jnp.concatenate([jnp.full((qb, 1), sink_ref[hkv * group + g], F32) for g in range(group)], axis=0)
            m = jnp.maximum(m, sk)
        p = jnp.exp(s - m)
        den = jnp.sum(p, axis=-1, keepdims=True)
        if has_sink:
            den = den + jnp.exp(sk - m)
        o = jnp.dot(p.astype(BF16), v_ref[0, :, kv], preferred_element_type=F32) / den
        for g in range(group):
            o_ref[0, :, head(g)] = o[g * qb:(g + 1) * qb].astype(o_ref.dtype)


def dense_attention(q, k, v, group, q_block, sink=None):
    b, nq, qw = q.shape
    nk = k.shape[1]
    kvw = k.shape[2]
    in_specs = [pl.BlockSpec((1, q_block, qw), lambda bi, qi: (bi, qi, 0)),
                pl.BlockSpec((1, nk, kvw), lambda bi, qi: (bi, 0, 0)),
                pl.BlockSpec((1, nk, kvw), lambda bi, qi: (bi, 0, 0))]
    args = [q, k, v]
    if sink is not None:
        in_specs = [pl.BlockSpec(memory_space=pltpu.SMEM)] + in_specs
        args = [sink] + args
    return pl.pallas_call(
        functools.partial(_dense_attn_kernel, group=group, has_sink=sink is not None),
        out_shape=jax.ShapeDtypeStruct((b, nq, qw), BF16),
        grid=(b, nq // q_block),
        in_specs=in_specs,
        out_specs=pl.BlockSpec((1, q_block, qw), lambda bi, qi: (bi, qi, 0)),
        compiler_params=_params(("parallel", "parallel"), 56),
        name="dense_attention",
    )(*args)


def _window_attn_kernel(sink_ref, q_ref, kp_ref, kc_ref, kn_ref, vp_ref, vc_ref, vn_ref, ck_ref, cv_ref, o_ref,
                        *, group, n_tokens):
    blk = pl.program_id(1)
    wb = WIN_BLOCK
    rows, cols = group * wb, 3 * wb + ck_ref.shape[1]
    q_pos = blk * wb + lax.broadcasted_iota(jnp.int32, (rows, cols), 0) % wb
    col = lax.broadcasted_iota(jnp.int32, (rows, cols), 1)
    k_pos = (blk - 1) * wb + col
    local_ok = (jnp.abs(k_pos - q_pos) <= WINDOW) & (k_pos >= 0) & (k_pos < n_tokens)
    visible = (col >= 3 * wb) | local_ok
    for hkv in range(kc_ref.shape[2] // HEAD_DIM):
        head = lambda g: slice((hkv * group + g) * HEAD_DIM, (hkv * group + g + 1) * HEAD_DIM)
        kv = slice(hkv * HEAD_DIM, (hkv + 1) * HEAD_DIM)
        q = jnp.concatenate([q_ref[0, :, head(g)] for g in range(group)], axis=0)
        keys = jnp.concatenate([kp_ref[0, :, kv], kc_ref[0, :, kv], kn_ref[0, :, kv], ck_ref[0, :, kv]], axis=0)
        vals = jnp.concatenate([vp_ref[0, :, kv], vc_ref[0, :, kv], vn_ref[0, :, kv], cv_ref[0, :, kv]], axis=0)
        s = lax.dot_general(q, keys, (((1,), (1,)), ((), ())), preferred_element_type=F32)
        s = jnp.where(visible, s, NEG_INF)
        sk = jnp.concatenate([jnp.full((wb, 1), sink_ref[hkv * group + g], F32) for g in range(group)], axis=0)
        m = jnp.maximum(jnp.max(s, axis=-1, keepdims=True), sk)
        p = jnp.exp(s - m)
        den = jnp.sum(p, axis=-1, keepdims=True) + jnp.exp(sk - m)
        o = jnp.dot(p.astype(BF16), vals, preferred_element_type=F32) / den
        for g in range(group):
            o_ref[0, :, head(g)] = o[g * wb:(g + 1) * wb].astype(o_ref.dtype)


def window_attention(q, k, v, ctx_k, ctx_v, sink, group):
    b, n, qw = q.shape
    kvw = k.shape[2]
    nb = n // WIN_BLOCK
    nctx = ctx_k.shape[1]
    prev_map = lambda bi, i: (bi, jnp.maximum(i - 1, 0), 0)
    cur_map = lambda bi, i: (bi, i, 0)
    next_map = lambda bi, i: (bi, jnp.minimum(i + 1, nb - 1), 0)
    ctx_map = lambda bi, i: (bi, 0, 0)
    kv_block = (1, WIN_BLOCK, kvw)
    return pl.pallas_call(
        functools.partial(_window_attn_kernel, group=group, n_tokens=n),
        out_shape=jax.ShapeDtypeStruct((b, n, qw), BF16),
        grid=(b, nb),
        in_specs=[pl.BlockSpec(memory_space=pltpu.SMEM),
                  pl.BlockSpec((1, WIN_BLOCK, qw), cur_map),
                  pl.BlockSpec(kv_block, prev_map), pl.BlockSpec(kv_block, cur_map), pl.BlockSpec(kv_block, next_map),
                  pl.BlockSpec(kv_block, prev_map), pl.BlockSpec(kv_block, cur_map), pl.BlockSpec(kv_block, next_map),
                  pl.BlockSpec((1, nctx, kvw), ctx_map), pl.BlockSpec((1, nctx, kvw), ctx_map)],
        out_specs=pl.BlockSpec((1, WIN_BLOCK, qw), cur_map),
        compiler_params=_params(("parallel", "parallel"), 32),
        name="window_attention",
    )(sink, q, k, k, k, v, v, v, ctx_k, ctx_v)


NAT_QROWS = 8
NAT_KROWS = 16


def _nat_bias_table(rpb):
    w = GRID_W
    c = np.arange(w)[:, None]
    kc = np.arange(w)[None, :]
    ws = np.clip(c - NAT_COLS // 2, 0, w - NAT_COLS)
    col_ok = (kc >= ws) & (kc < ws + NAT_COLS)
    col_idx = np.clip(kc - c + NAT_COLS - 1, 0, 2 * NAT_COLS - 2)
    t = jnp.where(col_ok[None, None], rpb[:, :, col_idx], NEG_INF)
    dead = jnp.full_like(t[:, :1], NEG_INF)
    t = jnp.concatenate([dead, t, dead], axis=1)
    return jnp.concatenate([t[:, :-1], t[:, 1:]], axis=-1)


def _nat_attn_kernel(q_ref, k_ref, v_ref, ck_ref, cv_ref, t_ref, o_ref, *, grid_rows):
    m = pl.program_id(2)
    w = GRID_W
    r0 = m * NAT_QROWS
    kr0 = jnp.clip(r0 - NAT_ROWS // 2, 0, grid_rows - NAT_KROWS)
    tok0 = pl.multiple_of(kr0 * w, 4 * w)
    keys = k_ref[0, pl.ds(tok0, NAT_KROWS * w), :]
    vals = v_ref[0, pl.ds(tok0, NAT_KROWS * w), :]
    q = q_ref[0]
    s_loc = lax.dot_general(q, keys, (((1,), (1,)), ((), ())), preferred_element_type=F32)
    lane_hi = lax.broadcasted_iota(jnp.int32, (w, 2 * w), 1) >= w
    bias_rows = []
    for i in range(NAT_QROWS):
        r = r0 + i
        rs = jnp.clip(r - NAT_ROWS // 2, 0, grid_rows - NAT_ROWS)
        pieces = []
        for jj in range(NAT_KROWS // 2):
            kr = kr0 + 2 * jj
            d = kr - r + NAT_ROWS - 1
            ok_lo = ((kr >= rs) & (kr < rs + NAT_ROWS)).astype(jnp.int32)
            ok_hi = ((kr + 1 >= rs) & (kr + 1 < rs + NAT_ROWS)).astype(jnp.int32)
            piece = t_ref[0, jnp.clip(d, -1, 2 * NAT_ROWS - 2) + 1]
            ok = jnp.where(lane_hi, ok_hi, ok_lo) > 0
            pieces.append(jnp.where(ok, piece, NEG_INF))
        bias_rows.append(jnp.concatenate(pieces, axis=1))
    s_loc = s_loc + jnp.concatenate(bias_rows, axis=0)
    s_ctx = lax.dot_general(q, ck_ref[0], (((1,), (1,)), ((), ())), preferred_element_type=F32)
    mx = jnp.maximum(jnp.max(s_loc, axis=-1, keepdims=True), jnp.max(s_ctx, axis=-1, keepdims=True))
    p_loc = jnp.exp(s_loc - mx)
    p_ctx = jnp.exp(s_ctx - mx)
    den = jnp.sum(p_loc, axis=-1, keepdims=True) + jnp.sum(p_ctx, axis=-1, keepdims=True)
    o = (jnp.dot(p_loc.astype(BF16), vals, preferred_element_type=F32)
         + jnp.dot(p_ctx.astype(BF16), cv_ref[0], preferred_element_type=F32))
    o_ref[0] = (o / den).astype(o_ref.dtype)


def neighbourhood_attention(q, k, v, ctx_k, ctx_v, bias_table):
    b, n, hw = q.shape
    heads = hw // HEAD_DIM
    grid_rows = n // GRID_W
    nctx = ctx_k.shape[1]
    tq = NAT_QROWS * GRID_W
    return pl.pallas_call(
        functools.partial(_nat_attn_kernel, grid_rows=grid_rows),
        out_shape=jax.ShapeDtypeStruct((b, n, hw), BF16),
        grid=(b, heads, grid_rows // NAT_QROWS),
        in_specs=[pl.BlockSpec((1, tq, HEAD_DIM), lambda bi, h, m: (bi, m, h)),
                  pl.BlockSpec((1, n, HEAD_DIM), lambda bi, h, m: (bi, 0, h)),
                  pl.BlockSpec((1, n, HEAD_DIM), lambda bi, h, m: (bi, 0, h)),
                  pl.BlockSpec((1, nctx, HEAD_DIM), lambda bi, h, m: (bi, 0, h)),
                  pl.BlockSpec((1, nctx, HEAD_DIM), lambda bi, h, m: (bi, 0, h)),
                  pl.BlockSpec((1, 2 * NAT_ROWS, GRID_W, 2 * GRID_W), lambda bi, h, m: (h, 0, 0, 0))],
        out_specs=pl.BlockSpec((1, tq, HEAD_DIM), lambda bi, h, m: (bi, m, h)),
        compiler_params=_params(("parallel", "parallel", "arbitrary"), 32),
        name="neighbourhood_attention",
    )(q, k, v, ctx_k, ctx_v, bias_table)


def _out_proj_kernel(a_ref, b_ref, w_ref, x_ref, gate_ref, o_ref):
    ka = a_ref.shape[1]
    acc = jnp.dot(a_ref[...], w_ref[:ka, :], preferred_element_type=F32)
    acc = acc + jnp.dot(b_ref[...], w_ref[ka:, :], preferred_element_type=F32)
    o_ref[...] = x_ref[...] + gate_ref[0] * acc


def out_proj_residual(a, b_, w_bf16, x, gate):
    r, d = x.shape
    ka, kb = a.shape[1], b_.shape[1]
    nseg = gate.shape[0]
    tm, tn = 1024, 1024
    tiles_per_seg = (r // nseg) // tm
    return pl.pallas_call(
        _out_proj_kernel,
        out_shape=jax.ShapeDtypeStruct((r, d), F32),
        grid=(r // tm, d // tn),
        in_specs=[pl.BlockSpec((tm, ka), lambda i, j: (i, 0)),
                  pl.BlockSpec((tm, kb), lambda i, j: (i, 0)),
                  pl.BlockSpec((ka + kb, tn), lambda i, j: (0, j)),
                  pl.BlockSpec((tm, tn), lambda i, j: (i, j)),
                  pl.BlockSpec((1, 1, tn), lambda i, j: (i // tiles_per_seg, 0, j))],
        out_specs=pl.BlockSpec((tm, tn), lambda i, j: (i, j)),
        compiler_params=_params(("parallel", "parallel"), 56),
        name="out_proj_residual",
    )(a, b_, w_bf16, x, gate)


def _norm_router_kernel(x_ref, g_ref, shift_ref, scale_ref, whi_ref, wlo_ref, h_ref, logit_ref):
    h = _mod_norm(x_ref[...], g_ref[...], shift_ref[0], scale_ref[0])
    h_hi = h.astype(BF16)
    h_lo = (h - h_hi.astype(F32)).astype(BF16)
    h_ref[...] = h_hi
    w_hi = whi_ref[...]
    logit_ref[...] = (jnp.dot(h_hi, w_hi, preferred_element_type=F32)
                      + jnp.dot(h_lo, w_hi, preferred_element_type=F32)
                      + jnp.dot(h_hi, wlo_ref[...], preferred_element_type=F32))


def norm_router(x, g, shift, scale, router_w):
    r, d = x.shape
    nseg = shift.shape[0]
    tm = 512
    tiles_per_seg = (r // nseg) // tm
    w_pad = jnp.zeros((d, LANES), F32).at[:, :N_EXPERTS].set(router_w)
    w_hi = w_pad.astype(BF16)
    w_lo = (w_pad - w_hi.astype(F32)).astype(BF16)
    return pl.pallas_call(
        _norm_router_kernel,
        out_shape=[jax.ShapeDtypeStruct((r, d), BF16), jax.ShapeDtypeStruct((r, LANES), F32)],
        grid=(r // tm,),
        in_specs=[pl.BlockSpec((tm, d), lambda i: (i, 0)),
                  pl.BlockSpec((1, d), lambda i: (0, 0)),
                  pl.BlockSpec((1, 1, d), lambda i: (i // tiles_per_seg, 0, 0)),
                  pl.BlockSpec((1, 1, d), lambda i: (i // tiles_per_seg, 0, 0)),
                  pl.BlockSpec((d, LANES), lambda i: (0, 0)),
                  pl.BlockSpec((d, LANES), lambda i: (0, 0))],
        out_specs=[pl.BlockSpec((tm, d), lambda i: (i, 0)), pl.BlockSpec((tm, LANES), lambda i: (i, 0))],
        compiler_params=_params(("parallel",), 32),
        name="norm_router",
    )(x, g.reshape(1, d), shift, scale, w_hi, w_lo)


CUMSUM_BLOCK = 256


def _exclusive_cumsum_lanes(flags):
    e, n = flags.shape
    cb = min(CUMSUM_BLOCK, n)
    tri = (lax.broadcasted_iota(jnp.int32, (cb, cb), 0) < lax.broadcasted_iota(jnp.int32, (cb, cb), 1)).astype(BF16)
    carry = jnp.zeros((e, 1), F32)
    out = []
    for j in range(n // cb):
        blk = flags[:, j * cb:(j + 1) * cb]
        out.append(jnp.dot(blk.astype(BF16), tri, preferred_element_type=F32) + carry)
        carry = carry + jnp.sum(blk, axis=-1, keepdims=True)
    return jnp.concatenate(out, axis=1) if len(out) > 1 else out[0]


def _select_kernel(logit_ref, pos_ref, gate_ref, *, cap):
    lg = logit_ref[0]
    e, n = lg.shape
    ex = jnp.exp(lg - jnp.max(lg, axis=0, keepdims=True))
    aff = ex / jnp.sum(ex, axis=0, keepdims=True)
    key = pltpu.bitcast(aff, jnp.int32)
    thr = jnp.zeros((e, 1), jnp.int32)
    for bit in range(30, -1, -1):
        cand = thr | (1 << bit)
        cnt = jnp.sum((key >= cand).astype(F32), axis=-1, keepdims=True)
        thr = jnp.where(cnt >= cap, cand, thr)
    above = (key > thr).astype(F32)
    tied = (key == thr).astype(F32)
    need = cap - jnp.sum(above, axis=-1, keepdims=True)
    take = above + tied * (_exclusive_cumsum_lanes(tied) < need).astype(F32)
    slot = _exclusive_cumsum_lanes(take)
    pos = jnp.where(take > 0, slot, -1.0).astype(jnp.int32)
    pos_ref[0] = pos
    s_iota = lax.broadcasted_iota(jnp.int32, (cap, n), 0)
    for ei in range(e):
        hit = pos[ei:ei + 1, :] == s_iota
        gate_ref[0, ei] = jnp.sum(jnp.where(hit, aff[ei:ei + 1, :], 0.0), axis=-1, keepdims=True)


def select_tokens(logits_t, cap):
    nseg, e, n = logits_t.shape
    return pl.pallas_call(
        functools.partial(_select_kernel, cap=cap),
        out_shape=[jax.ShapeDtypeStruct((nseg, e, n), jnp.int32),
                   jax.ShapeDtypeStruct((nseg, e, cap, 1), F32)],
        grid=(nseg,),
        in_specs=[pl.BlockSpec((1, e, n), lambda s: (s, 0, 0))],
        out_specs=[pl.BlockSpec((1, e, n), lambda s: (s, 0, 0)),
                   pl.BlockSpec((1, e, cap, 1), lambda s: (s, 0, 0, 0))],
        compiler_params=_params(("parallel",), 48),
        name="select_tokens",
    )(logits_t)


def _gather_kernel(pos_ref, h_ref, o_ref, acc_ref, *, cap):
    kt = pl.program_id(2)
    eb = pos_ref.shape[1]
    tk = pos_ref.shape[3]
    s_iota = lax.broadcasted_iota(jnp.int32, (cap, tk), 0)
    onehot = jnp.concatenate([(pos_ref[0, i] == s_iota).astype(BF16) for i in range(eb)], axis=0)
    part = jnp.dot(onehot, h_ref[0], preferred_element_type=F32)

    @pl.when(kt == 0)
    def _():
        acc_ref[...] = part

    @pl.when(kt > 0)
    def _():
        acc_ref[...] += part

    @pl.when(kt == pl.num_programs(2) - 1)
    def _():
        o_ref[0] = acc_ref[...].reshape(o_ref.shape[1:]).astype(o_ref.dtype)


def gather_tokens(pos, h, cap, experts_per_step, tk):
    nseg, e, n = pos.shape
    d = h.shape[2]
    eb = experts_per_step
    return pl.pallas_call(
        functools.partial(_gather_kernel, cap=cap),
        out_shape=jax.ShapeDtypeStruct((nseg, e, cap, d), BF16),
        grid=(nseg, e // eb, n // tk),
        in_specs=[pl.BlockSpec((1, eb, 1, tk), lambda s, ei, kt: (s, ei, 0, kt)),
                  pl.BlockSpec((1, tk, d), lambda s, ei, kt: (s, kt, 0))],
        out_specs=pl.BlockSpec((1, eb, cap, d), lambda s, ei, kt: (s, ei, 0, 0)),
        scratch_shapes=[pltpu.VMEM((eb * cap, d), F32)],
        compiler_params=_params(("parallel", "parallel", "arbitrary"), 48),
        name="gather_tokens",
    )(pos.reshape(nseg, e, 1, n), h)


def _expert_up_kernel(xc_ref, xl_ref, wg_ref, wu_ref, o_ref):
    d = xc_ref.shape[-1]
    wg = wg_ref[0, 0].astype(BF16)
    wu = wu_ref[0, 0].astype(BF16)
    row = 0
    for x_ref in (xc_ref, xl_ref):
        rows = x_ref.shape[0] * x_ref.shape[2]
        x = x_ref[...].reshape(rows, d)
        a = jnp.dot(x, wg, preferred_element_type=F32)
        b = jnp.dot(x, wu, preferred_element_type=F32)
        o_ref[0, row:row + rows, :] = (_silu(a) * b).astype(o_ref.dtype)
        row += rows


def expert_up(xg_ctx, xg_lat, w_gate, w_up, layer):
    _, e, d, f = w_gate.shape
    tf = 256
    sc, _, cc, _ = xg_ctx.shape
    sl, _, cl, _ = xg_lat.shape
    rows = sc * cc + sl * cl
    return pl.pallas_call(
        _expert_up_kernel,
        out_shape=jax.ShapeDtypeStruct((e, rows, f), BF16),
        grid=(e, f // tf),
        in_specs=[pl.BlockSpec((sc, 1, cc, d), lambda ei, j: (0, ei, 0, 0)),
                  pl.BlockSpec((sl, 1, cl, d), lambda ei, j: (0, ei, 0, 0)),
                  pl.BlockSpec((1, 1, d, tf), lambda ei, j: (layer, ei, 0, j)),
                  pl.BlockSpec((1, 1, d, tf), lambda ei, j: (layer, ei, 0, j))],
        out_specs=pl.BlockSpec((1, rows, tf), lambda ei, j: (ei, 0, j)),
        compiler_params=_params(("parallel", "arbitrary"), 56),
        name="expert_up",
    )(xg_ctx, xg_lat, w_gate, w_up)


def _expert_down_kernel(h_ref, w_ref, gate_ref, o_ref):
    y = jnp.dot(h_ref[0], w_ref[0, 0].astype(BF16), preferred_element_type=F32)
    o_ref[0] = (y * gate_ref[0]).astype(o_ref.dtype)


def expert_down(hid, w_down, gates, layer):
    e, rows, f = hid.shape
    d = w_down.shape[3]
    td = 256
    return pl.pallas_call(
        _expert_down_kernel,
        out_shape=jax.ShapeDtypeStruct((e, rows, d), BF16),
        grid=(e, d // td),
        in_specs=[pl.BlockSpec((1, rows, f), lambda ei, j: (ei, 0, 0)),
                  pl.BlockSpec((1, 1, f, td), lambda ei, j: (layer, ei, 0, j)),
                  pl.BlockSpec((1, rows, 1), lambda ei, j: (ei, 0, 0))],
        out_specs=pl.BlockSpec((1, rows, td), lambda ei, j: (ei, 0, j)),
        compiler_params=_params(("parallel", "arbitrary"), 48),
        name="expert_down",
    )(hid, w_down, gates)


def _combine_kernel(post_ref, y_ref, x_ref, gate_ref, o_ref, acc_ref, *, cap):
    ei = pl.program_id(3)
    eb = y_ref.shape[0]
    tt = post_ref.shape[1]
    post = post_ref[0].astype(F32)
    e_iota = lax.broadcasted_iota(jnp.int32, post.shape, 1)
    lane = lax.broadcasted_iota(jnp.int32, (tt, eb * cap), 1)
    hit = None
    for i in range(eb):
        col = jnp.sum(jnp.where(e_iota == ei * eb + i, post, 0.0), axis=-1, keepdims=True).astype(jnp.int32)
        h = (jnp.where(col >= 0, col + i * cap, -1) == lane)
        hit = h if hit is None else (hit | h)
    y = y_ref[...].reshape(eb * cap, y_ref.shape[-1])
    part = jnp.dot(hit.astype(BF16), y, preferred_element_type=F32)

    @pl.when(ei == 0)
    def _():
        acc_ref[...] = part

    @pl.when(ei > 0)
    def _():
        acc_ref[...] += part

    @pl.when(ei == pl.num_programs(3) - 1)
    def _():
        o_ref[...] = x_ref[...] + gate_ref[0] * acc_ref[...]


def combine_tokens(pos_t, y, x, gate, cap, row_offset, tt, td, experts_per_step):
    nseg, n, e = pos_t.shape
    r, d = x.shape
    eb = experts_per_step
    nt = n // tt
    rb = row_offset // cap
    return pl.pallas_call(
        functools.partial(_combine_kernel, cap=cap),
        out_shape=jax.ShapeDtypeStruct((r, d), F32),
        grid=(nseg, nt, d // td, e // eb),
        in_specs=[pl.BlockSpec((1, tt, e), lambda s, t, j, ei: (s, t, 0)),
                  pl.BlockSpec((eb, cap, td), lambda s, t, j, ei: (ei, rb + s, j)),
                  pl.BlockSpec((tt, td), lambda s, t, j, ei: (s * nt + t, j)),
                  pl.BlockSpec((1, 1, td), lambda s, t, j, ei: (s * gate.shape[0] // nseg, 0, j))],
        out_specs=pl.BlockSpec((tt, td), lambda s, t, j, ei: (s * nt + t, j)),
        scratch_shapes=[pltpu.VMEM((tt, td), F32)],
        compiler_params=_params(("parallel", "parallel", "parallel", "arbitrary"), 48),
        name="combine_tokens",
    )(pos_t, y, x, gate)


def moe_layer(groups, g_ffn, router_w, w_gate, w_up, w_down, layer):
    sel = []
    for grp in groups:
        nseg, n = grp["nseg"], grp["n"]
        cap = (EC_CAPACITY * n) // N_EXPERTS
        h, logits = norm_router(grp["x"], g_ffn, grp["shift"], grp["scale"], router_w)
        logits_t = jnp.swapaxes(logits[:, :N_EXPERTS].reshape(nseg, n, N_EXPERTS), 1, 2)
        pos, gates = select_tokens(logits_t, cap)
        eb = N_EXPERTS if n <= 512 else 1
        xg = gather_tokens(pos, h.reshape(nseg, n, D_MODEL), cap, eb, min(n, 2048))
        sel.append(dict(cap=cap, pos=pos, xg=xg,
                        gates=jnp.swapaxes(gates, 0, 1).reshape(N_EXPERTS, nseg * cap, 1)))
    hid = expert_up(sel[0]["xg"], sel[1]["xg"], w_gate, w_up, layer)
    y = expert_down(hid, w_down, jnp.concatenate([s["gates"] for s in sel], axis=1), layer)
    out = []
    row_offset = 0
    for grp, s in zip(groups, sel):
        n, cap = grp["n"], s["cap"]
        small = n <= 512
        out.append(combine_tokens(jnp.swapaxes(s["pos"], 1, 2), y, grp["x"], grp["gate"], cap, row_offset,
                                  n if small else 512, D_MODEL if small else 1024,
                                  N_EXPERTS if small else 4))
        row_offset += grp["nseg"] * cap
    return out


def _rope_tables(n):
    t = jnp.arange(n)
    row = (t // GRID_W).astype(F32)
    col = (t % GRID_W).astype(F32)
    nf = HEAD_DIM // 4
    inv = ROPE_THETA ** (-jnp.arange(nf, dtype=F32) / nf)
    ang = jnp.concatenate([row[:, None] * inv, col[:, None] * inv], axis=-1)
    cos, sin = jnp.cos(ang), jnp.sin(ang)
    return jnp.concatenate([cos, cos], axis=-1), jnp.concatenate([-sin, sin], axis=-1)


def kernel(x_prompt, x_sample, cache_win_k, cache_win_v, cache_nat_k, cache_nat_v, cache_glob_k, cache_glob_v, c, c_ctx, mod_w, mod_b, norm_mix_w, norm_ffn_w, even_w_in, even_w_out, conv_w, conv_b, win_sink, win_q_norm, win_k_norm, odd_w_in, odd_w_out, nat_rpb, nat_q_norm, nat_k_norm, glob_q_norm, glob_k_norm, router_w, expert_w_gate, expert_w_up, expert_w_down):
    bp, sp, d = x_prompt.shape
    bs, ss, _ = x_sample.shape
    depth = mod_w.shape[0]
    rope = _rope_tables(ss)

    cond8 = jnp.zeros((8, d), F32).at[0].set(c_ctx).at[1:1 + bs].set(c)
    mods = modulation(cond8, mod_w, mod_b).reshape(depth, 8, N_MOD, d)

    xp = x_prompt.reshape(bp * sp, d)
    xs = x_sample.reshape(bs * ss, d)
    win_k, win_v, nat_k, nat_v, glob_k, glob_v = [], [], [], [], [], []

    def flat_cache(cache, i):
        return cache[:, i].reshape(bs, cache.shape[2], -1).astype(BF16)

    for layer in range(depth):
        i = layer // 2
        mp = [mods[layer, 0:1, k].reshape(1, 1, d) for k in range(N_MOD)]
        ms = [mods[layer, 1:1 + bs, k].reshape(bs, 1, d) for k in range(N_MOD)]
        if layer % 2 == 0:
            w_in = even_w_in[i].astype(BF16)
            w_out = even_w_out[i].astype(BF16)
            q0 = 3 * CONV_DIM
            qw = WIN_Q_HEADS * HEAD_DIM
            kw = WIN_KV_HEADS * HEAD_DIM
            group = WIN_Q_HEADS // WIN_KV_HEADS
            pp = norm_proj(xp, norm_mix_w[layer], mp[0], mp[1], w_in, 4)
            ya = conv_gate(pp, conv_w[i], conv_b[i], sp)
            q = head_norm(pp, q0 // qw, qw, win_q_norm[i], scale=ATTN_SCALE)
            k, k32 = head_norm(pp, (q0 + qw) // kw, kw, win_k_norm[i], want_f32=True)
            v32 = pp[:, q0 + qw + kw:]
            yb = dense_attention(q.reshape(bp, sp, qw), k.reshape(bp, sp, kw), v32.astype(BF16).reshape(bp, sp, kw),
                                 group, sp, sink=win_sink[i])
            xp = out_proj_residual(ya, yb.reshape(bp * sp, qw), w_out, xp, mp[2])
            win_k.append(k32.reshape(bp, sp, WIN_KV_HEADS, HEAD_DIM))
            win_v.append(v32.reshape(bp, sp, WIN_KV_HEADS, HEAD_DIM))
            ps = norm_proj(xs, norm_mix_w[layer], ms[0], ms[1], w_in, 4)
            ya = conv_gate(ps, conv_w[i], conv_b[i], ss)
            q = head_norm(ps, q0 // qw, qw, win_q_norm[i], scale=ATTN_SCALE, rope_tables=rope)
            k = head_norm(ps, (q0 + qw) // kw, kw, win_k_norm[i], rope_tables=rope)
            v = ps[:, q0 + qw + kw:].astype(BF16)
            yb = window_attention(q.reshape(bs, ss, qw), k.reshape(bs, ss, kw), v.reshape(bs, ss, kw),
                                  flat_cache(cache_win_k, i), flat_cache(cache_win_v, i), win_sink[i], group)
            xs = out_proj_residual(ya, yb.reshape(bs * ss, qw), w_out, xs, ms[2])
        else:
            w_in = odd_w_in[i].astype(BF16)
            w_out = odd_w_out[i].astype(BF16)
            nw = NAT_HEADS * HEAD_DIM
            gqw = GLOB_Q_HEADS * HEAD_DIM
            gkw = GLOB_KV_HEADS * HEAD_DIM
            group = GLOB_Q_HEADS // GLOB_KV_HEADS
            kd0 = 3 * nw + gqw
            pp = norm_proj(xp, norm_mix_w[layer], mp[0], mp[1], w_in, 4)
            qc = head_norm(pp, 0, nw, nat_q_norm[i], scale=ATTN_SCALE)
            kc, kc32 = head_norm(pp, 1, nw, nat_k_norm[i], want_f32=True)
            vc32 = pp[:, 2 * nw:3 * nw]
            qd = head_norm(pp, 3, gqw, glob_q_norm[i], scale=ATTN_SCALE)
            kd, kd32 = head_norm(pp, kd0 // gkw, gkw, glob_k_norm[i], want_f32=True)
            vd32 = pp[:, kd0 + gkw:]
            yc = dense_attention(qc.reshape(bp, sp, nw), kc.reshape(bp, sp, nw),
                                 vc32.astype(BF16).reshape(bp, sp, nw), 1, sp)
            yd = dense_attention(qd.reshape(bp, sp, gqw), kd.reshape(bp, sp, gkw),
                                 vd32.astype(BF16).reshape(bp, sp, gkw), group, sp)
            xp = out_proj_residual(yc.reshape(bp * sp, nw), yd.reshape(bp * sp, gqw), w_out, xp, mp[2])
            nat_k.append(kc32.reshape(bp, sp, NAT_HEADS, HEAD_DIM))
            nat_v.append(vc32.reshape(bp, sp, NAT_HEADS, HEAD_DIM))
            glob_k.append(kd32.reshape(bp, sp, GLOB_KV_HEADS, HEAD_DIM))
            glob_v.append(vd32.reshape(bp, sp, GLOB_KV_HEADS, HEAD_DIM))
            ps = norm_proj(xs, norm_mix_w[layer], ms[0], ms[1], w_in, 4)
            qc = head_norm(ps, 0, nw, nat_q_norm[i], scale=ATTN_SCALE)
            kc = head_norm(ps, 1, nw, nat_k_norm[i])
            vc = ps[:, 2 * nw:3 * nw].astype(BF16)
            qd = head_norm(ps, 3, gqw, glob_q_norm[i], scale=ATTN_SCALE, rope_tables=rope)
            kd = head_norm(ps, kd0 // gkw, gkw, glob_k_norm[i], rope_tables=rope)
            vd = ps[:, kd0 + gkw:].astype(BF16)
            yc = neighbourhood_attention(qc.reshape(bs, ss, nw), kc.reshape(bs, ss, nw), vc.reshape(bs, ss, nw),
                                         flat_cache(cache_nat_k, i), flat_cache(cache_nat_v, i),
                                         _nat_bias_table(nat_rpb[i]))
            k_all = jnp.concatenate([kd.reshape(bs, ss, gkw), flat_cache(cache_glob_k, i)], axis=1)
            v_all = jnp.concatenate([vd.reshape(bs, ss, gkw), flat_cache(cache_glob_v, i)], axis=1)
            yd = dense_attention(qd.reshape(bs, ss, gqw), k_all, v_all, group, 128)
            xs = out_proj_residual(yc.reshape(bs * ss, nw), yd.reshape(bs * ss, gqw), w_out, xs, ms[2])

        xp, xs = moe_layer(
            [dict(x=xp, shift=mp[3], scale=mp[4], gate=mp[5], nseg=bp, n=sp),
             dict(x=xs, shift=ms[3], scale=ms[4], gate=ms[5], nseg=bs, n=ss)],
            norm_ffn_w[layer], router_w[layer], expert_w_gate, expert_w_up, expert_w_down, layer)

    stack = lambda xs_: jnp.stack(xs_, axis=1)
    return (xp.reshape(bp, sp, d), xs.reshape(bs, ss, d), stack(win_k), stack(win_v), stack(nat_k),
            stack(nat_v), stack(glob_k), stack(glob_v))
```

```python
import functools

import numpy as np
import jax
import jax.numpy as jnp
from jax import lax
from jax.experimental import pallas as pl
from jax.experimental.pallas import tpu as pltpu

D_MODEL = 2048
HEAD_DIM = 128
GRID_W = 64
CONV_DIM = 512
WIN_Q_HEADS = 12
WIN_KV_HEADS = 4
WIN_BLOCK = 128
WINDOW = 128
NAT_HEADS = 8
NAT_ROWS = 8
NAT_COLS = 16
GLOB_Q_HEADS = 8
GLOB_KV_HEADS = 2
ROPE_THETA = 10000.0
N_EXPERTS = 16
EXPERT_FF = 2048
EC_CAPACITY = 2
N_MOD = 6
NORM_EPS = 1e-6
NEG_INF = -1e30
ATTN_SCALE = HEAD_DIM ** -0.5

LANES = 128
MIB = 1024 * 1024
BF16 = jnp.bfloat16
F32 = jnp.float32


def _params(semantics, vmem_mib):
    return pltpu.CompilerParams(dimension_semantics=semantics, vmem_limit_bytes=vmem_mib * MIB)


def _silu(x):
    return x * (1.0 / (1.0 + jnp.exp(-x)))


def _modulation_kernel(cond_ref, w_ref, b_ref, o_ref):
    s = _silu(cond_ref[...])
    o_ref[0] = jnp.dot(s, w_ref[0], preferred_element_type=F32,
                       precision=lax.Precision.HIGHEST) + b_ref[0]


def modulation(cond8, mod_w, mod_b):
    depth, d, n = mod_w.shape
    tn = 512
    return pl.pallas_call(
        _modulation_kernel,
        out_shape=jax.ShapeDtypeStruct((depth, 8, n), F32),
        grid=(depth, n // tn),
        in_specs=[pl.BlockSpec((8, d), lambda l, j: (0, 0)),
                  pl.BlockSpec((1, d, tn), lambda l, j: (l, 0, j)),
                  pl.BlockSpec((1, 1, tn), lambda l, j: (l, 0, j))],
        out_specs=pl.BlockSpec((1, 8, tn), lambda l, j: (l, 0, j)),
        compiler_params=_params(("parallel", "parallel"), 48),
        name="modulation",
    )(cond8, mod_w, mod_b.reshape(depth, 1, n))


def _mod_norm(x, g, shift, scale):
    ms = jnp.mean(x * x, axis=-1, keepdims=True)
    y = x * lax.rsqrt(ms + NORM_EPS) * g
    return y * (1.0 + scale) + shift


def _norm_proj_kernel(x_ref, g_ref, shift_ref, scale_ref, w_ref, o_ref, h_ref):
    @pl.when(pl.program_id(1) == 0)
    def _():
        h_ref[...] = _mod_norm(x_ref[...], g_ref[...], shift_ref[0], scale_ref[0]).astype(BF16)

    o_ref[...] = jnp.dot(h_ref[...], w_ref[...], preferred_element_type=F32)


def norm_proj(x, g, shift, scale, w_bf16, n_tiles):
    r, d = x.shape
    n = w_bf16.shape[1]
    nseg = shift.shape[0]
    tm = 512
    tn = n // n_tiles
    tiles_per_seg = (r // nseg) // tm
    return pl.pallas_call(
        _norm_proj_kernel,
        out_shape=jax.ShapeDtypeStruct((r, n), F32),
        grid=(r // tm, n_tiles),
        in_specs=[pl.BlockSpec((tm, d), lambda i, j: (i, 0)),
                  pl.BlockSpec((1, d), lambda i, j: (0, 0)),
                  pl.BlockSpec((1, 1, d), lambda i, j: (i // tiles_per_seg, 0, 0)),
                  pl.BlockSpec((1, 1, d), lambda i, j: (i // tiles_per_seg, 0, 0)),
                  pl.BlockSpec((d, tn), lambda i, j: (0, j))],
        out_specs=pl.BlockSpec((tm, tn), lambda i, j: (i, j)),
        scratch_shapes=[pltpu.VMEM((tm, d), BF16)],
        compiler_params=_params(("parallel", "arbitrary"), 48),
        name="norm_proj",
    )(x, g.reshape(1, d), shift, scale, w_bf16)


def _conv_gate_kernel(ab_ref, ac_ref, ah_ref, w_ref, b_ref, o_ref, *, seq):
    u = ac_ref[...] * ah_ref[...]
    rows = u.shape[0]
    pos = lax.broadcasted_iota(jnp.int32, u.shape, 0) % seq
    prev = jnp.where(pos == 0, 0.0, pltpu.roll(u, 1, 0))
    nxt = jnp.where(pos == seq - 1, 0.0, pltpu.roll(u, rows - 1, 0))
    conv = prev * w_ref[0:1, :] + u * w_ref[1:2, :] + nxt * w_ref[2:3, :] + b_ref[...]
    o_ref[...] = (ab_ref[...] * conv).astype(o_ref.dtype)


def conv_gate(proj, conv_w, conv_b, seq):
    r = proj.shape[0]
    tr = 4096
    nc = CONV_DIM // LANES
    return pl.pallas_call(
        functools.partial(_conv_gate_kernel, seq=seq),
        out_shape=jax.ShapeDtypeStruct((r, CONV_DIM), BF16),
        grid=(r // tr, nc),
        in_specs=[pl.BlockSpec((tr, LANES), lambda i, c: (i, c)),
                  pl.BlockSpec((tr, LANES), lambda i, c: (i, nc + c)),
                  pl.BlockSpec((tr, LANES), lambda i, c: (i, 2 * nc + c)),
                  pl.BlockSpec((3, LANES), lambda i, c: (0, c)),
                  pl.BlockSpec((1, LANES), lambda i, c: (0, c))],
        out_specs=pl.BlockSpec((tr, LANES), lambda i, c: (i, c)),
        compiler_params=_params(("parallel", "parallel"), 48),
        name="conv_gate",
    )(proj, proj, proj, conv_w, conv_b.reshape(1, CONV_DIM))


def _head_norm_kernel(*refs, heads, rope, scale, want_f32):
    x_ref, g_ref = refs[0], refs[1]
    k = 2
    if rope:
        cos_ref, sin_ref = refs[2], refs[3]
        k = 4
    ob_ref = refs[k]
    g = g_ref[...]
    for h in range(heads):
        sl = slice(h * HEAD_DIM, (h + 1) * HEAD_DIM)
        x = x_ref[:, sl]
        y = x * lax.rsqrt(jnp.mean(x * x, axis=-1, keepdims=True) + NORM_EPS) * g
        if want_f32:
            refs[k + 1][:, sl] = y
        if rope:
            y = y * cos_ref[...] + pltpu.roll(y, HEAD_DIM // 2, 1) * sin_ref[...]
        ob_ref[:, sl] = (y * scale).astype(BF16)


def head_norm(proj, col_block, width, gain, scale=1.0, rope_tables=None, want_f32=False):
    r = proj.shape[0]
    tm = 512
    heads = width // HEAD_DIM
    rope = rope_tables is not None
    in_specs = [pl.BlockSpec((tm, width), lambda i: (i, col_block)),
                pl.BlockSpec((1, HEAD_DIM), lambda i: (0, 0))]
    args = [proj, gain.reshape(1, HEAD_DIM)]
    if rope:
        n_pos = rope_tables[0].shape[0]
        per = n_pos // tm
        in_specs += [pl.BlockSpec((tm, HEAD_DIM), lambda i: (i % per, 0))] * 2
        args += list(rope_tables)
    out_shape = [jax.ShapeDtypeStruct((r, width), BF16)]
    out_specs = [pl.BlockSpec((tm, width), lambda i: (i, 0))]
    if want_f32:
        out_shape.append(jax.ShapeDtypeStruct((r, width), F32))
        out_specs.append(pl.BlockSpec((tm, width), lambda i: (i, 0)))
    out = pl.pallas_call(
        functools.partial(_head_norm_kernel, heads=heads, rope=rope, scale=scale, want_f32=want_f32),
        out_shape=out_shape,
        grid=(r // tm,),
        in_specs=in_specs,
        out_specs=out_specs,
        compiler_params=_params(("parallel",), 32),
        name="head_norm",
    )(*args)
    return out if want_f32 else out[0]


def _dense_attn_kernel(*refs, group, has_sink):
    if has_sink:
        sink_ref, q_ref, k_ref, v_ref, o_ref = refs
    else:
        q_ref, k_ref, v_ref, o_ref = refs
    qb = q_ref.shape[1]
    for hkv in range(k_ref.shape[2] // HEAD_DIM):
        head = lambda g: slice((hkv * group + g) * HEAD_DIM, (hkv * group + g + 1) * HEAD_DIM)
        kv = slice(hkv * HEAD_DIM, (hkv + 1) * HEAD_DIM)
        q = jnp.concatenate([q_ref[0, :, head(g)] for g in range(group)], axis=0)
        s = lax.dot_general(q, k_ref[0, :, kv], (((1,), (1,)), ((), ())), preferred_element_type=F32)
        m = jnp.max(s, axis=-1, keepdims=True)
        if has_sink:
            sk = jnp.concatenate([jnp.full((qb, 1), sink_ref[hkv * group + g], F32) for g in range(group)], axis=0)
            m = jnp.maximum(m, sk)
        p = jnp.exp(s - m)
        den = jnp.sum(p, axis=-1, keepdims=True)
        if has_sink:
            den = den + jnp.exp(sk - m)
        o = jnp.dot(p.astype(BF16), v_ref[0, :, kv], preferred_element_type=F32) / den
        for g in range(group):
            o_ref[0, :, head(g)] = o[g * qb:(g + 1) * qb].astype(o_ref.dtype)


def dense_attention(q, k, v, group, q_block, sink=None):
    b, nq, qw = q.shape
    nk = k.shape[1]
    kvw = k.shape[2]
    in_specs = [pl.BlockSpec((1, q_block, qw), lambda bi, qi: (bi, qi, 0)),
                pl.BlockSpec((1, nk, kvw), lambda bi, qi: (bi, 0, 0)),
                pl.BlockSpec((1, nk, kvw), lambda bi, qi: (bi, 0, 0))]
    args = [q, k, v]
    if sink is not None:
        in_specs = [pl.BlockSpec(memory_space=pltpu.SMEM)] + in_specs
        args = [sink] + args
    return pl.pallas_call(
        functools.partial(_dense_attn_kernel, group=group, has_sink=sink is not None),
        out_shape=jax.ShapeDtypeStruct((b, nq, qw), BF16),
        grid=(b, nq // q_block),
        in_specs=in_specs,
        out_specs=pl.BlockSpec((1, q_block, qw), lambda bi, qi: (bi, qi, 0)),
        compiler_params=_params(("parallel", "parallel"), 56),
        name="dense_attention",
    )(*args)


def _window_attn_kernel(sink_ref, q_ref, kp_ref, kc_ref, kn_ref, vp_ref, vc_ref, vn_ref, ck_ref, cv_ref, o_ref,
                        *, group, n_tokens):
    blk = pl.program_id(1)
    wb = WIN_BLOCK
    rows, cols = group * wb, 3 * wb + ck_ref.shape[1]
    q_pos = blk * wb + lax.broadcasted_iota(jnp.int32, (rows, cols), 0) % wb
    col = lax.broadcasted_iota(jnp.int32, (rows, cols), 1)
    k_pos = (blk - 1) * wb + col
    local_ok = (jnp.abs(k_pos - q_pos) <= WINDOW) & (k_pos >= 0) & (k_pos < n_tokens)
    visible = (col >= 3 * wb) | local_ok
    for hkv in range(kc_ref.shape[2] // HEAD_DIM):
        head = lambda g: slice((hkv * group + g) * HEAD_DIM, (hkv * group + g + 1) * HEAD_DIM)
        kv = slice(hkv * HEAD_DIM, (hkv + 1) * HEAD_DIM)
        q = jnp.concatenate([q_ref[0, :, head(g)] for g in range(group)], axis=0)
        keys = jnp.concatenate([kp_ref[0, :, kv], kc_ref[0, :, kv], kn_ref[0, :, kv], ck_ref[0, :, kv]], axis=0)
        vals = jnp.concatenate([vp_ref[0, :, kv], vc_ref[0, :, kv], vn_ref[0, :, kv], cv_ref[0, :, kv]], axis=0)
        s = lax.dot_general(q, keys, (((1,), (1,)), ((), ())), preferred_element_type=F32)
        s = jnp.where(visible, s, NEG_INF)
        sk = jnp.concatenate([jnp.full((wb, 1), sink_ref[hkv * group + g], F32) for g in range(group)], axis=0)
        m = jnp.maximum(jnp.max(s, axis=-1, keepdims=True), sk)
        p = jnp.exp(s - m)
        den = jnp.sum(p, axis=-1, keepdims=True) + jnp.exp(sk - m)
        o = jnp.dot(p.astype(BF16), vals, preferred_element_type=F32) / den
        for g in range(group):
            o_ref[0, :, head(g)] = o[g * wb:(g + 1) * wb].astype(o_ref.dtype)


def window_attention(q, k, v, ctx_k, ctx_v, sink, group):
    b, n, qw = q.shape
    kvw = k.shape[2]
    nb = n // WIN_BLOCK
    nctx = ctx_k.shape[1]
    prev_map = lambda bi, i: (bi, jnp.maximum(i - 1, 0), 0)
    cur_map = lambda bi, i: (bi, i, 0)
    next_map = lambda bi, i: (bi, jnp.minimum(i + 1, nb - 1), 0)
    ctx_map = lambda bi, i: (bi, 0, 0)
    kv_block = (1, WIN_BLOCK, kvw)
    return pl.pallas_call(
        functools.partial(_window_attn_kernel, group=group, n_tokens=n),
        out_shape=jax.ShapeDtypeStruct((b, n, qw), BF16),
        grid=(b, nb),
        in_specs=[pl.BlockSpec(memory_space=pltpu.SMEM),
                  pl.BlockSpec((1, WIN_BLOCK, qw), cur_map),
                  pl.BlockSpec(kv_block, prev_map), pl.BlockSpec(kv_block, cur_map), pl.BlockSpec(kv_block, next_map),
                  pl.BlockSpec(kv_block, prev_map), pl.BlockSpec(kv_block, cur_map), pl.BlockSpec(kv_block, next_map),
                  pl.BlockSpec((1, nctx, kvw), ctx_map), pl.BlockSpec((1, nctx, kvw), ctx_map)],
        out_specs=pl.BlockSpec((1, WIN_BLOCK, qw), cur_map),
        compiler_params=_params(("parallel", "parallel"), 32),
        name="window_attention",
    )(sink, q, k, k, k, v, v, v, ctx_k, ctx_v)


NAT_QROWS = 8
NAT_KROWS = 16


def _nat_bias_table(rpb):
    w = GRID_W
    c = np.arange(w)[:, None]
    kc = np.arange(w)[None, :]
    ws = np.clip(c - NAT_COLS // 2, 0, w - NAT_COLS)
    col_ok = (kc >= ws) & (kc < ws + NAT_COLS)
    col_idx = np.clip(kc - c + NAT_COLS - 1, 0, 2 * NAT_COLS - 2)
    t = jnp.where(col_ok[None, None], rpb[:, :, col_idx], NEG_INF)
    dead = jnp.full_like(t[:, :1], NEG_INF)
    t = jnp.concatenate([dead, t, dead], axis=1)
    return jnp.concatenate([t[:, :-1], t[:, 1:]], axis=-1)


def _nat_attn_kernel(q_ref, k_ref, v_ref, ck_ref, cv_ref, t_ref, o_ref, *, grid_rows):
    m = pl.program_id(2)
    w = GRID_W
    r0 = m * NAT_QROWS
    kr0 = jnp.clip(r0 - NAT_ROWS // 2, 0, grid_rows - NAT_KROWS)
    tok0 = pl.multiple_of(kr0 * w, 4 * w)
    keys = k_ref[0, pl.ds(tok0, NAT_KROWS * w), :]
    vals = v_ref[0, pl.ds(tok0, NAT_KROWS * w), :]
    q = q_ref[0]
    s_loc = lax.dot_general(q, keys, (((1,), (1,)), ((), ())), preferred_element_type=F32)
    lane_hi = lax.broadcasted_iota(jnp.int32, (w, 2 * w), 1) >= w
    bias_rows = []
    for i in range(NAT_QROWS):
        r = r0 + i
        rs = jnp.clip(r - NAT_ROWS // 2, 0, grid_rows - NAT_ROWS)
        pieces = []
        for jj in range(NAT_KROWS // 2):
            kr = kr0 + 2 * jj
            d = kr - r + NAT_ROWS - 1
            ok_lo = ((kr >= rs) & (kr < rs + NAT_ROWS)).astype(jnp.int32)
            ok_hi = ((kr + 1 >= rs) & (kr + 1 < rs + NAT_ROWS)).astype(jnp.int32)
            piece = t_ref[0, jnp.clip(d, -1, 2 * NAT_ROWS - 2) + 1]
            ok = jnp.where(lane_hi, ok_hi, ok_lo) > 0
            pieces.append(jnp.where(ok, piece, NEG_INF))
        bias_rows.append(jnp.concatenate(pieces, axis=1))
    s_loc = s_loc + jnp.concatenate(bias_rows, axis=0)
    s_ctx = lax.dot_general(q, ck_ref[0], (((1,), (1,)), ((), ())), preferred_element_type=F32)
    mx = jnp.maximum(jnp.max(s_loc, axis=-1, keepdims=True), jnp.max(s_ctx, axis=-1, keepdims=True))
    p_loc = jnp.exp(s_loc - mx)
    p_ctx = jnp.exp(s_ctx - mx)
    den = jnp.sum(p_loc, axis=-1, keepdims=True) + jnp.sum(p_ctx, axis=-1, keepdims=True)
    o = (jnp.dot(p_loc.astype(BF16), vals, preferred_element_type=F32)
         + jnp.dot(p_ctx.astype(BF16), cv_ref[0], preferred_element_type=F32))
    o_ref[0] = (o / den).astype(o_ref.dtype)


def neighbourhood_attention(q, k, v, ctx_k, ctx_v, bias_table):
    b, n, hw = q.shape
    heads = hw // HEAD_DIM
    grid_rows = n // GRID_W
    nctx = ctx_k.shape[1]
    tq = NAT_QROWS * GRID_W
    return pl.pallas_call(
        functools.partial(_nat_attn_kernel, grid_rows=grid_rows),
        out_shape=jax.ShapeDtypeStruct((b, n, hw), BF16),
        grid=(b, heads, grid_rows // NAT_QROWS),
        in_specs=[pl.BlockSpec((1, tq, HEAD_DIM), lambda bi, h, m: (bi, m, h)),
                  pl.BlockSpec((1, n, HEAD_DIM), lambda bi, h, m: (bi, 0, h)),
                  pl.BlockSpec((1, n, HEAD_DIM), lambda bi, h, m: (bi, 0, h)),
                  pl.BlockSpec((1, nctx, HEAD_DIM), lambda bi, h, m: (bi, 0, h)),
                  pl.BlockSpec((1, nctx, HEAD_DIM), lambda bi, h, m: (bi, 0, h)),
                  pl.BlockSpec((1, 2 * NAT_ROWS, GRID_W, 2 * GRID_W), lambda bi, h, m: (h, 0, 0, 0))],
        out_specs=pl.BlockSpec((1, tq, HEAD_DIM), lambda bi, h, m: (bi, m, h)),
        compiler_params=_params(("parallel", "parallel", "arbitrary"), 32),
        name="neighbourhood_attention",
    )(q, k, v, ctx_k, ctx_v, bias_table)


def _out_proj_kernel(a_ref, b_ref, w_ref, x_ref, gate_ref, o_ref):
    ka = a_ref.shape[1]
    acc = jnp.dot(a_ref[...], w_ref[:ka, :], preferred_element_type=F32)
    acc = acc + jnp.dot(b_ref[...], w_ref[ka:, :], preferred_element_type=F32)
    o_ref[...] = x_ref[...] + gate_ref[0] * acc


def out_proj_residual(a, b_, w_bf16, x, gate):
    r, d = x.shape
    ka, kb = a.shape[1], b_.shape[1]
    nseg = gate.shape[0]
    tm, tn = 1024, 1024
    tiles_per_seg = (r // nseg) // tm
    return pl.pallas_call(
        _out_proj_kernel,
        out_shape=jax.ShapeDtypeStruct((r, d), F32),
        grid=(r // tm, d // tn),
        in_specs=[pl.BlockSpec((tm, ka), lambda i, j: (i, 0)),
                  pl.BlockSpec((tm, kb), lambda i, j: (i, 0)),
                  pl.BlockSpec((ka + kb, tn), lambda i, j: (0, j)),
                  pl.BlockSpec((tm, tn), lambda i, j: (i, j)),
                  pl.BlockSpec((1, 1, tn), lambda i, j: (i // tiles_per_seg, 0, j))],
        out_specs=pl.BlockSpec((tm, tn), lambda i, j: (i, j)),
        compiler_params=_params(("parallel", "parallel"), 56),
        name="out_proj_residual",
    )(a, b_, w_bf16, x, gate)


def _norm_router_kernel(x_ref, g_ref, shift_ref, scale_ref, whi_ref, wlo_ref, h_ref, logit_ref):
    h = _mod_norm(x_ref[...], g_ref[...], shift_ref[0], scale_ref[0])
    h_hi = h.astype(BF16)
    h_lo = (h - h_hi.astype(F32)).astype(BF16)
    h_ref[...] = h_hi
    w_hi = whi_ref[...]
    logit_ref[...] = (jnp.dot(h_hi, w_hi, preferred_element_type=F32)
                      + jnp.dot(h_lo, w_hi, preferred_element_type=F32)
                      + jnp.dot(h_hi, wlo_ref[...], preferred_element_type=F32))


def norm_router(x, g, shift, scale, router_w):
    r, d = x.shape
    nseg = shift.shape[0]
    tm = 512
    tiles_per_seg = (r // nseg) // tm
    w_pad = jnp.zeros((d, LANES), F32).at[:, :N_EXPERTS].set(router_w)
    w_hi = w_pad.astype(BF16)
    w_lo = (w_pad - w_hi.astype(F32)).astype(BF16)
    return pl.pallas_call(
        _norm_router_kernel,
        out_shape=[jax.ShapeDtypeStruct((r, d), BF16), jax.ShapeDtypeStruct((r, LANES), F32)],
        grid=(r // tm,),
        in_specs=[pl.BlockSpec((tm, d), lambda i: (i, 0)),
                  pl.BlockSpec((1, d), lambda i: (0, 0)),
                  pl.BlockSpec((1, 1, d), lambda i: (i // tiles_per_seg, 0, 0)),
                  pl.BlockSpec((1, 1, d), lambda i: (i // tiles_per_seg, 0, 0)),
                  pl.BlockSpec((d, LANES), lambda i: (0, 0)),
                  pl.BlockSpec((d, LANES), lambda i: (0, 0))],
        out_specs=[pl.BlockSpec((tm, d), lambda i: (i, 0)), pl.BlockSpec((tm, LANES), lambda i: (i, 0))],
        compiler_params=_params(("parallel",), 32),
        name="norm_router",
    )(x, g.reshape(1, d), shift, scale, w_hi, w_lo)


CUMSUM_BLOCK = 256


def _exclusive_cumsum_lanes(flags):
    e, n = flags.shape
    cb = min(CUMSUM_BLOCK, n)
    tri = (lax.broadcasted_iota(jnp.int32, (cb, cb), 0) < lax.broadcasted_iota(jnp.int32, (cb, cb), 1)).astype(BF16)
    carry = jnp.zeros((e, 1), F32)
    out = []
    for j in range(n // cb):
        blk = flags[:, j * cb:(j + 1) * cb]
        out.append(jnp.dot(blk.astype(BF16), tri, preferred_element_type=F32) + carry)
        carry = carry + jnp.sum(blk, axis=-1, keepdims=True)
    return jnp.concatenate(out, axis=1) if len(out) > 1 else out[0]


def _select_kernel(logit_ref, pos_ref, gate_ref, *, cap):
    lg = logit_ref[0]
    e, n = lg.shape
    ex = jnp.exp(lg - jnp.max(lg, axis=0, keepdims=True))
    aff = ex / jnp.sum(ex, axis=0, keepdims=True)

    def enough(t):
        return jnp.sum((aff >= t).astype(F32), axis=-1, keepdims=True) >= cap

    hi = jnp.full((e, 1), 2.0, F32)
    for step in (64, 32, 16, 8, 4, 2, 1):
        cand = hi * (2.0 ** -step)
        hi = jnp.where(enough(cand), hi, cand)
    lo = hi * 0.5
    lo = jnp.where(enough(lo), lo, 0.0)
    for _ in range(23):
        mid = (lo + hi) * 0.5
        ok = enough(mid)
        lo = jnp.where(ok, mid, lo)
        hi = jnp.where(ok, hi, mid)
    thr = lo
    above = (aff > thr).astype(F32)
    tied = (aff == thr).astype(F32)
    need = cap - jnp.sum(above, axis=-1, keepdims=True)
    take = above + tied * (_exclusive_cumsum_lanes(tied) < need).astype(F32)
    slot = _exclusive_cumsum_lanes(take)
    pos = jnp.where(take > 0, slot, -1.0).astype(jnp.int32)
    pos_ref[0] = pos
    s_iota = lax.broadcasted_iota(jnp.int32, (cap, n), 0)
    for ei in range(e):
        hit = pos[ei:ei + 1, :] == s_iota
        gate_ref[0, ei] = jnp.sum(jnp.where(hit, aff[ei:ei + 1, :], 0.0), axis=-1, keepdims=True)


def select_tokens(logits_t, cap):
    nseg, e, n = logits_t.shape
    return pl.pallas_call(
        functools.partial(_select_kernel, cap=cap),
        out_shape=[jax.ShapeDtypeStruct((nseg, e, n), jnp.int32),
                   jax.ShapeDtypeStruct((nseg, e, cap, 1), F32)],
        grid=(nseg,),
        in_specs=[pl.BlockSpec((1, e, n), lambda s: (s, 0, 0))],
        out_specs=[pl.BlockSpec((1, e, n), lambda s: (s, 0, 0)),
                   pl.BlockSpec((1, e, cap, 1), lambda s: (s, 0, 0, 0))],
        compiler_params=_params(("parallel",), 48),
        name="select_tokens",
    )(logits_t)


def _gather_kernel(pos_ref, h_ref, o_ref, acc_ref, *, cap):
    kt = pl.program_id(2)
    eb = pos_ref.shape[1]
    tk = pos_ref.shape[3]
    s_iota = lax.broadcasted_iota(jnp.int32, (cap, tk), 0)
    onehot = jnp.concatenate([(pos_ref[0, i] == s_iota).astype(BF16) for i in range(eb)], axis=0)
    part = jnp.dot(onehot, h_ref[0], preferred_element_type=F32)

    @pl.when(kt == 0)
    def _():
        acc_ref[...] = part

    @pl.when(kt > 0)
    def _():
        acc_ref[...] += part

    @pl.when(kt == pl.num_programs(2) - 1)
    def _():
        o_ref[0] = acc_ref[...].reshape(o_ref.shape[1:]).astype(o_ref.dtype)


def gather_tokens(pos, h, cap, experts_per_step, tk):
    nseg, e, n = pos.shape
    d = h.shape[2]
    eb = experts_per_step
    return pl.pallas_call(
        functools.partial(_gather_kernel, cap=cap),
        out_shape=jax.ShapeDtypeStruct((nseg, e, cap, d), BF16),
        grid=(nseg, e // eb, n // tk),
        in_specs=[pl.BlockSpec((1, eb, 1, tk), lambda s, ei, kt: (s, ei, 0, kt)),
                  pl.BlockSpec((1, tk, d), lambda s, ei, kt: (s, kt, 0))],
        out_specs=pl.BlockSpec((1, eb, cap, d), lambda s, ei, kt: (s, ei, 0, 0)),
        scratch_shapes=[pltpu.VMEM((eb * cap, d), F32)],
        compiler_params=_params(("parallel", "parallel", "arbitrary"), 48),
        name="gather_tokens",
    )(pos.reshape(nseg, e, 1, n), h)


def _expert_up_kernel(xc_ref, xl_ref, wg_ref, wu_ref, o_ref):
    d = xc_ref.shape[-1]
    wg = wg_ref[0, 0].astype(BF16)
    wu = wu_ref[0, 0].astype(BF16)
    row = 0
    for x_ref in (xc_ref, xl_ref):
        rows = x_ref.shape[0] * x_ref.shape[2]
        x = x_ref[...].reshape(rows, d)
        a = jnp.dot(x, wg, preferred_element_type=F32)
        b = jnp.dot(x, wu, preferred_element_type=F32)
        o_ref[0, row:row + rows, :] = (_silu(a) * b).astype(o_ref.dtype)
        row += rows


def expert_up(xg_ctx, xg_lat, w_gate, w_up, layer):
    _, e, d, f = w_gate.shape
    tf = 256
    sc, _, cc, _ = xg_ctx.shape
    sl, _, cl, _ = xg_lat.shape
    rows = sc * cc + sl * cl
    return pl.pallas_call(
        _expert_up_kernel,
        out_shape=jax.ShapeDtypeStruct((e, rows, f), BF16),
        grid=(e, f // tf),
        in_specs=[pl.BlockSpec((sc, 1, cc, d), lambda ei, j: (0, ei, 0, 0)),
                  pl.BlockSpec((sl, 1, cl, d), lambda ei, j: (0, ei, 0, 0)),
                  pl.BlockSpec((1, 1, d, tf), lambda ei, j: (layer, ei, 0, j)),
                  pl.BlockSpec((1, 1, d, tf), lambda ei, j: (layer, ei, 0, j))],
        out_specs=pl.BlockSpec((1, rows, tf), lambda ei, j: (ei, 0, j)),
        compiler_params=_params(("parallel", "arbitrary"), 56),
        name="expert_up",
    )(xg_ctx, xg_lat, w_gate, w_up)


def _expert_down_kernel(h_ref, w_ref, gate_ref, o_ref):
    y = jnp.dot(h_ref[0], w_ref[0, 0].astype(BF16), preferred_element_type=F32)
    o_ref[0] = (y * gate_ref[0]).astype(o_ref.dtype)


def expert_down(hid, w_down, gates, layer):
    e, rows, f = hid.shape
    d = w_down.shape[3]
    td = 256
    return pl.pallas_call(
        _expert_down_kernel,
        out_shape=jax.ShapeDtypeStruct((e, rows, d), BF16),
        grid=(e, d // td),
        in_specs=[pl.BlockSpec((1, rows, f), lambda ei, j: (ei, 0, 0)),
                  pl.BlockSpec((1, 1, f, td), lambda ei, j: (layer, ei, 0, j)),
                  pl.BlockSpec((1, rows, 1), lambda ei, j: (ei, 0, 0))],
        out_specs=pl.BlockSpec((1, rows, td), lambda ei, j: (ei, 0, j)),
        compiler_params=_params(("parallel", "arbitrary"), 48),
        name="expert_down",
    )(hid, w_down, gates)


def _combine_kernel(post_ref, y_ref, x_ref, gate_ref, o_ref, hit_ref, *, cap):
    ne = y_ref.shape[0]
    tt = post_ref.shape[1]

    @pl.when(pl.program_id(2) == 0)
    def _():
        post = post_ref[0]
        width = max(cap, LANES)
        per = width // cap
        lane = lax.broadcasted_iota(jnp.int32, (tt, width), 1)
        for blk in range(ne // per):
            hit = None
            for i in range(per):
                col = post[:, blk * per + i:blk * per + i + 1]
                h = jnp.where(col >= 0, col + i * cap, -1) == lane
                hit = h if hit is None else (hit | h)
            hit_ref[:, blk * width:(blk + 1) * width] = jnp.where(hit, 1.0, 0.0).astype(BF16)

    y = y_ref[...].reshape(ne * cap, y_ref.shape[-1])
    o_ref[...] = x_ref[...] + gate_ref[0] * jnp.dot(hit_ref[...], y, preferred_element_type=F32)


def combine_tokens(pos_t, y, x, gate, cap, row_offset, tt, td):
    nseg, n, e = pos_t.shape
    r, d = x.shape
    nt = n // tt
    rb = row_offset // cap
    return pl.pallas_call(
        functools.partial(_combine_kernel, cap=cap),
        out_shape=jax.ShapeDtypeStruct((r, d), F32),
        grid=(nseg, nt, d // td),
        in_specs=[pl.BlockSpec((1, tt, e), lambda s, t, j: (s, t, 0)),
                  pl.BlockSpec((e, cap, td), lambda s, t, j: (0, rb + s, j)),
                  pl.BlockSpec((tt, td), lambda s, t, j: (s * nt + t, j)),
                  pl.BlockSpec((1, 1, td), lambda s, t, j: (s * gate.shape[0] // nseg, 0, j))],
        out_specs=pl.BlockSpec((tt, td), lambda s, t, j: (s * nt + t, j)),
        scratch_shapes=[pltpu.VMEM((tt, e * cap), BF16)],
        compiler_params=_params(("parallel", "parallel", "arbitrary"), 56),
        name="combine_tokens",
    )(pos_t, y, x, gate)


def moe_layer(groups, g_ffn, router_w, w_gate, w_up, w_down, layer):
    sel = []
    for grp in groups:
        nseg, n = grp["nseg"], grp["n"]
        cap = (EC_CAPACITY * n) // N_EXPERTS
        h, logits = norm_router(grp["x"], g_ffn, grp["shift"], grp["scale"], router_w)
        logits_t = jnp.swapaxes(logits[:, :N_EXPERTS].reshape(nseg, n, N_EXPERTS), 1, 2)
        pos, gates = select_tokens(logits_t, cap)
        eb = N_EXPERTS if n <= 512 else 1
        xg = gather_tokens(pos, h.reshape(nseg, n, D_MODEL), cap, eb, min(n, 2048))
        sel.append(dict(cap=cap, pos=pos, xg=xg,
                        gates=jnp.swapaxes(gates, 0, 1).reshape(N_EXPERTS, nseg * cap, 1)))
    hid = expert_up(sel[0]["xg"], sel[1]["xg"], w_gate, w_up, layer)
    y = expert_down(hid, w_down, jnp.concatenate([s["gates"] for s in sel], axis=1), layer)
    out = []
    row_offset = 0
    for grp, s in zip(groups, sel):
        n, cap = grp["n"], s["cap"]
        small = n <= 512
        out.append(combine_tokens(jnp.swapaxes(s["pos"], 1, 2), y, grp["x"], grp["gate"], cap, row_offset,
                                  n if small else 1024, D_MODEL if small else 512))
        row_offset += grp["nseg"] * cap
    return out


def _rope_tables(n):
    t = jnp.arange(n)
    row = (t // GRID_W).astype(F32)
    col = (t % GRID_W).astype(F32)
    nf = HEAD_DIM // 4
    inv = ROPE_THETA ** (-jnp.arange(nf, dtype=F32) / nf)
    ang = jnp.concatenate([row[:, None] * inv, col[:, None] * inv], axis=-1)
    cos, sin = jnp.cos(ang), jnp.sin(ang)
    return jnp.concatenate([cos, cos], axis=-1), jnp.concatenate([-sin, sin], axis=-1)


def kernel(x_prompt, x_sample, cache_win_k, cache_win_v, cache_nat_k, cache_nat_v, cache_glob_k, cache_glob_v, c, c_ctx, mod_w, mod_b, norm_mix_w, norm_ffn_w, even_w_in, even_w_out, conv_w, conv_b, win_sink, win_q_norm, win_k_norm, odd_w_in, odd_w_out, nat_rpb, nat_q_norm, nat_k_norm, glob_q_norm, glob_k_norm, router_w, expert_w_gate, expert_w_up, expert_w_down):
    bp, sp, d = x_prompt.shape
    bs, ss, _ = x_sample.shape
    depth = mod_w.shape[0]
    rope = _rope_tables(ss)

    cond8 = jnp.zeros((8, d), F32).at[0].set(c_ctx).at[1:1 + bs].set(c)
    mods = modulation(cond8, mod_w, mod_b).reshape(depth, 8, N_MOD, d)

    xp = x_prompt.reshape(bp * sp, d)
    xs = x_sample.reshape(bs * ss, d)
    win_k, win_v, nat_k, nat_v, glob_k, glob_v = [], [], [], [], [], []

    def flat_cache(cache, i):
        return cache[:, i].reshape(bs, cache.shape[2], -1).astype(BF16)

    for layer in range(depth):
        i = layer // 2
        mp = [mods[layer, 0:1, k].reshape(1, 1, d) for k in range(N_MOD)]
        ms = [mods[layer, 1:1 + bs, k].reshape(bs, 1, d) for k in range(N_MOD)]
        if layer % 2 == 0:
            w_in = even_w_in[i].astype(BF16)
            w_out = even_w_out[i].astype(BF16)
            q0 = 3 * CONV_DIM
            qw = WIN_Q_HEADS * HEAD_DIM
            kw = WIN_KV_HEADS * HEAD_DIM
            group = WIN_Q_HEADS // WIN_KV_HEADS
            pp = norm_proj(xp, norm_mix_w[layer], mp[0], mp[1], w_in, 4)
            ya = conv_gate(pp, conv_w[i], conv_b[i], sp)
            q = head_norm(pp, q0 // qw, qw, win_q_norm[i], scale=ATTN_SCALE)
            k, k32 = head_norm(pp, (q0 + qw) // kw, kw, win_k_norm[i], want_f32=True)
            v32 = pp[:, q0 + qw + kw:]
            yb = dense_attention(q.reshape(bp, sp, qw), k.reshape(bp, sp, kw), v32.astype(BF16).reshape(bp, sp, kw),
                                 group, sp, sink=win_sink[i])
            xp = out_proj_residual(ya, yb.reshape(bp * sp, qw), w_out, xp, mp[2])
            win_k.append(k32.reshape(bp, sp, WIN_KV_HEADS, HEAD_DIM))
            win_v.append(v32.reshape(bp, sp, WIN_KV_HEADS, HEAD_DIM))
            ps = norm_proj(xs, norm_mix_w[layer], ms[0], ms[1], w_in, 4)
            ya = conv_gate(ps, conv_w[i], conv_b[i], ss)
            q = head_norm(ps, q0 // qw, qw, win_q_norm[i], scale=ATTN_SCALE, rope_tables=rope)
            k = head_norm(ps, (q0 + qw) // kw, kw, win_k_norm[i], rope_tables=rope)
            v = ps[:, q0 + qw + kw:].astype(BF16)
            yb = window_attention(q.reshape(bs, ss, qw), k.reshape(bs, ss, kw), v.reshape(bs, ss, kw),
                                  flat_cache(cache_win_k, i), flat_cache(cache_win_v, i), win_sink[i], group)
            xs = out_proj_residual(ya, yb.reshape(bs * ss, qw), w_out, xs, ms[2])
        else:
            w_in = odd_w_in[i].astype(BF16)
            w_out = odd_w_out[i].astype(BF16)
            nw = NAT_HEADS * HEAD_DIM
            gqw = GLOB_Q_HEADS * HEAD_DIM
            gkw = GLOB_KV_HEADS * HEAD_DIM
            group = GLOB_Q_HEADS // GLOB_KV_HEADS
            kd0 = 3 * nw + gqw
            pp = norm_proj(xp, norm_mix_w[layer], mp[0], mp[1], w_in, 4)
            qc = head_norm(pp, 0, nw, nat_q_norm[i], scale=ATTN_SCALE)
            kc, kc32 = head_norm(pp, 1, nw, nat_k_norm[i], want_f32=True)
            vc32 = pp[:, 2 * nw:3 * nw]
            qd = head_norm(pp, 3, gqw, glob_q_norm[i], scale=ATTN_SCALE)
            kd, kd32 = head_norm(pp, kd0 // gkw, gkw, glob_k_norm[i], want_f32=True)
            vd32 = pp[:, kd0 + gkw:]
            yc = dense_attention(qc.reshape(bp, sp, nw), kc.reshape(bp, sp, nw),
                                 vc32.astype(BF16).reshape(bp, sp, nw), 1, sp)
            yd = dense_attention(qd.reshape(bp, sp, gqw), kd.reshape(bp, sp, gkw),
                                 vd32.astype(BF16).reshape(bp, sp, gkw), group, sp)
            xp = out_proj_residual(yc.reshape(bp * sp, nw), yd.reshape(bp * sp, gqw), w_out, xp, mp[2])
            nat_k.append(kc32.reshape(bp, sp, NAT_HEADS, HEAD_DIM))
            nat_v.append(vc32.reshape(bp, sp, NAT_HEADS, HEAD_DIM))
            glob_k.append(kd32.reshape(bp, sp, GLOB_KV_HEADS, HEAD_DIM))
            glob_v.append(vd32.reshape(bp, sp, GLOB_KV_HEADS, HEAD_DIM))
            ps = norm_proj(xs, norm_mix_w[layer], ms[0], ms[1], w_in, 4)
            qc = head_norm(ps, 0, nw, nat_q_norm[i], scale=ATTN_SCALE)
            kc = head_norm(ps, 1, nw, nat_k_norm[i])
            vc = ps[:, 2 * nw:3 * nw].astype(BF16)
            qd = head_norm(ps, 3, gqw, glob_q_norm[i], scale=ATTN_SCALE, rope_tables=rope)
            kd = head_norm(ps, kd0 // gkw, gkw, glob_k_norm[i], rope_tables=rope)
            vd = ps[:, kd0 + gkw:].astype(BF16)
            yc = neighbourhood_attention(qc.reshape(bs, ss, nw), kc.reshape(bs, ss, nw), vc.reshape(bs, ss, nw),
                                         flat_cache(cache_nat_k, i), flat_cache(cache_nat_v, i),
                                         _nat_bias_table(nat_rpb[i]))
            k_all = jnp.concatenate([kd.reshape(bs, ss, gkw), flat_cache(cache_glob_k, i)], axis=1)
            v_all = jnp.concatenate([vd.reshape(bs, ss, gkw), flat_cache(cache_glob_v, i)], axis=1)
            yd = dense_attention(qd.reshape(bs, ss, gqw), k_all, v_all, group, 128)
            xs = out_proj_residual(yc.reshape(bs * ss, nw), yd.reshape(bs * ss, gqw), w_out, xs, ms[2])

        xp, xs = moe_layer(
            [dict(x=xp, shift=mp[3], scale=mp[4], gate=mp[5], nseg=bp, n=sp),
             dict(x=xs, shift=ms[3], scale=ms[4], gate=ms[5], nseg=bs, n=ss)],
            norm_ffn_w[layer], router_w[layer], expert_w_gate, expert_w_up, expert_w_down, layer)

    stack = lambda xs_: jnp.stack(xs_, axis=1)
    return (xp.reshape(bp, sp, d), xs.reshape(bs, ss, d), stack(win_k), stack(win_v), stack(nat_k),
            stack(nat_v), stack(glob_k), stack(glob_v))
```

```python
import functools

import numpy as np
import jax
import jax.numpy as jnp
from jax import lax
from jax.experimental import pallas as pl
from jax.experimental.pallas import tpu as pltpu

D_MODEL = 2048
HEAD_DIM = 128
GRID_W = 64
CONV_DIM = 512
WIN_Q_HEADS = 12
WIN_KV_HEADS = 4
WIN_BLOCK = 128
WINDOW = 128
NAT_HEADS = 8
NAT_ROWS = 8
NAT_COLS = 16
GLOB_Q_HEADS = 8
GLOB_KV_HEADS = 2
ROPE_THETA = 10000.0
N_EXPERTS = 16
EXPERT_FF = 2048
EC_CAPACITY = 2
N_MOD = 6
NORM_EPS = 1e-6
NEG_INF = -1e30
ATTN_SCALE = HEAD_DIM ** -0.5

LANES = 128
MIB = 1024 * 1024
BF16 = jnp.bfloat16
F32 = jnp.float32


def _params(semantics, vmem_mib):
    return pltpu.CompilerParams(dimension_semantics=semantics, vmem_limit_bytes=vmem_mib * MIB)


def _silu(x):
    return x * (1.0 / (1.0 + jnp.exp(-x)))


def _modulation_kernel(cond_ref, w_ref, b_ref, o_ref):
    s = _silu(cond_ref[...])
    o_ref[0] = jnp.dot(s, w_ref[0], preferred_element_type=F32,
                       precision=lax.Precision.HIGHEST) + b_ref[0]


def modulation(cond8, mod_w, mod_b):
    depth, d, n = mod_w.shape
    tn = 512
    return pl.pallas_call(
        _modulation_kernel,
        out_shape=jax.ShapeDtypeStruct((depth, 8, n), F32),
        grid=(depth, n // tn),
        in_specs=[pl.BlockSpec((8, d), lambda l, j: (0, 0)),
                  pl.BlockSpec((1, d, tn), lambda l, j: (l, 0, j)),
                  pl.BlockSpec((1, 1, tn), lambda l, j: (l, 0, j))],
        out_specs=pl.BlockSpec((1, 8, tn), lambda l, j: (l, 0, j)),
        compiler_params=_params(("parallel", "parallel"), 48),
        name="modulation",
    )(cond8, mod_w, mod_b.reshape(depth, 1, n))


def _mod_norm(x, g, shift, scale):
    ms = jnp.mean(x * x, axis=-1, keepdims=True)
    y = x * lax.rsqrt(ms + NORM_EPS) * g
    return y * (1.0 + scale) + shift


def _mod_norm_kernel(x_ref, g_ref, shift_ref, scale_ref, h_ref):
    h_ref[...] = _mod_norm(x_ref[...], g_ref[...], shift_ref[0], scale_ref[0]).astype(BF16)


def mod_norm(x, g, shift, scale):
    r, d = x.shape
    nseg = shift.shape[0]
    tm = 512
    tiles_per_seg = (r // nseg) // tm
    return pl.pallas_call(
        _mod_norm_kernel,
        out_shape=jax.ShapeDtypeStruct((r, d), BF16),
        grid=(r // tm,),
        in_specs=[pl.BlockSpec((tm, d), lambda i: (i, 0)),
                  pl.BlockSpec((1, d), lambda i: (0, 0)),
                  pl.BlockSpec((1, 1, d), lambda i: (i // tiles_per_seg, 0, 0)),
                  pl.BlockSpec((1, 1, d), lambda i: (i // tiles_per_seg, 0, 0))],
        out_specs=pl.BlockSpec((tm, d), lambda i: (i, 0)),
        compiler_params=_params(("parallel",), 32),
        name="mod_norm",
    )(x, g.reshape(1, d), shift, scale)


def _proj_kernel(*refs, head_norm, rope, scale, want_bf16, want_f32):
    h_ref, w_ref = refs[0], refs[1]
    k = 2
    if head_norm:
        g_ref = refs[k]
        k += 1
    if rope:
        cos_ref, sin_ref = refs[k], refs[k + 1]
        k += 2
    outs = refs[k:]
    acc = jnp.dot(h_ref[...], w_ref[...], preferred_element_type=F32)
    if not head_norm:
        if want_bf16:
            outs[0][...] = acc.astype(BF16)
        if want_f32:
            outs[-1][...] = acc
        return
    for hd in range(acc.shape[1] // HEAD_DIM):
        sl = slice(hd * HEAD_DIM, (hd + 1) * HEAD_DIM)
        x = acc[:, sl]
        y = x * lax.rsqrt(jnp.mean(x * x, axis=-1, keepdims=True) + NORM_EPS) * g_ref[...]
        if want_f32:
            outs[-1][:, sl] = y
        if rope:
            y = y * cos_ref[...] + pltpu.roll(y, HEAD_DIM // 2, 1) * sin_ref[...]
        outs[0][:, sl] = (y * scale).astype(BF16)


def project(h, w_bf16, col0, width, gain=None, scale=1.0, rope_tables=None, want_bf16=True, want_f32=False):
    r, d = h.shape
    tm = 1024
    tn = min(width, 512)
    cb0 = col0 // tn
    rope = rope_tables is not None
    in_specs = [pl.BlockSpec((tm, d), lambda j, i: (i, 0)),
                pl.BlockSpec((d, tn), lambda j, i: (0, cb0 + j))]
    args = [h, w_bf16]
    if gain is not None:
        in_specs.append(pl.BlockSpec((1, HEAD_DIM), lambda j, i: (0, 0)))
        args.append(gain.reshape(1, HEAD_DIM))
    if rope:
        per = rope_tables[0].shape[0] // tm
        in_specs += [pl.BlockSpec((tm, HEAD_DIM), lambda j, i: (i % per, 0))] * 2
        args += list(rope_tables)
    out_shape, out_specs = [], []
    for want, dt in ((want_bf16, BF16), (want_f32, F32)):
        if want:
            out_shape.append(jax.ShapeDtypeStruct((r, width), dt))
            out_specs.append(pl.BlockSpec((tm, tn), lambda j, i: (i, j)))
    out = pl.pallas_call(
        functools.partial(_proj_kernel, head_norm=gain is not None, rope=rope, scale=scale,
                          want_bf16=want_bf16, want_f32=want_f32),
        out_shape=out_shape,
        grid=(width // tn, r // tm),
        in_specs=in_specs,
        out_specs=out_specs,
        compiler_params=_params(("parallel", "parallel"), 48),
        name="project",
    )(*args)
    return out if len(out) > 1 else out[0]


def _conv_gate_kernel(ab_ref, ac_ref, ah_ref, w_ref, b_ref, o_ref, *, seq):
    u = ac_ref[...] * ah_ref[...]
    rows = u.shape[0]
    pos = lax.broadcasted_iota(jnp.int32, u.shape, 0) % seq
    prev = jnp.where(pos == 0, 0.0, pltpu.roll(u, 1, 0))
    nxt = jnp.where(pos == seq - 1, 0.0, pltpu.roll(u, rows - 1, 0))
    conv = prev * w_ref[0:1, :] + u * w_ref[1:2, :] + nxt * w_ref[2:3, :] + b_ref[...]
    o_ref[...] = (ab_ref[...] * conv).astype(o_ref.dtype)


def conv_gate(proj, conv_w, conv_b, seq):
    r = proj.shape[0]
    tr = 4096
    nc = CONV_DIM // LANES
    return pl.pallas_call(
        functools.partial(_conv_gate_kernel, seq=seq),
        out_shape=jax.ShapeDtypeStruct((r, CONV_DIM), BF16),
        grid=(r // tr, nc),
        in_specs=[pl.BlockSpec((tr, LANES), lambda i, c: (i, c)),
                  pl.BlockSpec((tr, LANES), lambda i, c: (i, nc + c)),
                  pl.BlockSpec((tr, LANES), lambda i, c: (i, 2 * nc + c)),
                  pl.BlockSpec((3, LANES), lambda i, c: (0, c)),
                  pl.BlockSpec((1, LANES), lambda i, c: (0, c))],
        out_specs=pl.BlockSpec((tr, LANES), lambda i, c: (i, c)),
        compiler_params=_params(("parallel", "parallel"), 48),
        name="conv_gate",
    )(proj, proj, proj, conv_w, conv_b.reshape(1, CONV_DIM))


def _dense_attn_kernel(*refs, group, has_sink):
    if has_sink:
        sink_ref, q_ref, k_ref, v_ref, o_ref = refs
    else:
        q_ref, k_ref, v_ref, o_ref = refs
    qb = q_ref.shape[1]
    for hkv in range(k_ref.shape[2] // HEAD_DIM):
        head = lambda g: slice((hkv * group + g) * HEAD_DIM, (hkv * group + g + 1) * HEAD_DIM)
        kv = slice(hkv * HEAD_DIM, (hkv + 1) * HEAD_DIM)
        q = jnp.concatenate([q_ref[0, :, head(g)] for g in range(group)], axis=0)
        s = lax.dot_general(q, k_ref[0, :, kv], (((1,), (1,)), ((), ())), preferred_element_type=F32)
        m = jnp.max(s, axis=-1, keepdims=True)
        if has_sink:
            sk = jnp.concatenate([jnp.full((qb, 1), sink_ref[hkv * group + g], F32) for g in range(group)], axis=0)
            m = jnp.maximum(m, sk)
        p = jnp.exp(s - m)
        den = jnp.sum(p, axis=-1, keepdims=True)
        if has_sink:
            den = den + jnp.exp(sk - m)
        o = jnp.dot(p.astype(BF16), v_ref[0, :, kv], preferred_element_type=F32) / den
        for g in range(group):
            o_ref[0, :, head(g)] = o[g * qb:(g + 1) * qb].astype(o_ref.dtype)


def dense_attention(q, k, v, group, q_block, sink=None):
    b, nq, qw = q.shape
    nk = k.shape[1]
    kvw = k.shape[2]
    in_specs = [pl.BlockSpec((1, q_block, qw), lambda bi, qi: (bi, qi, 0)),
                pl.BlockSpec((1, nk, kvw), lambda bi, qi: (bi, 0, 0)),
                pl.BlockSpec((1, nk, kvw), lambda bi, qi: (bi, 0, 0))]
    args = [q, k, v]
    if sink is not None:
        in_specs = [pl.BlockSpec(memory_space=pltpu.SMEM)] + in_specs
        args = [sink] + args
    return pl.pallas_call(
        functools.partial(_dense_attn_kernel, group=group, has_sink=sink is not None),
        out_shape=jax.ShapeDtypeStruct((b, nq, qw), BF16),
        grid=(b, nq // q_block),
        in_specs=in_specs,
        out_specs=pl.BlockSpec((1, q_block, qw), lambda bi, qi: (bi, qi, 0)),
        compiler_params=_params(("parallel", "parallel"), 56),
        name="dense_attention",
    )(*args)


def _window_attn_kernel(sink_ref, q_ref, kp_ref, kc_ref, kn_ref, vp_ref, vc_ref, vn_ref, ck_ref, cv_ref, o_ref,
                        *, group, n_tokens):
    blk = pl.program_id(1)
    wb = WIN_BLOCK
    rows, cols = group * wb, 3 * wb + ck_ref.shape[1]
    q_pos = blk * wb + lax.broadcasted_iota(jnp.int32, (rows, cols), 0) % wb
    col = lax.broadcasted_iota(jnp.int32, (rows, cols), 1)
    k_pos = (blk - 1) * wb + col
    local_ok = (jnp.abs(k_pos - q_pos) <= WINDOW) & (k_pos >= 0) & (k_pos < n_tokens)
    visible = (col >= 3 * wb) | local_ok
    for hkv in range(kc_ref.shape[2] // HEAD_DIM):
        head = lambda g: slice((hkv * group + g) * HEAD_DIM, (hkv * group + g + 1) * HEAD_DIM)
        kv = slice(hkv * HEAD_DIM, (hkv + 1) * HEAD_DIM)
        q = jnp.concatenate([q_ref[0, :, head(g)] for g in range(group)], axis=0)
        keys = jnp.concatenate([kp_ref[0, :, kv], kc_ref[0, :, kv], kn_ref[0, :, kv], ck_ref[0, :, kv]], axis=0)
        vals = jnp.concatenate([vp_ref[0, :, kv], vc_ref[0, :, kv], vn_ref[0, :, kv], cv_ref[0, :, kv]], axis=0)
        s = lax.dot_general(q, keys, (((1,), (1,)), ((), ())), preferred_element_type=F32)
        s = jnp.where(visible, s, NEG_INF)
        sk = jnp.concatenate([jnp.full((wb, 1), sink_ref[hkv * group + g], F32) for g in range(group)], axis=0)
        m = jnp.maximum(jnp.max(s, axis=-1, keepdims=True), sk)
        p = jnp.exp(s - m)
        den = jnp.sum(p, axis=-1, keepdims=True) + jnp.exp(sk - m)
        o = jnp.dot(p.astype(BF16), vals, preferred_element_type=F32) / den
        for g in range(group):
            o_ref[0, :, head(g)] = o[g * wb:(g + 1) * wb].astype(o_ref.dtype)


def window_attention(q, k, v, ctx_k, ctx_v, sink, group):
    b, n, qw = q.shape
    kvw = k.shape[2]
    nb = n // WIN_BLOCK
    nctx = ctx_k.shape[1]
    prev_map = lambda bi, i: (bi, jnp.maximum(i - 1, 0), 0)
    cur_map = lambda bi, i: (bi, i, 0)
    next_map = lambda bi, i: (bi, jnp.minimum(i + 1, nb - 1), 0)
    ctx_map = lambda bi, i: (bi, 0, 0)
    kv_block = (1, WIN_BLOCK, kvw)
    return pl.pallas_call(
        functools.partial(_window_attn_kernel, group=group, n_tokens=n),
        out_shape=jax.ShapeDtypeStruct((b, n, qw), BF16),
        grid=(b, nb),
        in_specs=[pl.BlockSpec(memory_space=pltpu.SMEM),
                  pl.BlockSpec((1, WIN_BLOCK, qw), cur_map),
                  pl.BlockSpec(kv_block, prev_map), pl.BlockSpec(kv_block, cur_map), pl.BlockSpec(kv_block, next_map),
                  pl.BlockSpec(kv_block, prev_map), pl.BlockSpec(kv_block, cur_map), pl.BlockSpec(kv_block, next_map),
                  pl.BlockSpec((1, nctx, kvw), ctx_map), pl.BlockSpec((1, nctx, kvw), ctx_map)],
        out_specs=pl.BlockSpec((1, WIN_BLOCK, qw), cur_map),
        compiler_params=_params(("parallel", "parallel"), 32),
        name="window_attention",
    )(sink, q, k, k, k, v, v, v, ctx_k, ctx_v)


NAT_QROWS = 8
NAT_KROWS = 16


def _nat_bias_table(rpb):
    w = GRID_W
    c = np.arange(w)[:, None]
    kc = np.arange(w)[None, :]
    ws = np.clip(c - NAT_COLS // 2, 0, w - NAT_COLS)
    col_ok = (kc >= ws) & (kc < ws + NAT_COLS)
    col_idx = np.clip(kc - c + NAT_COLS - 1, 0, 2 * NAT_COLS - 2)
    t = jnp.where(col_ok[None, None], rpb[:, :, col_idx], NEG_INF)
    dead = jnp.full_like(t[:, :1], NEG_INF)
    t = jnp.concatenate([dead, t, dead], axis=1)
    return jnp.concatenate([t[:, :-1], t[:, 1:]], axis=-1)


def _nat_attn_kernel(q_ref, k_ref, v_ref, ck_ref, cv_ref, t_ref, o_ref, *, grid_rows):
    m = pl.program_id(2)
    w = GRID_W
    r0 = m * NAT_QROWS
    kr0 = jnp.clip(r0 - NAT_ROWS // 2, 0, grid_rows - NAT_KROWS)
    tok0 = pl.multiple_of(kr0 * w, 4 * w)
    keys = k_ref[0, pl.ds(tok0, NAT_KROWS * w), :]
    vals = v_ref[0, pl.ds(tok0, NAT_KROWS * w), :]
    q = q_ref[0]
    s_loc = lax.dot_general(q, keys, (((1,), (1,)), ((), ())), preferred_element_type=F32)
    lane_hi = lax.broadcasted_iota(jnp.int32, (w, 2 * w), 1) >= w
    bias_rows = []
    for i in range(NAT_QROWS):
        r = r0 + i
        rs = jnp.clip(r - NAT_ROWS // 2, 0, grid_rows - NAT_ROWS)
        pieces = []
        for jj in range(NAT_KROWS // 2):
            kr = kr0 + 2 * jj
            d = kr - r + NAT_ROWS - 1
            ok_lo = ((kr >= rs) & (kr < rs + NAT_ROWS)).astype(jnp.int32)
            ok_hi = ((kr + 1 >= rs) & (kr + 1 < rs + NAT_ROWS)).astype(jnp.int32)
            piece = t_ref[0, jnp.clip(d, -1, 2 * NAT_ROWS - 2) + 1]
            ok = jnp.where(lane_hi, ok_hi, ok_lo) > 0
            pieces.append(jnp.where(ok, piece, NEG_INF))
        bias_rows.append(jnp.concatenate(pieces, axis=1))
    s_loc = s_loc + jnp.concatenate(bias_rows, axis=0)
    s_ctx = lax.dot_general(q, ck_ref[0], (((1,), (1,)), ((), ())), preferred_element_type=F32)
    mx = jnp.maximum(jnp.max(s_loc, axis=-1, keepdims=True), jnp.max(s_ctx, axis=-1, keepdims=True))
    p_loc = jnp.exp(s_loc - mx)
    p_ctx = jnp.exp(s_ctx - mx)
    den = jnp.sum(p_loc, axis=-1, keepdims=True) + jnp.sum(p_ctx, axis=-1, keepdims=True)
    o = (jnp.dot(p_loc.astype(BF16), vals, preferred_element_type=F32)
         + jnp.dot(p_ctx.astype(BF16), cv_ref[0], preferred_element_type=F32))
    o_ref[0] = (o / den).astype(o_ref.dtype)


def neighbourhood_attention(q, k, v, ctx_k, ctx_v, bias_table):
    b, n, hw = q.shape
    heads = hw // HEAD_DIM
    grid_rows = n // GRID_W
    nctx = ctx_k.shape[1]
    tq = NAT_QROWS * GRID_W
    return pl.pallas_call(
        functools.partial(_nat_attn_kernel, grid_rows=grid_rows),
        out_shape=jax.ShapeDtypeStruct((b, n, hw), BF16),
        grid=(b, heads, grid_rows // NAT_QROWS),
        in_specs=[pl.BlockSpec((1, tq, HEAD_DIM), lambda bi, h, m: (bi, m, h)),
                  pl.BlockSpec((1, n, HEAD_DIM), lambda bi, h, m: (bi, 0, h)),
                  pl.BlockSpec((1, n, HEAD_DIM), lambda bi, h, m: (bi, 0, h)),
                  pl.BlockSpec((1, nctx, HEAD_DIM), lambda bi, h, m: (bi, 0, h)),
                  pl.BlockSpec((1, nctx, HEAD_DIM), lambda bi, h, m: (bi, 0, h)),
                  pl.BlockSpec((1, 2 * NAT_ROWS, GRID_W, 2 * GRID_W), lambda bi, h, m: (h, 0, 0, 0))],
        out_specs=pl.BlockSpec((1, tq, HEAD_DIM), lambda bi, h, m: (bi, m, h)),
        compiler_params=_params(("parallel", "parallel", "arbitrary"), 32),
        name="neighbourhood_attention",
    )(q, k, v, ctx_k, ctx_v, bias_table)


def _out_proj_kernel(a_ref, b_ref, w_ref, x_ref, gate_ref, o_ref):
    ka = a_ref.shape[1]
    acc = jnp.dot(a_ref[...], w_ref[:ka, :], preferred_element_type=F32)
    acc = acc + jnp.dot(b_ref[...], w_ref[ka:, :], preferred_element_type=F32)
    o_ref[...] = x_ref[...] + gate_ref[0] * acc


def out_proj_residual(a, b_, w_bf16, x, gate):
    r, d = x.shape
    ka, kb = a.shape[1], b_.shape[1]
    nseg = gate.shape[0]
    tm, tn = 1024, 1024
    tiles_per_seg = (r // nseg) // tm
    return pl.pallas_call(
        _out_proj_kernel,
        out_shape=jax.ShapeDtypeStruct((r, d), F32),
        grid=(r // tm, d // tn),
        in_specs=[pl.BlockSpec((tm, ka), lambda i, j: (i, 0)),
                  pl.BlockSpec((tm, kb), lambda i, j: (i, 0)),
                  pl.BlockSpec((ka + kb, tn), lambda i, j: (0, j)),
                  pl.BlockSpec((tm, tn), lambda i, j: (i, j)),
                  pl.BlockSpec((1, 1, tn), lambda i, j: (i // tiles_per_seg, 0, j))],
        out_specs=pl.BlockSpec((tm, tn), lambda i, j: (i, j)),
        compiler_params=_params(("parallel", "parallel"), 56),
        name="out_proj_residual",
    )(a, b_, w_bf16, x, gate)


def _norm_router_kernel(x_ref, g_ref, shift_ref, scale_ref, whi_ref, wlo_ref, h_ref, logit_ref):
    h = _mod_norm(x_ref[...], g_ref[...], shift_ref[0], scale_ref[0])
    h_hi = h.astype(BF16)
    h_lo = (h - h_hi.astype(F32)).astype(BF16)
    h_ref[...] = h_hi
    w_hi = whi_ref[...]
    logit_ref[...] = (jnp.dot(h_hi, w_hi, preferred_element_type=F32)
                      + jnp.dot(h_lo, w_hi, preferred_element_type=F32)
                      + jnp.dot(h_hi, wlo_ref[...], preferred_element_type=F32))


def norm_router(x, g, shift, scale, router_w):
    r, d = x.shape
    nseg = shift.shape[0]
    tm = 512
    tiles_per_seg = (r // nseg) // tm
    w_pad = jnp.zeros((d, LANES), F32).at[:, :N_EXPERTS].set(router_w)
    w_hi = w_pad.astype(BF16)
    w_lo = (w_pad - w_hi.astype(F32)).astype(BF16)
    return pl.pallas_call(
        _norm_router_kernel,
        out_shape=[jax.ShapeDtypeStruct((r, d), BF16), jax.ShapeDtypeStruct((r, LANES), F32)],
        grid=(r // tm,),
        in_specs=[pl.BlockSpec((tm, d), lambda i: (i, 0)),
                  pl.BlockSpec((1, d), lambda i: (0, 0)),
                  pl.BlockSpec((1, 1, d), lambda i: (i // tiles_per_seg, 0, 0)),
                  pl.BlockSpec((1, 1, d), lambda i: (i // tiles_per_seg, 0, 0)),
                  pl.BlockSpec((d, LANES), lambda i: (0, 0)),
                  pl.BlockSpec((d, LANES), lambda i: (0, 0))],
        out_specs=[pl.BlockSpec((tm, d), lambda i: (i, 0)), pl.BlockSpec((tm, LANES), lambda i: (i, 0))],
        compiler_params=_params(("parallel",), 32),
        name="norm_router",
    )(x, g.reshape(1, d), shift, scale, w_hi, w_lo)


CUMSUM_BLOCK = 256


def _exclusive_cumsum_lanes(flags):
    e, n = flags.shape
    cb = min(CUMSUM_BLOCK, n)
    tri = (lax.broadcasted_iota(jnp.int32, (cb, cb), 0) < lax.broadcasted_iota(jnp.int32, (cb, cb), 1)).astype(BF16)
    carry = jnp.zeros((e, 1), F32)
    out = []
    for j in range(n // cb):
        blk = flags[:, j * cb:(j + 1) * cb]
        out.append(jnp.dot(blk.astype(BF16), tri, preferred_element_type=F32) + carry)
        carry = carry + jnp.sum(blk, axis=-1, keepdims=True)
    return jnp.concatenate(out, axis=1) if len(out) > 1 else out[0]


def _select_kernel(logit_ref, pos_ref, gate_ref, *, cap):
    lg = logit_ref[0]
    e, n = lg.shape
    ex = jnp.exp(lg - jnp.max(lg, axis=0, keepdims=True))
    aff = ex / jnp.sum(ex, axis=0, keepdims=True)

    def enough(t):
        return jnp.sum((aff >= t).astype(F32), axis=-1, keepdims=True) >= cap

    hi = jnp.full((e, 1), 2.0, F32)
    for step in (64, 32, 16, 8, 4, 2, 1):
        cand = hi * (2.0 ** -step)
        hi = jnp.where(enough(cand), hi, cand)
    lo = hi * 0.5
    lo = jnp.where(enough(lo), lo, 0.0)
    for _ in range(23):
        mid = (lo + hi) * 0.5
        ok = enough(mid)
        lo = jnp.where(ok, mid, lo)
        hi = jnp.where(ok, hi, mid)
    thr = lo
    above = (aff > thr).astype(F32)
    tied = (aff == thr).astype(F32)
    need = cap - jnp.sum(above, axis=-1, keepdims=True)
    take = above + tied * (_exclusive_cumsum_lanes(tied) < need).astype(F32)
    slot = _exclusive_cumsum_lanes(take)
    pos = jnp.where(take > 0, slot, -1.0).astype(jnp.int32)
    pos_ref[0] = pos
    s_iota = lax.broadcasted_iota(jnp.int32, (cap, n), 0)
    for ei in range(e):
        hit = pos[ei:ei + 1, :] == s_iota
        gate_ref[0, ei] = jnp.sum(jnp.where(hit, aff[ei:ei + 1, :], 0.0), axis=-1, keepdims=True)


def select_tokens(logits_t, cap):
    nseg, e, n = logits_t.shape
    return pl.pallas_call(
        functools.partial(_select_kernel, cap=cap),
        out_shape=[jax.ShapeDtypeStruct((nseg, e, n), jnp.int32),
                   jax.ShapeDtypeStruct((nseg, e, cap, 1), F32)],
        grid=(nseg,),
        in_specs=[pl.BlockSpec((1, e, n), lambda s: (s, 0, 0))],
        out_specs=[pl.BlockSpec((1, e, n), lambda s: (s, 0, 0)),
                   pl.BlockSpec((1, e, cap, 1), lambda s: (s, 0, 0, 0))],
        compiler_params=_params(("parallel",), 48),
        name="select_tokens",
    )(logits_t)


def _gather_kernel(pos_ref, h_ref, o_ref, acc_ref, *, cap):
    kt = pl.program_id(2)
    eb = pos_ref.shape[1]
    tk = pos_ref.shape[3]
    s_iota = lax.broadcasted_iota(jnp.int32, (cap, tk), 0)
    onehot = jnp.concatenate([(pos_ref[0, i] == s_iota).astype(BF16) for i in range(eb)], axis=0)
    part = jnp.dot(onehot, h_ref[0], preferred_element_type=F32)

    @pl.when(kt == 0)
    def _():
        acc_ref[...] = part

    @pl.when(kt > 0)
    def _():
        acc_ref[...] += part

    @pl.when(kt == pl.num_programs(2) - 1)
    def _():
        o_ref[0] = acc_ref[...].reshape(o_ref.shape[1:]).astype(o_ref.dtype)


def gather_tokens(pos, h, cap, experts_per_step, tk):
    nseg, e, n = pos.shape
    d = h.shape[2]
    eb = experts_per_step
    return pl.pallas_call(
        functools.partial(_gather_kernel, cap=cap),
        out_shape=jax.ShapeDtypeStruct((nseg, e, cap, d), BF16),
        grid=(nseg, e // eb, n // tk),
        in_specs=[pl.BlockSpec((1, eb, 1, tk), lambda s, ei, kt: (s, ei, 0, kt)),
                  pl.BlockSpec((1, tk, d), lambda s, ei, kt: (s, kt, 0))],
        out_specs=pl.BlockSpec((1, eb, cap, d), lambda s, ei, kt: (s, ei, 0, 0)),
        scratch_shapes=[pltpu.VMEM((eb * cap, d), F32)],
        compiler_params=_params(("parallel", "parallel", "arbitrary"), 48),
        name="gather_tokens",
    )(pos.reshape(nseg, e, 1, n), h)


def _expert_up_kernel(xc_ref, xl_ref, wg_ref, wu_ref, o_ref):
    d = xc_ref.shape[-1]
    wg = wg_ref[0, 0].astype(BF16)
    wu = wu_ref[0, 0].astype(BF16)
    row = 0
    for x_ref in (xc_ref, xl_ref):
        rows = x_ref.shape[0] * x_ref.shape[2]
        x = x_ref[...].reshape(rows, d)
        a = jnp.dot(x, wg, preferred_element_type=F32)
        b = jnp.dot(x, wu, preferred_element_type=F32)
        o_ref[0, row:row + rows, :] = (_silu(a) * b).astype(o_ref.dtype)
        row += rows


def expert_up(xg_ctx, xg_lat, w_gate, w_up, layer):
    _, e, d, f = w_gate.shape
    tf = 256
    sc, _, cc, _ = xg_ctx.shape
    sl, _, cl, _ = xg_lat.shape
    rows = sc * cc + sl * cl
    return pl.pallas_call(
        _expert_up_kernel,
        out_shape=jax.ShapeDtypeStruct((e, rows, f), BF16),
        grid=(e, f // tf),
        in_specs=[pl.BlockSpec((sc, 1, cc, d), lambda ei, j: (0, ei, 0, 0)),
                  pl.BlockSpec((sl, 1, cl, d), lambda ei, j: (0, ei, 0, 0)),
                  pl.BlockSpec((1, 1, d, tf), lambda ei, j: (layer, ei, 0, j)),
                  pl.BlockSpec((1, 1, d, tf), lambda ei, j: (layer, ei, 0, j))],
        out_specs=pl.BlockSpec((1, rows, tf), lambda ei, j: (ei, 0, j)),
        compiler_params=_params(("parallel", "arbitrary"), 56),
        name="expert_up",
    )(xg_ctx, xg_lat, w_gate, w_up)


def _expert_down_kernel(h_ref, w_ref, gate_ref, o_ref):
    y = jnp.dot(h_ref[0], w_ref[0, 0].astype(BF16), preferred_element_type=F32)
    o_ref[0] = (y * gate_ref[0]).astype(o_ref.dtype)


def expert_down(hid, w_down, gates, layer):
    e, rows, f = hid.shape
    d = w_down.shape[3]
    td = 256
    return pl.pallas_call(
        _expert_down_kernel,
        out_shape=jax.ShapeDtypeStruct((e, rows, d), BF16),
        grid=(e, d // td),
        in_specs=[pl.BlockSpec((1, rows, f), lambda ei, j: (ei, 0, 0)),
                  pl.BlockSpec((1, 1, f, td), lambda ei, j: (layer, ei, 0, j)),
                  pl.BlockSpec((1, rows, 1), lambda ei, j: (ei, 0, 0))],
        out_specs=pl.BlockSpec((1, rows, td), lambda ei, j: (ei, 0, j)),
        compiler_params=_params(("parallel", "arbitrary"), 48),
        name="expert_down",
    )(hid, w_down, gates)


def _combine_kernel(post_ref, y_ref, x_ref, gate_ref, o_ref, hit_ref, *, cap):
    ne = y_ref.shape[0]
    tt = post_ref.shape[1]

    @pl.when(pl.program_id(2) == 0)
    def _():
        post = post_ref[0]
        width = max(cap, LANES)
        per = width // cap
        lane = lax.broadcasted_iota(jnp.int32, (tt, width), 1)
        for blk in range(ne // per):
            hit = None
            for i in range(per):
                col = post[:, blk * per + i:blk * per + i + 1]
                h = jnp.where(col >= 0, col + i * cap, -1) == lane
                hit = h if hit is None else (hit | h)
            hit_ref[:, blk * width:(blk + 1) * width] = jnp.where(hit, 1.0, 0.0).astype(BF16)

    y = y_ref[...].reshape(ne * cap, y_ref.shape[-1])
    o_ref[...] = x_ref[...] + gate_ref[0] * jnp.dot(hit_ref[...], y, preferred_element_type=F32)


def combine_tokens(pos_t, y, x, gate, cap, row_offset, tt, td):
    nseg, n, e = pos_t.shape
    r, d = x.shape
    nt = n // tt
    rb = row_offset // cap
    return pl.pallas_call(
        functools.partial(_combine_kernel, cap=cap),
        out_shape=jax.ShapeDtypeStruct((r, d), F32),
        grid=(nseg, nt, d // td),
        in_specs=[pl.BlockSpec((1, tt, e), lambda s, t, j: (s, t, 0)),
                  pl.BlockSpec((e, cap, td), lambda s, t, j: (0, rb + s, j)),
                  pl.BlockSpec((tt, td), lambda s, t, j: (s * nt + t, j)),
                  pl.BlockSpec((1, 1, td), lambda s, t, j: (s * gate.shape[0] // nseg, 0, j))],
        out_specs=pl.BlockSpec((tt, td), lambda s, t, j: (s * nt + t, j)),
        scratch_shapes=[pltpu.VMEM((tt, e * cap), BF16)],
        compiler_params=_params(("parallel", "parallel", "arbitrary"), 56),
        name="combine_tokens",
    )(pos_t, y, x, gate)


def moe_layer(groups, g_ffn, router_w, w_gate, w_up, w_down, layer):
    sel = []
    for grp in groups:
        nseg, n = grp["nseg"], grp["n"]
        cap = (EC_CAPACITY * n) // N_EXPERTS
        h, logits = norm_router(grp["x"], g_ffn, grp["shift"], grp["scale"], router_w)
        logits_t = jnp.swapaxes(logits[:, :N_EXPERTS].reshape(nseg, n, N_EXPERTS), 1, 2)
        pos, gates = select_tokens(logits_t, cap)
        eb = N_EXPERTS if n <= 512 else 1
        xg = gather_tokens(pos, h.reshape(nseg, n, D_MODEL), cap, eb, min(n, 2048))
        sel.append(dict(cap=cap, pos=pos, xg=xg,
                        gates=jnp.swapaxes(gates, 0, 1).reshape(N_EXPERTS, nseg * cap, 1)))
    hid = expert_up(sel[0]["xg"], sel[1]["xg"], w_gate, w_up, layer)
    y = expert_down(hid, w_down, jnp.concatenate([s["gates"] for s in sel], axis=1), layer)
    out = []
    row_offset = 0
    for grp, s in zip(groups, sel):
        n, cap = grp["n"], s["cap"]
        small = n <= 512
        out.append(combine_tokens(jnp.swapaxes(s["pos"], 1, 2), y, grp["x"], grp["gate"], cap, row_offset,
                                  n if small else 1024, D_MODEL if small else 512))
        row_offset += grp["nseg"] * cap
    return out


def _rope_tables(n):
    t = jnp.arange(n)
    row = (t // GRID_W).astype(F32)
    col = (t % GRID_W).astype(F32)
    nf = HEAD_DIM // 4
    inv = ROPE_THETA ** (-jnp.arange(nf, dtype=F32) / nf)
    ang = jnp.concatenate([row[:, None] * inv, col[:, None] * inv], axis=-1)
    cos, sin = jnp.cos(ang), jnp.sin(ang)
    return jnp.concatenate([cos, cos], axis=-1), jnp.concatenate([-sin, sin], axis=-1)


def kernel(x_prompt, x_sample, cache_win_k, cache_win_v, cache_nat_k, cache_nat_v, cache_glob_k, cache_glob_v, c, c_ctx, mod_w, mod_b, norm_mix_w, norm_ffn_w, even_w_in, even_w_out, conv_w, conv_b, win_sink, win_q_norm, win_k_norm, odd_w_in, odd_w_out, nat_rpb, nat_q_norm, nat_k_norm, glob_q_norm, glob_k_norm, router_w, expert_w_gate, expert_w_up, expert_w_down):
    bp, sp, d = x_prompt.shape
    bs, ss, _ = x_sample.shape
    depth = mod_w.shape[0]
    rope = _rope_tables(ss)

    cond8 = jnp.zeros((8, d), F32).at[0].set(c_ctx).at[1:1 + bs].set(c)
    mods = modulation(cond8, mod_w, mod_b).reshape(depth, 8, N_MOD, d)

    xp = x_prompt.reshape(bp * sp, d)
    xs = x_sample.reshape(bs * ss, d)
    win_k, win_v, nat_k, nat_v, glob_k, glob_v = [], [], [], [], [], []

    def flat_cache(cache, i):
        return cache[:, i].reshape(bs, cache.shape[2], -1).astype(BF16)

    for layer in range(depth):
        i = layer // 2
        mp = [mods[layer, 0:1, k].reshape(1, 1, d) for k in range(N_MOD)]
        ms = [mods[layer, 1:1 + bs, k].reshape(bs, 1, d) for k in range(N_MOD)]
        if layer % 2 == 0:
            w_in = even_w_in[i].astype(BF16)
            w_out = even_w_out[i].astype(BF16)
            q0 = 3 * CONV_DIM
            qw = WIN_Q_HEADS * HEAD_DIM
            kw = WIN_KV_HEADS * HEAD_DIM
            group = WIN_Q_HEADS // WIN_KV_HEADS
            h = mod_norm(xp, norm_mix_w[layer], mp[0], mp[1])
            ya = conv_gate(project(h, w_in, 0, q0, want_bf16=False, want_f32=True), conv_w[i], conv_b[i], sp)
            q = project(h, w_in, q0, qw, gain=win_q_norm[i], scale=ATTN_SCALE)
            k, k32 = project(h, w_in, q0 + qw, kw, gain=win_k_norm[i], want_f32=True)
            v, v32 = project(h, w_in, q0 + qw + kw, kw, want_f32=True)
            yb = dense_attention(q.reshape(bp, sp, qw), k.reshape(bp, sp, kw), v.reshape(bp, sp, kw),
                                 group, sp, sink=win_sink[i])
            xp = out_proj_residual(ya, yb.reshape(bp * sp, qw), w_out, xp, mp[2])
            win_k.append(k32.reshape(bp, sp, WIN_KV_HEADS, HEAD_DIM))
            win_v.append(v32.reshape(bp, sp, WIN_KV_HEADS, HEAD_DIM))
            h = mod_norm(xs, norm_mix_w[layer], ms[0], ms[1])
            ya = conv_gate(project(h, w_in, 0, q0, want_bf16=False, want_f32=True), conv_w[i], conv_b[i], ss)
            q = project(h, w_in, q0, qw, gain=win_q_norm[i], scale=ATTN_SCALE, rope_tables=rope)
            k = project(h, w_in, q0 + qw, kw, gain=win_k_norm[i], rope_tables=rope)
            v = project(h, w_in, q0 + qw + kw, kw)
            yb = window_attention(q.reshape(bs, ss, qw), k.reshape(bs, ss, kw), v.reshape(bs, ss, kw),
                                  flat_cache(cache_win_k, i), flat_cache(cache_win_v, i), win_sink[i], group)
            xs = out_proj_residual(ya, yb.reshape(bs * ss, qw), w_out, xs, ms[2])
        else:
            w_in = odd_w_in[i].astype(BF16)
            w_out = odd_w_out[i].astype(BF16)
            nw = NAT_HEADS * HEAD_DIM
            gqw = GLOB_Q_HEADS * HEAD_DIM
            gkw = GLOB_KV_HEADS * HEAD_DIM
            group = GLOB_Q_HEADS // GLOB_KV_HEADS
            kd0 = 3 * nw + gqw
            h = mod_norm(xp, norm_mix_w[layer], mp[0], mp[1])
            qc = project(h, w_in, 0, nw, gain=nat_q_norm[i], scale=ATTN_SCALE)
            kc, kc32 = project(h, w_in, nw, nw, gain=nat_k_norm[i], want_f32=True)
            vc, vc32 = project(h, w_in, 2 * nw, nw, want_f32=True)
            qd = project(h, w_in, 3 * nw, gqw, gain=glob_q_norm[i], scale=ATTN_SCALE)
            kd, kd32 = project(h, w_in, kd0, gkw, gain=glob_k_norm[i], want_f32=True)
            vd, vd32 = project(h, w_in, kd0 + gkw, gkw, want_f32=True)
            yc = dense_attention(qc.reshape(bp, sp, nw), kc.reshape(bp, sp, nw), vc.reshape(bp, sp, nw), 1, sp)
            yd = dense_attention(qd.reshape(bp, sp, gqw), kd.reshape(bp, sp, gkw), vd.reshape(bp, sp, gkw), group, sp)
            xp = out_proj_residual(yc.reshape(bp * sp, nw), yd.reshape(bp * sp, gqw), w_out, xp, mp[2])
            nat_k.append(kc32.reshape(bp, sp, NAT_HEADS, HEAD_DIM))
            nat_v.append(vc32.reshape(bp, sp, NAT_HEADS, HEAD_DIM))
            glob_k.append(kd32.reshape(bp, sp, GLOB_KV_HEADS, HEAD_DIM))
            glob_v.append(vd32.reshape(bp, sp, GLOB_KV_HEADS, HEAD_DIM))
            h = mod_norm(xs, norm_mix_w[layer], ms[0], ms[1])
            qc = project(h, w_in, 0, nw, gain=nat_q_norm[i], scale=ATTN_SCALE)
            kc = project(h, w_in, nw, nw, gain=nat_k_norm[i])
            vc = project(h, w_in, 2 * nw, nw)
            qd = project(h, w_in, 3 * nw, gqw, gain=glob_q_norm[i], scale=ATTN_SCALE, rope_tables=rope)
            kd = project(h, w_in, kd0, gkw, gain=glob_k_norm[i], rope_tables=rope)
            vd = project(h, w_in, kd0 + gkw, gkw)
            yc = neighbourhood_attention(qc.reshape(bs, ss, nw), kc.reshape(bs, ss, nw), vc.reshape(bs, ss, nw),
                                         flat_cache(cache_nat_k, i), flat_cache(cache_nat_v, i),
                                         _nat_bias_table(nat_rpb[i]))
            k_all = jnp.concatenate([kd.reshape(bs, ss, gkw), flat_cache(cache_glob_k, i)], axis=1)
            v_all = jnp.concatenate([vd.reshape(bs, ss, gkw), flat_cache(cache_glob_v, i)], axis=1)
            yd = dense_attention(qd.reshape(bs, ss, gqw), k_all, v_all, group, 128)
            xs = out_proj_residual(yc.reshape(bs * ss, nw), yd.reshape(bs * ss, gqw), w_out, xs, ms[2])

        xp, xs = moe_layer(
            [dict(x=xp, shift=mp[3], scale=mp[4], gate=mp[5], nseg=bp, n=sp),
             dict(x=xs, shift=ms[3], scale=ms[4], gate=ms[5], nseg=bs, n=ss)],
            norm_ffn_w[layer], router_w[layer], expert_w_gate, expert_w_up, expert_w_down, layer)

    stack = lambda xs_: jnp.stack(xs_, axis=1)
    return (xp.reshape(bp, sp, d), xs.reshape(bs, ss, d), stack(win_k), stack(win_v), stack(nat_k),
            stack(nat_v), stack(glob_k), stack(glob_v))
```

```python
import functools

import numpy as np
import jax
import jax.numpy as jnp
from jax import lax
from jax.experimental import pallas as pl
from jax.experimental.pallas import tpu as pltpu

D_MODEL = 2048
HEAD_DIM = 128
GRID_W = 64
CONV_DIM = 512
WIN_Q_HEADS = 12
WIN_KV_HEADS = 4
WIN_BLOCK = 128
WINDOW = 128
NAT_HEADS = 8
NAT_ROWS = 8
NAT_COLS = 16
GLOB_Q_HEADS = 8
GLOB_KV_HEADS = 2
ROPE_THETA = 10000.0
N_EXPERTS = 16
EXPERT_FF = 2048
EC_CAPACITY = 2
N_MOD = 6
NORM_EPS = 1e-6
NEG_INF = -1e30
ATTN_SCALE = HEAD_DIM ** -0.5

LANES = 128
MIB = 1024 * 1024
BF16 = jnp.bfloat16
F32 = jnp.float32


def _params(semantics, vmem_mib):
    return pltpu.CompilerParams(dimension_semantics=semantics, vmem_limit_bytes=vmem_mib * MIB)


def _silu(x):
    return x * (1.0 / (1.0 + jnp.exp(-x)))


def _modulation_kernel(cond_ref, w_ref, b_ref, o_ref):
    s = _silu(cond_ref[...]).astype(BF16)
    o_ref[0] = jnp.dot(s, w_ref[0].astype(BF16), preferred_element_type=F32) + b_ref[0]


def modulation(cond8, mod_w, mod_b):
    depth, d, n = mod_w.shape
    tn = 1024
    return pl.pallas_call(
        _modulation_kernel,
        out_shape=jax.ShapeDtypeStruct((depth, 8, n), F32),
        grid=(depth, n // tn),
        in_specs=[pl.BlockSpec((8, d), lambda l, j: (0, 0)),
                  pl.BlockSpec((1, d, tn), lambda l, j: (l, 0, j)),
                  pl.BlockSpec((1, 1, tn), lambda l, j: (l, 0, j))],
        out_specs=pl.BlockSpec((1, 8, tn), lambda l, j: (l, 0, j)),
        compiler_params=_params(("parallel", "parallel"), 48),
        name="modulation",
    )(cond8, mod_w, mod_b.reshape(depth, 1, n))


def _mod_norm(x, g, shift, scale):
    ms = jnp.mean(x * x, axis=-1, keepdims=True)
    y = x * lax.rsqrt(ms + NORM_EPS) * g
    return y * (1.0 + scale) + shift


def _mod_norm_kernel(x_ref, g_ref, shift_ref, scale_ref, h_ref):
    h_ref[...] = _mod_norm(x_ref[...], g_ref[...], shift_ref[0], scale_ref[0]).astype(BF16)


def mod_norm(x, g, shift, scale):
    r, d = x.shape
    nseg = shift.shape[0]
    tm = 512
    tiles_per_seg = (r // nseg) // tm
    return pl.pallas_call(
        _mod_norm_kernel,
        out_shape=jax.ShapeDtypeStruct((r, d), BF16),
        grid=(r // tm,),
        in_specs=[pl.BlockSpec((tm, d), lambda i: (i, 0)),
                  pl.BlockSpec((1, d), lambda i: (0, 0)),
                  pl.BlockSpec((1, 1, d), lambda i: (i // tiles_per_seg, 0, 0)),
                  pl.BlockSpec((1, 1, d), lambda i: (i // tiles_per_seg, 0, 0))],
        out_specs=pl.BlockSpec((tm, d), lambda i: (i, 0)),
        compiler_params=_params(("parallel",), 32),
        name="mod_norm",
    )(x, g.reshape(1, d), shift, scale)


def _proj_kernel(*refs, head_norm, rope, scale, want_bf16, want_f32):
    h_ref, w_ref = refs[0], refs[1]
    k = 2
    if head_norm:
        g_ref = refs[k]
        k += 1
    if rope:
        cos_ref, sin_ref = refs[k], refs[k + 1]
        k += 2
    outs = refs[k:]
    acc = jnp.dot(h_ref[...], w_ref[...], preferred_element_type=F32)
    if not head_norm:
        if want_bf16:
            outs[0][...] = acc.astype(BF16)
        if want_f32:
            outs[-1][...] = acc
        return
    ones = jnp.ones((HEAD_DIM, HEAD_DIM), BF16)
    for hd in range(acc.shape[1] // HEAD_DIM):
        sl = slice(hd * HEAD_DIM, (hd + 1) * HEAD_DIM)
        x = acc[:, sl]
        ssq = jnp.dot((x * x).astype(BF16), ones, preferred_element_type=F32)
        y = x * lax.rsqrt(ssq * (1.0 / HEAD_DIM) + NORM_EPS) * g_ref[...]
        if want_f32:
            outs[-1][:, sl] = y
        if rope:
            y = y * cos_ref[...] + pltpu.roll(y, HEAD_DIM // 2, 1) * sin_ref[...]
        outs[0][:, sl] = (y * scale).astype(BF16)


def project(h, w_bf16, col0, width, gain=None, scale=1.0, rope_tables=None, want_bf16=True, want_f32=False):
    r, d = h.shape
    tm = 1024
    tn = min(width, 512)
    cb0 = col0 // tn
    rope = rope_tables is not None
    in_specs = [pl.BlockSpec((tm, d), lambda j, i: (i, 0)),
                pl.BlockSpec((d, tn), lambda j, i: (0, cb0 + j))]
    args = [h, w_bf16]
    if gain is not None:
        in_specs.append(pl.BlockSpec((1, HEAD_DIM), lambda j, i: (0, 0)))
        args.append(gain.reshape(1, HEAD_DIM))
    if rope:
        per = rope_tables[0].shape[0] // tm
        in_specs += [pl.BlockSpec((tm, HEAD_DIM), lambda j, i: (i % per, 0))] * 2
        args += list(rope_tables)
    out_shape, out_specs = [], []
    for want, dt in ((want_bf16, BF16), (want_f32, F32)):
        if want:
            out_shape.append(jax.ShapeDtypeStruct((r, width), dt))
            out_specs.append(pl.BlockSpec((tm, tn), lambda j, i: (i, j)))
    out = pl.pallas_call(
        functools.partial(_proj_kernel, head_norm=gain is not None, rope=rope, scale=scale,
                          want_bf16=want_bf16, want_f32=want_f32),
        out_shape=out_shape,
        grid=(width // tn, r // tm),
        in_specs=in_specs,
        out_specs=out_specs,
        compiler_params=_params(("parallel", "parallel"), 48),
        name="project",
    )(*args)
    return out if len(out) > 1 else out[0]


def _conv_gate_kernel(ab_ref, ac_ref, ah_ref, w_ref, b_ref, o_ref, *, seq):
    u = ac_ref[...] * ah_ref[...]
    rows = u.shape[0]
    pos = lax.broadcasted_iota(jnp.int32, u.shape, 0) % seq
    prev = jnp.where(pos == 0, 0.0, pltpu.roll(u, 1, 0))
    nxt = jnp.where(pos == seq - 1, 0.0, pltpu.roll(u, rows - 1, 0))
    conv = prev * w_ref[0:1, :] + u * w_ref[1:2, :] + nxt * w_ref[2:3, :] + b_ref[...]
    o_ref[...] = (ab_ref[...] * conv).astype(o_ref.dtype)


def conv_gate(proj, conv_w, conv_b, seq):
    r = proj.shape[0]
    tr = 4096
    nc = CONV_DIM // LANES
    return pl.pallas_call(
        functools.partial(_conv_gate_kernel, seq=seq),
        out_shape=jax.ShapeDtypeStruct((r, CONV_DIM), BF16),
        grid=(r // tr, nc),
        in_specs=[pl.BlockSpec((tr, LANES), lambda i, c: (i, c)),
                  pl.BlockSpec((tr, LANES), lambda i, c: (i, nc + c)),
                  pl.BlockSpec((tr, LANES), lambda i, c: (i, 2 * nc + c)),
                  pl.BlockSpec((3, LANES), lambda i, c: (0, c)),
                  pl.BlockSpec((1, LANES), lambda i, c: (0, c))],
        out_specs=pl.BlockSpec((tr, LANES), lambda i, c: (i, c)),
        compiler_params=_params(("parallel", "parallel"), 48),
        name="conv_gate",
    )(proj, proj, proj, conv_w, conv_b.reshape(1, CONV_DIM))


def _dense_attn_kernel(*refs, group, has_sink):
    if has_sink:
        sink_ref, q_ref, k_ref, v_ref, o_ref = refs
    else:
        q_ref, k_ref, v_ref, o_ref = refs
    qb = q_ref.shape[1]
    for hkv in range(k_ref.shape[2] // HEAD_DIM):
        head = lambda g: slice((hkv * group + g) * HEAD_DIM, (hkv * group + g + 1) * HEAD_DIM)
        kv = slice(hkv * HEAD_DIM, (hkv + 1) * HEAD_DIM)
        q = jnp.concatenate([q_ref[0, :, head(g)] for g in range(group)], axis=0)
        s = lax.dot_general(q, k_ref[0, :, kv], (((1,), (1,)), ((), ())), preferred_element_type=F32)
        m = jnp.max(s, axis=-1, keepdims=True)
        if has_sink:
            sk = jnp.concatenate([jnp.full((qb, 1), sink_ref[hkv * group + g], F32) for g in range(group)], axis=0)
            m = jnp.maximum(m, sk)
        p = jnp.exp(s - m)
        den = jnp.sum(p, axis=-1, keepdims=True)
        if has_sink:
            den = den + jnp.exp(sk - m)
        o = jnp.dot(p.astype(BF16), v_ref[0, :, kv], preferred_element_type=F32) / den
        for g in range(group):
            o_ref[0, :, head(g)] = o[g * qb:(g + 1) * qb].astype(o_ref.dtype)


def dense_attention(q, k, v, group, q_block, sink=None):
    b, nq, qw = q.shape
    nk = k.shape[1]
    kvw = k.shape[2]
    in_specs = [pl.BlockSpec((1, q_block, qw), lambda bi, qi: (bi, qi, 0)),
                pl.BlockSpec((1, nk, kvw), lambda bi, qi: (bi, 0, 0)),
                pl.BlockSpec((1, nk, kvw), lambda bi, qi: (bi, 0, 0))]
    args = [q, k, v]
    if sink is not None:
        in_specs = [pl.BlockSpec(memory_space=pltpu.SMEM)] + in_specs
        args = [sink] + args
    return pl.pallas_call(
        functools.partial(_dense_attn_kernel, group=group, has_sink=sink is not None),
        out_shape=jax.ShapeDtypeStruct((b, nq, qw), BF16),
        grid=(b, nq // q_block),
        in_specs=in_specs,
        out_specs=pl.BlockSpec((1, q_block, qw), lambda bi, qi: (bi, qi, 0)),
        compiler_params=_params(("parallel", "parallel"), 56),
        name="dense_attention",
    )(*args)


def _window_attn_kernel(sink_ref, q_ref, kp_ref, kc_ref, kn_ref, vp_ref, vc_ref, vn_ref, ck_ref, cv_ref, o_ref,
                        *, group, n_tokens):
    blk = pl.program_id(1)
    wb = WIN_BLOCK
    rows, cols = group * wb, 3 * wb + ck_ref.shape[1]
    q_pos = blk * wb + lax.broadcasted_iota(jnp.int32, (rows, cols), 0) % wb
    col = lax.broadcasted_iota(jnp.int32, (rows, cols), 1)
    k_pos = (blk - 1) * wb + col
    local_ok = (jnp.abs(k_pos - q_pos) <= WINDOW) & (k_pos >= 0) & (k_pos < n_tokens)
    visible = (col >= 3 * wb) | local_ok
    for hkv in range(kc_ref.shape[2] // HEAD_DIM):
        head = lambda g: slice((hkv * group + g) * HEAD_DIM, (hkv * group + g + 1) * HEAD_DIM)
        kv = slice(hkv * HEAD_DIM, (hkv + 1) * HEAD_DIM)
        q = jnp.concatenate([q_ref[0, :, head(g)] for g in range(group)], axis=0)
        keys = jnp.concatenate([kp_ref[0, :, kv], kc_ref[0, :, kv], kn_ref[0, :, kv], ck_ref[0, :, kv]], axis=0)
        vals = jnp.concatenate([vp_ref[0, :, kv], vc_ref[0, :, kv], vn_ref[0, :, kv], cv_ref[0, :, kv]], axis=0)
        s = lax.dot_general(q, keys, (((1,), (1,)), ((), ())), preferred_element_type=F32)
        s = jnp.where(visible, s, NEG_INF)
        sk = jnp.concatenate([jnp.full((wb, 1), sink_ref[hkv * group + g], F32) for g in range(group)], axis=0)
        m = jnp.maximum(jnp.max(s, axis=-1, keepdims=True), sk)
        p = jnp.exp(s - m)
        den = jnp.sum(p, axis=-1, keepdims=True) + jnp.exp(sk - m)
        o = jnp.dot(p.astype(BF16), vals, preferred_element_type=F32) / den
        for g in range(group):
            o_ref[0, :, head(g)] = o[g * wb:(g + 1) * wb].astype(o_ref.dtype)


def window_attention(q, k, v, ctx_k, ctx_v, sink, group):
    b, n, qw = q.shape
    kvw = k.shape[2]
    nb = n // WIN_BLOCK
    nctx = ctx_k.shape[1]
    prev_map = lambda bi, i: (bi, jnp.maximum(i - 1, 0), 0)
    cur_map = lambda bi, i: (bi, i, 0)
    next_map = lambda bi, i: (bi, jnp.minimum(i + 1, nb - 1), 0)
    ctx_map = lambda bi, i: (bi, 0, 0)
    kv_block = (1, WIN_BLOCK, kvw)
    return pl.pallas_call(
        functools.partial(_window_attn_kernel, group=group, n_tokens=n),
        out_shape=jax.ShapeDtypeStruct((b, n, qw), BF16),
        grid=(b, nb),
        in_specs=[pl.BlockSpec(memory_space=pltpu.SMEM),
                  pl.BlockSpec((1, WIN_BLOCK, qw), cur_map),
                  pl.BlockSpec(kv_block, prev_map), pl.BlockSpec(kv_block, cur_map), pl.BlockSpec(kv_block, next_map),
                  pl.BlockSpec(kv_block, prev_map), pl.BlockSpec(kv_block, cur_map), pl.BlockSpec(kv_block, next_map),
                  pl.BlockSpec((1, nctx, kvw), ctx_map), pl.BlockSpec((1, nctx, kvw), ctx_map)],
        out_specs=pl.BlockSpec((1, WIN_BLOCK, qw), cur_map),
        compiler_params=_params(("parallel", "parallel"), 32),
        name="window_attention",
    )(sink, q, k, k, k, v, v, v, ctx_k, ctx_v)


NAT_QROWS = 8
NAT_KROWS = 16


def _nat_bias_table(rpb):
    w = GRID_W
    c = np.arange(w)[:, None]
    kc = np.arange(w)[None, :]
    ws = np.clip(c - NAT_COLS // 2, 0, w - NAT_COLS)
    col_ok = (kc >= ws) & (kc < ws + NAT_COLS)
    col_idx = np.clip(kc - c + NAT_COLS - 1, 0, 2 * NAT_COLS - 2)
    t = jnp.where(col_ok[None, None], rpb[:, :, col_idx], NEG_INF)
    dead = jnp.full_like(t[:, :1], NEG_INF)
    t = jnp.concatenate([dead, t, dead], axis=1)
    return jnp.concatenate([t[:, :-1], t[:, 1:]], axis=-1)


def _nat_attn_kernel(q_ref, k_ref, v_ref, ck_ref, cv_ref, t_ref, o_ref, *, grid_rows):
    m = pl.program_id(2)
    w = GRID_W
    r0 = m * NAT_QROWS
    kr0 = jnp.clip(r0 - NAT_ROWS // 2, 0, grid_rows - NAT_KROWS)
    tok0 = pl.multiple_of(kr0 * w, 4 * w)
    keys = k_ref[0, pl.ds(tok0, NAT_KROWS * w), :]
    vals = v_ref[0, pl.ds(tok0, NAT_KROWS * w), :]
    q = q_ref[0]
    s_loc = lax.dot_general(q, keys, (((1,), (1,)), ((), ())), preferred_element_type=F32)
    lane_hi = lax.broadcasted_iota(jnp.int32, (w, 2 * w), 1) >= w
    bias_rows = []
    for i in range(NAT_QROWS):
        r = r0 + i
        rs = jnp.clip(r - NAT_ROWS // 2, 0, grid_rows - NAT_ROWS)
        pieces = []
        for jj in range(NAT_KROWS // 2):
            kr = kr0 + 2 * jj
            d = kr - r + NAT_ROWS - 1
            ok_lo = ((kr >= rs) & (kr < rs + NAT_ROWS)).astype(jnp.int32)
            ok_hi = ((kr + 1 >= rs) & (kr + 1 < rs + NAT_ROWS)).astype(jnp.int32)
            piece = t_ref[0, jnp.clip(d, -1, 2 * NAT_ROWS - 2) + 1]
            ok = jnp.where(lane_hi, ok_hi, ok_lo) > 0
            pieces.append(jnp.where(ok, piece, NEG_INF))
        bias_rows.append(jnp.concatenate(pieces, axis=1))
    s_loc = s_loc + jnp.concatenate(bias_rows, axis=0)
    s_ctx = lax.dot_general(q, ck_ref[0], (((1,), (1,)), ((), ())), preferred_element_type=F32)
    mx = jnp.maximum(jnp.max(s_loc, axis=-1, keepdims=True), jnp.max(s_ctx, axis=-1, keepdims=True))
    p_loc = jnp.exp(s_loc - mx)
    p_ctx = jnp.exp(s_ctx - mx)
    den = jnp.sum(p_loc, axis=-1, keepdims=True) + jnp.sum(p_ctx, axis=-1, keepdims=True)
    o = (jnp.dot(p_loc.astype(BF16), vals, preferred_element_type=F32)
         + jnp.dot(p_ctx.astype(BF16), cv_ref[0], preferred_element_type=F32))
    o_ref[0] = (o / den).astype(o_ref.dtype)


def neighbourhood_attention(q, k, v, ctx_k, ctx_v, bias_table):
    b, n, hw = q.shape
    heads = hw // HEAD_DIM
    grid_rows = n // GRID_W
    nctx = ctx_k.shape[1]
    tq = NAT_QROWS * GRID_W
    return pl.pallas_call(
        functools.partial(_nat_attn_kernel, grid_rows=grid_rows),
        out_shape=jax.ShapeDtypeStruct((b, n, hw), BF16),
        grid=(b, heads, grid_rows // NAT_QROWS),
        in_specs=[pl.BlockSpec((1, tq, HEAD_DIM), lambda bi, h, m: (bi, m, h)),
                  pl.BlockSpec((1, n, HEAD_DIM), lambda bi, h, m: (bi, 0, h)),
                  pl.BlockSpec((1, n, HEAD_DIM), lambda bi, h, m: (bi, 0, h)),
                  pl.BlockSpec((1, nctx, HEAD_DIM), lambda bi, h, m: (bi, 0, h)),
                  pl.BlockSpec((1, nctx, HEAD_DIM), lambda bi, h, m: (bi, 0, h)),
                  pl.BlockSpec((1, 2 * NAT_ROWS, GRID_W, 2 * GRID_W), lambda bi, h, m: (h, 0, 0, 0))],
        out_specs=pl.BlockSpec((1, tq, HEAD_DIM), lambda bi, h, m: (bi, m, h)),
        compiler_params=_params(("parallel", "parallel", "arbitrary"), 32),
        name="neighbourhood_attention",
    )(q, k, v, ctx_k, ctx_v, bias_table)


def _out_proj_kernel(a_ref, b_ref, w_ref, x_ref, gate_ref, o_ref):
    ka = a_ref.shape[1]
    acc = jnp.dot(a_ref[...], w_ref[:ka, :], preferred_element_type=F32)
    acc = acc + jnp.dot(b_ref[...], w_ref[ka:, :], preferred_element_type=F32)
    o_ref[...] = x_ref[...] + gate_ref[0] * acc


def out_proj_residual(a, b_, w_bf16, x, gate):
    r, d = x.shape
    ka, kb = a.shape[1], b_.shape[1]
    nseg = gate.shape[0]
    tm, tn = 1024, 1024
    tiles_per_seg = (r // nseg) // tm
    return pl.pallas_call(
        _out_proj_kernel,
        out_shape=jax.ShapeDtypeStruct((r, d), F32),
        grid=(r // tm, d // tn),
        in_specs=[pl.BlockSpec((tm, ka), lambda i, j: (i, 0)),
                  pl.BlockSpec((tm, kb), lambda i, j: (i, 0)),
                  pl.BlockSpec((ka + kb, tn), lambda i, j: (0, j)),
                  pl.BlockSpec((tm, tn), lambda i, j: (i, j)),
                  pl.BlockSpec((1, 1, tn), lambda i, j: (i // tiles_per_seg, 0, j))],
        out_specs=pl.BlockSpec((tm, tn), lambda i, j: (i, j)),
        compiler_params=_params(("parallel", "parallel"), 56),
        name="out_proj_residual",
    )(a, b_, w_bf16, x, gate)


def _norm_router_kernel(x_ref, g_ref, shift_ref, scale_ref, whi_ref, wlo_ref, *out_refs):
    logit_ref = out_refs[-1]
    h = _mod_norm(x_ref[...], g_ref[...], shift_ref[0], scale_ref[0])
    h_hi = h.astype(BF16)
    h_lo = (h - h_hi.astype(F32)).astype(BF16)
    if len(out_refs) > 1:
        out_refs[0][...] = h_hi
    w_hi = whi_ref[...]
    logit_ref[...] = (jnp.dot(h_hi, w_hi, preferred_element_type=F32)
                      + jnp.dot(h_lo, w_hi, preferred_element_type=F32)
                      + jnp.dot(h_hi, wlo_ref[...], preferred_element_type=F32))


def norm_router(x, g, shift, scale, router_w, want_h):
    r, d = x.shape
    nseg = shift.shape[0]
    tm = 512
    tiles_per_seg = (r // nseg) // tm
    w_pad = jnp.zeros((d, LANES), F32).at[:, :N_EXPERTS].set(router_w)
    w_hi = w_pad.astype(BF16)
    w_lo = (w_pad - w_hi.astype(F32)).astype(BF16)
    out_shape = [jax.ShapeDtypeStruct((r, LANES), F32)]
    out_specs = [pl.BlockSpec((tm, LANES), lambda i: (i, 0))]
    if want_h:
        out_shape.insert(0, jax.ShapeDtypeStruct((r, d), BF16))
        out_specs.insert(0, pl.BlockSpec((tm, d), lambda i: (i, 0)))
    return pl.pallas_call(
        _norm_router_kernel,
        out_shape=out_shape,
        grid=(r // tm,),
        in_specs=[pl.BlockSpec((tm, d), lambda i: (i, 0)),
                  pl.BlockSpec((1, d), lambda i: (0, 0)),
                  pl.BlockSpec((1, 1, d), lambda i: (i // tiles_per_seg, 0, 0)),
                  pl.BlockSpec((1, 1, d), lambda i: (i // tiles_per_seg, 0, 0)),
                  pl.BlockSpec((d, LANES), lambda i: (0, 0)),
                  pl.BlockSpec((d, LANES), lambda i: (0, 0))],
        out_specs=out_specs,
        compiler_params=_params(("parallel",), 32),
        name="norm_router",
    )(x, g.reshape(1, d), shift, scale, w_hi, w_lo)


CUMSUM_BLOCK = 256


def _exclusive_cumsum_lanes(flags):
    e, n = flags.shape
    cb = min(CUMSUM_BLOCK, n)
    tri = (lax.broadcasted_iota(jnp.int32, (cb, cb), 0) < lax.broadcasted_iota(jnp.int32, (cb, cb), 1)).astype(BF16)
    carry = jnp.zeros((e, 1), F32)
    out = []
    for j in range(n // cb):
        blk = flags[:, j * cb:(j + 1) * cb]
        out.append(jnp.dot(blk.astype(BF16), tri, preferred_element_type=F32) + carry)
        carry = carry + jnp.sum(blk, axis=-1, keepdims=True)
    return jnp.concatenate(out, axis=1) if len(out) > 1 else out[0]


def _select_kernel(logit_ref, pos_ref, gate_ref, *idx_refs, cap):
    lg = logit_ref[0]
    e, n = lg.shape
    ex = jnp.exp(lg - jnp.max(lg, axis=0, keepdims=True))
    aff = ex / jnp.sum(ex, axis=0, keepdims=True)

    def enough(t):
        return jnp.sum((aff >= t).astype(F32), axis=-1, keepdims=True) >= cap

    hi = jnp.full((e, 1), 2.0, F32)
    for step in (64, 32, 16, 8, 4, 2, 1):
        cand = hi * (2.0 ** -step)
        hi = jnp.where(enough(cand), hi, cand)
    lo = hi * 0.5
    lo = jnp.where(enough(lo), lo, 0.0)
    for _ in range(23):
        mid = (lo + hi) * 0.5
        ok = enough(mid)
        lo = jnp.where(ok, mid, lo)
        hi = jnp.where(ok, hi, mid)
    thr = lo
    above = (aff > thr).astype(F32)
    tied = (aff == thr).astype(F32)
    need = cap - jnp.sum(above, axis=-1, keepdims=True)
    take = above + tied * (_exclusive_cumsum_lanes(tied) < need).astype(F32)
    slot = _exclusive_cumsum_lanes(take)
    pos = jnp.where(take > 0, slot, -1.0).astype(jnp.int32)
    pos_ref[0] = pos
    s_iota = lax.broadcasted_iota(jnp.int32, (cap, n), 0)
    token = lax.broadcasted_iota(jnp.int32, (1, n), 1).astype(F32)
    for ei in range(e):
        hit = pos[ei:ei + 1, :] == s_iota
        gate_ref[0, ei] = jnp.sum(jnp.where(hit, aff[ei:ei + 1, :], 0.0), axis=-1, keepdims=True)
        if idx_refs:
            idx_refs[0][0, ei] = jnp.sum(jnp.where(hit, token, 0.0), axis=-1, keepdims=True).astype(jnp.int32)


def select_tokens(logits_t, cap, want_idx):
    nseg, e, n = logits_t.shape
    slots = (nseg, e, cap, 1)
    out_shape = [jax.ShapeDtypeStruct((nseg, e, n), jnp.int32), jax.ShapeDtypeStruct(slots, F32)]
    out_specs = [pl.BlockSpec((1, e, n), lambda s: (s, 0, 0)), pl.BlockSpec((1, e, cap, 1), lambda s: (s, 0, 0, 0))]
    if want_idx:
        out_shape.append(jax.ShapeDtypeStruct(slots, jnp.int32))
        out_specs.append(pl.BlockSpec((1, e, cap, 1), lambda s: (s, 0, 0, 0)))
    return pl.pallas_call(
        functools.partial(_select_kernel, cap=cap),
        out_shape=out_shape,
        grid=(nseg,),
        in_specs=[pl.BlockSpec((1, e, n), lambda s: (s, 0, 0))],
        out_specs=out_specs,
        compiler_params=_params(("parallel",), 48),
        name="select_tokens",
    )(logits_t)


def _gather_rows_kernel(idx_ref, x_hbm, g_ref, shift_ref, scale_ref, o_ref, buf, sem, *, cap, n, experts):
    step = pl.program_id(0)
    slot = step % 2

    def row_copy(st, sl, k):
        row = (st // experts) * n + idx_ref[st * cap + k]
        return pltpu.make_async_copy(x_hbm.at[pl.ds(row, 1), :], buf.at[sl, pl.ds(k, 1), :], sem.at[sl])

    def issue(st, sl):
        def body(k, carry):
            row_copy(st, sl, k).start()
            return carry
        lax.fori_loop(0, cap, body, 0)

    @pl.when(step == 0)
    def _():
        issue(step, slot)

    @pl.when(step + 1 < pl.num_programs(0))
    def _():
        issue(step + 1, 1 - slot)

    pltpu.make_async_copy(x_hbm.at[pl.ds(0, cap), :], buf.at[slot], sem.at[slot]).wait()
    h = _mod_norm(buf[slot], g_ref[...], shift_ref[0], scale_ref[0])
    o_ref[0, 0] = h.astype(o_ref.dtype)


def gather_rows(idx, x, g, shift, scale, n):
    nseg, e, cap = idx.shape
    d = x.shape[1]
    return pl.pallas_call(
        functools.partial(_gather_rows_kernel, cap=cap, n=n, experts=e),
        out_shape=jax.ShapeDtypeStruct((nseg, e, cap, d), BF16),
        grid_spec=pltpu.PrefetchScalarGridSpec(
            num_scalar_prefetch=1,
            grid=(nseg * e,),
            in_specs=[pl.BlockSpec(memory_space=pl.ANY),
                      pl.BlockSpec((1, d), lambda st, idx_ref: (0, 0)),
                      pl.BlockSpec((1, 1, d), lambda st, idx_ref: (st // e, 0, 0)),
                      pl.BlockSpec((1, 1, d), lambda st, idx_ref: (st // e, 0, 0))],
            out_specs=pl.BlockSpec((1, 1, cap, d), lambda st, idx_ref: (st // e, st % e, 0, 0)),
            scratch_shapes=[pltpu.VMEM((2, cap, d), F32), pltpu.SemaphoreType.DMA((2,))]),
        compiler_params=_params(("arbitrary",), 32),
        name="gather_rows",
    )(idx.reshape(-1), x, g.reshape(1, d), shift, scale)


def _gather_kernel(pos_ref, h_ref, o_ref, acc_ref, *, cap):
    kt = pl.program_id(2)
    eb = pos_ref.shape[1]
    tk = pos_ref.shape[3]
    s_iota = lax.broadcasted_iota(jnp.int32, (cap, tk), 0)
    onehot = jnp.concatenate([(pos_ref[0, i] == s_iota).astype(BF16) for i in range(eb)], axis=0)
    part = jnp.dot(onehot, h_ref[0], preferred_element_type=F32)

    @pl.when(kt == 0)
    def _():
        acc_ref[...] = part

    @pl.when(kt > 0)
    def _():
        acc_ref[...] += part

    @pl.when(kt == pl.num_programs(2) - 1)
    def _():
        o_ref[0] = acc_ref[...].reshape(o_ref.shape[1:]).astype(o_ref.dtype)


def gather_tokens(pos, h, cap, experts_per_step, tk):
    nseg, e, n = pos.shape
    d = h.shape[2]
    eb = experts_per_step
    return pl.pallas_call(
        functools.partial(_gather_kernel, cap=cap),
        out_shape=jax.ShapeDtypeStruct((nseg, e, cap, d), BF16),
        grid=(nseg, e // eb, n // tk),
        in_specs=[pl.BlockSpec((1, eb, 1, tk), lambda s, ei, kt: (s, ei, 0, kt)),
                  pl.BlockSpec((1, tk, d), lambda s, ei, kt: (s, kt, 0))],
        out_specs=pl.BlockSpec((1, eb, cap, d), lambda s, ei, kt: (s, ei, 0, 0)),
        scratch_shapes=[pltpu.VMEM((eb * cap, d), F32)],
        compiler_params=_params(("parallel", "parallel", "arbitrary"), 48),
        name="gather_tokens",
    )(pos.reshape(nseg, e, 1, n), h)


def _expert_up_kernel(xc_ref, xl_ref, wg_ref, wu_ref, o_ref):
    d = xc_ref.shape[-1]
    wg = wg_ref[0, 0].astype(BF16)
    wu = wu_ref[0, 0].astype(BF16)
    row = 0
    for x_ref in (xc_ref, xl_ref):
        rows = x_ref.shape[0] * x_ref.shape[2]
        x = x_ref[...].reshape(rows, d)
        a = jnp.dot(x, wg, preferred_element_type=F32)
        b = jnp.dot(x, wu, preferred_element_type=F32)
        o_ref[0, row:row + rows, :] = (_silu(a) * b).astype(o_ref.dtype)
        row += rows


def expert_up(xg_ctx, xg_lat, w_gate, w_up, layer):
    _, e, d, f = w_gate.shape
    tf = 256
    sc, _, cc, _ = xg_ctx.shape
    sl, _, cl, _ = xg_lat.shape
    rows = sc * cc + sl * cl
    return pl.pallas_call(
        _expert_up_kernel,
        out_shape=jax.ShapeDtypeStruct((e, rows, f), BF16),
        grid=(e, f // tf),
        in_specs=[pl.BlockSpec((sc, 1, cc, d), lambda ei, j: (0, ei, 0, 0)),
                  pl.BlockSpec((sl, 1, cl, d), lambda ei, j: (0, ei, 0, 0)),
                  pl.BlockSpec((1, 1, d, tf), lambda ei, j: (layer, ei, 0, j)),
                  pl.BlockSpec((1, 1, d, tf), lambda ei, j: (layer, ei, 0, j))],
        out_specs=pl.BlockSpec((1, rows, tf), lambda ei, j: (ei, 0, j)),
        compiler_params=_params(("parallel", "arbitrary"), 56),
        name="expert_up",
    )(xg_ctx, xg_lat, w_gate, w_up)


def _expert_down_kernel(h_ref, w_ref, gate_ref, o_ref):
    y = jnp.dot(h_ref[0], w_ref[0, 0].astype(BF16), preferred_element_type=F32)
    o_ref[0] = (y * gate_ref[0]).astype(o_ref.dtype)


def expert_down(hid, w_down, gates, layer):
    e, rows, f = hid.shape
    d = w_down.shape[3]
    td = 256
    return pl.pallas_call(
        _expert_down_kernel,
        out_shape=jax.ShapeDtypeStruct((e, rows, d), BF16),
        grid=(e, d // td),
        in_specs=[pl.BlockSpec((1, rows, f), lambda ei, j: (ei, 0, 0)),
                  pl.BlockSpec((1, 1, f, td), lambda ei, j: (layer, ei, 0, j)),
                  pl.BlockSpec((1, rows, 1), lambda ei, j: (ei, 0, 0))],
        out_specs=pl.BlockSpec((1, rows, td), lambda ei, j: (ei, 0, j)),
        compiler_params=_params(("parallel", "arbitrary"), 48),
        name="expert_down",
    )(hid, w_down, gates)


def _combine_kernel(post_ref, y_ref, x_ref, gate_ref, o_ref, hit_ref, *, cap):
    ne = y_ref.shape[0]
    tt = post_ref.shape[1]

    @pl.when(pl.program_id(2) == 0)
    def _():
        post = post_ref[0]
        width = max(cap, LANES)
        per = width // cap
        lane = lax.broadcasted_iota(jnp.int32, (tt, width), 1)
        for blk in range(ne // per):
            hit = None
            for i in range(per):
                col = post[:, blk * per + i:blk * per + i + 1]
                h = jnp.where(col >= 0, col + i * cap, -1) == lane
                hit = h if hit is None else (hit | h)
            hit_ref[:, blk * width:(blk + 1) * width] = jnp.where(hit, 1.0, 0.0).astype(BF16)

    y = y_ref[...].reshape(ne * cap, y_ref.shape[-1])
    o_ref[...] = x_ref[...] + gate_ref[0] * jnp.dot(hit_ref[...], y, preferred_element_type=F32)


def combine_tokens(pos_t, y, x, gate, cap, row_offset, tt, td):
    nseg, n, e = pos_t.shape
    r, d = x.shape
    nt = n // tt
    rb = row_offset // cap
    return pl.pallas_call(
        functools.partial(_combine_kernel, cap=cap),
        out_shape=jax.ShapeDtypeStruct((r, d), F32),
        grid=(nseg, nt, d // td),
        in_specs=[pl.BlockSpec((1, tt, e), lambda s, t, j: (s, t, 0)),
                  pl.BlockSpec((e, cap, td), lambda s, t, j: (0, rb + s, j)),
                  pl.BlockSpec((tt, td), lambda s, t, j: (s * nt + t, j)),
                  pl.BlockSpec((1, 1, td), lambda s, t, j: (s * gate.shape[0] // nseg, 0, j))],
        out_specs=pl.BlockSpec((tt, td), lambda s, t, j: (s * nt + t, j)),
        scratch_shapes=[pltpu.VMEM((tt, e * cap), BF16)],
        compiler_params=_params(("parallel", "parallel", "arbitrary"), 56),
        name="combine_tokens",
    )(pos_t, y, x, gate)


def moe_layer(groups, g_ffn, router_w, w_gate, w_up, w_down, layer):
    sel = []
    for grp in groups:
        nseg, n = grp["nseg"], grp["n"]
        cap = (EC_CAPACITY * n) // N_EXPERTS
        dense = n <= 512
        routed = norm_router(grp["x"], g_ffn, grp["shift"], grp["scale"], router_w, want_h=dense)
        logits_t = jnp.swapaxes(routed[-1][:, :N_EXPERTS].reshape(nseg, n, N_EXPERTS), 1, 2)
        picked = select_tokens(logits_t, cap, want_idx=not dense)
        pos, gates = picked[0], picked[1]
        if dense:
            xg = gather_tokens(pos, routed[0].reshape(nseg, n, D_MODEL), cap, N_EXPERTS, n)
        else:
            xg = gather_rows(picked[2].reshape(nseg, N_EXPERTS, cap), grp["x"], g_ffn, grp["shift"], grp["scale"], n)
        sel.append(dict(cap=cap, pos=pos, xg=xg,
                        gates=jnp.swapaxes(gates, 0, 1).reshape(N_EXPERTS, nseg * cap, 1)))
    hid = expert_up(sel[0]["xg"], sel[1]["xg"], w_gate, w_up, layer)
    y = expert_down(hid, w_down, jnp.concatenate([s["gates"] for s in sel], axis=1), layer)
    out = []
    row_offset = 0
    for grp, s in zip(groups, sel):
        n, cap = grp["n"], s["cap"]
        small = n <= 512
        out.append(combine_tokens(jnp.swapaxes(s["pos"], 1, 2), y, grp["x"], grp["gate"], cap, row_offset,
                                  n if small else 1024, D_MODEL if small else 512))
        row_offset += grp["nseg"] * cap
    return out


def _rope_tables(n):
    t = jnp.arange(n)
    row = (t // GRID_W).astype(F32)
    col = (t % GRID_W).astype(F32)
    nf = HEAD_DIM // 4
    inv = ROPE_THETA ** (-jnp.arange(nf, dtype=F32) / nf)
    ang = jnp.concatenate([row[:, None] * inv, col[:, None] * inv], axis=-1)
    cos, sin = jnp.cos(ang), jnp.sin(ang)
    return jnp.concatenate([cos, cos], axis=-1), jnp.concatenate([-sin, sin], axis=-1)


def kernel(x_prompt, x_sample, cache_win_k, cache_win_v, cache_nat_k, cache_nat_v, cache_glob_k, cache_glob_v, c, c_ctx, mod_w, mod_b, norm_mix_w, norm_ffn_w, even_w_in, even_w_out, conv_w, conv_b, win_sink, win_q_norm, win_k_norm, odd_w_in, odd_w_out, nat_rpb, nat_q_norm, nat_k_norm, glob_q_norm, glob_k_norm, router_w, expert_w_gate, expert_w_up, expert_w_down):
    bp, sp, d = x_prompt.shape
    bs, ss, _ = x_sample.shape
    depth = mod_w.shape[0]
    rope = _rope_tables(ss)

    cond8 = jnp.zeros((8, d), F32).at[0].set(c_ctx).at[1:1 + bs].set(c)
    mods = modulation(cond8, mod_w, mod_b).reshape(depth, 8, N_MOD, d)

    xp = x_prompt.reshape(bp * sp, d)
    xs = x_sample.reshape(bs * ss, d)
    win_k, win_v, nat_k, nat_v, glob_k, glob_v = [], [], [], [], [], []

    def flat_cache(cache, i):
        return cache[:, i].reshape(bs, cache.shape[2], -1).astype(BF16)

    for layer in range(depth):
        i = layer // 2
        mp = [mods[layer, 0:1, k].reshape(1, 1, d) for k in range(N_MOD)]
        ms = [mods[layer, 1:1 + bs, k].reshape(bs, 1, d) for k in range(N_MOD)]
        if layer % 2 == 0:
            w_in = even_w_in[i].astype(BF16)
            w_out = even_w_out[i].astype(BF16)
            q0 = 3 * CONV_DIM
            qw = WIN_Q_HEADS * HEAD_DIM
            kw = WIN_KV_HEADS * HEAD_DIM
            group = WIN_Q_HEADS // WIN_KV_HEADS
            h = mod_norm(xp, norm_mix_w[layer], mp[0], mp[1])
            ya = conv_gate(project(h, w_in, 0, q0, want_bf16=False, want_f32=True), conv_w[i], conv_b[i], sp)
            q = project(h, w_in, q0, qw, gain=win_q_norm[i], scale=ATTN_SCALE)
            k, k32 = project(h, w_in, q0 + qw, kw, gain=win_k_norm[i], want_f32=True)
            v, v32 = project(h, w_in, q0 + qw + kw, kw, want_f32=True)
            yb = dense_attention(q.reshape(bp, sp, qw), k.reshape(bp, sp, kw), v.reshape(bp, sp, kw),
                                 group, sp, sink=win_sink[i])
            xp = out_proj_residual(ya, yb.reshape(bp * sp, qw), w_out, xp, mp[2])
            win_k.append(k32.reshape(bp, sp, WIN_KV_HEADS, HEAD_DIM))
            win_v.append(v32.reshape(bp, sp, WIN_KV_HEADS, HEAD_DIM))
            h = mod_norm(xs, norm_mix_w[layer], ms[0], ms[1])
            ya = conv_gate(project(h, w_in, 0, q0, want_bf16=False, want_f32=True), conv_w[i], conv_b[i], ss)
            q = project(h, w_in, q0, qw, gain=win_q_norm[i], scale=ATTN_SCALE, rope_tables=rope)
            k = project(h, w_in, q0 + qw, kw, gain=win_k_norm[i], rope_tables=rope)
            v = project(h, w_in, q0 + qw + kw, kw)
            yb = window_attention(q.reshape(bs, ss, qw), k.reshape(bs, ss, kw), v.reshape(bs, ss, kw),
                                  flat_cache(cache_win_k, i), flat_cache(cache_win_v, i), win_sink[i], group)
            xs = out_proj_residual(ya, yb.reshape(bs * ss, qw), w_out, xs, ms[2])
        else:
            w_in = odd_w_in[i].astype(BF16)
            w_out = odd_w_out[i].astype(BF16)
            nw = NAT_HEADS * HEAD_DIM
            gqw = GLOB_Q_HEADS * HEAD_DIM
            gkw = GLOB_KV_HEADS * HEAD_DIM
            group = GLOB_Q_HEADS // GLOB_KV_HEADS
            kd0 = 3 * nw + gqw
            h = mod_norm(xp, norm_mix_w[layer], mp[0], mp[1])
            qc = project(h, w_in, 0, nw, gain=nat_q_norm[i], scale=ATTN_SCALE)
            kc, kc32 = project(h, w_in, nw, nw, gain=nat_k_norm[i], want_f32=True)
            vc, vc32 = project(h, w_in, 2 * nw, nw, want_f32=True)
            qd = project(h, w_in, 3 * nw, gqw, gain=glob_q_norm[i], scale=ATTN_SCALE)
            kd, kd32 = project(h, w_in, kd0, gkw, gain=glob_k_norm[i], want_f32=True)
            vd, vd32 = project(h, w_in, kd0 + gkw, gkw, want_f32=True)
            yc = dense_attention(qc.reshape(bp, sp, nw), kc.reshape(bp, sp, nw), vc.reshape(bp, sp, nw), 1, sp)
            yd = dense_attention(qd.reshape(bp, sp, gqw), kd.reshape(bp, sp, gkw), vd.reshape(bp, sp, gkw), group, sp)
            xp = out_proj_residual(yc.reshape(bp * sp, nw), yd.reshape(bp * sp, gqw), w_out, xp, mp[2])
            nat_k.append(kc32.reshape(bp, sp, NAT_HEADS, HEAD_DIM))
            nat_v.append(vc32.reshape(bp, sp, NAT_HEADS, HEAD_DIM))
            glob_k.append(kd32.reshape(bp, sp, GLOB_KV_HEADS, HEAD_DIM))
            glob_v.append(vd32.reshape(bp, sp, GLOB_KV_HEADS, HEAD_DIM))
            h = mod_norm(xs, norm_mix_w[layer], ms[0], ms[1])
            qc = project(h, w_in, 0, nw, gain=nat_q_norm[i], scale=ATTN_SCALE)
            kc = project(h, w_in, nw, nw, gain=nat_k_norm[i])
            vc = project(h, w_in, 2 * nw, nw)
            qd = project(h, w_in, 3 * nw, gqw, gain=glob_q_norm[i], scale=ATTN_SCALE, rope_tables=rope)
            kd = project(h, w_in, kd0, gkw, gain=glob_k_norm[i], rope_tables=rope)
            vd = project(h, w_in, kd0 + gkw, gkw)
            yc = neighbourhood_attention(qc.reshape(bs, ss, nw), kc.reshape(bs, ss, nw), vc.reshape(bs, ss, nw),
                                         flat_cache(cache_nat_k, i), flat_cache(cache_nat_v, i),
                                         _nat_bias_table(nat_rpb[i]))
            k_all = jnp.concatenate([kd.reshape(bs, ss, gkw), flat_cache(cache_glob_k, i)], axis=1)
            v_all = jnp.concatenate([vd.reshape(bs, ss, gkw), flat_cache(cache_glob_v, i)], axis=1)
            yd = dense_attention(qd.reshape(bs, ss, gqw), k_all, v_all, group, 128)
            xs = out_proj_residual(yc.reshape(bs * ss, nw), yd.reshape(bs * ss, gqw), w_out, xs, ms[2])

        xp, xs = moe_layer(
            [dict(x=xp, shift=mp[3], scale=mp[4], gate=mp[5], nseg=bp, n=sp),
             dict(x=xs, shift=ms[3], scale=ms[4], gate=ms[5], nseg=bs, n=ss)],
            norm_ffn_w[layer], router_w[layer], expert_w_gate, expert_w_up, expert_w_down, layer)

    stack = lambda xs_: jnp.stack(xs_, axis=1)
    return (xp.reshape(bp, sp, d), xs.reshape(bs, ss, d), stack(win_k), stack(win_v), stack(nat_k),
            stack(nat_v), stack(glob_k), stack(glob_v))
```

```python
import functools

import numpy as np
import jax
import jax.numpy as jnp
from jax import lax
from jax.experimental import pallas as pl
from jax.experimental.pallas import tpu as pltpu

D_MODEL = 2048
HEAD_DIM = 128
GRID_W = 64
CONV_DIM = 512
WIN_Q_HEADS = 12
WIN_KV_HEADS = 4
WIN_BLOCK = 128
WINDOW = 128
NAT_HEADS = 8
NAT_ROWS = 8
NAT_COLS = 16
GLOB_Q_HEADS = 8
GLOB_KV_HEADS = 2
ROPE_THETA = 10000.0
N_EXPERTS = 16
EXPERT_FF = 2048
EC_CAPACITY = 2
N_MOD = 6
NORM_EPS = 1e-6
NEG_INF = -1e30
ATTN_SCALE = HEAD_DIM ** -0.5

LANES = 128
MIB = 1024 * 1024
BF16 = jnp.bfloat16
F32 = jnp.float32


def _params(semantics, vmem_mib, **extra):
    return pltpu.CompilerParams(dimension_semantics=semantics, vmem_limit_bytes=vmem_mib * MIB, **extra)


def _silu(x):
    return x * (1.0 / (1.0 + jnp.exp(-x)))


def _modulation_kernel(cond_ref, w_ref, b_ref, o_ref):
    s = _silu(cond_ref[...]).astype(BF16)
    o_ref[0] = jnp.dot(s, w_ref[0].astype(BF16), preferred_element_type=F32) + b_ref[0]


def modulation(cond8, mod_w, mod_b):
    depth, d, n = mod_w.shape
    tn = 1024
    return pl.pallas_call(
        _modulation_kernel,
        out_shape=jax.ShapeDtypeStruct((depth, 8, n), F32),
        grid=(depth, n // tn),
        in_specs=[pl.BlockSpec((8, d), lambda l, j: (0, 0)),
                  pl.BlockSpec((1, d, tn), lambda l, j: (l, 0, j)),
                  pl.BlockSpec((1, 1, tn), lambda l, j: (l, 0, j))],
        out_specs=pl.BlockSpec((1, 8, tn), lambda l, j: (l, 0, j)),
        compiler_params=_params(("parallel", "parallel"), 48),
        name="modulation",
    )(cond8, mod_w, mod_b.reshape(depth, 1, n))


def _mod_norm(x, g, shift, scale):
    ms = jnp.mean(x * x, axis=-1, keepdims=True)
    y = x * lax.rsqrt(ms + NORM_EPS) * g
    return y * (1.0 + scale) + shift


def _mod_norm_kernel(x_ref, g_ref, shift_ref, scale_ref, h_ref):
    h_ref[...] = _mod_norm(x_ref[...], g_ref[...], shift_ref[0], scale_ref[0]).astype(BF16)


def mod_norm(x, g, shift, scale):
    r, d = x.shape
    nseg = shift.shape[0]
    tm = 512
    tiles_per_seg = (r // nseg) // tm
    return pl.pallas_call(
        _mod_norm_kernel,
        out_shape=jax.ShapeDtypeStruct((r, d), BF16),
        grid=(r // tm,),
        in_specs=[pl.BlockSpec((tm, d), lambda i: (i, 0)),
                  pl.BlockSpec((1, d), lambda i: (0, 0)),
                  pl.BlockSpec((1, 1, d), lambda i: (i // tiles_per_seg, 0, 0)),
                  pl.BlockSpec((1, 1, d), lambda i: (i // tiles_per_seg, 0, 0))],
        out_specs=pl.BlockSpec((tm, d), lambda i: (i, 0)),
        compiler_params=_params(("parallel",), 32),
        name="mod_norm",
    )(x, g.reshape(1, d), shift, scale)


def _proj_kernel(*refs, head_norm, rope, scale, want_bf16, want_f32):
    h_ref, w_ref = refs[0], refs[1]
    k = 2
    if head_norm:
        g_ref = refs[k]
        k += 1
    if rope:
        cos_ref, sin_ref = refs[k], refs[k + 1]
        k += 2
    outs = refs[k:]
    acc = jnp.dot(h_ref[...], w_ref[...], preferred_element_type=F32)
    if not head_norm:
        if want_bf16:
            outs[0][...] = acc.astype(BF16)
        if want_f32:
            outs[-1][...] = acc
        return
    ones = jnp.ones((HEAD_DIM, HEAD_DIM), BF16)
    for hd in range(acc.shape[1] // HEAD_DIM):
        sl = slice(hd * HEAD_DIM, (hd + 1) * HEAD_DIM)
        x = acc[:, sl]
        ssq = jnp.dot((x * x).astype(BF16), ones, preferred_element_type=F32)
        y = x * lax.rsqrt(ssq * (1.0 / HEAD_DIM) + NORM_EPS) * g_ref[...]
        if want_f32:
            outs[-1][:, sl] = y
        if rope:
            y = y * cos_ref[...] + pltpu.roll(y, HEAD_DIM // 2, 1) * sin_ref[...]
        outs[0][:, sl] = (y * scale).astype(BF16)


def project(h, w_bf16, col0, width, gain=None, scale=1.0, rope_tables=None, want_bf16=True, want_f32=False):
    r, d = h.shape
    tm = 1024
    tn = min(width, 512)
    cb0 = col0 // tn
    rope = rope_tables is not None
    in_specs = [pl.BlockSpec((tm, d), lambda j, i: (i, 0)),
                pl.BlockSpec((d, tn), lambda j, i: (0, cb0 + j))]
    args = [h, w_bf16]
    if gain is not None:
        in_specs.append(pl.BlockSpec((1, HEAD_DIM), lambda j, i: (0, 0)))
        args.append(gain.reshape(1, HEAD_DIM))
    if rope:
        per = rope_tables[0].shape[0] // tm
        in_specs += [pl.BlockSpec((tm, HEAD_DIM), lambda j, i: (i % per, 0))] * 2
        args += list(rope_tables)
    out_shape, out_specs = [], []
    for want, dt in ((want_bf16, BF16), (want_f32, F32)):
        if want:
            out_shape.append(jax.ShapeDtypeStruct((r, width), dt))
            out_specs.append(pl.BlockSpec((tm, tn), lambda j, i: (i, j)))
    out = pl.pallas_call(
        functools.partial(_proj_kernel, head_norm=gain is not None, rope=rope, scale=scale,
                          want_bf16=want_bf16, want_f32=want_f32),
        out_shape=out_shape,
        grid=(width // tn, r // tm),
        in_specs=in_specs,
        out_specs=out_specs,
        compiler_params=_params(("parallel", "parallel"), 48),
        name="project",
    )(*args)
    return out if len(out) > 1 else out[0]


def _conv_gate_kernel(ab_ref, ac_ref, ah_ref, w_ref, b_ref, o_ref, *, seq):
    u = ac_ref[...] * ah_ref[...]
    rows = u.shape[0]
    pos = lax.broadcasted_iota(jnp.int32, u.shape, 0) % seq
    prev = jnp.where(pos == 0, 0.0, pltpu.roll(u, 1, 0))
    nxt = jnp.where(pos == seq - 1, 0.0, pltpu.roll(u, rows - 1, 0))
    conv = prev * w_ref[0:1, :] + u * w_ref[1:2, :] + nxt * w_ref[2:3, :] + b_ref[...]
    o_ref[...] = (ab_ref[...] * conv).astype(o_ref.dtype)


def conv_gate(proj, conv_w, conv_b, seq):
    r = proj.shape[0]
    tr = 4096
    nc = CONV_DIM // LANES
    return pl.pallas_call(
        functools.partial(_conv_gate_kernel, seq=seq),
        out_shape=jax.ShapeDtypeStruct((r, CONV_DIM), BF16),
        grid=(r // tr, nc),
        in_specs=[pl.BlockSpec((tr, LANES), lambda i, c: (i, c)),
                  pl.BlockSpec((tr, LANES), lambda i, c: (i, nc + c)),
                  pl.BlockSpec((tr, LANES), lambda i, c: (i, 2 * nc + c)),
                  pl.BlockSpec((3, LANES), lambda i, c: (0, c)),
                  pl.BlockSpec((1, LANES), lambda i, c: (0, c))],
        out_specs=pl.BlockSpec((tr, LANES), lambda i, c: (i, c)),
        compiler_params=_params(("parallel", "parallel"), 48),
        name="conv_gate",
    )(proj, proj, proj, conv_w, conv_b.reshape(1, CONV_DIM))


def _dense_attn_kernel(*refs, group, has_sink):
    if has_sink:
        sink_ref, q_ref, k_ref, v_ref, o_ref = refs
    else:
        q_ref, k_ref, v_ref, o_ref = refs
    qb = q_ref.shape[1]
    for hkv in range(k_ref.shape[2] // HEAD_DIM):
        head = lambda g: slice((hkv * group + g) * HEAD_DIM, (hkv * group + g + 1) * HEAD_DIM)
        kv = slice(hkv * HEAD_DIM, (hkv + 1) * HEAD_DIM)
        q = jnp.concatenate([q_ref[0, :, head(g)] for g in range(group)], axis=0)
        s = lax.dot_general(q, k_ref[0, :, kv], (((1,), (1,)), ((), ())), preferred_element_type=F32)
        m = jnp.max(s, axis=-1, keepdims=True)
        if has_sink:
            sk = jnp.concatenate([jnp.full((qb, 1), sink_ref[hkv * group + g], F32) for g in range(group)], axis=0)
            m = jnp.maximum(m, sk)
        p = jnp.exp(s - m)
        den = jnp.sum(p, axis=-1, keepdims=True)
        if has_sink:
            den = den + jnp.exp(sk - m)
        o = jnp.dot(p.astype(BF16), v_ref[0, :, kv], preferred_element_type=F32) / den
        for g in range(group):
            o_ref[0, :, head(g)] = o[g * qb:(g + 1) * qb].astype(o_ref.dtype)


def dense_attention(q, k, v, group, q_block, sink=None):
    b, nq, qw = q.shape
    nk = k.shape[1]
    kvw = k.shape[2]
    in_specs = [pl.BlockSpec((1, q_block, qw), lambda bi, qi: (bi, qi, 0)),
                pl.BlockSpec((1, nk, kvw), lambda bi, qi: (bi, 0, 0)),
                pl.BlockSpec((1, nk, kvw), lambda bi, qi: (bi, 0, 0))]
    args = [q, k, v]
    if sink is not None:
        in_specs = [pl.BlockSpec(memory_space=pltpu.SMEM)] + in_specs
        args = [sink] + args
    return pl.pallas_call(
        functools.partial(_dense_attn_kernel, group=group, has_sink=sink is not None),
        out_shape=jax.ShapeDtypeStruct((b, nq, qw), BF16),
        grid=(b, nq // q_block),
        in_specs=in_specs,
        out_specs=pl.BlockSpec((1, q_block, qw), lambda bi, qi: (bi, qi, 0)),
        compiler_params=_params(("parallel", "parallel"), 56),
        name="dense_attention",
    )(*args)


def _window_attn_kernel(sink_ref, q_ref, kp_ref, kc_ref, kn_ref, vp_ref, vc_ref, vn_ref, ck_ref, cv_ref, o_ref,
                        *, group, n_tokens):
    blk = pl.program_id(1)
    wb = WIN_BLOCK
    rows, cols = group * wb, 3 * wb + ck_ref.shape[1]
    q_pos = blk * wb + lax.broadcasted_iota(jnp.int32, (rows, cols), 0) % wb
    col = lax.broadcasted_iota(jnp.int32, (rows, cols), 1)
    k_pos = (blk - 1) * wb + col
    local_ok = (jnp.abs(k_pos - q_pos) <= WINDOW) & (k_pos >= 0) & (k_pos < n_tokens)
    visible = (col >= 3 * wb) | local_ok
    for hkv in range(kc_ref.shape[2] // HEAD_DIM):
        head = lambda g: slice((hkv * group + g) * HEAD_DIM, (hkv * group + g + 1) * HEAD_DIM)
        kv = slice(hkv * HEAD_DIM, (hkv + 1) * HEAD_DIM)
        q = jnp.concatenate([q_ref[0, :, head(g)] for g in range(group)], axis=0)
        keys = jnp.concatenate([kp_ref[0, :, kv], kc_ref[0, :, kv], kn_ref[0, :, kv], ck_ref[0, :, kv]], axis=0)
        vals = jnp.concatenate([vp_ref[0, :, kv], vc_ref[0, :, kv], vn_ref[0, :, kv], cv_ref[0, :, kv]], axis=0)
        s = lax.dot_general(q, keys, (((1,), (1,)), ((), ())), preferred_element_type=F32)
        s = jnp.where(visible, s, NEG_INF)
        sk = jnp.concatenate([jnp.full((wb, 1), sink_ref[hkv * group + g], F32) for g in range(group)], axis=0)
        m = jnp.maximum(jnp.max(s, axis=-1, keepdims=True), sk)
        p = jnp.exp(s - m)
        den = jnp.sum(p, axis=-1, keepdims=True) + jnp.exp(sk - m)
        o = jnp.dot(p.astype(BF16), vals, preferred_element_type=F32) / den
        for g in range(group):
            o_ref[0, :, head(g)] = o[g * wb:(g + 1) * wb].astype(o_ref.dtype)


def window_attention(q, k, v, ctx_k, ctx_v, sink, group):
    b, n, qw = q.shape
    kvw = k.shape[2]
    nb = n // WIN_BLOCK
    nctx = ctx_k.shape[1]
    prev_map = lambda bi, i: (bi, jnp.maximum(i - 1, 0), 0)
    cur_map = lambda bi, i: (bi, i, 0)
    next_map = lambda bi, i: (bi, jnp.minimum(i + 1, nb - 1), 0)
    ctx_map = lambda bi, i: (bi, 0, 0)
    kv_block = (1, WIN_BLOCK, kvw)
    return pl.pallas_call(
        functools.partial(_window_attn_kernel, group=group, n_tokens=n),
        out_shape=jax.ShapeDtypeStruct((b, n, qw), BF16),
        grid=(b, nb),
        in_specs=[pl.BlockSpec(memory_space=pltpu.SMEM),
                  pl.BlockSpec((1, WIN_BLOCK, qw), cur_map),
                  pl.BlockSpec(kv_block, prev_map), pl.BlockSpec(kv_block, cur_map), pl.BlockSpec(kv_block, next_map),
                  pl.BlockSpec(kv_block, prev_map), pl.BlockSpec(kv_block, cur_map), pl.BlockSpec(kv_block, next_map),
                  pl.BlockSpec((1, nctx, kvw), ctx_map), pl.BlockSpec((1, nctx, kvw), ctx_map)],
        out_specs=pl.BlockSpec((1, WIN_BLOCK, qw), cur_map),
        compiler_params=_params(("parallel", "parallel"), 32),
        name="window_attention",
    )(sink, q, k, k, k, v, v, v, ctx_k, ctx_v)


NAT_QROWS = 8
NAT_KROWS = 16


def _nat_bias_table(rpb):
    w = GRID_W
    c = np.arange(w)[:, None]
    kc = np.arange(w)[None, :]
    ws = np.clip(c - NAT_COLS // 2, 0, w - NAT_COLS)
    col_ok = (kc >= ws) & (kc < ws + NAT_COLS)
    col_idx = np.clip(kc - c + NAT_COLS - 1, 0, 2 * NAT_COLS - 2)
    t = jnp.where(col_ok[None, None], rpb[:, :, col_idx], NEG_INF)
    dead = jnp.full_like(t[:, :1], NEG_INF)
    t = jnp.concatenate([dead, t, dead], axis=1)
    return jnp.concatenate([t[:, :-1], t[:, 1:]], axis=-1)


def _nat_attn_kernel(q_ref, k_ref, v_ref, ck_ref, cv_ref, t_ref, o_ref, *, grid_rows):
    m = pl.program_id(2)
    w = GRID_W
    r0 = m * NAT_QROWS
    kr0 = jnp.clip(r0 - NAT_ROWS // 2, 0, grid_rows - NAT_KROWS)
    tok0 = pl.multiple_of(kr0 * w, 4 * w)
    keys = k_ref[0, pl.ds(tok0, NAT_KROWS * w), :]
    vals = v_ref[0, pl.ds(tok0, NAT_KROWS * w), :]
    q = q_ref[0]
    s_loc = lax.dot_general(q, keys, (((1,), (1,)), ((), ())), preferred_element_type=F32)
    lane_hi = lax.broadcasted_iota(jnp.int32, (w, 2 * w), 1) >= w
    bias_rows = []
    for i in range(NAT_QROWS):
        r = r0 + i
        rs = jnp.clip(r - NAT_ROWS // 2, 0, grid_rows - NAT_ROWS)
        pieces = []
        for jj in range(NAT_KROWS // 2):
            kr = kr0 + 2 * jj
            d = kr - r + NAT_ROWS - 1
            ok_lo = ((kr >= rs) & (kr < rs + NAT_ROWS)).astype(jnp.int32)
            ok_hi = ((kr + 1 >= rs) & (kr + 1 < rs + NAT_ROWS)).astype(jnp.int32)
            piece = t_ref[0, jnp.clip(d, -1, 2 * NAT_ROWS - 2) + 1]
            ok = jnp.where(lane_hi, ok_hi, ok_lo) > 0
            pieces.append(jnp.where(ok, piece, NEG_INF))
        bias_rows.append(jnp.concatenate(pieces, axis=1))
    s_loc = s_loc + jnp.concatenate(bias_rows, axis=0)
    s_ctx = lax.dot_general(q, ck_ref[0], (((1,), (1,)), ((), ())), preferred_element_type=F32)
    mx = jnp.maximum(jnp.max(s_loc, axis=-1, keepdims=True), jnp.max(s_ctx, axis=-1, keepdims=True))
    p_loc = jnp.exp(s_loc - mx)
    p_ctx = jnp.exp(s_ctx - mx)
    den = jnp.sum(p_loc, axis=-1, keepdims=True) + jnp.sum(p_ctx, axis=-1, keepdims=True)
    o = (jnp.dot(p_loc.astype(BF16), vals, preferred_element_type=F32)
         + jnp.dot(p_ctx.astype(BF16), cv_ref[0], preferred_element_type=F32))
    o_ref[0] = (o / den).astype(o_ref.dtype)


def neighbourhood_attention(q, k, v, ctx_k, ctx_v, bias_table):
    b, n, hw = q.shape
    heads = hw // HEAD_DIM
    grid_rows = n // GRID_W
    nctx = ctx_k.shape[1]
    tq = NAT_QROWS * GRID_W
    return pl.pallas_call(
        functools.partial(_nat_attn_kernel, grid_rows=grid_rows),
        out_shape=jax.ShapeDtypeStruct((b, n, hw), BF16),
        grid=(b, heads, grid_rows // NAT_QROWS),
        in_specs=[pl.BlockSpec((1, tq, HEAD_DIM), lambda bi, h, m: (bi, m, h)),
                  pl.BlockSpec((1, n, HEAD_DIM), lambda bi, h, m: (bi, 0, h)),
                  pl.BlockSpec((1, n, HEAD_DIM), lambda bi, h, m: (bi, 0, h)),
                  pl.BlockSpec((1, nctx, HEAD_DIM), lambda bi, h, m: (bi, 0, h)),
                  pl.BlockSpec((1, nctx, HEAD_DIM), lambda bi, h, m: (bi, 0, h)),
                  pl.BlockSpec((1, 2 * NAT_ROWS, GRID_W, 2 * GRID_W), lambda bi, h, m: (h, 0, 0, 0))],
        out_specs=pl.BlockSpec((1, tq, HEAD_DIM), lambda bi, h, m: (bi, m, h)),
        compiler_params=_params(("parallel", "parallel", "arbitrary"), 32),
        name="neighbourhood_attention",
    )(q, k, v, ctx_k, ctx_v, bias_table)


def _out_proj_kernel(a_ref, b_ref, w_ref, x_ref, gate_ref, o_ref):
    ka = a_ref.shape[1]
    acc = jnp.dot(a_ref[...], w_ref[:ka, :], preferred_element_type=F32)
    acc = acc + jnp.dot(b_ref[...], w_ref[ka:, :], preferred_element_type=F32)
    o_ref[...] = x_ref[...] + gate_ref[0] * acc


def out_proj_residual(a, b_, w_bf16, x, gate):
    r, d = x.shape
    ka, kb = a.shape[1], b_.shape[1]
    nseg = gate.shape[0]
    tm, tn = 1024, 1024
    tiles_per_seg = (r // nseg) // tm
    return pl.pallas_call(
        _out_proj_kernel,
        out_shape=jax.ShapeDtypeStruct((r, d), F32),
        grid=(r // tm, d // tn),
        in_specs=[pl.BlockSpec((tm, ka), lambda i, j: (i, 0)),
                  pl.BlockSpec((tm, kb), lambda i, j: (i, 0)),
                  pl.BlockSpec((ka + kb, tn), lambda i, j: (0, j)),
                  pl.BlockSpec((tm, tn), lambda i, j: (i, j)),
                  pl.BlockSpec((1, 1, tn), lambda i, j: (i // tiles_per_seg, 0, j))],
        out_specs=pl.BlockSpec((tm, tn), lambda i, j: (i, j)),
        compiler_params=_params(("parallel", "parallel"), 56),
        name="out_proj_residual",
    )(a, b_, w_bf16, x, gate)


def _norm_router_kernel(x_ref, g_ref, shift_ref, scale_ref, whi_ref, wlo_ref, *out_refs):
    logit_ref = out_refs[-1]
    h = _mod_norm(x_ref[...], g_ref[...], shift_ref[0], scale_ref[0])
    h_hi = h.astype(BF16)
    h_lo = (h - h_hi.astype(F32)).astype(BF16)
    if len(out_refs) > 1:
        out_refs[0][...] = h_hi
    w_hi = whi_ref[...]
    logit_ref[...] = (jnp.dot(h_hi, w_hi, preferred_element_type=F32)
                      + jnp.dot(h_lo, w_hi, preferred_element_type=F32)
                      + jnp.dot(h_hi, wlo_ref[...], preferred_element_type=F32))


def norm_router(x, g, shift, scale, router_w, want_h):
    r, d = x.shape
    nseg = shift.shape[0]
    tm = 512
    tiles_per_seg = (r // nseg) // tm
    w_pad = jnp.zeros((d, LANES), F32).at[:, :N_EXPERTS].set(router_w)
    w_hi = w_pad.astype(BF16)
    w_lo = (w_pad - w_hi.astype(F32)).astype(BF16)
    out_shape = [jax.ShapeDtypeStruct((r, LANES), F32)]
    out_specs = [pl.BlockSpec((tm, LANES), lambda i: (i, 0))]
    if want_h:
        out_shape.insert(0, jax.ShapeDtypeStruct((r, d), BF16))
        out_specs.insert(0, pl.BlockSpec((tm, d), lambda i: (i, 0)))
    return pl.pallas_call(
        _norm_router_kernel,
        out_shape=out_shape,
        grid=(r // tm,),
        in_specs=[pl.BlockSpec((tm, d), lambda i: (i, 0)),
                  pl.BlockSpec((1, d), lambda i: (0, 0)),
                  pl.BlockSpec((1, 1, d), lambda i: (i // tiles_per_seg, 0, 0)),
                  pl.BlockSpec((1, 1, d), lambda i: (i // tiles_per_seg, 0, 0)),
                  pl.BlockSpec((d, LANES), lambda i: (0, 0)),
                  pl.BlockSpec((d, LANES), lambda i: (0, 0))],
        out_specs=out_specs,
        compiler_params=_params(("parallel",), 32),
        name="norm_router",
    )(x, g.reshape(1, d), shift, scale, w_hi, w_lo)


CUMSUM_BLOCK = 256


def _exclusive_cumsum_lanes(flags):
    e, n = flags.shape
    cb = min(CUMSUM_BLOCK, n)
    tri = (lax.broadcasted_iota(jnp.int32, (cb, cb), 0) < lax.broadcasted_iota(jnp.int32, (cb, cb), 1)).astype(BF16)
    carry = jnp.zeros((e, 1), F32)
    out = []
    for j in range(n // cb):
        blk = flags[:, j * cb:(j + 1) * cb]
        out.append(jnp.dot(blk.astype(BF16), tri, preferred_element_type=F32) + carry)
        carry = carry + jnp.sum(blk, axis=-1, keepdims=True)
    return jnp.concatenate(out, axis=1) if len(out) > 1 else out[0]


def _select_kernel(logit_ref, pos_ref, gate_ref, *idx_refs, cap):
    lg = logit_ref[0]
    e, n = lg.shape
    ex = jnp.exp(lg - jnp.max(lg, axis=0, keepdims=True))
    aff = ex / jnp.sum(ex, axis=0, keepdims=True)

    def enough(t):
        return jnp.sum((aff >= t).astype(F32), axis=-1, keepdims=True) >= cap

    hi = jnp.full((e, 1), 2.0, F32)
    for step in (64, 32, 16, 8, 4, 2, 1):
        cand = hi * (2.0 ** -step)
        hi = jnp.where(enough(cand), hi, cand)
    lo = hi * 0.5
    lo = jnp.where(enough(lo), lo, 0.0)
    for _ in range(23):
        mid = (lo + hi) * 0.5
        ok = enough(mid)
        lo = jnp.where(ok, mid, lo)
        hi = jnp.where(ok, hi, mid)
    thr = lo
    above = (aff > thr).astype(F32)
    tied = (aff == thr).astype(F32)
    need = cap - jnp.sum(above, axis=-1, keepdims=True)
    take = above + tied * (_exclusive_cumsum_lanes(tied) < need).astype(F32)
    slot = _exclusive_cumsum_lanes(take)
    pos = jnp.where(take > 0, slot, -1.0).astype(jnp.int32)
    pos_ref[0] = pos
    s_iota = lax.broadcasted_iota(jnp.int32, (cap, n), 0)
    token = lax.broadcasted_iota(jnp.int32, (1, n), 1)
    t_hi = (token // 64).astype(F32)
    t_lo = (token % 64).astype(F32)
    row = lax.broadcasted_iota(jnp.int32, (8, n), 0)
    for ei in range(e):
        a = aff[ei:ei + 1, :]
        a0 = a.astype(BF16).astype(F32)
        a1 = (a - a0).astype(BF16).astype(F32)
        a2 = a - a0 - a1
        vals = jnp.where(row == 0, a0, jnp.where(row == 1, a1, jnp.where(row == 2, a2,
                         jnp.where(row == 3, t_hi, jnp.where(row == 4, t_lo, 0.0)))))
        hit = jnp.where(pos[ei:ei + 1, :] == s_iota, 1.0, 0.0).astype(BF16)
        res = lax.dot_general(vals.astype(BF16), hit, (((1,), (1,)), ((), ())), preferred_element_type=F32)
        gate_ref[0, ei:ei + 1, :] = res[0:1] + res[1:2] + res[2:3]
        if idx_refs:
            idx_refs[0][0, ei:ei + 1, :] = (res[3:4] * 64.0 + res[4:5]).astype(jnp.int32)


def select_tokens(logits_t, cap, want_idx):
    nseg, e, n = logits_t.shape
    slots = (nseg, e, cap)
    out_shape = [jax.ShapeDtypeStruct((nseg, e, n), jnp.int32), jax.ShapeDtypeStruct(slots, F32)]
    out_specs = [pl.BlockSpec((1, e, n), lambda s: (s, 0, 0)), pl.BlockSpec((1, e, cap), lambda s: (s, 0, 0))]
    if want_idx:
        out_shape.append(jax.ShapeDtypeStruct(slots, jnp.int32))
        out_specs.append(pl.BlockSpec((1, e, cap), lambda s: (s, 0, 0)))
    return pl.pallas_call(
        functools.partial(_select_kernel, cap=cap),
        out_shape=out_shape,
        grid=(nseg,),
        in_specs=[pl.BlockSpec((1, e, n), lambda s: (s, 0, 0))],
        out_specs=out_specs,
        compiler_params=_params(("parallel",), 48),
        name="select_tokens",
    )(logits_t)


def _gather_rows_kernel(idx_ref, x_hbm, g_ref, shift_ref, scale_ref, o_ref, buf, sem, *, cap, n, experts):
    step = pl.program_id(0)
    slot = step % 2

    def row_copy(st, sl, k):
        row = (st // experts) * n + idx_ref[st * cap + k]
        return pltpu.make_async_copy(x_hbm.at[pl.ds(row, 1), :], buf.at[sl, pl.ds(k, 1), :], sem.at[sl])

    def issue(st, sl):
        def body(k, carry):
            row_copy(st, sl, k).start()
            return carry
        lax.fori_loop(0, cap, body, 0, unroll=8)

    @pl.when(step == 0)
    def _():
        issue(step, slot)

    @pl.when(step + 1 < pl.num_programs(0))
    def _():
        issue(step + 1, 1 - slot)

    pltpu.make_async_copy(x_hbm.at[pl.ds(0, cap), :], buf.at[slot], sem.at[slot]).wait()
    h = _mod_norm(buf[slot], g_ref[...], shift_ref[0], scale_ref[0])
    o_ref[0, 0] = h.astype(o_ref.dtype)


def gather_rows(idx, x, g, shift, scale, n):
    nseg, e, cap = idx.shape
    d = x.shape[1]
    return pl.pallas_call(
        functools.partial(_gather_rows_kernel, cap=cap, n=n, experts=e),
        out_shape=jax.ShapeDtypeStruct((nseg, e, cap, d), BF16),
        grid_spec=pltpu.PrefetchScalarGridSpec(
            num_scalar_prefetch=1,
            grid=(nseg * e,),
            in_specs=[pl.BlockSpec(memory_space=pl.ANY),
                      pl.BlockSpec((1, d), lambda st, idx_ref: (0, 0)),
                      pl.BlockSpec((1, 1, d), lambda st, idx_ref: (st // e, 0, 0)),
                      pl.BlockSpec((1, 1, d), lambda st, idx_ref: (st // e, 0, 0))],
            out_specs=pl.BlockSpec((1, 1, cap, d), lambda st, idx_ref: (st // e, st % e, 0, 0)),
            scratch_shapes=[pltpu.VMEM((2, cap, d), F32), pltpu.SemaphoreType.DMA((2,))]),
        compiler_params=_params(("arbitrary",), 32, disable_bounds_checks=True),
        name="gather_rows",
    )(idx.reshape(-1), x, g.reshape(1, d), shift, scale)


def _gather_kernel(pos_ref, h_ref, o_ref, acc_ref, *, cap):
    kt = pl.program_id(2)
    eb = pos_ref.shape[1]
    tk = pos_ref.shape[3]
    s_iota = lax.broadcasted_iota(jnp.int32, (cap, tk), 0)
    onehot = jnp.concatenate([(pos_ref[0, i] == s_iota).astype(BF16) for i in range(eb)], axis=0)
    part = jnp.dot(onehot, h_ref[0], preferred_element_type=F32)

    @pl.when(kt == 0)
    def _():
        acc_ref[...] = part

    @pl.when(kt > 0)
    def _():
        acc_ref[...] += part

    @pl.when(kt == pl.num_programs(2) - 1)
    def _():
        o_ref[0] = acc_ref[...].reshape(o_ref.shape[1:]).astype(o_ref.dtype)


def gather_tokens(pos, h, cap, experts_per_step, tk):
    nseg, e, n = pos.shape
    d = h.shape[2]
    eb = experts_per_step
    return pl.pallas_call(
        functools.partial(_gather_kernel, cap=cap),
        out_shape=jax.ShapeDtypeStruct((nseg, e, cap, d), BF16),
        grid=(nseg, e // eb, n // tk),
        in_specs=[pl.BlockSpec((1, eb, 1, tk), lambda s, ei, kt: (s, ei, 0, kt)),
                  pl.BlockSpec((1, tk, d), lambda s, ei, kt: (s, kt, 0))],
        out_specs=pl.BlockSpec((1, eb, cap, d), lambda s, ei, kt: (s, ei, 0, 0)),
        scratch_shapes=[pltpu.VMEM((eb * cap, d), F32)],
        compiler_params=_params(("parallel", "parallel", "arbitrary"), 48),
        name="gather_tokens",
    )(pos.reshape(nseg, e, 1, n), h)


def _expert_up_kernel(xc_ref, xl_ref, wg_ref, wu_ref, o_ref):
    d = xc_ref.shape[-1]
    wg = wg_ref[0, 0].astype(BF16)
    wu = wu_ref[0, 0].astype(BF16)
    row = 0
    for x_ref in (xc_ref, xl_ref):
        rows = x_ref.shape[0] * x_ref.shape[2]
        x = x_ref[...].reshape(rows, d)
        a = jnp.dot(x, wg, preferred_element_type=F32)
        b = jnp.dot(x, wu, preferred_element_type=F32)
        o_ref[0, row:row + rows, :] = (_silu(a) * b).astype(o_ref.dtype)
        row += rows


def expert_up(xg_ctx, xg_lat, w_gate, w_up, layer):
    _, e, d, f = w_gate.shape
    tf = 512
    sc, _, cc, _ = xg_ctx.shape
    sl, _, cl, _ = xg_lat.shape
    rows = sc * cc + sl * cl
    return pl.pallas_call(
        _expert_up_kernel,
        out_shape=jax.ShapeDtypeStruct((e, rows, f), BF16),
        grid=(e, f // tf),
        in_specs=[pl.BlockSpec((sc, 1, cc, d), lambda ei, j: (0, ei, 0, 0)),
                  pl.BlockSpec((sl, 1, cl, d), lambda ei, j: (0, ei, 0, 0)),
                  pl.BlockSpec((1, 1, d, tf), lambda ei, j: (layer, ei, 0, j)),
                  pl.BlockSpec((1, 1, d, tf), lambda ei, j: (layer, ei, 0, j))],
        out_specs=pl.BlockSpec((1, rows, tf), lambda ei, j: (ei, 0, j)),
        compiler_params=_params(("parallel", "arbitrary"), 56),
        name="expert_up",
    )(xg_ctx, xg_lat, w_gate, w_up)


def _expert_down_kernel(h_ref, w_ref, gate_ref, o_ref):
    y = jnp.dot(h_ref[0], w_ref[0, 0].astype(BF16), preferred_element_type=F32)
    o_ref[0] = (y * gate_ref[0]).astype(o_ref.dtype)


def expert_down(hid, w_down, gates, layer):
    e, rows, f = hid.shape
    d = w_down.shape[3]
    td = 512
    return pl.pallas_call(
        _expert_down_kernel,
        out_shape=jax.ShapeDtypeStruct((e, rows, d), BF16),
        grid=(e, d // td),
        in_specs=[pl.BlockSpec((1, rows, f), lambda ei, j: (ei, 0, 0)),
                  pl.BlockSpec((1, 1, f, td), lambda ei, j: (layer, ei, 0, j)),
                  pl.BlockSpec((1, rows, 1), lambda ei, j: (ei, 0, 0))],
        out_specs=pl.BlockSpec((1, rows, td), lambda ei, j: (ei, 0, j)),
        compiler_params=_params(("parallel", "arbitrary"), 48),
        name="expert_down",
    )(hid, w_down, gates)


def _combine_kernel(post_ref, y_ref, x_ref, gate_ref, o_ref, hit_ref, *, cap):
    ne = y_ref.shape[0]
    tt = post_ref.shape[1]

    @pl.when(pl.program_id(2) == 0)
    def _():
        post = post_ref[0]
        width = max(cap, LANES)
        per = width // cap
        lane = lax.broadcasted_iota(jnp.int32, (tt, width), 1)
        for blk in range(ne // per):
            hit = None
            for i in range(per):
                col = post[:, blk * per + i:blk * per + i + 1]
                h = jnp.where(col >= 0, col + i * cap, -1) == lane
                hit = h if hit is None else (hit | h)
            hit_ref[:, blk * width:(blk + 1) * width] = jnp.where(hit, 1.0, 0.0).astype(BF16)

    y = y_ref[...].reshape(ne * cap, y_ref.shape[-1])
    o_ref[...] = x_ref[...] + gate_ref[0] * jnp.dot(hit_ref[...], y, preferred_element_type=F32)


def combine_tokens(pos_t, y, x, gate, cap, row_offset, tt, td):
    nseg, n, e = pos_t.shape
    r, d = x.shape
    nt = n // tt
    rb = row_offset // cap
    return pl.pallas_call(
        functools.partial(_combine_kernel, cap=cap),
        out_shape=jax.ShapeDtypeStruct((r, d), F32),
        grid=(nseg, nt, d // td),
        in_specs=[pl.BlockSpec((1, tt, e), lambda s, t, j: (s, t, 0)),
                  pl.BlockSpec((e, cap, td), lambda s, t, j: (0, rb + s, j)),
                  pl.BlockSpec((tt, td), lambda s, t, j: (s * nt + t, j)),
                  pl.BlockSpec((1, 1, td), lambda s, t, j: (s * gate.shape[0] // nseg, 0, j))],
        out_specs=pl.BlockSpec((tt, td), lambda s, t, j: (s * nt + t, j)),
        scratch_shapes=[pltpu.VMEM((tt, e * cap), BF16)],
        compiler_params=_params(("parallel", "parallel", "arbitrary"), 56),
        name="combine_tokens",
    )(pos_t, y, x, gate)


def moe_layer(groups, g_ffn, router_w, w_gate, w_up, w_down, layer):
    sel = []
    for grp in groups:
        nseg, n = grp["nseg"], grp["n"]
        cap = (EC_CAPACITY * n) // N_EXPERTS
        dense = n <= 512
        routed = norm_router(grp["x"], g_ffn, grp["shift"], grp["scale"], router_w, want_h=dense)
        logits_t = jnp.swapaxes(routed[-1][:, :N_EXPERTS].reshape(nseg, n, N_EXPERTS), 1, 2)
        picked = select_tokens(logits_t, cap, want_idx=not dense)
        pos, gates = picked[0], picked[1]
        if dense:
            xg = gather_tokens(pos, routed[0].reshape(nseg, n, D_MODEL), cap, N_EXPERTS, n)
        else:
            xg = gather_rows(picked[2].reshape(nseg, N_EXPERTS, cap), grp["x"], g_ffn, grp["shift"], grp["scale"], n)
        sel.append(dict(cap=cap, pos=pos, xg=xg,
                        gates=jnp.swapaxes(gates, 0, 1).reshape(N_EXPERTS, nseg * cap, 1)))
    hid = expert_up(sel[0]["xg"], sel[1]["xg"], w_gate, w_up, layer)
    y = expert_down(hid, w_down, jnp.concatenate([s["gates"] for s in sel], axis=1), layer)
    out = []
    row_offset = 0
    for grp, s in zip(groups, sel):
        n, cap = grp["n"], s["cap"]
        small = n <= 512
        out.append(combine_tokens(jnp.swapaxes(s["pos"], 1, 2), y, grp["x"], grp["gate"], cap, row_offset,
                                  n if small else 1024, D_MODEL if small else 512))
        row_offset += grp["nseg"] * cap
    return out


def _rope_tables(n):
    t = jnp.arange(n)
    row = (t // GRID_W).astype(F32)
    col = (t % GRID_W).astype(F32)
    nf = HEAD_DIM // 4
    inv = ROPE_THETA ** (-jnp.arange(nf, dtype=F32) / nf)
    ang = jnp.concatenate([row[:, None] * inv, col[:, None] * inv], axis=-1)
    cos, sin = jnp.cos(ang), jnp.sin(ang)
    return jnp.concatenate([cos, cos], axis=-1), jnp.concatenate([-sin, sin], axis=-1)


def kernel(x_prompt, x_sample, cache_win_k, cache_win_v, cache_nat_k, cache_nat_v, cache_glob_k, cache_glob_v, c, c_ctx, mod_w, mod_b, norm_mix_w, norm_ffn_w, even_w_in, even_w_out, conv_w, conv_b, win_sink, win_q_norm, win_k_norm, odd_w_in, odd_w_out, nat_rpb, nat_q_norm, nat_k_norm, glob_q_norm, glob_k_norm, router_w, expert_w_gate, expert_w_up, expert_w_down):
    bp, sp, d = x_prompt.shape
    bs, ss, _ = x_sample.shape
    depth = mod_w.shape[0]
    rope = _rope_tables(ss)

    cond8 = jnp.zeros((8, d), F32).at[0].set(c_ctx).at[1:1 + bs].set(c)
    mods = modulation(cond8, mod_w, mod_b).reshape(depth, 8, N_MOD, d)

    xp = x_prompt.reshape(bp * sp, d)
    xs = x_sample.reshape(bs * ss, d)
    win_k, win_v, nat_k, nat_v, glob_k, glob_v = [], [], [], [], [], []

    def flat_cache(cache, i):
        return cache[:, i].reshape(bs, cache.shape[2], -1).astype(BF16)

    for layer in range(depth):
        i = layer // 2
        mp = [mods[layer, 0:1, k].reshape(1, 1, d) for k in range(N_MOD)]
        ms = [mods[layer, 1:1 + bs, k].reshape(bs, 1, d) for k in range(N_MOD)]
        if layer % 2 == 0:
            w_in = even_w_in[i].astype(BF16)
            w_out = even_w_out[i].astype(BF16)
            q0 = 3 * CONV_DIM
            qw = WIN_Q_HEADS * HEAD_DIM
            kw = WIN_KV_HEADS * HEAD_DIM
            group = WIN_Q_HEADS // WIN_KV_HEADS
            h = mod_norm(xp, norm_mix_w[layer], mp[0], mp[1])
            ya = conv_gate(project(h, w_in, 0, q0, want_bf16=False, want_f32=True), conv_w[i], conv_b[i], sp)
            q = project(h, w_in, q0, qw, gain=win_q_norm[i], scale=ATTN_SCALE)
            k, k32 = project(h, w_in, q0 + qw, kw, gain=win_k_norm[i], want_f32=True)
            v, v32 = project(h, w_in, q0 + qw + kw, kw, want_f32=True)
            yb = dense_attention(q.reshape(bp, sp, qw), k.reshape(bp, sp, kw), v.reshape(bp, sp, kw),
                                 group, sp, sink=win_sink[i])
            xp = out_proj_residual(ya, yb.reshape(bp * sp, qw), w_out, xp, mp[2])
            win_k.append(k32.reshape(bp, sp, WIN_KV_HEADS, HEAD_DIM))
            win_v.append(v32.reshape(bp, sp, WIN_KV_HEADS, HEAD_DIM))
            h = mod_norm(xs, norm_mix_w[layer], ms[0], ms[1])
            ya = conv_gate(project(h, w_in, 0, q0, want_bf16=False, want_f32=True), conv_w[i], conv_b[i], ss)
            q = project(h, w_in, q0, qw, gain=win_q_norm[i], scale=ATTN_SCALE, rope_tables=rope)
            k = project(h, w_in, q0 + qw, kw, gain=win_k_norm[i], rope_tables=rope)
            v = project(h, w_in, q0 + qw + kw, kw)
            yb = window_attention(q.reshape(bs, ss, qw), k.reshape(bs, ss, kw), v.reshape(bs, ss, kw),
                                  flat_cache(cache_win_k, i), flat_cache(cache_win_v, i), win_sink[i], group)
            xs = out_proj_residual(ya, yb.reshape(bs * ss, qw), w_out, xs, ms[2])
        else:
            w_in = odd_w_in[i].astype(BF16)
            w_out = odd_w_out[i].astype(BF16)
            nw = NAT_HEADS * HEAD_DIM
            gqw = GLOB_Q_HEADS * HEAD_DIM
            gkw = GLOB_KV_HEADS * HEAD_DIM
            group = GLOB_Q_HEADS // GLOB_KV_HEADS
            kd0 = 3 * nw + gqw
            h = mod_norm(xp, norm_mix_w[layer], mp[0], mp[1])
            qc = project(h, w_in, 0, nw, gain=nat_q_norm[i], scale=ATTN_SCALE)
            kc, kc32 = project(h, w_in, nw, nw, gain=nat_k_norm[i], want_f32=True)
            vc, vc32 = project(h, w_in, 2 * nw, nw, want_f32=True)
            qd = project(h, w_in, 3 * nw, gqw, gain=glob_q_norm[i], scale=ATTN_SCALE)
            kd, kd32 = project(h, w_in, kd0, gkw, gain=glob_k_norm[i], want_f32=True)
            vd, vd32 = project(h, w_in, kd0 + gkw, gkw, want_f32=True)
            yc = dense_attention(qc.reshape(bp, sp, nw), kc.reshape(bp, sp, nw), vc.reshape(bp, sp, nw), 1, sp)
            yd = dense_attention(qd.reshape(bp, sp, gqw), kd.reshape(bp, sp, gkw), vd.reshape(bp, sp, gkw), group, sp)
            xp = out_proj_residual(yc.reshape(bp * sp, nw), yd.reshape(bp * sp, gqw), w_out, xp, mp[2])
            nat_k.append(kc32.reshape(bp, sp, NAT_HEADS, HEAD_DIM))
            nat_v.append(vc32.reshape(bp, sp, NAT_HEADS, HEAD_DIM))
            glob_k.append(kd32.reshape(bp, sp, GLOB_KV_HEADS, HEAD_DIM))
            glob_v.append(vd32.reshape(bp, sp, GLOB_KV_HEADS, HEAD_DIM))
            h = mod_norm(xs, norm_mix_w[layer], ms[0], ms[1])
            qc = project(h, w_in, 0, nw, gain=nat_q_norm[i], scale=ATTN_SCALE)
            kc = project(h, w_in, nw, nw, gain=nat_k_norm[i])
            vc = project(h, w_in, 2 * nw, nw)
            qd = project(h, w_in, 3 * nw, gqw, gain=glob_q_norm[i], scale=ATTN_SCALE, rope_tables=rope)
            kd = project(h, w_in, kd0, gkw, gain=glob_k_norm[i], rope_tables=rope)
            vd = project(h, w_in, kd0 + gkw, gkw)
            yc = neighbourhood_attention(qc.reshape(bs, ss, nw), kc.reshape(bs, ss, nw), vc.reshape(bs, ss, nw),
                                         flat_cache(cache_nat_k, i), flat_cache(cache_nat_v, i),
                                         _nat_bias_table(nat_rpb[i]))
            k_all = jnp.concatenate([kd.reshape(bs, ss, gkw), flat_cache(cache_glob_k, i)], axis=1)
            v_all = jnp.concatenate([vd.reshape(bs, ss, gkw), flat_cache(cache_glob_v, i)], axis=1)
            yd = dense_attention(qd.reshape(bs, ss, gqw), k_all, v_all, group, 128)
            xs = out_proj_residual(yc.reshape(bs * ss, nw), yd.reshape(bs * ss, gqw), w_out, xs, ms[2])

        xp, xs = moe_layer(
            [dict(x=xp, shift=mp[3], scale=mp[4], gate=mp[5], nseg=bp, n=sp),
             dict(x=xs, shift=ms[3], scale=ms[4], gate=ms[5], nseg=bs, n=ss)],
            norm_ffn_w[layer], router_w[layer], expert_w_gate, expert_w_up, expert_w_down, layer)

    stack = lambda xs_: jnp.stack(xs_, axis=1)
    return (xp.reshape(bp, sp, d), xs.reshape(bs, ss, d), stack(win_k), stack(win_v), stack(nat_k),
            stack(nat_v), stack(glob_k), stack(glob_v))
```

```python
import functools

import numpy as np
import jax
import jax.numpy as jnp
from jax import lax
from jax.experimental import pallas as pl
from jax.experimental.pallas import tpu as pltpu

D_MODEL = 2048
HEAD_DIM = 128
GRID_W = 64
CONV_DIM = 512
WIN_Q_HEADS = 12
WIN_KV_HEADS = 4
WIN_BLOCK = 128
WINDOW = 128
NAT_HEADS = 8
NAT_ROWS = 8
NAT_COLS = 16
GLOB_Q_HEADS = 8
GLOB_KV_HEADS = 2
ROPE_THETA = 10000.0
N_EXPERTS = 16
EXPERT_FF = 2048
EC_CAPACITY = 2
N_MOD = 6
NORM_EPS = 1e-6
NEG_INF = -1e30
ATTN_SCALE = HEAD_DIM ** -0.5

LANES = 128
MIB = 1024 * 1024
BF16 = jnp.bfloat16
F32 = jnp.float32


def _params(semantics, vmem_mib, **extra):
    return pltpu.CompilerParams(dimension_semantics=semantics, vmem_limit_bytes=vmem_mib * MIB, **extra)


def _silu(x):
    return x * (1.0 / (1.0 + jnp.exp(-x)))


def _modulation_kernel(cond_ref, w_ref, b_ref, o_ref):
    s = _silu(cond_ref[...]).astype(BF16)
    o_ref[0] = jnp.dot(s, w_ref[0].astype(BF16), preferred_element_type=F32) + b_ref[0]


def modulation(cond8, mod_w, mod_b):
    depth, d, n = mod_w.shape
    tn = 1024
    return pl.pallas_call(
        _modulation_kernel,
        out_shape=jax.ShapeDtypeStruct((depth, 8, n), F32),
        grid=(depth, n // tn),
        in_specs=[pl.BlockSpec((8, d), lambda l, j: (0, 0)),
                  pl.BlockSpec((1, d, tn), lambda l, j: (l, 0, j)),
                  pl.BlockSpec((1, 1, tn), lambda l, j: (l, 0, j))],
        out_specs=pl.BlockSpec((1, 8, tn), lambda l, j: (l, 0, j)),
        compiler_params=_params(("parallel", "parallel"), 48),
        name="modulation",
    )(cond8, mod_w, mod_b.reshape(depth, 1, n))


def _mod_norm(x, g, shift, scale):
    ms = jnp.mean(x * x, axis=-1, keepdims=True)
    y = x * lax.rsqrt(ms + NORM_EPS) * g
    return y * (1.0 + scale) + shift


def _mod_norm_kernel(x_ref, g_ref, shift_ref, scale_ref, h_ref):
    h_ref[...] = _mod_norm(x_ref[...], g_ref[...], shift_ref[0], scale_ref[0]).astype(BF16)


def mod_norm(x, g, shift, scale):
    r, d = x.shape
    nseg = shift.shape[0]
    tm = 512
    tiles_per_seg = (r // nseg) // tm
    return pl.pallas_call(
        _mod_norm_kernel,
        out_shape=jax.ShapeDtypeStruct((r, d), BF16),
        grid=(r // tm,),
        in_specs=[pl.BlockSpec((tm, d), lambda i: (i, 0)),
                  pl.BlockSpec((1, d), lambda i: (0, 0)),
                  pl.BlockSpec((1, 1, d), lambda i: (i // tiles_per_seg, 0, 0)),
                  pl.BlockSpec((1, 1, d), lambda i: (i // tiles_per_seg, 0, 0))],
        out_specs=pl.BlockSpec((tm, d), lambda i: (i, 0)),
        compiler_params=_params(("parallel",), 32),
        name="mod_norm",
    )(x, g.reshape(1, d), shift, scale)


def _proj_kernel(*refs, head_norm, rope, scale, want_bf16, want_f32):
    h_ref, w_ref = refs[0], refs[1]
    k = 2
    if head_norm:
        g_ref = refs[k]
        k += 1
    if rope:
        cos_ref, sin_ref = refs[k], refs[k + 1]
        k += 2
    outs = refs[k:]
    acc = jnp.dot(h_ref[...], w_ref[...], preferred_element_type=F32)
    if not head_norm:
        if want_bf16:
            outs[0][...] = acc.astype(BF16)
        if want_f32:
            outs[-1][...] = acc
        return
    ones = jnp.ones((HEAD_DIM, HEAD_DIM), BF16)
    for hd in range(acc.shape[1] // HEAD_DIM):
        sl = slice(hd * HEAD_DIM, (hd + 1) * HEAD_DIM)
        x = acc[:, sl]
        ssq = jnp.dot((x * x).astype(BF16), ones, preferred_element_type=F32)
        y = x * lax.rsqrt(ssq * (1.0 / HEAD_DIM) + NORM_EPS) * g_ref[...]
        if want_f32:
            outs[-1][:, sl] = y
        if rope:
            y = y * cos_ref[...] + pltpu.roll(y, HEAD_DIM // 2, 1) * sin_ref[...]
        outs[0][:, sl] = (y * scale).astype(BF16)


def project(h, w_bf16, col0, width, gain=None, scale=1.0, rope_tables=None, want_bf16=True, want_f32=False):
    r, d = h.shape
    tm = 1024
    tn = min(width, 512)
    cb0 = col0 // tn
    rope = rope_tables is not None
    in_specs = [pl.BlockSpec((tm, d), lambda j, i: (i, 0)),
                pl.BlockSpec((d, tn), lambda j, i: (0, cb0 + j))]
    args = [h, w_bf16]
    if gain is not None:
        in_specs.append(pl.BlockSpec((1, HEAD_DIM), lambda j, i: (0, 0)))
        args.append(gain.reshape(1, HEAD_DIM))
    if rope:
        per = rope_tables[0].shape[0] // tm
        in_specs += [pl.BlockSpec((tm, HEAD_DIM), lambda j, i: (i % per, 0))] * 2
        args += list(rope_tables)
    out_shape, out_specs = [], []
    for want, dt in ((want_bf16, BF16), (want_f32, F32)):
        if want:
            out_shape.append(jax.ShapeDtypeStruct((r, width), dt))
            out_specs.append(pl.BlockSpec((tm, tn), lambda j, i: (i, j)))
    out = pl.pallas_call(
        functools.partial(_proj_kernel, head_norm=gain is not None, rope=rope, scale=scale,
                          want_bf16=want_bf16, want_f32=want_f32),
        out_shape=out_shape,
        grid=(width // tn, r // tm),
        in_specs=in_specs,
        out_specs=out_specs,
        compiler_params=_params(("parallel", "parallel"), 48),
        name="project",
    )(*args)
    return out if len(out) > 1 else out[0]


def _conv_gate_kernel(ab_ref, ac_ref, ah_ref, w_ref, b_ref, o_ref, *, seq):
    u = ac_ref[...] * ah_ref[...]
    rows = u.shape[0]
    pos = lax.broadcasted_iota(jnp.int32, u.shape, 0) % seq
    prev = jnp.where(pos == 0, 0.0, pltpu.roll(u, 1, 0))
    nxt = jnp.where(pos == seq - 1, 0.0, pltpu.roll(u, rows - 1, 0))
    conv = prev * w_ref[0:1, :] + u * w_ref[1:2, :] + nxt * w_ref[2:3, :] + b_ref[...]
    o_ref[...] = (ab_ref[...] * conv).astype(o_ref.dtype)


def conv_gate(proj, conv_w, conv_b, seq):
    r = proj.shape[0]
    tr = 4096
    nc = CONV_DIM // LANES
    return pl.pallas_call(
        functools.partial(_conv_gate_kernel, seq=seq),
        out_shape=jax.ShapeDtypeStruct((r, CONV_DIM), BF16),
        grid=(r // tr, nc),
        in_specs=[pl.BlockSpec((tr, LANES), lambda i, c: (i, c)),
                  pl.BlockSpec((tr, LANES), lambda i, c: (i, nc + c)),
                  pl.BlockSpec((tr, LANES), lambda i, c: (i, 2 * nc + c)),
                  pl.BlockSpec((3, LANES), lambda i, c: (0, c)),
                  pl.BlockSpec((1, LANES), lambda i, c: (0, c))],
        out_specs=pl.BlockSpec((tr, LANES), lambda i, c: (i, c)),
        compiler_params=_params(("parallel", "parallel"), 48),
        name="conv_gate",
    )(proj, proj, proj, conv_w, conv_b.reshape(1, CONV_DIM))


def _dense_attn_kernel(*refs, group, has_sink):
    if has_sink:
        sink_ref, q_ref, k_ref, v_ref, o_ref = refs
    else:
        q_ref, k_ref, v_ref, o_ref = refs
    qb = q_ref.shape[1]
    for hkv in range(k_ref.shape[2] // HEAD_DIM):
        head = lambda g: slice((hkv * group + g) * HEAD_DIM, (hkv * group + g + 1) * HEAD_DIM)
        kv = slice(hkv * HEAD_DIM, (hkv + 1) * HEAD_DIM)
        q = jnp.concatenate([q_ref[0, :, head(g)] for g in range(group)], axis=0)
        s = lax.dot_general(q, k_ref[0, :, kv], (((1,), (1,)), ((), ())), preferred_element_type=F32)
        m = jnp.max(s, axis=-1, keepdims=True)
        if has_sink:
            sk = jnp.concatenate([jnp.full((qb, 1), sink_ref[hkv * group + g], F32) for g in range(group)], axis=0)
            m = jnp.maximum(m, sk)
        p = jnp.exp(s - m)
        den = jnp.sum(p, axis=-1, keepdims=True)
        if has_sink:
            den = den + jnp.exp(sk - m)
        o = jnp.dot(p.astype(BF16), v_ref[0, :, kv], preferred_element_type=F32) / den
        for g in range(group):
            o_ref[0, :, head(g)] = o[g * qb:(g + 1) * qb].astype(o_ref.dtype)


def dense_attention(q, k, v, group, q_block, sink=None):
    b, nq, qw = q.shape
    nk = k.shape[1]
    kvw = k.shape[2]
    in_specs = [pl.BlockSpec((1, q_block, qw), lambda bi, qi: (bi, qi, 0)),
                pl.BlockSpec((1, nk, kvw), lambda bi, qi: (bi, 0, 0)),
                pl.BlockSpec((1, nk, kvw), lambda bi, qi: (bi, 0, 0))]
    args = [q, k, v]
    if sink is not None:
        in_specs = [pl.BlockSpec(memory_space=pltpu.SMEM)] + in_specs
        args = [sink] + args
    return pl.pallas_call(
        functools.partial(_dense_attn_kernel, group=group, has_sink=sink is not None),
        out_shape=jax.ShapeDtypeStruct((b, nq, qw), BF16),
        grid=(b, nq // q_block),
        in_specs=in_specs,
        out_specs=pl.BlockSpec((1, q_block, qw), lambda bi, qi: (bi, qi, 0)),
        compiler_params=_params(("parallel", "parallel"), 56),
        name="dense_attention",
    )(*args)


def _window_attn_kernel(sink_ref, q_ref, kp_ref, kc_ref, kn_ref, vp_ref, vc_ref, vn_ref, ck_ref, cv_ref, o_ref,
                        *, group, n_tokens):
    blk = pl.program_id(1)
    wb = WIN_BLOCK
    rows, cols = group * wb, 3 * wb + ck_ref.shape[1]
    q_pos = blk * wb + lax.broadcasted_iota(jnp.int32, (rows, cols), 0) % wb
    col = lax.broadcasted_iota(jnp.int32, (rows, cols), 1)
    k_pos = (blk - 1) * wb + col
    local_ok = (jnp.abs(k_pos - q_pos) <= WINDOW) & (k_pos >= 0) & (k_pos < n_tokens)
    visible = (col >= 3 * wb) | local_ok
    for hkv in range(kc_ref.shape[2] // HEAD_DIM):
        head = lambda g: slice((hkv * group + g) * HEAD_DIM, (hkv * group + g + 1) * HEAD_DIM)
        kv = slice(hkv * HEAD_DIM, (hkv + 1) * HEAD_DIM)
        q = jnp.concatenate([q_ref[0, :, head(g)] for g in range(group)], axis=0)
        keys = jnp.concatenate([kp_ref[0, :, kv], kc_ref[0, :, kv], kn_ref[0, :, kv], ck_ref[0, :, kv]], axis=0)
        vals = jnp.concatenate([vp_ref[0, :, kv], vc_ref[0, :, kv], vn_ref[0, :, kv], cv_ref[0, :, kv]], axis=0)
        s = lax.dot_general(q, keys, (((1,), (1,)), ((), ())), preferred_element_type=F32)
        s = jnp.where(visible, s, NEG_INF)
        sk = jnp.concatenate([jnp.full((wb, 1), sink_ref[hkv * group + g], F32) for g in range(group)], axis=0)
        m = jnp.maximum(jnp.max(s, axis=-1, keepdims=True), sk)
        p = jnp.exp(s - m)
        den = jnp.sum(p, axis=-1, keepdims=True) + jnp.exp(sk - m)
        o = jnp.dot(p.astype(BF16), vals, preferred_element_type=F32) / den
        for g in range(group):
            o_ref[0, :, head(g)] = o[g * wb:(g + 1) * wb].astype(o_ref.dtype)


def window_attention(q, k, v, ctx_k, ctx_v, sink, group):
    b, n, qw = q.shape
    kvw = k.shape[2]
    nb = n // WIN_BLOCK
    nctx = ctx_k.shape[1]
    prev_map = lambda bi, i: (bi, jnp.maximum(i - 1, 0), 0)
    cur_map = lambda bi, i: (bi, i, 0)
    next_map = lambda bi, i: (bi, jnp.minimum(i + 1, nb - 1), 0)
    ctx_map = lambda bi, i: (bi, 0, 0)
    kv_block = (1, WIN_BLOCK, kvw)
    return pl.pallas_call(
        functools.partial(_window_attn_kernel, group=group, n_tokens=n),
        out_shape=jax.ShapeDtypeStruct((b, n, qw), BF16),
        grid=(b, nb),
        in_specs=[pl.BlockSpec(memory_space=pltpu.SMEM),
                  pl.BlockSpec((1, WIN_BLOCK, qw), cur_map),
                  pl.BlockSpec(kv_block, prev_map), pl.BlockSpec(kv_block, cur_map), pl.BlockSpec(kv_block, next_map),
                  pl.BlockSpec(kv_block, prev_map), pl.BlockSpec(kv_block, cur_map), pl.BlockSpec(kv_block, next_map),
                  pl.BlockSpec((1, nctx, kvw), ctx_map), pl.BlockSpec((1, nctx, kvw), ctx_map)],
        out_specs=pl.BlockSpec((1, WIN_BLOCK, qw), cur_map),
        compiler_params=_params(("parallel", "parallel"), 32),
        name="window_attention",
    )(sink, q, k, k, k, v, v, v, ctx_k, ctx_v)


NAT_QROWS = 8
NAT_KROWS = 16


def _nat_bias_table(rpb):
    w = GRID_W
    c = np.arange(w)[:, None]
    kc = np.arange(w)[None, :]
    ws = np.clip(c - NAT_COLS // 2, 0, w - NAT_COLS)
    col_ok = (kc >= ws) & (kc < ws + NAT_COLS)
    col_idx = np.clip(kc - c + NAT_COLS - 1, 0, 2 * NAT_COLS - 2)
    t = jnp.where(col_ok[None, None], rpb[:, :, col_idx], NEG_INF)
    dead = jnp.full_like(t[:, :1], NEG_INF)
    t = jnp.concatenate([dead, t, dead], axis=1)
    return jnp.concatenate([t[:, :-1], t[:, 1:]], axis=-1)


def _nat_attn_kernel(q_ref, k_ref, v_ref, ck_ref, cv_ref, t_ref, o_ref, *, grid_rows):
    m = pl.program_id(2)
    w = GRID_W
    r0 = m * NAT_QROWS
    kr0 = jnp.clip(r0 - NAT_ROWS // 2, 0, grid_rows - NAT_KROWS)
    tok0 = pl.multiple_of(kr0 * w, 4 * w)
    keys = k_ref[0, pl.ds(tok0, NAT_KROWS * w), :]
    vals = v_ref[0, pl.ds(tok0, NAT_KROWS * w), :]
    q = q_ref[0]
    s_loc = lax.dot_general(q, keys, (((1,), (1,)), ((), ())), preferred_element_type=F32)
    lane_hi = lax.broadcasted_iota(jnp.int32, (w, 2 * w), 1) >= w
    bias_rows = []
    for i in range(NAT_QROWS):
        r = r0 + i
        rs = jnp.clip(r - NAT_ROWS // 2, 0, grid_rows - NAT_ROWS)
        pieces = []
        for jj in range(NAT_KROWS // 2):
            kr = kr0 + 2 * jj
            d = kr - r + NAT_ROWS - 1
            ok_lo = ((kr >= rs) & (kr < rs + NAT_ROWS)).astype(jnp.int32)
            ok_hi = ((kr + 1 >= rs) & (kr + 1 < rs + NAT_ROWS)).astype(jnp.int32)
            piece = t_ref[0, jnp.clip(d, -1, 2 * NAT_ROWS - 2) + 1]
            ok = jnp.where(lane_hi, ok_hi, ok_lo) > 0
            pieces.append(jnp.where(ok, piece, NEG_INF))
        bias_rows.append(jnp.concatenate(pieces, axis=1))
    s_loc = s_loc + jnp.concatenate(bias_rows, axis=0)
    s_ctx = lax.dot_general(q, ck_ref[0], (((1,), (1,)), ((), ())), preferred_element_type=F32)
    mx = jnp.maximum(jnp.max(s_loc, axis=-1, keepdims=True), jnp.max(s_ctx, axis=-1, keepdims=True))
    p_loc = jnp.exp(s_loc - mx)
    p_ctx = jnp.exp(s_ctx - mx)
    den = jnp.sum(p_loc, axis=-1, keepdims=True) + jnp.sum(p_ctx, axis=-1, keepdims=True)
    o = (jnp.dot(p_loc.astype(BF16), vals, preferred_element_type=F32)
         + jnp.dot(p_ctx.astype(BF16), cv_ref[0], preferred_element_type=F32))
    o_ref[0] = (o / den).astype(o_ref.dtype)


def neighbourhood_attention(q, k, v, ctx_k, ctx_v, bias_table):
    b, n, hw = q.shape
    heads = hw // HEAD_DIM
    grid_rows = n // GRID_W
    nctx = ctx_k.shape[1]
    tq = NAT_QROWS * GRID_W
    return pl.pallas_call(
        functools.partial(_nat_attn_kernel, grid_rows=grid_rows),
        out_shape=jax.ShapeDtypeStruct((b, n, hw), BF16),
        grid=(b, heads, grid_rows // NAT_QROWS),
        in_specs=[pl.BlockSpec((1, tq, HEAD_DIM), lambda bi, h, m: (bi, m, h)),
                  pl.BlockSpec((1, n, HEAD_DIM), lambda bi, h, m: (bi, 0, h)),
                  pl.BlockSpec((1, n, HEAD_DIM), lambda bi, h, m: (bi, 0, h)),
                  pl.BlockSpec((1, nctx, HEAD_DIM), lambda bi, h, m: (bi, 0, h)),
                  pl.BlockSpec((1, nctx, HEAD_DIM), lambda bi, h, m: (bi, 0, h)),
                  pl.BlockSpec((1, 2 * NAT_ROWS, GRID_W, 2 * GRID_W), lambda bi, h, m: (h, 0, 0, 0))],
        out_specs=pl.BlockSpec((1, tq, HEAD_DIM), lambda bi, h, m: (bi, m, h)),
        compiler_params=_params(("parallel", "parallel", "arbitrary"), 32),
        name="neighbourhood_attention",
    )(q, k, v, ctx_k, ctx_v, bias_table)


def _out_proj_kernel(a_ref, b_ref, w_ref, x_ref, gate_ref, o_ref):
    ka = a_ref.shape[1]
    acc = jnp.dot(a_ref[...], w_ref[:ka, :], preferred_element_type=F32)
    acc = acc + jnp.dot(b_ref[...], w_ref[ka:, :], preferred_element_type=F32)
    o_ref[...] = x_ref[...] + gate_ref[0] * acc


def out_proj_residual(a, b_, w_bf16, x, gate):
    r, d = x.shape
    ka, kb = a.shape[1], b_.shape[1]
    nseg = gate.shape[0]
    tm, tn = 1024, 1024
    tiles_per_seg = (r // nseg) // tm
    return pl.pallas_call(
        _out_proj_kernel,
        out_shape=jax.ShapeDtypeStruct((r, d), F32),
        grid=(r // tm, d // tn),
        in_specs=[pl.BlockSpec((tm, ka), lambda i, j: (i, 0)),
                  pl.BlockSpec((tm, kb), lambda i, j: (i, 0)),
                  pl.BlockSpec((ka + kb, tn), lambda i, j: (0, j)),
                  pl.BlockSpec((tm, tn), lambda i, j: (i, j)),
                  pl.BlockSpec((1, 1, tn), lambda i, j: (i // tiles_per_seg, 0, j))],
        out_specs=pl.BlockSpec((tm, tn), lambda i, j: (i, j)),
        compiler_params=_params(("parallel", "parallel"), 56),
        name="out_proj_residual",
    )(a, b_, w_bf16, x, gate)


def _norm_router_kernel(x_ref, g_ref, shift_ref, scale_ref, whi_ref, wlo_ref, *out_refs):
    logit_ref = out_refs[-1]
    h = _mod_norm(x_ref[...], g_ref[...], shift_ref[0], scale_ref[0])
    h_hi = h.astype(BF16)
    h_lo = (h - h_hi.astype(F32)).astype(BF16)
    if len(out_refs) > 1:
        out_refs[0][...] = h_hi
    w_hi = whi_ref[...]
    logit_ref[...] = (jnp.dot(h_hi, w_hi, preferred_element_type=F32)
                      + jnp.dot(h_lo, w_hi, preferred_element_type=F32)
                      + jnp.dot(h_hi, wlo_ref[...], preferred_element_type=F32))


def norm_router(x, g, shift, scale, router_w, want_h):
    r, d = x.shape
    nseg = shift.shape[0]
    tm = 512
    tiles_per_seg = (r // nseg) // tm
    w_pad = jnp.zeros((d, LANES), F32).at[:, :N_EXPERTS].set(router_w)
    w_hi = w_pad.astype(BF16)
    w_lo = (w_pad - w_hi.astype(F32)).astype(BF16)
    out_shape = [jax.ShapeDtypeStruct((r, LANES), F32)]
    out_specs = [pl.BlockSpec((tm, LANES), lambda i: (i, 0))]
    if want_h:
        out_shape.insert(0, jax.ShapeDtypeStruct((r, d), BF16))
        out_specs.insert(0, pl.BlockSpec((tm, d), lambda i: (i, 0)))
    return pl.pallas_call(
        _norm_router_kernel,
        out_shape=out_shape,
        grid=(r // tm,),
        in_specs=[pl.BlockSpec((tm, d), lambda i: (i, 0)),
                  pl.BlockSpec((1, d), lambda i: (0, 0)),
                  pl.BlockSpec((1, 1, d), lambda i: (i // tiles_per_seg, 0, 0)),
                  pl.BlockSpec((1, 1, d), lambda i: (i // tiles_per_seg, 0, 0)),
                  pl.BlockSpec((d, LANES), lambda i: (0, 0)),
                  pl.BlockSpec((d, LANES), lambda i: (0, 0))],
        out_specs=out_specs,
        compiler_params=_params(("parallel",), 32),
        name="norm_router",
    )(x, g.reshape(1, d), shift, scale, w_hi, w_lo)


CUMSUM_BLOCK = 256


def _exclusive_cumsum_lanes(flags):
    e, n = flags.shape
    cb = min(CUMSUM_BLOCK, n)
    tri = (lax.broadcasted_iota(jnp.int32, (cb, cb), 0) < lax.broadcasted_iota(jnp.int32, (cb, cb), 1)).astype(BF16)
    carry = jnp.zeros((e, 1), F32)
    out = []
    for j in range(n // cb):
        blk = flags[:, j * cb:(j + 1) * cb]
        out.append(jnp.dot(blk.astype(BF16), tri, preferred_element_type=F32) + carry)
        carry = carry + jnp.sum(blk, axis=-1, keepdims=True)
    return jnp.concatenate(out, axis=1) if len(out) > 1 else out[0]


def _select_kernel(logit_ref, pos_ref, gate_ref, *idx_refs, cap):
    lg = logit_ref[0]
    e, n = lg.shape
    ex = jnp.exp(lg - jnp.max(lg, axis=0, keepdims=True))
    aff = ex / jnp.sum(ex, axis=0, keepdims=True)

    def enough(t):
        return jnp.sum((aff >= t).astype(F32), axis=-1, keepdims=True) >= cap

    hi = jnp.full((e, 1), 2.0, F32)
    for step in (64, 32, 16, 8, 4, 2, 1):
        cand = hi * (2.0 ** -step)
        hi = jnp.where(enough(cand), hi, cand)
    lo = hi * 0.5
    lo = jnp.where(enough(lo), lo, 0.0)
    for _ in range(23):
        mid = (lo + hi) * 0.5
        ok = enough(mid)
        lo = jnp.where(ok, mid, lo)
        hi = jnp.where(ok, hi, mid)
    thr = lo
    above = (aff > thr).astype(F32)
    tied = (aff == thr).astype(F32)
    need = cap - jnp.sum(above, axis=-1, keepdims=True)
    take = above + tied * (_exclusive_cumsum_lanes(tied) < need).astype(F32)
    slot = _exclusive_cumsum_lanes(take)
    pos = jnp.where(take > 0, slot, -1.0).astype(jnp.int32)
    pos_ref[0] = pos
    s_iota = lax.broadcasted_iota(jnp.int32, (cap, n), 0)
    token = lax.broadcasted_iota(jnp.int32, (1, n), 1)
    t_hi = (token // 64).astype(F32)
    t_lo = (token % 64).astype(F32)
    row = lax.broadcasted_iota(jnp.int32, (8, n), 0)
    for ei in range(e):
        a = aff[ei:ei + 1, :]
        a0 = a.astype(BF16).astype(F32)
        a1 = (a - a0).astype(BF16).astype(F32)
        a2 = a - a0 - a1
        vals = jnp.where(row == 0, a0, jnp.where(row == 1, a1, jnp.where(row == 2, a2,
                         jnp.where(row == 3, t_hi, jnp.where(row == 4, t_lo, 0.0)))))
        hit = jnp.where(pos[ei:ei + 1, :] == s_iota, 1.0, 0.0).astype(BF16)
        res = lax.dot_general(vals.astype(BF16), hit, (((1,), (1,)), ((), ())), preferred_element_type=F32)
        gate_ref[0, ei:ei + 1, :] = res[0:1] + res[1:2] + res[2:3]
        if idx_refs:
            idx_refs[0][0, ei:ei + 1, :] = (res[3:4] * 64.0 + res[4:5]).astype(jnp.int32)


def select_tokens(logits_t, cap, want_idx):
    nseg, e, n = logits_t.shape
    slots = (nseg, e, cap)
    out_shape = [jax.ShapeDtypeStruct((nseg, e, n), jnp.int32), jax.ShapeDtypeStruct(slots, F32)]
    out_specs = [pl.BlockSpec((1, e, n), lambda s: (s, 0, 0)), pl.BlockSpec((1, e, cap), lambda s: (s, 0, 0))]
    if want_idx:
        out_shape.append(jax.ShapeDtypeStruct(slots, jnp.int32))
        out_specs.append(pl.BlockSpec((1, e, cap), lambda s: (s, 0, 0)))
    return pl.pallas_call(
        functools.partial(_select_kernel, cap=cap),
        out_shape=out_shape,
        grid=(nseg,),
        in_specs=[pl.BlockSpec((1, e, n), lambda s: (s, 0, 0))],
        out_specs=out_specs,
        compiler_params=_params(("parallel",), 48),
        name="select_tokens",
    )(logits_t)


def _gather_rows_kernel(idx_ref, x_hbm, g_ref, shift_ref, scale_ref, o_ref, buf, sem, *, cap, n, experts):
    step = pl.program_id(0)
    slot = step % 2

    def row_copy(st, sl, k):
        row = (st // experts) * n + idx_ref[st * cap + k]
        return pltpu.make_async_copy(x_hbm.at[pl.ds(row, 1), :], buf.at[sl, pl.ds(k, 1), :], sem.at[sl])

    def issue(st, sl):
        def body(k, carry):
            row_copy(st, sl, k).start()
            return carry
        lax.fori_loop(0, cap, body, 0, unroll=8)

    @pl.when(step == 0)
    def _():
        issue(step, slot)

    @pl.when(step + 1 < pl.num_programs(0))
    def _():
        issue(step + 1, 1 - slot)

    pltpu.make_async_copy(x_hbm.at[pl.ds(0, cap), :], buf.at[slot], sem.at[slot]).wait()
    h = _mod_norm(buf[slot], g_ref[...], shift_ref[0], scale_ref[0])
    o_ref[0, 0] = h.astype(o_ref.dtype)


def gather_rows(idx, x, g, shift, scale, n):
    nseg, e, cap = idx.shape
    d = x.shape[1]
    return pl.pallas_call(
        functools.partial(_gather_rows_kernel, cap=cap, n=n, experts=e),
        out_shape=jax.ShapeDtypeStruct((nseg, e, cap, d), BF16),
        grid_spec=pltpu.PrefetchScalarGridSpec(
            num_scalar_prefetch=1,
            grid=(nseg * e,),
            in_specs=[pl.BlockSpec(memory_space=pl.ANY),
                      pl.BlockSpec((1, d), lambda st, idx_ref: (0, 0)),
                      pl.BlockSpec((1, 1, d), lambda st, idx_ref: (st // e, 0, 0)),
                      pl.BlockSpec((1, 1, d), lambda st, idx_ref: (st // e, 0, 0))],
            out_specs=pl.BlockSpec((1, 1, cap, d), lambda st, idx_ref: (st // e, st % e, 0, 0)),
            scratch_shapes=[pltpu.VMEM((2, cap, d), F32), pltpu.SemaphoreType.DMA((2,))]),
        compiler_params=_params(("arbitrary",), 32, disable_bounds_checks=True),
        name="gather_rows",
    )(idx.reshape(-1), x, g.reshape(1, d), shift, scale)


def _gather_kernel(pos_ref, h_ref, o_ref, acc_ref, *, cap):
    kt = pl.program_id(2)
    eb = pos_ref.shape[1]
    tk = pos_ref.shape[3]
    s_iota = lax.broadcasted_iota(jnp.int32, (cap, tk), 0)
    onehot = jnp.concatenate([(pos_ref[0, i] == s_iota).astype(BF16) for i in range(eb)], axis=0)
    part = jnp.dot(onehot, h_ref[0], preferred_element_type=F32)

    @pl.when(kt == 0)
    def _():
        acc_ref[...] = part

    @pl.when(kt > 0)
    def _():
        acc_ref[...] += part

    @pl.when(kt == pl.num_programs(2) - 1)
    def _():
        o_ref[0] = acc_ref[...].reshape(o_ref.shape[1:]).astype(o_ref.dtype)


def gather_tokens(pos, h, cap, experts_per_step, tk):
    nseg, e, n = pos.shape
    d = h.shape[2]
    eb = experts_per_step
    return pl.pallas_call(
        functools.partial(_gather_kernel, cap=cap),
        out_shape=jax.ShapeDtypeStruct((nseg, e, cap, d), BF16),
        grid=(nseg, e // eb, n // tk),
        in_specs=[pl.BlockSpec((1, eb, 1, tk), lambda s, ei, kt: (s, ei, 0, kt)),
                  pl.BlockSpec((1, tk, d), lambda s, ei, kt: (s, kt, 0))],
        out_specs=pl.BlockSpec((1, eb, cap, d), lambda s, ei, kt: (s, ei, 0, 0)),
        scratch_shapes=[pltpu.VMEM((eb * cap, d), F32)],
        compiler_params=_params(("parallel", "parallel", "arbitrary"), 48),
        name="gather_tokens",
    )(pos.reshape(nseg, e, 1, n), h)


def _expert_up_kernel(xc_ref, xl_ref, wg_ref, wu_ref, o_ref):
    d = xc_ref.shape[-1]
    wg = wg_ref[0, 0].astype(BF16)
    wu = wu_ref[0, 0].astype(BF16)
    row = 0
    for x_ref in (xc_ref, xl_ref):
        rows = x_ref.shape[0] * x_ref.shape[2]
        x = x_ref[...].reshape(rows, d)
        a = jnp.dot(x, wg, preferred_element_type=F32)
        b = jnp.dot(x, wu, preferred_element_type=F32)
        o_ref[0, row:row + rows, :] = (_silu(a) * b).astype(o_ref.dtype)
        row += rows


def expert_up(xg_ctx, xg_lat, w_gate, w_up, layer):
    _, e, d, f = w_gate.shape
    tf = 512
    sc, _, cc, _ = xg_ctx.shape
    sl, _, cl, _ = xg_lat.shape
    rows = sc * cc + sl * cl
    return pl.pallas_call(
        _expert_up_kernel,
        out_shape=jax.ShapeDtypeStruct((e, rows, f), BF16),
        grid=(e, f // tf),
        in_specs=[pl.BlockSpec((sc, 1, cc, d), lambda ei, j: (0, ei, 0, 0)),
                  pl.BlockSpec((sl, 1, cl, d), lambda ei, j: (0, ei, 0, 0)),
                  pl.BlockSpec((1, 1, d, tf), lambda ei, j: (layer, ei, 0, j)),
                  pl.BlockSpec((1, 1, d, tf), lambda ei, j: (layer, ei, 0, j))],
        out_specs=pl.BlockSpec((1, rows, tf), lambda ei, j: (ei, 0, j)),
        compiler_params=_params(("parallel", "arbitrary"), 56),
        name="expert_up",
    )(xg_ctx, xg_lat, w_gate, w_up)


def _expert_down_kernel(h_ref, w_ref, gate_ref, o_ref):
    y = jnp.dot(h_ref[0], w_ref[0, 0].astype(BF16), preferred_element_type=F32)
    o_ref[0] = (y * gate_ref[0]).astype(o_ref.dtype)


def expert_down(hid, w_down, gates, layer):
    e, rows, f = hid.shape
    d = w_down.shape[3]
    td = 512
    return pl.pallas_call(
        _expert_down_kernel,
        out_shape=jax.ShapeDtypeStruct((e, rows, d), BF16),
        grid=(e, d // td),
        in_specs=[pl.BlockSpec((1, rows, f), lambda ei, j: (ei, 0, 0)),
                  pl.BlockSpec((1, 1, f, td), lambda ei, j: (layer, ei, 0, j)),
                  pl.BlockSpec((1, rows, 1), lambda ei, j: (ei, 0, 0))],
        out_specs=pl.BlockSpec((1, rows, td), lambda ei, j: (ei, 0, j)),
        compiler_params=_params(("parallel", "arbitrary"), 48),
        name="expert_down",
    )(hid, w_down, gates)


def _combine_kernel(post_ref, y_ref, x_ref, gate_ref, o_ref, hit_ref, *, cap):
    ne = y_ref.shape[0]
    tt = post_ref.shape[1]

    @pl.when(pl.program_id(2) == 0)
    def _():
        post = post_ref[0]
        width = max(cap, LANES)
        per = width // cap
        lane = lax.broadcasted_iota(jnp.int32, (tt, width), 1)
        for blk in range(ne // per):
            hit = None
            for i in range(per):
                col = post[:, blk * per + i:blk * per + i + 1]
                h = jnp.where(col >= 0, col + i * cap, -1) == lane
                hit = h if hit is None else (hit | h)
            hit_ref[:, blk * width:(blk + 1) * width] = jnp.where(hit, 1.0, 0.0).astype(BF16)

    y = y_ref[...].reshape(ne * cap, y_ref.shape[-1])
    o_ref[...] = x_ref[...] + gate_ref[0] * jnp.dot(hit_ref[...], y, preferred_element_type=F32)


def combine_tokens(pos_t, y, x, gate, cap, row_offset, tt, td):
    nseg, n, e = pos_t.shape
    r, d = x.shape
    nt = n // tt
    rb = row_offset // cap
    return pl.pallas_call(
        functools.partial(_combine_kernel, cap=cap),
        out_shape=jax.ShapeDtypeStruct((r, d), F32),
        grid=(nseg, nt, d // td),
        in_specs=[pl.BlockSpec((1, tt, e), lambda s, t, j: (s, t, 0)),
                  pl.BlockSpec((e, cap, td), lambda s, t, j: (0, rb + s, j)),
                  pl.BlockSpec((tt, td), lambda s, t, j: (s * nt + t, j)),
                  pl.BlockSpec((1, 1, td), lambda s, t, j: (s * gate.shape[0] // nseg, 0, j))],
        out_specs=pl.BlockSpec((tt, td), lambda s, t, j: (s * nt + t, j)),
        scratch_shapes=[pltpu.VMEM((tt, e * cap), BF16)],
        compiler_params=_params(("parallel", "parallel", "arbitrary"), 56),
        name="combine_tokens",
    )(pos_t, y, x, gate)


COMBINE_WINDOW = 256
COMBINE_ALIGN = 64


def _combine_window_kernel(ws_ref, post_ref, *refs, experts):
    y_refs = refs[:experts]
    x_ref, gate_ref, o_ref, hit_ref = refs[experts:]
    s, t = pl.program_id(0), pl.program_id(1)
    tt = post_ref.shape[1]
    win = COMBINE_WINDOW

    @pl.when(pl.program_id(2) == 0)
    def _():
        post = post_ref[0]
        lane = lax.broadcasted_iota(jnp.int32, (tt, win), 1)
        for e in range(experts):
            start = ws_ref[(s * pl.num_programs(1) + t) * experts + e]
            col = post[:, e:e + 1]
            local = jnp.where(col >= 0, col - start, -1)
            hit_ref[:, e * win:(e + 1) * win] = jnp.where(local == lane, 1.0, 0.0).astype(BF16)

    acc = jnp.dot(hit_ref[:, 0:win], y_refs[0][0], preferred_element_type=F32)
    for e in range(1, experts):
        acc = acc + jnp.dot(hit_ref[:, e * win:(e + 1) * win], y_refs[e][0], preferred_element_type=F32)
    o_ref[...] = x_ref[...] + gate_ref[0] * acc


def combine_tokens_windowed(win_start, pos_t, y, x, gate, cap, row_offset, tt, td):
    nseg, n, e = pos_t.shape
    r, d = x.shape
    nt = n // tt

    def y_spec(ei):
        return pl.BlockSpec((pl.Element(1), pl.Element(COMBINE_WINDOW), pl.Element(td)),
                            lambda s, t, j, ws: (
                                ei,
                                pl.multiple_of(row_offset + s * cap + ws[(s * nt + t) * e + ei], COMBINE_ALIGN),
                                j * td))

    return pl.pallas_call(
        functools.partial(_combine_window_kernel, experts=e),
        out_shape=jax.ShapeDtypeStruct((r, d), F32),
        grid_spec=pltpu.PrefetchScalarGridSpec(
            num_scalar_prefetch=1,
            grid=(nseg, nt, d // td),
            in_specs=[pl.BlockSpec((1, tt, e), lambda s, t, j, ws: (s, t, 0))]
                     + [y_spec(ei) for ei in range(e)]
                     + [pl.BlockSpec((tt, td), lambda s, t, j, ws: (s * nt + t, j)),
                        pl.BlockSpec((1, 1, td), lambda s, t, j, ws: (s * gate.shape[0] // nseg, 0, j))],
            out_specs=pl.BlockSpec((tt, td), lambda s, t, j, ws: (s * nt + t, j)),
            scratch_shapes=[pltpu.VMEM((tt, e * COMBINE_WINDOW), BF16)]),
        compiler_params=_params(("parallel", "parallel", "arbitrary"), 56),
        name="combine_tokens_windowed",
    )(win_start.reshape(-1), pos_t, *([y] * e), x, gate)


def _combine_windows(pos, tt):
    nseg, e, n = pos.shape
    taken = (pos >= 0).reshape(nseg, e, n // tt, tt).sum(-1)
    first = jnp.cumsum(taken, axis=-1) - taken
    cap = taken.sum(-1, keepdims=True)
    start = jnp.minimum(first // COMBINE_ALIGN * COMBINE_ALIGN, cap - COMBINE_WINDOW)
    fits = jnp.all(first + taken <= start + COMBINE_WINDOW)
    return jnp.swapaxes(start, 1, 2).astype(jnp.int32), fits


def moe_layer(groups, g_ffn, router_w, w_gate, w_up, w_down, layer):
    sel = []
    for grp in groups:
        nseg, n = grp["nseg"], grp["n"]
        cap = (EC_CAPACITY * n) // N_EXPERTS
        dense = n <= 512
        routed = norm_router(grp["x"], g_ffn, grp["shift"], grp["scale"], router_w, want_h=dense)
        logits_t = jnp.swapaxes(routed[-1][:, :N_EXPERTS].reshape(nseg, n, N_EXPERTS), 1, 2)
        picked = select_tokens(logits_t, cap, want_idx=not dense)
        pos, gates = picked[0], picked[1]
        if dense:
            xg = gather_tokens(pos, routed[0].reshape(nseg, n, D_MODEL), cap, N_EXPERTS, n)
        else:
            xg = gather_rows(picked[2].reshape(nseg, N_EXPERTS, cap), grp["x"], g_ffn, grp["shift"], grp["scale"], n)
        sel.append(dict(cap=cap, pos=pos, xg=xg,
                        gates=jnp.swapaxes(gates, 0, 1).reshape(N_EXPERTS, nseg * cap, 1)))
    hid = expert_up(sel[0]["xg"], sel[1]["xg"], w_gate, w_up, layer)
    y = expert_down(hid, w_down, jnp.concatenate([s["gates"] for s in sel], axis=1), layer)
    out = []
    row_offset = 0
    for grp, s in zip(groups, sel):
        n, cap = grp["n"], s["cap"]
        pos_t = jnp.swapaxes(s["pos"], 1, 2)
        if n <= 512:
            out.append(combine_tokens(pos_t, y, grp["x"], grp["gate"], cap, row_offset, n, D_MODEL))
        else:
            tt, td = 1024, 512
            start, fits = _combine_windows(s["pos"], tt)
            args = (pos_t, y, grp["x"], grp["gate"], cap, row_offset, tt, td)
            out.append(lax.cond(fits, lambda a=args, w=start: combine_tokens_windowed(w, *a),
                                lambda a=args: combine_tokens(*a)))
        row_offset += grp["nseg"] * cap
    return out


def _rope_tables(n):
    t = jnp.arange(n)
    row = (t // GRID_W).astype(F32)
    col = (t % GRID_W).astype(F32)
    nf = HEAD_DIM // 4
    inv = ROPE_THETA ** (-jnp.arange(nf, dtype=F32) / nf)
    ang = jnp.concatenate([row[:, None] * inv, col[:, None] * inv], axis=-1)
    cos, sin = jnp.cos(ang), jnp.sin(ang)
    return jnp.concatenate([cos, cos], axis=-1), jnp.concatenate([-sin, sin], axis=-1)


def kernel(x_prompt, x_sample, cache_win_k, cache_win_v, cache_nat_k, cache_nat_v, cache_glob_k, cache_glob_v, c, c_ctx, mod_w, mod_b, norm_mix_w, norm_ffn_w, even_w_in, even_w_out, conv_w, conv_b, win_sink, win_q_norm, win_k_norm, odd_w_in, odd_w_out, nat_rpb, nat_q_norm, nat_k_norm, glob_q_norm, glob_k_norm, router_w, expert_w_gate, expert_w_up, expert_w_down):
    bp, sp, d = x_prompt.shape
    bs, ss, _ = x_sample.shape
    depth = mod_w.shape[0]
    rope = _rope_tables(ss)

    cond8 = jnp.zeros((8, d), F32).at[0].set(c_ctx).at[1:1 + bs].set(c)
    mods = modulation(cond8, mod_w, mod_b).reshape(depth, 8, N_MOD, d)

    xp = x_prompt.reshape(bp * sp, d)
    xs = x_sample.reshape(bs * ss, d)
    win_k, win_v, nat_k, nat_v, glob_k, glob_v = [], [], [], [], [], []

    def flat_cache(cache, i):
        return cache[:, i].reshape(bs, cache.shape[2], -1).astype(BF16)

    for layer in range(depth):
        i = layer // 2
        mp = [mods[layer, 0:1, k].reshape(1, 1, d) for k in range(N_MOD)]
        ms = [mods[layer, 1:1 + bs, k].reshape(bs, 1, d) for k in range(N_MOD)]
        if layer % 2 == 0:
            w_in = even_w_in[i].astype(BF16)
            w_out = even_w_out[i].astype(BF16)
            q0 = 3 * CONV_DIM
            qw = WIN_Q_HEADS * HEAD_DIM
            kw = WIN_KV_HEADS * HEAD_DIM
            group = WIN_Q_HEADS // WIN_KV_HEADS
            h = mod_norm(xp, norm_mix_w[layer], mp[0], mp[1])
            ya = conv_gate(project(h, w_in, 0, q0, want_bf16=False, want_f32=True), conv_w[i], conv_b[i], sp)
            q = project(h, w_in, q0, qw, gain=win_q_norm[i], scale=ATTN_SCALE)
            k, k32 = project(h, w_in, q0 + qw, kw, gain=win_k_norm[i], want_f32=True)
            v, v32 = project(h, w_in, q0 + qw + kw, kw, want_f32=True)
            yb = dense_attention(q.reshape(bp, sp, qw), k.reshape(bp, sp, kw), v.reshape(bp, sp, kw),
                                 group, sp, sink=win_sink[i])
            xp = out_proj_residual(ya, yb.reshape(bp * sp, qw), w_out, xp, mp[2])
            win_k.append(k32.reshape(bp, sp, WIN_KV_HEADS, HEAD_DIM))
            win_v.append(v32.reshape(bp, sp, WIN_KV_HEADS, HEAD_DIM))
            h = mod_norm(xs, norm_mix_w[layer], ms[0], ms[1])
            ya = conv_gate(project(h, w_in, 0, q0, want_bf16=False, want_f32=True), conv_w[i], conv_b[i], ss)
            q = project(h, w_in, q0, qw, gain=win_q_norm[i], scale=ATTN_SCALE, rope_tables=rope)
            k = project(h, w_in, q0 + qw, kw, gain=win_k_norm[i], rope_tables=rope)
            v = project(h, w_in, q0 + qw + kw, kw)
            yb = window_attention(q.reshape(bs, ss, qw), k.reshape(bs, ss, kw), v.reshape(bs, ss, kw),
                                  flat_cache(cache_win_k, i), flat_cache(cache_win_v, i), win_sink[i], group)
            xs = out_proj_residual(ya, yb.reshape(bs * ss, qw), w_out, xs, ms[2])
        else:
            w_in = odd_w_in[i].astype(BF16)
            w_out = odd_w_out[i].astype(BF16)
            nw = NAT_HEADS * HEAD_DIM
            gqw = GLOB_Q_HEADS * HEAD_DIM
            gkw = GLOB_KV_HEADS * HEAD_DIM
            group = GLOB_Q_HEADS // GLOB_KV_HEADS
            kd0 = 3 * nw + gqw
            h = mod_norm(xp, norm_mix_w[layer], mp[0], mp[1])
            qc = project(h, w_in, 0, nw, gain=nat_q_norm[i], scale=ATTN_SCALE)
            kc, kc32 = project(h, w_in, nw, nw, gain=nat_k_norm[i], want_f32=True)
            vc, vc32 = project(h, w_in, 2 * nw, nw, want_f32=True)
            qd = project(h, w_in, 3 * nw, gqw, gain=glob_q_norm[i], scale=ATTN_SCALE)
            kd, kd32 = project(h, w_in, kd0, gkw, gain=glob_k_norm[i], want_f32=True)
            vd, vd32 = project(h, w_in, kd0 + gkw, gkw, want_f32=True)
            yc = dense_attention(qc.reshape(bp, sp, nw), kc.reshape(bp, sp, nw), vc.reshape(bp, sp, nw), 1, sp)
            yd = dense_attention(qd.reshape(bp, sp, gqw), kd.reshape(bp, sp, gkw), vd.reshape(bp, sp, gkw), group, sp)
            xp = out_proj_residual(yc.reshape(bp * sp, nw), yd.reshape(bp * sp, gqw), w_out, xp, mp[2])
            nat_k.append(kc32.reshape(bp, sp, NAT_HEADS, HEAD_DIM))
            nat_v.append(vc32.reshape(bp, sp, NAT_HEADS, HEAD_DIM))
            glob_k.append(kd32.reshape(bp, sp, GLOB_KV_HEADS, HEAD_DIM))
            glob_v.append(vd32.reshape(bp, sp, GLOB_KV_HEADS, HEAD_DIM))
            h = mod_norm(xs, norm_mix_w[layer], ms[0], ms[1])
            qc = project(h, w_in, 0, nw, gain=nat_q_norm[i], scale=ATTN_SCALE)
            kc = project(h, w_in, nw, nw, gain=nat_k_norm[i])
            vc = project(h, w_in, 2 * nw, nw)
            qd = project(h, w_in, 3 * nw, gqw, gain=glob_q_norm[i], scale=ATTN_SCALE, rope_tables=rope)
            kd = project(h, w_in, kd0, gkw, gain=glob_k_norm[i], rope_tables=rope)
            vd = project(h, w_in, kd0 + gkw, gkw)
            yc = neighbourhood_attention(qc.reshape(bs, ss, nw), kc.reshape(bs, ss, nw), vc.reshape(bs, ss, nw),
                                         flat_cache(cache_nat_k, i), flat_cache(cache_nat_v, i),
                                         _nat_bias_table(nat_rpb[i]))
            k_all = jnp.concatenate([kd.reshape(bs, ss, gkw), flat_cache(cache_glob_k, i)], axis=1)
            v_all = jnp.concatenate([vd.reshape(bs, ss, gkw), flat_cache(cache_glob_v, i)], axis=1)
            yd = dense_attention(qd.reshape(bs, ss, gqw), k_all, v_all, group, 128)
            xs = out_proj_residual(yc.reshape(bs * ss, nw), yd.reshape(bs * ss, gqw), w_out, xs, ms[2])

        xp, xs = moe_layer(
            [dict(x=xp, shift=mp[3], scale=mp[4], gate=mp[5], nseg=bp, n=sp),
             dict(x=xs, shift=ms[3], scale=ms[4], gate=ms[5], nseg=bs, n=ss)],
            norm_ffn_w[layer], router_w[layer], expert_w_gate, expert_w_up, expert_w_down, layer)

    stack = lambda xs_: jnp.stack(xs_, axis=1)
    return (xp.reshape(bp, sp, d), xs.reshape(bs, ss, d), stack(win_k), stack(win_v), stack(nat_k),
            stack(nat_v), stack(glob_k), stack(glob_v))
```

```python
import functools

import numpy as np
import jax
import jax.numpy as jnp
from jax import lax
from jax.experimental import pallas as pl
from jax.experimental.pallas import tpu as pltpu

D_MODEL = 2048
HEAD_DIM = 128
GRID_W = 64
CONV_DIM = 512
WIN_Q_HEADS = 12
WIN_KV_HEADS = 4
WIN_BLOCK = 128
WINDOW = 128
NAT_HEADS = 8
NAT_ROWS = 8
NAT_COLS = 16
GLOB_Q_HEADS = 8
GLOB_KV_HEADS = 2
ROPE_THETA = 10000.0
N_EXPERTS = 16
EXPERT_FF = 2048
EC_CAPACITY = 2
N_MOD = 6
NORM_EPS = 1e-6
NEG_INF = -1e30
ATTN_SCALE = HEAD_DIM ** -0.5

LANES = 128
MIB = 1024 * 1024
BF16 = jnp.bfloat16
F32 = jnp.float32


def _params(semantics, vmem_mib, **extra):
    return pltpu.CompilerParams(dimension_semantics=semantics, vmem_limit_bytes=vmem_mib * MIB, **extra)


def _silu(x):
    return x * (1.0 / (1.0 + jnp.exp(-x)))


def _modulation_kernel(cond_ref, w_ref, b_ref, o_ref):
    s = _silu(cond_ref[...]).astype(BF16)
    o_ref[0] = jnp.dot(s, w_ref[0].astype(BF16), preferred_element_type=F32) + b_ref[0]


def modulation(cond8, mod_w, mod_b):
    depth, d, n = mod_w.shape
    tn = 1024
    return pl.pallas_call(
        _modulation_kernel,
        out_shape=jax.ShapeDtypeStruct((depth, 8, n), F32),
        grid=(depth, n // tn),
        in_specs=[pl.BlockSpec((8, d), lambda l, j: (0, 0)),
                  pl.BlockSpec((1, d, tn), lambda l, j: (l, 0, j)),
                  pl.BlockSpec((1, 1, tn), lambda l, j: (l, 0, j))],
        out_specs=pl.BlockSpec((1, 8, tn), lambda l, j: (l, 0, j)),
        compiler_params=_params(("parallel", "parallel"), 48),
        name="modulation",
    )(cond8, mod_w, mod_b.reshape(depth, 1, n))


def _mod_norm(x, g, shift, scale):
    ms = jnp.mean(x * x, axis=-1, keepdims=True)
    y = x * lax.rsqrt(ms + NORM_EPS) * g
    return y * (1.0 + scale) + shift


def _mod_norm_kernel(x_ref, g_ref, shift_ref, scale_ref, h_ref):
    h_ref[...] = _mod_norm(x_ref[...], g_ref[...], shift_ref[0], scale_ref[0]).astype(BF16)


def mod_norm(x, g, shift, scale):
    r, d = x.shape
    nseg = shift.shape[0]
    tm = 512
    tiles_per_seg = (r // nseg) // tm
    return pl.pallas_call(
        _mod_norm_kernel,
        out_shape=jax.ShapeDtypeStruct((r, d), BF16),
        grid=(r // tm,),
        in_specs=[pl.BlockSpec((tm, d), lambda i: (i, 0)),
                  pl.BlockSpec((1, d), lambda i: (0, 0)),
                  pl.BlockSpec((1, 1, d), lambda i: (i // tiles_per_seg, 0, 0)),
                  pl.BlockSpec((1, 1, d), lambda i: (i // tiles_per_seg, 0, 0))],
        out_specs=pl.BlockSpec((tm, d), lambda i: (i, 0)),
        compiler_params=_params(("parallel",), 32),
        name="mod_norm",
    )(x, g.reshape(1, d), shift, scale)


def _proj_kernel(*refs, head_norm, rope, scale, want_bf16, want_f32):
    h_ref, w_ref = refs[0], refs[1]
    k = 2
    if head_norm:
        g_ref = refs[k]
        k += 1
    if rope:
        cos_ref, sin_ref = refs[k], refs[k + 1]
        k += 2
    outs = refs[k:]
    acc = jnp.dot(h_ref[...], w_ref[...], preferred_element_type=F32)
    if not head_norm:
        if want_bf16:
            outs[0][...] = acc.astype(BF16)
        if want_f32:
            outs[-1][...] = acc
        return
    ones = jnp.ones((HEAD_DIM, HEAD_DIM), BF16)
    for hd in range(acc.shape[1] // HEAD_DIM):
        sl = slice(hd * HEAD_DIM, (hd + 1) * HEAD_DIM)
        x = acc[:, sl]
        ssq = jnp.dot((x * x).astype(BF16), ones, preferred_element_type=F32)
        y = x * lax.rsqrt(ssq * (1.0 / HEAD_DIM) + NORM_EPS) * g_ref[...]
        if want_f32:
            outs[-1][:, sl] = y
        if rope:
            y = y * cos_ref[...] + pltpu.roll(y, HEAD_DIM // 2, 1) * sin_ref[...]
        outs[0][:, sl] = (y * scale).astype(BF16)


def project(h, w_bf16, col0, width, gain=None, scale=1.0, rope_tables=None, want_bf16=True, want_f32=False):
    r, d = h.shape
    tm = 1024
    tn = width
    cb0 = col0 // tn
    rope = rope_tables is not None
    in_specs = [pl.BlockSpec((tm, d), lambda j, i: (i, 0)),
                pl.BlockSpec((d, tn), lambda j, i: (0, cb0 + j))]
    args = [h, w_bf16]
    if gain is not None:
        in_specs.append(pl.BlockSpec((1, HEAD_DIM), lambda j, i: (0, 0)))
        args.append(gain.reshape(1, HEAD_DIM))
    if rope:
        per = rope_tables[0].shape[0] // tm
        in_specs += [pl.BlockSpec((tm, HEAD_DIM), lambda j, i: (i % per, 0))] * 2
        args += list(rope_tables)
    out_shape, out_specs = [], []
    for want, dt in ((want_bf16, BF16), (want_f32, F32)):
        if want:
            out_shape.append(jax.ShapeDtypeStruct((r, width), dt))
            out_specs.append(pl.BlockSpec((tm, tn), lambda j, i: (i, j)))
    out = pl.pallas_call(
        functools.partial(_proj_kernel, head_norm=gain is not None, rope=rope, scale=scale,
                          want_bf16=want_bf16, want_f32=want_f32),
        out_shape=out_shape,
        grid=(width // tn, r // tm),
        in_specs=in_specs,
        out_specs=out_specs,
        compiler_params=_params(("parallel", "parallel"), 56),
        name="project",
    )(*args)
    return out if len(out) > 1 else out[0]


def _conv_gate_kernel(ab_ref, ac_ref, ah_ref, w_ref, b_ref, o_ref, *, seq):
    u = ac_ref[...] * ah_ref[...]
    rows = u.shape[0]
    pos = lax.broadcasted_iota(jnp.int32, u.shape, 0) % seq
    prev = jnp.where(pos == 0, 0.0, pltpu.roll(u, 1, 0))
    nxt = jnp.where(pos == seq - 1, 0.0, pltpu.roll(u, rows - 1, 0))
    conv = prev * w_ref[0:1, :] + u * w_ref[1:2, :] + nxt * w_ref[2:3, :] + b_ref[...]
    o_ref[...] = (ab_ref[...] * conv).astype(o_ref.dtype)


def conv_gate(proj, conv_w, conv_b, seq):
    r = proj.shape[0]
    tr = 4096
    nc = CONV_DIM // LANES
    return pl.pallas_call(
        functools.partial(_conv_gate_kernel, seq=seq),
        out_shape=jax.ShapeDtypeStruct((r, CONV_DIM), BF16),
        grid=(r // tr, nc),
        in_specs=[pl.BlockSpec((tr, LANES), lambda i, c: (i, c)),
                  pl.BlockSpec((tr, LANES), lambda i, c: (i, nc + c)),
                  pl.BlockSpec((tr, LANES), lambda i, c: (i, 2 * nc + c)),
                  pl.BlockSpec((3, LANES), lambda i, c: (0, c)),
                  pl.BlockSpec((1, LANES), lambda i, c: (0, c))],
        out_specs=pl.BlockSpec((tr, LANES), lambda i, c: (i, c)),
        compiler_params=_params(("parallel", "parallel"), 48),
        name="conv_gate",
    )(proj, proj, proj, conv_w, conv_b.reshape(1, CONV_DIM))


def _dense_attn_kernel(*refs, group, has_sink):
    if has_sink:
        sink_ref, q_ref, k_ref, v_ref, o_ref = refs
    else:
        q_ref, k_ref, v_ref, o_ref = refs
    qb = q_ref.shape[1]
    for hkv in range(k_ref.shape[2] // HEAD_DIM):
        head = lambda g: slice((hkv * group + g) * HEAD_DIM, (hkv * group + g + 1) * HEAD_DIM)
        kv = slice(hkv * HEAD_DIM, (hkv + 1) * HEAD_DIM)
        q = jnp.concatenate([q_ref[0, :, head(g)] for g in range(group)], axis=0)
        s = lax.dot_general(q, k_ref[0, :, kv], (((1,), (1,)), ((), ())), preferred_element_type=F32)
        m = jnp.max(s, axis=-1, keepdims=True)
        if has_sink:
            sk = jnp.concatenate([jnp.full((qb, 1), sink_ref[hkv * group + g], F32) for g in range(group)], axis=0)
            m = jnp.maximum(m, sk)
        p = jnp.exp(s - m)
        den = jnp.sum(p, axis=-1, keepdims=True)
        if has_sink:
            den = den + jnp.exp(sk - m)
        o = jnp.dot(p.astype(BF16), v_ref[0, :, kv], preferred_element_type=F32) / den
        for g in range(group):
            o_ref[0, :, head(g)] = o[g * qb:(g + 1) * qb].astype(o_ref.dtype)


def dense_attention(q, k, v, group, q_block, sink=None):
    b, nq, qw = q.shape
    nk = k.shape[1]
    kvw = k.shape[2]
    in_specs = [pl.BlockSpec((1, q_block, qw), lambda bi, qi: (bi, qi, 0)),
                pl.BlockSpec((1, nk, kvw), lambda bi, qi: (bi, 0, 0)),
                pl.BlockSpec((1, nk, kvw), lambda bi, qi: (bi, 0, 0))]
    args = [q, k, v]
    if sink is not None:
        in_specs = [pl.BlockSpec(memory_space=pltpu.SMEM)] + in_specs
        args = [sink] + args
    return pl.pallas_call(
        functools.partial(_dense_attn_kernel, group=group, has_sink=sink is not None),
        out_shape=jax.ShapeDtypeStruct((b, nq, qw), BF16),
        grid=(b, nq // q_block),
        in_specs=in_specs,
        out_specs=pl.BlockSpec((1, q_block, qw), lambda bi, qi: (bi, qi, 0)),
        compiler_params=_params(("parallel", "parallel"), 56),
        name="dense_attention",
    )(*args)


def _window_attn_kernel(sink_ref, q_ref, kp_ref, kc_ref, kn_ref, vp_ref, vc_ref, vn_ref, ck_ref, cv_ref, o_ref,
                        *, group, n_tokens):
    blk = pl.program_id(1)
    wb = WIN_BLOCK
    rows, cols = group * wb, 3 * wb + ck_ref.shape[1]
    q_pos = blk * wb + lax.broadcasted_iota(jnp.int32, (rows, cols), 0) % wb
    col = lax.broadcasted_iota(jnp.int32, (rows, cols), 1)
    k_pos = (blk - 1) * wb + col
    local_ok = (jnp.abs(k_pos - q_pos) <= WINDOW) & (k_pos >= 0) & (k_pos < n_tokens)
    visible = (col >= 3 * wb) | local_ok
    for hkv in range(kc_ref.shape[2] // HEAD_DIM):
        head = lambda g: slice((hkv * group + g) * HEAD_DIM, (hkv * group + g + 1) * HEAD_DIM)
        kv = slice(hkv * HEAD_DIM, (hkv + 1) * HEAD_DIM)
        q = jnp.concatenate([q_ref[0, :, head(g)] for g in range(group)], axis=0)
        keys = jnp.concatenate([kp_ref[0, :, kv], kc_ref[0, :, kv], kn_ref[0, :, kv], ck_ref[0, :, kv]], axis=0)
        vals = jnp.concatenate([vp_ref[0, :, kv], vc_ref[0, :, kv], vn_ref[0, :, kv], cv_ref[0, :, kv]], axis=0)
        s = lax.dot_general(q, keys, (((1,), (1,)), ((), ())), preferred_element_type=F32)
        s = jnp.where(visible, s, NEG_INF)
        sk = jnp.concatenate([jnp.full((wb, 1), sink_ref[hkv * group + g], F32) for g in range(group)], axis=0)
        m = jnp.maximum(jnp.max(s, axis=-1, keepdims=True), sk)
        p = jnp.exp(s - m)
        den = jnp.sum(p, axis=-1, keepdims=True) + jnp.exp(sk - m)
        o = jnp.dot(p.astype(BF16), vals, preferred_element_type=F32) / den
        for g in range(group):
            o_ref[0, :, head(g)] = o[g * wb:(g + 1) * wb].astype(o_ref.dtype)


def window_attention(q, k, v, ctx_k, ctx_v, sink, group):
    b, n, qw = q.shape
    kvw = k.shape[2]
    nb = n // WIN_BLOCK
    nctx = ctx_k.shape[1]
    prev_map = lambda bi, i: (bi, jnp.maximum(i - 1, 0), 0)
    cur_map = lambda bi, i: (bi, i, 0)
    next_map = lambda bi, i: (bi, jnp.minimum(i + 1, nb - 1), 0)
    ctx_map = lambda bi, i: (bi, 0, 0)
    kv_block = (1, WIN_BLOCK, kvw)
    return pl.pallas_call(
        functools.partial(_window_attn_kernel, group=group, n_tokens=n),
        out_shape=jax.ShapeDtypeStruct((b, n, qw), BF16),
        grid=(b, nb),
        in_specs=[pl.BlockSpec(memory_space=pltpu.SMEM),
                  pl.BlockSpec((1, WIN_BLOCK, qw), cur_map),
                  pl.BlockSpec(kv_block, prev_map), pl.BlockSpec(kv_block, cur_map), pl.BlockSpec(kv_block, next_map),
                  pl.BlockSpec(kv_block, prev_map), pl.BlockSpec(kv_block, cur_map), pl.BlockSpec(kv_block, next_map),
                  pl.BlockSpec((1, nctx, kvw), ctx_map), pl.BlockSpec((1, nctx, kvw), ctx_map)],
        out_specs=pl.BlockSpec((1, WIN_BLOCK, qw), cur_map),
        compiler_params=_params(("parallel", "parallel"), 32),
        name="window_attention",
    )(sink, q, k, k, k, v, v, v, ctx_k, ctx_v)


NAT_QROWS = 8
NAT_KROWS = 16


def _nat_bias_table(rpb):
    w = GRID_W
    c = np.arange(w)[:, None]
    kc = np.arange(w)[None, :]
    ws = np.clip(c - NAT_COLS // 2, 0, w - NAT_COLS)
    col_ok = (kc >= ws) & (kc < ws + NAT_COLS)
    col_idx = np.clip(kc - c + NAT_COLS - 1, 0, 2 * NAT_COLS - 2)
    t = jnp.where(col_ok[None, None], rpb[:, :, col_idx], NEG_INF)
    dead = jnp.full_like(t[:, :1], NEG_INF)
    t = jnp.concatenate([dead, t, dead], axis=1)
    return jnp.concatenate([t[:, :-1], t[:, 1:]], axis=-1)


def _nat_attn_kernel(q_ref, k_ref, v_ref, ck_ref, cv_ref, t_ref, o_ref, *, grid_rows):
    m = pl.program_id(2)
    w = GRID_W
    r0 = m * NAT_QROWS
    kr0 = jnp.clip(r0 - NAT_ROWS // 2, 0, grid_rows - NAT_KROWS)
    tok0 = pl.multiple_of(kr0 * w, 4 * w)
    keys = k_ref[0, pl.ds(tok0, NAT_KROWS * w), :]
    vals = v_ref[0, pl.ds(tok0, NAT_KROWS * w), :]
    q = q_ref[0]
    s_loc = lax.dot_general(q, keys, (((1,), (1,)), ((), ())), preferred_element_type=F32)
    lane_hi = lax.broadcasted_iota(jnp.int32, (w, 2 * w), 1) >= w
    bias_rows = []
    for i in range(NAT_QROWS):
        r = r0 + i
        rs = jnp.clip(r - NAT_ROWS // 2, 0, grid_rows - NAT_ROWS)
        pieces = []
        for jj in range(NAT_KROWS // 2):
            kr = kr0 + 2 * jj
            d = kr - r + NAT_ROWS - 1
            ok_lo = ((kr >= rs) & (kr < rs + NAT_ROWS)).astype(jnp.int32)
            ok_hi = ((kr + 1 >= rs) & (kr + 1 < rs + NAT_ROWS)).astype(jnp.int32)
            piece = t_ref[0, jnp.clip(d, -1, 2 * NAT_ROWS - 2) + 1]
            ok = jnp.where(lane_hi, ok_hi, ok_lo) > 0
            pieces.append(jnp.where(ok, piece, NEG_INF))
        bias_rows.append(jnp.concatenate(pieces, axis=1))
    s_loc = s_loc + jnp.concatenate(bias_rows, axis=0)
    s_ctx = lax.dot_general(q, ck_ref[0], (((1,), (1,)), ((), ())), preferred_element_type=F32)
    mx = jnp.maximum(jnp.max(s_loc, axis=-1, keepdims=True), jnp.max(s_ctx, axis=-1, keepdims=True))
    p_loc = jnp.exp(s_loc - mx)
    p_ctx = jnp.exp(s_ctx - mx)
    den = jnp.sum(p_loc, axis=-1, keepdims=True) + jnp.sum(p_ctx, axis=-1, keepdims=True)
    o = (jnp.dot(p_loc.astype(BF16), vals, preferred_element_type=F32)
         + jnp.dot(p_ctx.astype(BF16), cv_ref[0], preferred_element_type=F32))
    o_ref[0] = (o / den).astype(o_ref.dtype)


def neighbourhood_attention(q, k, v, ctx_k, ctx_v, bias_table):
    b, n, hw = q.shape
    heads = hw // HEAD_DIM
    grid_rows = n // GRID_W
    nctx = ctx_k.shape[1]
    tq = NAT_QROWS * GRID_W
    return pl.pallas_call(
        functools.partial(_nat_attn_kernel, grid_rows=grid_rows),
        out_shape=jax.ShapeDtypeStruct((b, n, hw), BF16),
        grid=(b, heads, grid_rows // NAT_QROWS),
        in_specs=[pl.BlockSpec((1, tq, HEAD_DIM), lambda bi, h, m: (bi, m, h)),
                  pl.BlockSpec((1, n, HEAD_DIM), lambda bi, h, m: (bi, 0, h)),
                  pl.BlockSpec((1, n, HEAD_DIM), lambda bi, h, m: (bi, 0, h)),
                  pl.BlockSpec((1, nctx, HEAD_DIM), lambda bi, h, m: (bi, 0, h)),
                  pl.BlockSpec((1, nctx, HEAD_DIM), lambda bi, h, m: (bi, 0, h)),
                  pl.BlockSpec((1, 2 * NAT_ROWS, GRID_W, 2 * GRID_W), lambda bi, h, m: (h, 0, 0, 0))],
        out_specs=pl.BlockSpec((1, tq, HEAD_DIM), lambda bi, h, m: (bi, m, h)),
        compiler_params=_params(("parallel", "parallel", "arbitrary"), 32),
        name="neighbourhood_attention",
    )(q, k, v, ctx_k, ctx_v, bias_table)


def _out_proj_kernel(a_ref, b_ref, w_ref, x_ref, gate_ref, o_ref):
    ka = a_ref.shape[1]
    acc = jnp.dot(a_ref[...], w_ref[:ka, :], preferred_element_type=F32)
    acc = acc + jnp.dot(b_ref[...], w_ref[ka:, :], preferred_element_type=F32)
    o_ref[...] = x_ref[...] + gate_ref[0] * acc


def out_proj_residual(a, b_, w_bf16, x, gate):
    r, d = x.shape
    ka, kb = a.shape[1], b_.shape[1]
    nseg = gate.shape[0]
    tm, tn = 1024, 1024
    tiles_per_seg = (r // nseg) // tm
    return pl.pallas_call(
        _out_proj_kernel,
        out_shape=jax.ShapeDtypeStruct((r, d), F32),
        grid=(r // tm, d // tn),
        in_specs=[pl.BlockSpec((tm, ka), lambda i, j: (i, 0)),
                  pl.BlockSpec((tm, kb), lambda i, j: (i, 0)),
                  pl.BlockSpec((ka + kb, tn), lambda i, j: (0, j)),
                  pl.BlockSpec((tm, tn), lambda i, j: (i, j)),
                  pl.BlockSpec((1, 1, tn), lambda i, j: (i // tiles_per_seg, 0, j))],
        out_specs=pl.BlockSpec((tm, tn), lambda i, j: (i, j)),
        compiler_params=_params(("parallel", "parallel"), 56),
        name="out_proj_residual",
    )(a, b_, w_bf16, x, gate)


def _norm_router_kernel(x_ref, g_ref, shift_ref, scale_ref, whi_ref, wlo_ref, *out_refs):
    logit_ref = out_refs[-1]
    h = _mod_norm(x_ref[...], g_ref[...], shift_ref[0], scale_ref[0])
    h_hi = h.astype(BF16)
    h_lo = (h - h_hi.astype(F32)).astype(BF16)
    if len(out_refs) > 1:
        out_refs[0][...] = h_hi
    w_hi = whi_ref[...]
    logit_ref[...] = (jnp.dot(h_hi, w_hi, preferred_element_type=F32)
                      + jnp.dot(h_lo, w_hi, preferred_element_type=F32)
                      + jnp.dot(h_hi, wlo_ref[...], preferred_element_type=F32))


def norm_router(x, g, shift, scale, router_w, want_h):
    r, d = x.shape
    nseg = shift.shape[0]
    tm = 512
    tiles_per_seg = (r // nseg) // tm
    w_pad = jnp.zeros((d, LANES), F32).at[:, :N_EXPERTS].set(router_w)
    w_hi = w_pad.astype(BF16)
    w_lo = (w_pad - w_hi.astype(F32)).astype(BF16)
    out_shape = [jax.ShapeDtypeStruct((r, LANES), F32)]
    out_specs = [pl.BlockSpec((tm, LANES), lambda i: (i, 0))]
    if want_h:
        out_shape.insert(0, jax.ShapeDtypeStruct((r, d), BF16))
        out_specs.insert(0, pl.BlockSpec((tm, d), lambda i: (i, 0)))
    return pl.pallas_call(
        _norm_router_kernel,
        out_shape=out_shape,
        grid=(r // tm,),
        in_specs=[pl.BlockSpec((tm, d), lambda i: (i, 0)),
                  pl.BlockSpec((1, d), lambda i: (0, 0)),
                  pl.BlockSpec((1, 1, d), lambda i: (i // tiles_per_seg, 0, 0)),
                  pl.BlockSpec((1, 1, d), lambda i: (i // tiles_per_seg, 0, 0)),
                  pl.BlockSpec((d, LANES), lambda i: (0, 0)),
                  pl.BlockSpec((d, LANES), lambda i: (0, 0))],
        out_specs=out_specs,
        compiler_params=_params(("parallel",), 32),
        name="norm_router",
    )(x, g.reshape(1, d), shift, scale, w_hi, w_lo)


CUMSUM_BLOCK = 256


def _exclusive_cumsum_lanes(flags):
    e, n = flags.shape
    cb = min(CUMSUM_BLOCK, n)
    tri = (lax.broadcasted_iota(jnp.int32, (cb, cb), 0) < lax.broadcasted_iota(jnp.int32, (cb, cb), 1)).astype(BF16)
    carry = jnp.zeros((e, 1), F32)
    out = []
    for j in range(n // cb):
        blk = flags[:, j * cb:(j + 1) * cb]
        out.append(jnp.dot(blk.astype(BF16), tri, preferred_element_type=F32) + carry)
        carry = carry + jnp.sum(blk, axis=-1, keepdims=True)
    return jnp.concatenate(out, axis=1) if len(out) > 1 else out[0]


def _select_kernel(logit_ref, pos_ref, gate_ref, *idx_refs, cap):
    lg = logit_ref[0]
    e, n = lg.shape
    ex = jnp.exp(lg - jnp.max(lg, axis=0, keepdims=True))
    aff = ex / jnp.sum(ex, axis=0, keepdims=True)

    def enough(t):
        return jnp.sum((aff >= t).astype(F32), axis=-1, keepdims=True) >= cap

    hi = jnp.full((e, 1), 2.0, F32)
    for step in (64, 32, 16, 8, 4, 2, 1):
        cand = hi * (2.0 ** -step)
        hi = jnp.where(enough(cand), hi, cand)
    lo = hi * 0.5
    lo = jnp.where(enough(lo), lo, 0.0)
    for _ in range(23):
        mid = (lo + hi) * 0.5
        ok = enough(mid)
        lo = jnp.where(ok, mid, lo)
        hi = jnp.where(ok, hi, mid)
    thr = lo
    above = (aff > thr).astype(F32)
    tied = (aff == thr).astype(F32)
    need = cap - jnp.sum(above, axis=-1, keepdims=True)
    take = above + tied * (_exclusive_cumsum_lanes(tied) < need).astype(F32)
    slot = _exclusive_cumsum_lanes(take)
    pos = jnp.where(take > 0, slot, -1.0).astype(jnp.int32)
    pos_ref[0] = pos
    s_iota = lax.broadcasted_iota(jnp.int32, (cap, n), 0)
    token = lax.broadcasted_iota(jnp.int32, (1, n), 1)
    t_hi = (token // 64).astype(F32)
    t_lo = (token % 64).astype(F32)
    row = lax.broadcasted_iota(jnp.int32, (8, n), 0)
    for ei in range(e):
        a = aff[ei:ei + 1, :]
        a0 = a.astype(BF16).astype(F32)
        a1 = (a - a0).astype(BF16).astype(F32)
        a2 = a - a0 - a1
        vals = jnp.where(row == 0, a0, jnp.where(row == 1, a1, jnp.where(row == 2, a2,
                         jnp.where(row == 3, t_hi, jnp.where(row == 4, t_lo, 0.0)))))
        hit = jnp.where(pos[ei:ei + 1, :] == s_iota, 1.0, 0.0).astype(BF16)
        res = lax.dot_general(vals.astype(BF16), hit, (((1,), (1,)), ((), ())), preferred_element_type=F32)
        gate_ref[0, ei:ei + 1, :] = res[0:1] + res[1:2] + res[2:3]
        if idx_refs:
            idx_refs[0][0, ei:ei + 1, :] = (res[3:4] * 64.0 + res[4:5]).astype(jnp.int32)


def select_tokens(logits_t, cap, want_idx):
    nseg, e, n = logits_t.shape
    slots = (nseg, e, cap)
    out_shape = [jax.ShapeDtypeStruct((nseg, e, n), jnp.int32), jax.ShapeDtypeStruct(slots, F32)]
    out_specs = [pl.BlockSpec((1, e, n), lambda s: (s, 0, 0)), pl.BlockSpec((1, e, cap), lambda s: (s, 0, 0))]
    if want_idx:
        out_shape.append(jax.ShapeDtypeStruct(slots, jnp.int32))
        out_specs.append(pl.BlockSpec((1, e, cap), lambda s: (s, 0, 0)))
    return pl.pallas_call(
        functools.partial(_select_kernel, cap=cap),
        out_shape=out_shape,
        grid=(nseg,),
        in_specs=[pl.BlockSpec((1, e, n), lambda s: (s, 0, 0))],
        out_specs=out_specs,
        compiler_params=_params(("parallel",), 48),
        name="select_tokens",
    )(logits_t)


def _gather_rows_kernel(idx_ref, x_hbm, g_ref, shift_ref, scale_ref, o_ref, buf, sem, *, cap, n, experts):
    step = pl.program_id(0)
    slot = step % 2

    def row_copy(st, sl, k):
        row = (st // experts) * n + idx_ref[st * cap + k]
        return pltpu.make_async_copy(x_hbm.at[pl.ds(row, 1), :], buf.at[sl, pl.ds(k, 1), :], sem.at[sl])

    def issue(st, sl):
        def body(k, carry):
            row_copy(st, sl, k).start()
            return carry
        lax.fori_loop(0, cap, body, 0, unroll=8)

    @pl.when(step == 0)
    def _():
        issue(step, slot)

    @pl.when(step + 1 < pl.num_programs(0))
    def _():
        issue(step + 1, 1 - slot)

    pltpu.make_async_copy(x_hbm.at[pl.ds(0, cap), :], buf.at[slot], sem.at[slot]).wait()
    h = _mod_norm(buf[slot], g_ref[...], shift_ref[0], scale_ref[0])
    o_ref[0, 0] = h.astype(o_ref.dtype)


def gather_rows(idx, x, g, shift, scale, n):
    nseg, e, cap = idx.shape
    d = x.shape[1]
    return pl.pallas_call(
        functools.partial(_gather_rows_kernel, cap=cap, n=n, experts=e),
        out_shape=jax.ShapeDtypeStruct((nseg, e, cap, d), BF16),
        grid_spec=pltpu.PrefetchScalarGridSpec(
            num_scalar_prefetch=1,
            grid=(nseg * e,),
            in_specs=[pl.BlockSpec(memory_space=pl.ANY),
                      pl.BlockSpec((1, d), lambda st, idx_ref: (0, 0)),
                      pl.BlockSpec((1, 1, d), lambda st, idx_ref: (st // e, 0, 0)),
                      pl.BlockSpec((1, 1, d), lambda st, idx_ref: (st // e, 0, 0))],
            out_specs=pl.BlockSpec((1, 1, cap, d), lambda st, idx_ref: (st // e, st % e, 0, 0)),
            scratch_shapes=[pltpu.VMEM((2, cap, d), F32), pltpu.SemaphoreType.DMA((2,))]),
        compiler_params=_params(("arbitrary",), 32, disable_bounds_checks=True),
        name="gather_rows",
    )(idx.reshape(-1), x, g.reshape(1, d), shift, scale)


def _gather_kernel(pos_ref, h_ref, o_ref, acc_ref, *, cap):
    kt = pl.program_id(2)
    eb = pos_ref.shape[1]
    tk = pos_ref.shape[3]
    s_iota = lax.broadcasted_iota(jnp.int32, (cap, tk), 0)
    onehot = jnp.concatenate([(pos_ref[0, i] == s_iota).astype(BF16) for i in range(eb)], axis=0)
    part = jnp.dot(onehot, h_ref[0], preferred_element_type=F32)

    @pl.when(kt == 0)
    def _():
        acc_ref[...] = part

    @pl.when(kt > 0)
    def _():
        acc_ref[...] += part

    @pl.when(kt == pl.num_programs(2) - 1)
    def _():
        o_ref[0] = acc_ref[...].reshape(o_ref.shape[1:]).astype(o_ref.dtype)


def gather_tokens(pos, h, cap, experts_per_step, tk):
    nseg, e, n = pos.shape
    d = h.shape[2]
    eb = experts_per_step
    return pl.pallas_call(
        functools.partial(_gather_kernel, cap=cap),
        out_shape=jax.ShapeDtypeStruct((nseg, e, cap, d), BF16),
        grid=(nseg, e // eb, n // tk),
        in_specs=[pl.BlockSpec((1, eb, 1, tk), lambda s, ei, kt: (s, ei, 0, kt)),
                  pl.BlockSpec((1, tk, d), lambda s, ei, kt: (s, kt, 0))],
        out_specs=pl.BlockSpec((1, eb, cap, d), lambda s, ei, kt: (s, ei, 0, 0)),
        scratch_shapes=[pltpu.VMEM((eb * cap, d), F32)],
        compiler_params=_params(("parallel", "parallel", "arbitrary"), 48),
        name="gather_tokens",
    )(pos.reshape(nseg, e, 1, n), h)


def _expert_up_kernel(xc_ref, xl_ref, wg_ref, wu_ref, o_ref):
    d = xc_ref.shape[-1]
    wg = wg_ref[0, 0].astype(BF16)
    wu = wu_ref[0, 0].astype(BF16)
    row = 0
    for x_ref in (xc_ref, xl_ref):
        rows = x_ref.shape[0] * x_ref.shape[2]
        x = x_ref[...].reshape(rows, d)
        a = jnp.dot(x, wg, preferred_element_type=F32)
        b = jnp.dot(x, wu, preferred_element_type=F32)
        o_ref[0, row:row + rows, :] = (_silu(a) * b).astype(o_ref.dtype)
        row += rows


def expert_up(xg_ctx, xg_lat, w_gate, w_up, layer):
    _, e, d, f = w_gate.shape
    tf = 512
    sc, _, cc, _ = xg_ctx.shape
    sl, _, cl, _ = xg_lat.shape
    rows = sc * cc + sl * cl
    return pl.pallas_call(
        _expert_up_kernel,
        out_shape=jax.ShapeDtypeStruct((e, rows, f), BF16),
        grid=(e, f // tf),
        in_specs=[pl.BlockSpec((sc, 1, cc, d), lambda ei, j: (0, ei, 0, 0)),
                  pl.BlockSpec((sl, 1, cl, d), lambda ei, j: (0, ei, 0, 0)),
                  pl.BlockSpec((1, 1, d, tf), lambda ei, j: (layer, ei, 0, j)),
                  pl.BlockSpec((1, 1, d, tf), lambda ei, j: (layer, ei, 0, j))],
        out_specs=pl.BlockSpec((1, rows, tf), lambda ei, j: (ei, 0, j)),
        compiler_params=_params(("parallel", "arbitrary"), 56),
        name="expert_up",
    )(xg_ctx, xg_lat, w_gate, w_up)


def _expert_down_kernel(h_ref, w_ref, gate_ref, o_ref):
    y = jnp.dot(h_ref[0], w_ref[0, 0].astype(BF16), preferred_element_type=F32)
    o_ref[0] = (y * gate_ref[0]).astype(o_ref.dtype)


def expert_down(hid, w_down, gates, layer):
    e, rows, f = hid.shape
    d = w_down.shape[3]
    td = 512
    return pl.pallas_call(
        _expert_down_kernel,
        out_shape=jax.ShapeDtypeStruct((e, rows, d), BF16),
        grid=(e, d // td),
        in_specs=[pl.BlockSpec((1, rows, f), lambda ei, j: (ei, 0, 0)),
                  pl.BlockSpec((1, 1, f, td), lambda ei, j: (layer, ei, 0, j)),
                  pl.BlockSpec((1, rows, 1), lambda ei, j: (ei, 0, 0))],
        out_specs=pl.BlockSpec((1, rows, td), lambda ei, j: (ei, 0, j)),
        compiler_params=_params(("parallel", "arbitrary"), 48),
        name="expert_down",
    )(hid, w_down, gates)


def _combine_kernel(post_ref, y_ref, x_ref, gate_ref, o_ref, hit_ref, *, cap):
    ne = y_ref.shape[0]
    tt = post_ref.shape[1]

    @pl.when(pl.program_id(2) == 0)
    def _():
        post = post_ref[0]
        width = max(cap, LANES)
        per = width // cap
        lane = lax.broadcasted_iota(jnp.int32, (tt, width), 1)
        for blk in range(ne // per):
            hit = None
            for i in range(per):
                col = post[:, blk * per + i:blk * per + i + 1]
                h = jnp.where(col >= 0, col + i * cap, -1) == lane
                hit = h if hit is None else (hit | h)
            hit_ref[:, blk * width:(blk + 1) * width] = jnp.where(hit, 1.0, 0.0).astype(BF16)

    y = y_ref[...].reshape(ne * cap, y_ref.shape[-1])
    o_ref[...] = x_ref[...] + gate_ref[0] * jnp.dot(hit_ref[...], y, preferred_element_type=F32)


def combine_tokens(pos_t, y, x, gate, cap, row_offset, tt, td):
    nseg, n, e = pos_t.shape
    r, d = x.shape
    nt = n // tt
    rb = row_offset // cap
    return pl.pallas_call(
        functools.partial(_combine_kernel, cap=cap),
        out_shape=jax.ShapeDtypeStruct((r, d), F32),
        grid=(nseg, nt, d // td),
        in_specs=[pl.BlockSpec((1, tt, e), lambda s, t, j: (s, t, 0)),
                  pl.BlockSpec((e, cap, td), lambda s, t, j: (0, rb + s, j)),
                  pl.BlockSpec((tt, td), lambda s, t, j: (s * nt + t, j)),
                  pl.BlockSpec((1, 1, td), lambda s, t, j: (s * gate.shape[0] // nseg, 0, j))],
        out_specs=pl.BlockSpec((tt, td), lambda s, t, j: (s * nt + t, j)),
        scratch_shapes=[pltpu.VMEM((tt, e * cap), BF16)],
        compiler_params=_params(("parallel", "parallel", "arbitrary"), 56),
        name="combine_tokens",
    )(pos_t, y, x, gate)


COMBINE_WINDOW = 256
COMBINE_ALIGN = 64


def _combine_window_kernel(ws_ref, post_ref, *refs, experts):
    y_refs = refs[:experts]
    x_ref, gate_ref, o_ref, hit_ref = refs[experts:]
    s, t = pl.program_id(0), pl.program_id(1)
    tt = post_ref.shape[1]
    win = COMBINE_WINDOW

    @pl.when(pl.program_id(2) == 0)
    def _():
        post = post_ref[0]
        lane = lax.broadcasted_iota(jnp.int32, (tt, win), 1)
        for e in range(experts):
            start = ws_ref[(s * pl.num_programs(1) + t) * experts + e]
            col = post[:, e:e + 1]
            local = jnp.where(col >= 0, col - start, -1)
            hit_ref[:, e * win:(e + 1) * win] = jnp.where(local == lane, 1.0, 0.0).astype(BF16)

    acc = jnp.dot(hit_ref[:, 0:win], y_refs[0][0], preferred_element_type=F32)
    for e in range(1, experts):
        acc = acc + jnp.dot(hit_ref[:, e * win:(e + 1) * win], y_refs[e][0], preferred_element_type=F32)
    o_ref[...] = x_ref[...] + gate_ref[0] * acc


def combine_tokens_windowed(win_start, pos_t, y, x, gate, cap, row_offset, tt, td):
    nseg, n, e = pos_t.shape
    r, d = x.shape
    nt = n // tt

    def y_spec(ei):
        return pl.BlockSpec((pl.Element(1), pl.Element(COMBINE_WINDOW), pl.Element(td)),
                            lambda s, t, j, ws: (
                                ei,
                                pl.multiple_of(row_offset + s * cap + ws[(s * nt + t) * e + ei], COMBINE_ALIGN),
                                j * td))

    return pl.pallas_call(
        functools.partial(_combine_window_kernel, experts=e),
        out_shape=jax.ShapeDtypeStruct((r, d), F32),
        grid_spec=pltpu.PrefetchScalarGridSpec(
            num_scalar_prefetch=1,
            grid=(nseg, nt, d // td),
            in_specs=[pl.BlockSpec((1, tt, e), lambda s, t, j, ws: (s, t, 0))]
                     + [y_spec(ei) for ei in range(e)]
                     + [pl.BlockSpec((tt, td), lambda s, t, j, ws: (s * nt + t, j)),
                        pl.BlockSpec((1, 1, td), lambda s, t, j, ws: (s * gate.shape[0] // nseg, 0, j))],
            out_specs=pl.BlockSpec((tt, td), lambda s, t, j, ws: (s * nt + t, j)),
            scratch_shapes=[pltpu.VMEM((tt, e * COMBINE_WINDOW), BF16)]),
        compiler_params=_params(("parallel", "parallel", "arbitrary"), 56),
        name="combine_tokens_windowed",
    )(win_start.reshape(-1), pos_t, *([y] * e), x, gate)


def _combine_windows(pos, tt):
    nseg, e, n = pos.shape
    taken = (pos >= 0).reshape(nseg, e, n // tt, tt).sum(-1)
    first = jnp.cumsum(taken, axis=-1) - taken
    cap = taken.sum(-1, keepdims=True)
    start = jnp.minimum(first // COMBINE_ALIGN * COMBINE_ALIGN, cap - COMBINE_WINDOW)
    fits = jnp.all(first + taken <= start + COMBINE_WINDOW)
    return jnp.swapaxes(start, 1, 2).astype(jnp.int32), fits


def moe_layer(groups, g_ffn, router_w, w_gate, w_up, w_down, layer):
    sel = []
    for grp in groups:
        nseg, n = grp["nseg"], grp["n"]
        cap = (EC_CAPACITY * n) // N_EXPERTS
        dense = n <= 512
        routed = norm_router(grp["x"], g_ffn, grp["shift"], grp["scale"], router_w, want_h=dense)
        logits_t = jnp.swapaxes(routed[-1][:, :N_EXPERTS].reshape(nseg, n, N_EXPERTS), 1, 2)
        picked = select_tokens(logits_t, cap, want_idx=not dense)
        pos, gates = picked[0], picked[1]
        if dense:
            xg = gather_tokens(pos, routed[0].reshape(nseg, n, D_MODEL), cap, N_EXPERTS, n)
        else:
            xg = gather_rows(picked[2].reshape(nseg, N_EXPERTS, cap), grp["x"], g_ffn, grp["shift"], grp["scale"], n)
        sel.append(dict(cap=cap, pos=pos, xg=xg,
                        gates=jnp.swapaxes(gates, 0, 1).reshape(N_EXPERTS, nseg * cap, 1)))
    hid = expert_up(sel[0]["xg"], sel[1]["xg"], w_gate, w_up, layer)
    y = expert_down(hid, w_down, jnp.concatenate([s["gates"] for s in sel], axis=1), layer)
    out = []
    row_offset = 0
    for grp, s in zip(groups, sel):
        n, cap = grp["n"], s["cap"]
        pos_t = jnp.swapaxes(s["pos"], 1, 2)
        if n <= 512:
            out.append(combine_tokens(pos_t, y, grp["x"], grp["gate"], cap, row_offset, n, D_MODEL))
        else:
            tt, td = 1024, 512
            start, fits = _combine_windows(s["pos"], tt)
            args = (pos_t, y, grp["x"], grp["gate"], cap, row_offset, tt, td)
            out.append(lax.cond(fits, lambda a=args, w=start: combine_tokens_windowed(w, *a),
                                lambda a=args: combine_tokens(*a)))
        row_offset += grp["nseg"] * cap
    return out


def _rope_tables(n):
    t = jnp.arange(n)
    row = (t // GRID_W).astype(F32)
    col = (t % GRID_W).astype(F32)
    nf = HEAD_DIM // 4
    inv = ROPE_THETA ** (-jnp.arange(nf, dtype=F32) / nf)
    ang = jnp.concatenate([row[:, None] * inv, col[:, None] * inv], axis=-1)
    cos, sin = jnp.cos(ang), jnp.sin(ang)
    return jnp.concatenate([cos, cos], axis=-1), jnp.concatenate([-sin, sin], axis=-1)


def kernel(x_prompt, x_sample, cache_win_k, cache_win_v, cache_nat_k, cache_nat_v, cache_glob_k, cache_glob_v, c, c_ctx, mod_w, mod_b, norm_mix_w, norm_ffn_w, even_w_in, even_w_out, conv_w, conv_b, win_sink, win_q_norm, win_k_norm, odd_w_in, odd_w_out, nat_rpb, nat_q_norm, nat_k_norm, glob_q_norm, glob_k_norm, router_w, expert_w_gate, expert_w_up, expert_w_down):
    bp, sp, d = x_prompt.shape
    bs, ss, _ = x_sample.shape
    depth = mod_w.shape[0]
    rope = _rope_tables(ss)

    cond8 = jnp.zeros((8, d), F32).at[0].set(c_ctx).at[1:1 + bs].set(c)
    mods = modulation(cond8, mod_w, mod_b).reshape(depth, 8, N_MOD, d)

    xp = x_prompt.reshape(bp * sp, d)
    xs = x_sample.reshape(bs * ss, d)
    win_k, win_v, nat_k, nat_v, glob_k, glob_v = [], [], [], [], [], []

    def flat_cache(cache, i):
        return cache[:, i].reshape(bs, cache.shape[2], -1).astype(BF16)

    for layer in range(depth):
        i = layer // 2
        mp = [mods[layer, 0:1, k].reshape(1, 1, d) for k in range(N_MOD)]
        ms = [mods[layer, 1:1 + bs, k].reshape(bs, 1, d) for k in range(N_MOD)]
        if layer % 2 == 0:
            w_in = even_w_in[i].astype(BF16)
            w_out = even_w_out[i].astype(BF16)
            q0 = 3 * CONV_DIM
            qw = WIN_Q_HEADS * HEAD_DIM
            kw = WIN_KV_HEADS * HEAD_DIM
            group = WIN_Q_HEADS // WIN_KV_HEADS
            h = mod_norm(xp, norm_mix_w[layer], mp[0], mp[1])
            ya = conv_gate(project(h, w_in, 0, q0, want_bf16=False, want_f32=True), conv_w[i], conv_b[i], sp)
            q = project(h, w_in, q0, qw, gain=win_q_norm[i], scale=ATTN_SCALE)
            k, k32 = project(h, w_in, q0 + qw, kw, gain=win_k_norm[i], want_f32=True)
            v, v32 = project(h, w_in, q0 + qw + kw, kw, want_f32=True)
            yb = dense_attention(q.reshape(bp, sp, qw), k.reshape(bp, sp, kw), v.reshape(bp, sp, kw),
                                 group, sp, sink=win_sink[i])
            xp = out_proj_residual(ya, yb.reshape(bp * sp, qw), w_out, xp, mp[2])
            win_k.append(k32.reshape(bp, sp, WIN_KV_HEADS, HEAD_DIM))
            win_v.append(v32.reshape(bp, sp, WIN_KV_HEADS, HEAD_DIM))
            h = mod_norm(xs, norm_mix_w[layer], ms[0], ms[1])
            ya = conv_gate(project(h, w_in, 0, q0, want_bf16=False, want_f32=True), conv_w[i], conv_b[i], ss)
            q = project(h, w_in, q0, qw, gain=win_q_norm[i], scale=ATTN_SCALE, rope_tables=rope)
            k = project(h, w_in, q0 + qw, kw, gain=win_k_norm[i], rope_tables=rope)
            v = project(h, w_in, q0 + qw + kw, kw)
            yb = window_attention(q.reshape(bs, ss, qw), k.reshape(bs, ss, kw), v.reshape(bs, ss, kw),
                                  flat_cache(cache_win_k, i), flat_cache(cache_win_v, i), win_sink[i], group)
            xs = out_proj_residual(ya, yb.reshape(bs * ss, qw), w_out, xs, ms[2])
        else:
            w_in = odd_w_in[i].astype(BF16)
            w_out = odd_w_out[i].astype(BF16)
            nw = NAT_HEADS * HEAD_DIM
            gqw = GLOB_Q_HEADS * HEAD_DIM
            gkw = GLOB_KV_HEADS * HEAD_DIM
            group = GLOB_Q_HEADS // GLOB_KV_HEADS
            kd0 = 3 * nw + gqw
            h = mod_norm(xp, norm_mix_w[layer], mp[0], mp[1])
            qc = project(h, w_in, 0, nw, gain=nat_q_norm[i], scale=ATTN_SCALE)
            kc, kc32 = project(h, w_in, nw, nw, gain=nat_k_norm[i], want_f32=True)
            vc, vc32 = project(h, w_in, 2 * nw, nw, want_f32=True)
            qd = project(h, w_in, 3 * nw, gqw, gain=glob_q_norm[i], scale=ATTN_SCALE)
            kd, kd32 = project(h, w_in, kd0, gkw, gain=glob_k_norm[i], want_f32=True)
            vd, vd32 = project(h, w_in, kd0 + gkw, gkw, want_f32=True)
            yc = dense_attention(qc.reshape(bp, sp, nw), kc.reshape(bp, sp, nw), vc.reshape(bp, sp, nw), 1, sp)
            yd = dense_attention(qd.reshape(bp, sp, gqw), kd.reshape(bp, sp, gkw), vd.reshape(bp, sp, gkw), group, sp)
            xp = out_proj_residual(yc.reshape(bp * sp, nw), yd.reshape(bp * sp, gqw), w_out, xp, mp[2])
            nat_k.append(kc32.reshape(bp, sp, NAT_HEADS, HEAD_DIM))
            nat_v.append(vc32.reshape(bp, sp, NAT_HEADS, HEAD_DIM))
            glob_k.append(kd32.reshape(bp, sp, GLOB_KV_HEADS, HEAD_DIM))
            glob_v.append(vd32.reshape(bp, sp, GLOB_KV_HEADS, HEAD_DIM))
            h = mod_norm(xs, norm_mix_w[layer], ms[0], ms[1])
            qc = project(h, w_in, 0, nw, gain=nat_q_norm[i], scale=ATTN_SCALE)
            kc = project(h, w_in, nw, nw, gain=nat_k_norm[i])
            vc = project(h, w_in, 2 * nw, nw)
            qd = project(h, w_in, 3 * nw, gqw, gain=glob_q_norm[i], scale=ATTN_SCALE, rope_tables=rope)
            kd = project(h, w_in, kd0, gkw, gain=glob_k_norm[i], rope_tables=rope)
            vd = project(h, w_in, kd0 + gkw, gkw)
            yc = neighbourhood_attention(qc.reshape(bs, ss, nw), kc.reshape(bs, ss, nw), vc.reshape(bs, ss, nw),
                                         flat_cache(cache_nat_k, i), flat_cache(cache_nat_v, i),
                                         _nat_bias_table(nat_rpb[i]))
            k_all = jnp.concatenate([kd.reshape(bs, ss, gkw), flat_cache(cache_glob_k, i)], axis=1)
            v_all = jnp.concatenate([vd.reshape(bs, ss, gkw), flat_cache(cache_glob_v, i)], axis=1)
            yd = dense_attention(qd.reshape(bs, ss, gqw), k_all, v_all, group, 128)
            xs = out_proj_residual(yc.reshape(bs * ss, nw), yd.reshape(bs * ss, gqw), w_out, xs, ms[2])

        xp, xs = moe_layer(
            [dict(x=xp, shift=mp[3], scale=mp[4], gate=mp[5], nseg=bp, n=sp),
             dict(x=xs, shift=ms[3], scale=ms[4], gate=ms[5], nseg=bs, n=ss)],
            norm_ffn_w[layer], router_w[layer], expert_w_gate, expert_w_up, expert_w_down, layer)

    stack = lambda xs_: jnp.stack(xs_, axis=1)
    return (xp.reshape(bp, sp, d), xs.reshape(bs, ss, d), stack(win_k), stack(win_v), stack(nat_k),
            stack(nat_v), stack(glob_k), stack(glob_v))
```

```python
import functools

import numpy as np
import jax
import jax.numpy as jnp
from jax import lax
from jax.experimental import pallas as pl
from jax.experimental.pallas import tpu as pltpu

D_MODEL = 2048
HEAD_DIM = 128
GRID_W = 64
CONV_DIM = 512
WIN_Q_HEADS = 12
WIN_KV_HEADS = 4
WIN_BLOCK = 128
WINDOW = 128
NAT_HEADS = 8
NAT_ROWS = 8
NAT_COLS = 16
GLOB_Q_HEADS = 8
GLOB_KV_HEADS = 2
ROPE_THETA = 10000.0
N_EXPERTS = 16
EXPERT_FF = 2048
EC_CAPACITY = 2
N_MOD = 6
NORM_EPS = 1e-6
NEG_INF = -1e30
ATTN_SCALE = HEAD_DIM ** -0.5

LANES = 128
MIB = 1024 * 1024
BF16 = jnp.bfloat16
F32 = jnp.float32


def _params(semantics, vmem_mib, **extra):
    return pltpu.CompilerParams(dimension_semantics=semantics, vmem_limit_bytes=vmem_mib * MIB, **extra)


def _silu(x):
    return x * (1.0 / (1.0 + jnp.exp(-x)))


def _modulation_kernel(cond_ref, w_ref, b_ref, o_ref):
    s = _silu(cond_ref[...]).astype(BF16)
    o_ref[0] = jnp.dot(s, w_ref[0].astype(BF16), preferred_element_type=F32) + b_ref[0]


def modulation(cond8, mod_w, mod_b):
    depth, d, n = mod_w.shape
    tn = 1024
    return pl.pallas_call(
        _modulation_kernel,
        out_shape=jax.ShapeDtypeStruct((depth, 8, n), F32),
        grid=(depth, n // tn),
        in_specs=[pl.BlockSpec((8, d), lambda l, j: (0, 0)),
                  pl.BlockSpec((1, d, tn), lambda l, j: (l, 0, j)),
                  pl.BlockSpec((1, 1, tn), lambda l, j: (l, 0, j))],
        out_specs=pl.BlockSpec((1, 8, tn), lambda l, j: (l, 0, j)),
        compiler_params=_params(("parallel", "parallel"), 48),
        name="modulation",
    )(cond8, mod_w, mod_b.reshape(depth, 1, n))


def _mod_norm(x, g, shift, scale):
    ms = jnp.mean(x * x, axis=-1, keepdims=True)
    y = x * lax.rsqrt(ms + NORM_EPS) * g
    return y * (1.0 + scale) + shift


def _mod_norm_kernel(x_ref, g_ref, shift_ref, scale_ref, h_ref):
    h_ref[...] = _mod_norm(x_ref[...], g_ref[...], shift_ref[0], scale_ref[0]).astype(BF16)


def mod_norm(x, g, shift, scale):
    r, d = x.shape
    nseg = shift.shape[0]
    tm = 512
    tiles_per_seg = (r // nseg) // tm
    return pl.pallas_call(
        _mod_norm_kernel,
        out_shape=jax.ShapeDtypeStruct((r, d), BF16),
        grid=(r // tm,),
        in_specs=[pl.BlockSpec((tm, d), lambda i: (i, 0)),
                  pl.BlockSpec((1, d), lambda i: (0, 0)),
                  pl.BlockSpec((1, 1, d), lambda i: (i // tiles_per_seg, 0, 0)),
                  pl.BlockSpec((1, 1, d), lambda i: (i // tiles_per_seg, 0, 0))],
        out_specs=pl.BlockSpec((tm, d), lambda i: (i, 0)),
        compiler_params=_params(("parallel",), 32),
        name="mod_norm",
    )(x, g.reshape(1, d), shift, scale)


PROJ_TILE = 512


def _proj_kernel(*refs, tiles, n_gain, has_rope):
    h_ref, w_ref = refs[0], refs[1]
    k = 2
    if n_gain:
        g_ref = refs[k]
        k += 1
    if has_rope:
        cos_ref, sin_ref = refs[k], refs[k + 1]
        k += 2
    outs = refs[k:]
    j = pl.program_id(1)
    acc = jnp.dot(h_ref[...], w_ref[...], preferred_element_type=F32)
    ones = jnp.ones((HEAD_DIM, HEAD_DIM), BF16)

    def finish(parts):
        for lo, width, dst, gain_idx, scale, rope, o_bf16, o_f32 in parts:
            if gain_idx is None:
                x = acc[:, lo:lo + width]
                if o_bf16 is not None:
                    outs[o_bf16][:, dst:dst + width] = x.astype(BF16)
                if o_f32 is not None:
                    outs[o_f32][:, dst:dst + width] = x
                continue
            for hd in range(width // HEAD_DIM):
                x = acc[:, lo + hd * HEAD_DIM:lo + (hd + 1) * HEAD_DIM]
                sl = slice(dst + hd * HEAD_DIM, dst + (hd + 1) * HEAD_DIM)
                ssq = jnp.dot((x * x).astype(BF16), ones, preferred_element_type=F32)
                y = x * lax.rsqrt(ssq * (1.0 / HEAD_DIM) + NORM_EPS) * g_ref[gain_idx:gain_idx + 1, :]
                if o_f32 is not None:
                    outs[o_f32][:, sl] = y
                if rope:
                    y = y * cos_ref[...] + pltpu.roll(y, HEAD_DIM // 2, 1) * sin_ref[...]
                outs[o_bf16][:, sl] = (y * scale).astype(BF16)

    for jv, parts in enumerate(tiles):
        pl.when(j == jv)(functools.partial(finish, parts))


def project(h, w_bf16, pieces, rope_tables=None):
    r, d = h.shape
    n = w_bf16.shape[1]
    tm, tn = 1024, PROJ_TILE
    n_tiles = n // tn
    gains = [p[1] for p in pieces if p[1] is not None]
    has_rope = any(p[3] for p in pieces)
    tiles = [[] for _ in range(n_tiles)]
    out_shape, out_specs, result = [], [], []
    col0 = 0
    for width, gain, scale, rope, want_bf16, want_f32 in pieces:
        block_w = min(width, tn)
        j0, n_blk = col0 // tn, max(width // tn, 1)
        slots = []
        for want, dt in ((want_bf16, BF16), (want_f32, F32)):
            if want:
                slots.append(len(out_shape))
                out_shape.append(jax.ShapeDtypeStruct((r, width), dt))
                out_specs.append(pl.BlockSpec(
                    (tm, block_w), lambda i, j, j0=j0, n_blk=n_blk: (i, jnp.clip(j - j0, 0, n_blk - 1))))
            else:
                slots.append(None)
        result.append(slots)
        gain_idx = None if gain is None else [g is gain for g in gains].index(True)
        for jv in range(j0, (col0 + width - 1) // tn + 1):
            start = max(col0, jv * tn)
            stop = min(col0 + width, (jv + 1) * tn)
            tiles[jv].append((start - jv * tn, stop - start, (start - col0) % block_w, gain_idx, scale, rope,
                              slots[0], slots[1]))
        col0 += width
    assert col0 == n
    in_specs = [pl.BlockSpec((tm, d), lambda i, j: (i, 0)),
                pl.BlockSpec((d, tn), lambda i, j: (0, j))]
    args = [h, w_bf16]
    if gains:
        in_specs.append(pl.BlockSpec((len(gains), HEAD_DIM), lambda i, j: (0, 0)))
        args.append(jnp.stack(gains))
    if has_rope:
        per = rope_tables[0].shape[0] // tm
        in_specs += [pl.BlockSpec((tm, HEAD_DIM), lambda i, j: (i % per, 0))] * 2
        args += list(rope_tables)
    out = pl.pallas_call(
        functools.partial(_proj_kernel, tiles=tiles, n_gain=len(gains), has_rope=has_rope),
        out_shape=out_shape,
        grid=(r // tm, n_tiles),
        in_specs=in_specs,
        out_specs=out_specs,
        compiler_params=_params(("parallel", "arbitrary"), 56),
        name="project",
    )(*args)
    return [tuple(None if s is None else out[s] for s in slots) for slots in result]


def _conv_gate_kernel(ab_ref, ac_ref, ah_ref, w_ref, b_ref, o_ref, *, seq):
    u = ac_ref[...] * ah_ref[...]
    rows = u.shape[0]
    pos = lax.broadcasted_iota(jnp.int32, u.shape, 0) % seq
    prev = jnp.where(pos == 0, 0.0, pltpu.roll(u, 1, 0))
    nxt = jnp.where(pos == seq - 1, 0.0, pltpu.roll(u, rows - 1, 0))
    conv = prev * w_ref[0:1, :] + u * w_ref[1:2, :] + nxt * w_ref[2:3, :] + b_ref[...]
    o_ref[...] = (ab_ref[...] * conv).astype(o_ref.dtype)


def conv_gate(proj, conv_w, conv_b, seq):
    r = proj.shape[0]
    tr = 4096
    nc = CONV_DIM // LANES
    return pl.pallas_call(
        functools.partial(_conv_gate_kernel, seq=seq),
        out_shape=jax.ShapeDtypeStruct((r, CONV_DIM), BF16),
        grid=(r // tr, nc),
        in_specs=[pl.BlockSpec((tr, LANES), lambda i, c: (i, c)),
                  pl.BlockSpec((tr, LANES), lambda i, c: (i, nc + c)),
                  pl.BlockSpec((tr, LANES), lambda i, c: (i, 2 * nc + c)),
                  pl.BlockSpec((3, LANES), lambda i, c: (0, c)),
                  pl.BlockSpec((1, LANES), lambda i, c: (0, c))],
        out_specs=pl.BlockSpec((tr, LANES), lambda i, c: (i, c)),
        compiler_params=_params(("parallel", "parallel"), 48),
        name="conv_gate",
    )(proj, proj, proj, conv_w, conv_b.reshape(1, CONV_DIM))


def _dense_attn_kernel(*refs, group, has_sink):
    if has_sink:
        sink_ref, q_ref, k_ref, v_ref, o_ref = refs
    else:
        q_ref, k_ref, v_ref, o_ref = refs
    qb = q_ref.shape[1]
    for hkv in range(k_ref.shape[2] // HEAD_DIM):
        head = lambda g: slice((hkv * group + g) * HEAD_DIM, (hkv * group + g + 1) * HEAD_DIM)
        kv = slice(hkv * HEAD_DIM, (hkv + 1) * HEAD_DIM)
        q = jnp.concatenate([q_ref[0, :, head(g)] for g in range(group)], axis=0)
        s = lax.dot_general(q, k_ref[0, :, kv], (((1,), (1,)), ((), ())), preferred_element_type=F32)
        m = jnp.max(s, axis=-1, keepdims=True)
        if has_sink:
            sk = jnp.concatenate([jnp.full((qb, 1), sink_ref[hkv * group + g], F32) for g in range(group)], axis=0)
            m = jnp.maximum(m, sk)
        p = jnp.exp(s - m)
        den = jnp.sum(p, axis=-1, keepdims=True)
        if has_sink:
            den = den + jnp.exp(sk - m)
        o = jnp.dot(p.astype(BF16), v_ref[0, :, kv], preferred_element_type=F32) / den
        for g in range(group):
            o_ref[0, :, head(g)] = o[g * qb:(g + 1) * qb].astype(o_ref.dtype)


def dense_attention(q, k, v, group, q_block, sink=None):
    b, nq, qw = q.shape
    nk = k.shape[1]
    kvw = k.shape[2]
    in_specs = [pl.BlockSpec((1, q_block, qw), lambda bi, qi: (bi, qi, 0)),
                pl.BlockSpec((1, nk, kvw), lambda bi, qi: (bi, 0, 0)),
                pl.BlockSpec((1, nk, kvw), lambda bi, qi: (bi, 0, 0))]
    args = [q, k, v]
    if sink is not None:
        in_specs = [pl.BlockSpec(memory_space=pltpu.SMEM)] + in_specs
        args = [sink] + args
    return pl.pallas_call(
        functools.partial(_dense_attn_kernel, group=group, has_sink=sink is not None),
        out_shape=jax.ShapeDtypeStruct((b, nq, qw), BF16),
        grid=(b, nq // q_block),
        in_specs=in_specs,
        out_specs=pl.BlockSpec((1, q_block, qw), lambda bi, qi: (bi, qi, 0)),
        compiler_params=_params(("parallel", "parallel"), 56),
        name="dense_attention",
    )(*args)


def _window_attn_kernel(sink_ref, q_ref, kp_ref, kc_ref, kn_ref, vp_ref, vc_ref, vn_ref, ck_ref, cv_ref, o_ref,
                        *, group, n_tokens):
    blk = pl.program_id(1)
    wb = WIN_BLOCK
    rows, cols = group * wb, 3 * wb + ck_ref.shape[1]
    q_pos = blk * wb + lax.broadcasted_iota(jnp.int32, (rows, cols), 0) % wb
    col = lax.broadcasted_iota(jnp.int32, (rows, cols), 1)
    k_pos = (blk - 1) * wb + col
    local_ok = (jnp.abs(k_pos - q_pos) <= WINDOW) & (k_pos >= 0) & (k_pos < n_tokens)
    visible = (col >= 3 * wb) | local_ok
    for hkv in range(kc_ref.shape[2] // HEAD_DIM):
        head = lambda g: slice((hkv * group + g) * HEAD_DIM, (hkv * group + g + 1) * HEAD_DIM)
        kv = slice(hkv * HEAD_DIM, (hkv + 1) * HEAD_DIM)
        q = jnp.concatenate([q_ref[0, :, head(g)] for g in range(group)], axis=0)
        keys = jnp.concatenate([kp_ref[0, :, kv], kc_ref[0, :, kv], kn_ref[0, :, kv], ck_ref[0, :, kv]], axis=0)
        vals = jnp.concatenate([vp_ref[0, :, kv], vc_ref[0, :, kv], vn_ref[0, :, kv], cv_ref[0, :, kv]], axis=0)
        s = lax.dot_general(q, keys, (((1,), (1,)), ((), ())), preferred_element_type=F32)
        s = jnp.where(visible, s, NEG_INF)
        sk = jnp.concatenate([jnp.full((wb, 1), sink_ref[hkv * group + g], F32) for g in range(group)], axis=0)
        m = jnp.maximum(jnp.max(s, axis=-1, keepdims=True), sk)
        p = jnp.exp(s - m)
        den = jnp.sum(p, axis=-1, keepdims=True) + jnp.exp(sk - m)
        o = jnp.dot(p.astype(BF16), vals, preferred_element_type=F32) / den
        for g in range(group):
            o_ref[0, :, head(g)] = o[g * wb:(g + 1) * wb].astype(o_ref.dtype)


def window_attention(q, k, v, ctx_k, ctx_v, sink, group):
    b, n, qw = q.shape
    kvw = k.shape[2]
    nb = n // WIN_BLOCK
    nctx = ctx_k.shape[1]
    prev_map = lambda bi, i: (bi, jnp.maximum(i - 1, 0), 0)
    cur_map = lambda bi, i: (bi, i, 0)
    next_map = lambda bi, i: (bi, jnp.minimum(i + 1, nb - 1), 0)
    ctx_map = lambda bi, i: (bi, 0, 0)
    kv_block = (1, WIN_BLOCK, kvw)
    return pl.pallas_call(
        functools.partial(_window_attn_kernel, group=group, n_tokens=n),
        out_shape=jax.ShapeDtypeStruct((b, n, qw), BF16),
        grid=(b, nb),
        in_specs=[pl.BlockSpec(memory_space=pltpu.SMEM),
                  pl.BlockSpec((1, WIN_BLOCK, qw), cur_map),
                  pl.BlockSpec(kv_block, prev_map), pl.BlockSpec(kv_block, cur_map), pl.BlockSpec(kv_block, next_map),
                  pl.BlockSpec(kv_block, prev_map), pl.BlockSpec(kv_block, cur_map), pl.BlockSpec(kv_block, next_map),
                  pl.BlockSpec((1, nctx, kvw), ctx_map), pl.BlockSpec((1, nctx, kvw), ctx_map)],
        out_specs=pl.BlockSpec((1, WIN_BLOCK, qw), cur_map),
        compiler_params=_params(("parallel", "parallel"), 32),
        name="window_attention",
    )(sink, q, k, k, k, v, v, v, ctx_k, ctx_v)


NAT_QROWS = 8
NAT_KROWS = 16


def _nat_bias_table(rpb):
    w = GRID_W
    c = np.arange(w)[:, None]
    kc = np.arange(w)[None, :]
    ws = np.clip(c - NAT_COLS // 2, 0, w - NAT_COLS)
    col_ok = (kc >= ws) & (kc < ws + NAT_COLS)
    col_idx = np.clip(kc - c + NAT_COLS - 1, 0, 2 * NAT_COLS - 2)
    t = jnp.where(col_ok[None, None], rpb[:, :, col_idx], NEG_INF)
    dead = jnp.full_like(t[:, :1], NEG_INF)
    t = jnp.concatenate([dead, t, dead], axis=1)
    return jnp.concatenate([t[:, :-1], t[:, 1:]], axis=-1)


def _nat_attn_kernel(q_ref, k_ref, v_ref, ck_ref, cv_ref, t_ref, o_ref, *, grid_rows):
    m = pl.program_id(2)
    w = GRID_W
    r0 = m * NAT_QROWS
    kr0 = jnp.clip(r0 - NAT_ROWS // 2, 0, grid_rows - NAT_KROWS)
    tok0 = pl.multiple_of(kr0 * w, 4 * w)
    lane_hi = lax.broadcasted_iota(jnp.int32, (w, 2 * w), 1) >= w
    layout = []
    for i in range(NAT_QROWS):
        r = r0 + i
        rs = jnp.clip(r - NAT_ROWS // 2, 0, grid_rows - NAT_ROWS)
        row = []
        for jj in range(NAT_KROWS // 2):
            kr = kr0 + 2 * jj
            d = kr - r + NAT_ROWS - 1
            ok_lo = ((kr >= rs) & (kr < rs + NAT_ROWS)).astype(jnp.int32)
            ok_hi = ((kr + 1 >= rs) & (kr + 1 < rs + NAT_ROWS)).astype(jnp.int32)
            row.append((jnp.clip(d, -1, 2 * NAT_ROWS - 2) + 1, jnp.where(lane_hi, ok_hi, ok_lo) > 0))
        layout.append(row)
    for hh in range(q_ref.shape[2] // HEAD_DIM):
        hs = slice(hh * HEAD_DIM, (hh + 1) * HEAD_DIM)
        keys = k_ref[0, pl.ds(tok0, NAT_KROWS * w), hs]
        vals = v_ref[0, pl.ds(tok0, NAT_KROWS * w), hs]
        q = q_ref[0, :, hs]
        s_loc = lax.dot_general(q, keys, (((1,), (1,)), ((), ())), preferred_element_type=F32)
        bias = jnp.concatenate(
            [jnp.concatenate([jnp.where(ok, t_ref[hh, entry], NEG_INF) for entry, ok in row], axis=1)
             for row in layout], axis=0)
        s_loc = s_loc + bias
        s_ctx = lax.dot_general(q, ck_ref[0, :, hs], (((1,), (1,)), ((), ())), preferred_element_type=F32)
        mx = jnp.maximum(jnp.max(s_loc, axis=-1, keepdims=True), jnp.max(s_ctx, axis=-1, keepdims=True))
        p_loc = jnp.exp(s_loc - mx)
        p_ctx = jnp.exp(s_ctx - mx)
        den = jnp.sum(p_loc, axis=-1, keepdims=True) + jnp.sum(p_ctx, axis=-1, keepdims=True)
        o = (jnp.dot(p_loc.astype(BF16), vals, preferred_element_type=F32)
             + jnp.dot(p_ctx.astype(BF16), cv_ref[0, :, hs], preferred_element_type=F32))
        o_ref[0, :, hs] = (o / den).astype(o_ref.dtype)


NAT_HEADS_PER_STEP = 2


def neighbourhood_attention(q, k, v, ctx_k, ctx_v, bias_table):
    b, n, hw = q.shape
    hp = NAT_HEADS_PER_STEP
    hpw = hp * HEAD_DIM
    grid_rows = n // GRID_W
    nctx = ctx_k.shape[1]
    tq = NAT_QROWS * GRID_W
    return pl.pallas_call(
        functools.partial(_nat_attn_kernel, grid_rows=grid_rows),
        out_shape=jax.ShapeDtypeStruct((b, n, hw), BF16),
        grid=(b, hw // hpw, grid_rows // NAT_QROWS),
        in_specs=[pl.BlockSpec((1, tq, hpw), lambda bi, h, m: (bi, m, h)),
                  pl.BlockSpec((1, n, hpw), lambda bi, h, m: (bi, 0, h)),
                  pl.BlockSpec((1, n, hpw), lambda bi, h, m: (bi, 0, h)),
                  pl.BlockSpec((1, nctx, hpw), lambda bi, h, m: (bi, 0, h)),
                  pl.BlockSpec((1, nctx, hpw), lambda bi, h, m: (bi, 0, h)),
                  pl.BlockSpec((hp, 2 * NAT_ROWS, GRID_W, 2 * GRID_W), lambda bi, h, m: (h, 0, 0, 0))],
        out_specs=pl.BlockSpec((1, tq, hpw), lambda bi, h, m: (bi, m, h)),
        compiler_params=_params(("parallel", "parallel", "arbitrary"), 40),
        name="neighbourhood_attention",
    )(q, k, v, ctx_k, ctx_v, bias_table)


def _out_proj_kernel(a_ref, b_ref, w_ref, x_ref, gate_ref, o_ref):
    ka = a_ref.shape[1]
    acc = jnp.dot(a_ref[...], w_ref[:ka, :], preferred_element_type=F32)
    acc = acc + jnp.dot(b_ref[...], w_ref[ka:, :], preferred_element_type=F32)
    o_ref[...] = x_ref[...] + gate_ref[0] * acc


def out_proj_residual(a, b_, w_bf16, x, gate):
    r, d = x.shape
    ka, kb = a.shape[1], b_.shape[1]
    nseg = gate.shape[0]
    tm, tn = 1024, 1024
    tiles_per_seg = (r // nseg) // tm
    return pl.pallas_call(
        _out_proj_kernel,
        out_shape=jax.ShapeDtypeStruct((r, d), F32),
        grid=(r // tm, d // tn),
        in_specs=[pl.BlockSpec((tm, ka), lambda i, j: (i, 0)),
                  pl.BlockSpec((tm, kb), lambda i, j: (i, 0)),
                  pl.BlockSpec((ka + kb, tn), lambda i, j: (0, j)),
                  pl.BlockSpec((tm, tn), lambda i, j: (i, j)),
                  pl.BlockSpec((1, 1, tn), lambda i, j: (i // tiles_per_seg, 0, j))],
        out_specs=pl.BlockSpec((tm, tn), lambda i, j: (i, j)),
        compiler_params=_params(("parallel", "parallel"), 56),
        name="out_proj_residual",
    )(a, b_, w_bf16, x, gate)


def _norm_router_kernel(x_ref, g_ref, shift_ref, scale_ref, whi_ref, wlo_ref, *out_refs):
    logit_ref = out_refs[-1]
    h = _mod_norm(x_ref[...], g_ref[...], shift_ref[0], scale_ref[0])
    h_hi = h.astype(BF16)
    h_lo = (h - h_hi.astype(F32)).astype(BF16)
    if len(out_refs) > 1:
        out_refs[0][...] = h_hi
    w_hi = whi_ref[...]
    logit_ref[...] = (jnp.dot(h_hi, w_hi, preferred_element_type=F32)
                      + jnp.dot(h_lo, w_hi, preferred_element_type=F32)
                      + jnp.dot(h_hi, wlo_ref[...], preferred_element_type=F32))


def norm_router(x, g, shift, scale, router_w, want_h):
    r, d = x.shape
    nseg = shift.shape[0]
    tm = 512
    tiles_per_seg = (r // nseg) // tm
    w_pad = jnp.zeros((d, LANES), F32).at[:, :N_EXPERTS].set(router_w)
    w_hi = w_pad.astype(BF16)
    w_lo = (w_pad - w_hi.astype(F32)).astype(BF16)
    out_shape = [jax.ShapeDtypeStruct((r, LANES), F32)]
    out_specs = [pl.BlockSpec((tm, LANES), lambda i: (i, 0))]
    if want_h:
        out_shape.insert(0, jax.ShapeDtypeStruct((r, d), BF16))
        out_specs.insert(0, pl.BlockSpec((tm, d), lambda i: (i, 0)))
    return pl.pallas_call(
        _norm_router_kernel,
        out_shape=out_shape,
        grid=(r // tm,),
        in_specs=[pl.BlockSpec((tm, d), lambda i: (i, 0)),
                  pl.BlockSpec((1, d), lambda i: (0, 0)),
                  pl.BlockSpec((1, 1, d), lambda i: (i // tiles_per_seg, 0, 0)),
                  pl.BlockSpec((1, 1, d), lambda i: (i // tiles_per_seg, 0, 0)),
                  pl.BlockSpec((d, LANES), lambda i: (0, 0)),
                  pl.BlockSpec((d, LANES), lambda i: (0, 0))],
        out_specs=out_specs,
        compiler_params=_params(("parallel",), 32),
        name="norm_router",
    )(x, g.reshape(1, d), shift, scale, w_hi, w_lo)


CUMSUM_BLOCK = 256


def _exclusive_cumsum_lanes(flags):
    e, n = flags.shape
    cb = min(CUMSUM_BLOCK, n)
    tri = (lax.broadcasted_iota(jnp.int32, (cb, cb), 0) < lax.broadcasted_iota(jnp.int32, (cb, cb), 1)).astype(BF16)
    carry = jnp.zeros((e, 1), F32)
    out = []
    for j in range(n // cb):
        blk = flags[:, j * cb:(j + 1) * cb]
        out.append(jnp.dot(blk.astype(BF16), tri, preferred_element_type=F32) + carry)
        carry = carry + jnp.sum(blk, axis=-1, keepdims=True)
    return jnp.concatenate(out, axis=1) if len(out) > 1 else out[0]


def _select_kernel(logit_ref, pos_ref, gate_ref, *idx_refs, cap):
    lg = logit_ref[0]
    e, n = lg.shape
    ex = jnp.exp(lg - jnp.max(lg, axis=0, keepdims=True))
    aff = ex / jnp.sum(ex, axis=0, keepdims=True)

    def enough(t):
        return jnp.sum((aff >= t).astype(F32), axis=-1, keepdims=True) >= cap

    hi = jnp.full((e, 1), 2.0, F32)
    for step in (64, 32, 16, 8, 4, 2, 1):
        cand = hi * (2.0 ** -step)
        hi = jnp.where(enough(cand), hi, cand)
    lo = hi * 0.5
    lo = jnp.where(enough(lo), lo, 0.0)
    for _ in range(23):
        mid = (lo + hi) * 0.5
        ok = enough(mid)
        lo = jnp.where(ok, mid, lo)
        hi = jnp.where(ok, hi, mid)
    thr = lo
    above = (aff > thr).astype(F32)
    tied = (aff == thr).astype(F32)
    need = cap - jnp.sum(above, axis=-1, keepdims=True)
    take = above + tied * (_exclusive_cumsum_lanes(tied) < need).astype(F32)
    slot = _exclusive_cumsum_lanes(take)
    pos = jnp.where(take > 0, slot, -1.0).astype(jnp.int32)
    pos_ref[0] = pos
    s_iota = lax.broadcasted_iota(jnp.int32, (cap, n), 0)
    token = lax.broadcasted_iota(jnp.int32, (1, n), 1)
    t_hi = (token // 64).astype(F32)
    t_lo = (token % 64).astype(F32)
    row = lax.broadcasted_iota(jnp.int32, (8, n), 0)
    for ei in range(e):
        a = aff[ei:ei + 1, :]
        a0 = a.astype(BF16).astype(F32)
        a1 = (a - a0).astype(BF16).astype(F32)
        a2 = a - a0 - a1
        vals = jnp.where(row == 0, a0, jnp.where(row == 1, a1, jnp.where(row == 2, a2,
                         jnp.where(row == 3, t_hi, jnp.where(row == 4, t_lo, 0.0)))))
        hit = jnp.where(pos[ei:ei + 1, :] == s_iota, 1.0, 0.0).astype(BF16)
        res = lax.dot_general(vals.astype(BF16), hit, (((1,), (1,)), ((), ())), preferred_element_type=F32)
        gate_ref[0, ei:ei + 1, :] = res[0:1] + res[1:2] + res[2:3]
        if idx_refs:
            idx_refs[0][0, ei:ei + 1, :] = (res[3:4] * 64.0 + res[4:5]).astype(jnp.int32)


def select_tokens(logits_t, cap, want_idx):
    nseg, e, n = logits_t.shape
    slots = (nseg, e, cap)
    out_shape = [jax.ShapeDtypeStruct((nseg, e, n), jnp.int32), jax.ShapeDtypeStruct(slots, F32)]
    out_specs = [pl.BlockSpec((1, e, n), lambda s: (s, 0, 0)), pl.BlockSpec((1, e, cap), lambda s: (s, 0, 0))]
    if want_idx:
        out_shape.append(jax.ShapeDtypeStruct(slots, jnp.int32))
        out_specs.append(pl.BlockSpec((1, e, cap), lambda s: (s, 0, 0)))
    return pl.pallas_call(
        functools.partial(_select_kernel, cap=cap),
        out_shape=out_shape,
        grid=(nseg,),
        in_specs=[pl.BlockSpec((1, e, n), lambda s: (s, 0, 0))],
        out_specs=out_specs,
        compiler_params=_params(("parallel",), 48),
        name="select_tokens",
    )(logits_t)


def _gather_rows_kernel(idx_ref, x_hbm, g_ref, shift_ref, scale_ref, o_ref, buf, sem, *, cap, n, experts):
    step = pl.program_id(0)
    slot = step % 2

    def row_copy(st, sl, k):
        row = (st // experts) * n + idx_ref[st * cap + k]
        return pltpu.make_async_copy(x_hbm.at[pl.ds(row, 1), :], buf.at[sl, pl.ds(k, 1), :], sem.at[sl])

    def issue(st, sl):
        def body(k, carry):
            row_copy(st, sl, k).start()
            return carry
        lax.fori_loop(0, cap, body, 0, unroll=8)

    @pl.when(step == 0)
    def _():
        issue(step, slot)

    @pl.when(step + 1 < pl.num_programs(0))
    def _():
        issue(step + 1, 1 - slot)

    pltpu.make_async_copy(x_hbm.at[pl.ds(0, cap), :], buf.at[slot], sem.at[slot]).wait()
    h = _mod_norm(buf[slot], g_ref[...], shift_ref[0], scale_ref[0])
    o_ref[0, 0] = h.astype(o_ref.dtype)


def gather_rows(idx, x, g, shift, scale, n):
    nseg, e, cap = idx.shape
    d = x.shape[1]
    return pl.pallas_call(
        functools.partial(_gather_rows_kernel, cap=cap, n=n, experts=e),
        out_shape=jax.ShapeDtypeStruct((nseg, e, cap, d), BF16),
        grid_spec=pltpu.PrefetchScalarGridSpec(
            num_scalar_prefetch=1,
            grid=(nseg * e,),
            in_specs=[pl.BlockSpec(memory_space=pl.ANY),
                      pl.BlockSpec((1, d), lambda st, idx_ref: (0, 0)),
                      pl.BlockSpec((1, 1, d), lambda st, idx_ref: (st // e, 0, 0)),
                      pl.BlockSpec((1, 1, d), lambda st, idx_ref: (st // e, 0, 0))],
            out_specs=pl.BlockSpec((1, 1, cap, d), lambda st, idx_ref: (st // e, st % e, 0, 0)),
            scratch_shapes=[pltpu.VMEM((2, cap, d), F32), pltpu.SemaphoreType.DMA((2,))]),
        compiler_params=_params(("arbitrary",), 32, disable_bounds_checks=True),
        name="gather_rows",
    )(idx.reshape(-1), x, g.reshape(1, d), shift, scale)


def _gather_kernel(pos_ref, h_ref, o_ref, acc_ref, *, cap):
    kt = pl.program_id(2)
    eb = pos_ref.shape[1]
    tk = pos_ref.shape[3]
    s_iota = lax.broadcasted_iota(jnp.int32, (cap, tk), 0)
    onehot = jnp.concatenate([(pos_ref[0, i] == s_iota).astype(BF16) for i in range(eb)], axis=0)
    part = jnp.dot(onehot, h_ref[0], preferred_element_type=F32)

    @pl.when(kt == 0)
    def _():
        acc_ref[...] = part

    @pl.when(kt > 0)
    def _():
        acc_ref[...] += part

    @pl.when(kt == pl.num_programs(2) - 1)
    def _():
        o_ref[0] = acc_ref[...].reshape(o_ref.shape[1:]).astype(o_ref.dtype)


def gather_tokens(pos, h, cap, experts_per_step, tk):
    nseg, e, n = pos.shape
    d = h.shape[2]
    eb = experts_per_step
    return pl.pallas_call(
        functools.partial(_gather_kernel, cap=cap),
        out_shape=jax.ShapeDtypeStruct((nseg, e, cap, d), BF16),
        grid=(nseg, e // eb, n // tk),
        in_specs=[pl.BlockSpec((1, eb, 1, tk), lambda s, ei, kt: (s, ei, 0, kt)),
                  pl.BlockSpec((1, tk, d), lambda s, ei, kt: (s, kt, 0))],
        out_specs=pl.BlockSpec((1, eb, cap, d), lambda s, ei, kt: (s, ei, 0, 0)),
        scratch_shapes=[pltpu.VMEM((eb * cap, d), F32)],
        compiler_params=_params(("parallel", "parallel", "arbitrary"), 48),
        name="gather_tokens",
    )(pos.reshape(nseg, e, 1, n), h)


def _expert_up_kernel(xc_ref, xl_ref, wg_ref, wu_ref, o_ref):
    d = xc_ref.shape[-1]
    wg = wg_ref[0, 0].astype(BF16)
    wu = wu_ref[0, 0].astype(BF16)
    row = 0
    for x_ref in (xc_ref, xl_ref):
        rows = x_ref.shape[0] * x_ref.shape[2]
        x = x_ref[...].reshape(rows, d)
        a = jnp.dot(x, wg, preferred_element_type=F32)
        b = jnp.dot(x, wu, preferred_element_type=F32)
        o_ref[0, row:row + rows, :] = (_silu(a) * b).astype(o_ref.dtype)
        row += rows


def expert_up(xg_ctx, xg_lat, w_gate, w_up, layer):
    _, e, d, f = w_gate.shape
    tf = 512
    sc, _, cc, _ = xg_ctx.shape
    sl, _, cl, _ = xg_lat.shape
    rows = sc * cc + sl * cl
    return pl.pallas_call(
        _expert_up_kernel,
        out_shape=jax.ShapeDtypeStruct((e, rows, f), BF16),
        grid=(e, f // tf),
        in_specs=[pl.BlockSpec((sc, 1, cc, d), lambda ei, j: (0, ei, 0, 0)),
                  pl.BlockSpec((sl, 1, cl, d), lambda ei, j: (0, ei, 0, 0)),
                  pl.BlockSpec((1, 1, d, tf), lambda ei, j: (layer, ei, 0, j)),
                  pl.BlockSpec((1, 1, d, tf), lambda ei, j: (layer, ei, 0, j))],
        out_specs=pl.BlockSpec((1, rows, tf), lambda ei, j: (ei, 0, j)),
        compiler_params=_params(("parallel", "arbitrary"), 56),
        name="expert_up",
    )(xg_ctx, xg_lat, w_gate, w_up)


def _expert_down_kernel(h_ref, w_ref, gate_ref, o_ref):
    y = jnp.dot(h_ref[0], w_ref[0, 0].astype(BF16), preferred_element_type=F32)
    o_ref[0] = (y * gate_ref[0]).astype(o_ref.dtype)


def expert_down(hid, w_down, gates, layer):
    e, rows, f = hid.shape
    d = w_down.shape[3]
    td = 512
    return pl.pallas_call(
        _expert_down_kernel,
        out_shape=jax.ShapeDtypeStruct((e, rows, d), BF16),
        grid=(e, d // td),
        in_specs=[pl.BlockSpec((1, rows, f), lambda ei, j: (ei, 0, 0)),
                  pl.BlockSpec((1, 1, f, td), lambda ei, j: (layer, ei, 0, j)),
                  pl.BlockSpec((1, rows, 1), lambda ei, j: (ei, 0, 0))],
        out_specs=pl.BlockSpec((1, rows, td), lambda ei, j: (ei, 0, j)),
        compiler_params=_params(("parallel", "arbitrary"), 48),
        name="expert_down",
    )(hid, w_down, gates)


def _combine_kernel(post_ref, y_ref, x_ref, gate_ref, o_ref, hit_ref, *, cap):
    ne = y_ref.shape[0]
    tt = post_ref.shape[1]

    @pl.when(pl.program_id(2) == 0)
    def _():
        post = post_ref[0]
        width = max(cap, LANES)
        per = width // cap
        lane = lax.broadcasted_iota(jnp.int32, (tt, width), 1)
        for blk in range(ne // per):
            hit = None
            for i in range(per):
                col = post[:, blk * per + i:blk * per + i + 1]
                h = jnp.where(col >= 0, col + i * cap, -1) == lane
                hit = h if hit is None else (hit | h)
            hit_ref[:, blk * width:(blk + 1) * width] = jnp.where(hit, 1.0, 0.0).astype(BF16)

    y = y_ref[...].reshape(ne * cap, y_ref.shape[-1])
    o_ref[...] = x_ref[...] + gate_ref[0] * jnp.dot(hit_ref[...], y, preferred_element_type=F32)


def combine_tokens(pos_t, y, x, gate, cap, row_offset, tt, td):
    nseg, n, e = pos_t.shape
    r, d = x.shape
    nt = n // tt
    rb = row_offset // cap
    return pl.pallas_call(
        functools.partial(_combine_kernel, cap=cap),
        out_shape=jax.ShapeDtypeStruct((r, d), F32),
        grid=(nseg, nt, d // td),
        in_specs=[pl.BlockSpec((1, tt, e), lambda s, t, j: (s, t, 0)),
                  pl.BlockSpec((e, cap, td), lambda s, t, j: (0, rb + s, j)),
                  pl.BlockSpec((tt, td), lambda s, t, j: (s * nt + t, j)),
                  pl.BlockSpec((1, 1, td), lambda s, t, j: (s * gate.shape[0] // nseg, 0, j))],
        out_specs=pl.BlockSpec((tt, td), lambda s, t, j: (s * nt + t, j)),
        scratch_shapes=[pltpu.VMEM((tt, e * cap), BF16)],
        compiler_params=_params(("parallel", "parallel", "arbitrary"), 56),
        name="combine_tokens",
    )(pos_t, y, x, gate)


COMBINE_WINDOW = 256
COMBINE_ALIGN = 64


def _combine_window_kernel(ws_ref, post_ref, *refs, experts):
    y_refs = refs[:experts]
    x_ref, gate_ref, o_ref, hit_ref = refs[experts:]
    s, t = pl.program_id(0), pl.program_id(1)
    tt = post_ref.shape[1]
    win = COMBINE_WINDOW

    @pl.when(pl.program_id(2) == 0)
    def _():
        post = post_ref[0]
        lane = lax.broadcasted_iota(jnp.int32, (tt, win), 1)
        for e in range(experts):
            start = ws_ref[(s * pl.num_programs(1) + t) * experts + e]
            col = post[:, e:e + 1]
            local = jnp.where(col >= 0, col - start, -1)
            hit_ref[:, e * win:(e + 1) * win] = jnp.where(local == lane, 1.0, 0.0).astype(BF16)

    acc = jnp.dot(hit_ref[:, 0:win], y_refs[0][0], preferred_element_type=F32)
    for e in range(1, experts):
        acc = acc + jnp.dot(hit_ref[:, e * win:(e + 1) * win], y_refs[e][0], preferred_element_type=F32)
    o_ref[...] = x_ref[...] + gate_ref[0] * acc


def combine_tokens_windowed(win_start, pos_t, y, x, gate, cap, row_offset, tt, td):
    nseg, n, e = pos_t.shape
    r, d = x.shape
    nt = n // tt

    def y_spec(ei):
        return pl.BlockSpec((pl.Element(1), pl.Element(COMBINE_WINDOW), pl.Element(td)),
                            lambda s, t, j, ws: (
                                ei,
                                pl.multiple_of(row_offset + s * cap + ws[(s * nt + t) * e + ei], COMBINE_ALIGN),
                                j * td))

    return pl.pallas_call(
        functools.partial(_combine_window_kernel, experts=e),
        out_shape=jax.ShapeDtypeStruct((r, d), F32),
        grid_spec=pltpu.PrefetchScalarGridSpec(
            num_scalar_prefetch=1,
            grid=(nseg, nt, d // td),
            in_specs=[pl.BlockSpec((1, tt, e), lambda s, t, j, ws: (s, t, 0))]
                     + [y_spec(ei) for ei in range(e)]
                     + [pl.BlockSpec((tt, td), lambda s, t, j, ws: (s * nt + t, j)),
                        pl.BlockSpec((1, 1, td), lambda s, t, j, ws: (s * gate.shape[0] // nseg, 0, j))],
            out_specs=pl.BlockSpec((tt, td), lambda s, t, j, ws: (s * nt + t, j)),
            scratch_shapes=[pltpu.VMEM((tt, e * COMBINE_WINDOW), BF16)]),
        compiler_params=_params(("parallel", "parallel", "arbitrary"), 56),
        name="combine_tokens_windowed",
    )(win_start.reshape(-1), pos_t, *([y] * e), x, gate)


def _combine_windows(pos, tt):
    nseg, e, n = pos.shape
    taken = (pos >= 0).reshape(nseg, e, n // tt, tt).sum(-1)
    first = jnp.cumsum(taken, axis=-1) - taken
    cap = taken.sum(-1, keepdims=True)
    start = jnp.minimum(first // COMBINE_ALIGN * COMBINE_ALIGN, cap - COMBINE_WINDOW)
    fits = jnp.all(first + taken <= start + COMBINE_WINDOW)
    return jnp.swapaxes(start, 1, 2).astype(jnp.int32), fits


def moe_layer(groups, g_ffn, router_w, w_gate, w_up, w_down, layer):
    sel = []
    for grp in groups:
        nseg, n = grp["nseg"], grp["n"]
        cap = (EC_CAPACITY * n) // N_EXPERTS
        dense = n <= 512
        routed = norm_router(grp["x"], g_ffn, grp["shift"], grp["scale"], router_w, want_h=dense)
        logits_t = jnp.swapaxes(routed[-1][:, :N_EXPERTS].reshape(nseg, n, N_EXPERTS), 1, 2)
        picked = select_tokens(logits_t, cap, want_idx=not dense)
        pos, gates = picked[0], picked[1]
        if dense:
            xg = gather_tokens(pos, routed[0].reshape(nseg, n, D_MODEL), cap, N_EXPERTS, n)
        else:
            xg = gather_rows(picked[2].reshape(nseg, N_EXPERTS, cap), grp["x"], g_ffn, grp["shift"], grp["scale"], n)
        sel.append(dict(cap=cap, pos=pos, xg=xg,
                        gates=jnp.swapaxes(gates, 0, 1).reshape(N_EXPERTS, nseg * cap, 1)))
    hid = expert_up(sel[0]["xg"], sel[1]["xg"], w_gate, w_up, layer)
    y = expert_down(hid, w_down, jnp.concatenate([s["gates"] for s in sel], axis=1), layer)
    out = []
    row_offset = 0
    for grp, s in zip(groups, sel):
        n, cap = grp["n"], s["cap"]
        pos_t = jnp.swapaxes(s["pos"], 1, 2)
        if n <= 512:
            out.append(combine_tokens(pos_t, y, grp["x"], grp["gate"], cap, row_offset, n, D_MODEL))
        else:
            tt, td = 1024, 512
            start, fits = _combine_windows(s["pos"], tt)
            args = (pos_t, y, grp["x"], grp["gate"], cap, row_offset, tt, td)
            out.append(lax.cond(fits, lambda a=args, w=start: combine_tokens_windowed(w, *a),
                                lambda a=args: combine_tokens(*a)))
        row_offset += grp["nseg"] * cap
    return out


def _rope_tables(n):
    t = jnp.arange(n)
    row = (t // GRID_W).astype(F32)
    col = (t % GRID_W).astype(F32)
    nf = HEAD_DIM // 4
    inv = ROPE_THETA ** (-jnp.arange(nf, dtype=F32) / nf)
    ang = jnp.concatenate([row[:, None] * inv, col[:, None] * inv], axis=-1)
    cos, sin = jnp.cos(ang), jnp.sin(ang)
    return jnp.concatenate([cos, cos], axis=-1), jnp.concatenate([-sin, sin], axis=-1)


def kernel(x_prompt, x_sample, cache_win_k, cache_win_v, cache_nat_k, cache_nat_v, cache_glob_k, cache_glob_v, c, c_ctx, mod_w, mod_b, norm_mix_w, norm_ffn_w, even_w_in, even_w_out, conv_w, conv_b, win_sink, win_q_norm, win_k_norm, odd_w_in, odd_w_out, nat_rpb, nat_q_norm, nat_k_norm, glob_q_norm, glob_k_norm, router_w, expert_w_gate, expert_w_up, expert_w_down):
    bp, sp, d = x_prompt.shape
    bs, ss, _ = x_sample.shape
    depth = mod_w.shape[0]
    rope = _rope_tables(ss)

    cond8 = jnp.zeros((8, d), F32).at[0].set(c_ctx).at[1:1 + bs].set(c)
    mods = modulation(cond8, mod_w, mod_b).reshape(depth, 8, N_MOD, d)

    xp = x_prompt.reshape(bp * sp, d)
    xs = x_sample.reshape(bs * ss, d)
    win_k, win_v, nat_k, nat_v, glob_k, glob_v = [], [], [], [], [], []

    def flat_cache(cache, i):
        return cache[:, i].reshape(bs, cache.shape[2], -1).astype(BF16)

    for layer in range(depth):
        i = layer // 2
        mp = [mods[layer, 0:1, k].reshape(1, 1, d) for k in range(N_MOD)]
        ms = [mods[layer, 1:1 + bs, k].reshape(bs, 1, d) for k in range(N_MOD)]
        if layer % 2 == 0:
            w_in = even_w_in[i].astype(BF16)
            w_out = even_w_out[i].astype(BF16)
            q0 = 3 * CONV_DIM
            qw = WIN_Q_HEADS * HEAD_DIM
            kw = WIN_KV_HEADS * HEAD_DIM
            group = WIN_Q_HEADS // WIN_KV_HEADS
            h = mod_norm(xp, norm_mix_w[layer], mp[0], mp[1])
            (_, conv_in), (q, _), (k, k32), (v, v32) = project(h, w_in, [
                (q0, None, 1.0, False, False, True), (qw, win_q_norm[i], ATTN_SCALE, False, True, False),
                (kw, win_k_norm[i], 1.0, False, True, True), (kw, None, 1.0, False, True, True)])
            ya = conv_gate(conv_in, conv_w[i], conv_b[i], sp)
            yb = dense_attention(q.reshape(bp, sp, qw), k.reshape(bp, sp, kw), v.reshape(bp, sp, kw),
                                 group, sp, sink=win_sink[i])
            xp = out_proj_residual(ya, yb.reshape(bp * sp, qw), w_out, xp, mp[2])
            win_k.append(k32.reshape(bp, sp, WIN_KV_HEADS, HEAD_DIM))
            win_v.append(v32.reshape(bp, sp, WIN_KV_HEADS, HEAD_DIM))
            h = mod_norm(xs, norm_mix_w[layer], ms[0], ms[1])
            (_, conv_in), (q, _), (k, _), (v, _) = project(h, w_in, [
                (q0, None, 1.0, False, False, True), (qw, win_q_norm[i], ATTN_SCALE, True, True, False),
                (kw, win_k_norm[i], 1.0, True, True, False), (kw, None, 1.0, False, True, False)], rope)
            ya = conv_gate(conv_in, conv_w[i], conv_b[i], ss)
            yb = window_attention(q.reshape(bs, ss, qw), k.reshape(bs, ss, kw), v.reshape(bs, ss, kw),
                                  flat_cache(cache_win_k, i), flat_cache(cache_win_v, i), win_sink[i], group)
            xs = out_proj_residual(ya, yb.reshape(bs * ss, qw), w_out, xs, ms[2])
        else:
            w_in = odd_w_in[i].astype(BF16)
            w_out = odd_w_out[i].astype(BF16)
            nw = NAT_HEADS * HEAD_DIM
            gqw = GLOB_Q_HEADS * HEAD_DIM
            gkw = GLOB_KV_HEADS * HEAD_DIM
            group = GLOB_Q_HEADS // GLOB_KV_HEADS
            kd0 = 3 * nw + gqw
            h = mod_norm(xp, norm_mix_w[layer], mp[0], mp[1])
            (qc, _), (kc, kc32), (vc, vc32), (qd, _), (kd, kd32), (vd, vd32) = project(h, w_in, [
                (nw, nat_q_norm[i], ATTN_SCALE, False, True, False), (nw, nat_k_norm[i], 1.0, False, True, True),
                (nw, None, 1.0, False, True, True), (gqw, glob_q_norm[i], ATTN_SCALE, False, True, False),
                (gkw, glob_k_norm[i], 1.0, False, True, True), (gkw, None, 1.0, False, True, True)])
            yc = dense_attention(qc.reshape(bp, sp, nw), kc.reshape(bp, sp, nw), vc.reshape(bp, sp, nw), 1, sp)
            yd = dense_attention(qd.reshape(bp, sp, gqw), kd.reshape(bp, sp, gkw), vd.reshape(bp, sp, gkw), group, sp)
            xp = out_proj_residual(yc.reshape(bp * sp, nw), yd.reshape(bp * sp, gqw), w_out, xp, mp[2])
            nat_k.append(kc32.reshape(bp, sp, NAT_HEADS, HEAD_DIM))
            nat_v.append(vc32.reshape(bp, sp, NAT_HEADS, HEAD_DIM))
            glob_k.append(kd32.reshape(bp, sp, GLOB_KV_HEADS, HEAD_DIM))
            glob_v.append(vd32.reshape(bp, sp, GLOB_KV_HEADS, HEAD_DIM))
            h = mod_norm(xs, norm_mix_w[layer], ms[0], ms[1])
            (qc, _), (kc, _), (vc, _), (qd, _), (kd, _), (vd, _) = project(h, w_in, [
                (nw, nat_q_norm[i], ATTN_SCALE, False, True, False), (nw, nat_k_norm[i], 1.0, False, True, False),
                (nw, None, 1.0, False, True, False), (gqw, glob_q_norm[i], ATTN_SCALE, True, True, False),
                (gkw, glob_k_norm[i], 1.0, True, True, False), (gkw, None, 1.0, False, True, False)], rope)
            yc = neighbourhood_attention(qc.reshape(bs, ss, nw), kc.reshape(bs, ss, nw), vc.reshape(bs, ss, nw),
                                         flat_cache(cache_nat_k, i), flat_cache(cache_nat_v, i),
                                         _nat_bias_table(nat_rpb[i]))
            k_all = jnp.concatenate([kd.reshape(bs, ss, gkw), flat_cache(cache_glob_k, i)], axis=1)
            v_all = jnp.concatenate([vd.reshape(bs, ss, gkw), flat_cache(cache_glob_v, i)], axis=1)
            yd = dense_attention(qd.reshape(bs, ss, gqw), k_all, v_all, group, 128)
            xs = out_proj_residual(yc.reshape(bs * ss, nw), yd.reshape(bs * ss, gqw), w_out, xs, ms[2])

        xp, xs = moe_layer(
            [dict(x=xp, shift=mp[3], scale=mp[4], gate=mp[5], nseg=bp, n=sp),
             dict(x=xs, shift=ms[3], scale=ms[4], gate=ms[5], nseg=bs, n=ss)],
            norm_ffn_w[layer], router_w[layer], expert_w_gate, expert_w_up, expert_w_down, layer)

    stack = lambda xs_: jnp.stack(xs_, axis=1)
    return (xp.reshape(bp, sp, d), xs.reshape(bs, ss, d), stack(win_k), stack(win_v), stack(nat_k),
            stack(nat_v), stack(glob_k), stack(glob_v))
```

```python
import functools

import numpy as np
import jax
import jax.numpy as jnp
from jax import lax
from jax.experimental import pallas as pl
from jax.experimental.pallas import tpu as pltpu

D_MODEL = 2048
HEAD_DIM = 128
GRID_W = 64
CONV_DIM = 512
WIN_Q_HEADS = 12
WIN_KV_HEADS = 4
WIN_BLOCK = 128
WINDOW = 128
NAT_HEADS = 8
NAT_ROWS = 8
NAT_COLS = 16
GLOB_Q_HEADS = 8
GLOB_KV_HEADS = 2
ROPE_THETA = 10000.0
N_EXPERTS = 16
EXPERT_FF = 2048
EC_CAPACITY = 2
N_MOD = 6
NORM_EPS = 1e-6
NEG_INF = -1e30
ATTN_SCALE = HEAD_DIM ** -0.5

LANES = 128
MIB = 1024 * 1024
BF16 = jnp.bfloat16
F32 = jnp.float32


def _params(semantics, vmem_mib, **extra):
    return pltpu.CompilerParams(dimension_semantics=semantics, vmem_limit_bytes=vmem_mib * MIB, **extra)


def _silu(x):
    return x * (1.0 / (1.0 + jnp.exp(-x)))


def _modulation_kernel(cond_ref, w_ref, b_ref, o_ref):
    s = _silu(cond_ref[...]).astype(BF16)
    o_ref[0] = jnp.dot(s, w_ref[0].astype(BF16), preferred_element_type=F32) + b_ref[0]


def modulation(cond8, mod_w, mod_b):
    depth, d, n = mod_w.shape
    tn = 1024
    return pl.pallas_call(
        _modulation_kernel,
        out_shape=jax.ShapeDtypeStruct((depth, 8, n), F32),
        grid=(depth, n // tn),
        in_specs=[pl.BlockSpec((8, d), lambda l, j: (0, 0)),
                  pl.BlockSpec((1, d, tn), lambda l, j: (l, 0, j)),
                  pl.BlockSpec((1, 1, tn), lambda l, j: (l, 0, j))],
        out_specs=pl.BlockSpec((1, 8, tn), lambda l, j: (l, 0, j)),
        compiler_params=_params(("parallel", "parallel"), 48),
        name="modulation",
    )(cond8, mod_w, mod_b.reshape(depth, 1, n))


def _mod_norm(x, g, shift, scale):
    ms = jnp.mean(x * x, axis=-1, keepdims=True)
    y = x * lax.rsqrt(ms + NORM_EPS) * g
    return y * (1.0 + scale) + shift


def _mod_norm_kernel(x_ref, g_ref, shift_ref, scale_ref, h_ref):
    h_ref[...] = _mod_norm(x_ref[...], g_ref[...], shift_ref[0], scale_ref[0]).astype(BF16)


def mod_norm(x, g, shift, scale):
    r, d = x.shape
    nseg = shift.shape[0]
    tm = 512
    tiles_per_seg = (r // nseg) // tm
    return pl.pallas_call(
        _mod_norm_kernel,
        out_shape=jax.ShapeDtypeStruct((r, d), BF16),
        grid=(r // tm,),
        in_specs=[pl.BlockSpec((tm, d), lambda i: (i, 0)),
                  pl.BlockSpec((1, d), lambda i: (0, 0)),
                  pl.BlockSpec((1, 1, d), lambda i: (i // tiles_per_seg, 0, 0)),
                  pl.BlockSpec((1, 1, d), lambda i: (i // tiles_per_seg, 0, 0))],
        out_specs=pl.BlockSpec((tm, d), lambda i: (i, 0)),
        compiler_params=_params(("parallel",), 32),
        name="mod_norm",
    )(x, g.reshape(1, d), shift, scale)


PROJ_TILE = 512


def _proj_kernel(*refs, tiles, n_gain, has_rope):
    h_ref, w_ref = refs[0], refs[1]
    k = 2
    if n_gain:
        g_ref = refs[k]
        k += 1
    if has_rope:
        cos_ref, sin_ref = refs[k], refs[k + 1]
        k += 2
    outs = refs[k:]
    j = pl.program_id(1)
    acc = jnp.dot(h_ref[...], w_ref[...], preferred_element_type=F32)
    ones = jnp.ones((HEAD_DIM, HEAD_DIM), BF16)

    def finish(parts):
        for lo, width, dst, gain_idx, scale, rope, o_bf16, o_f32 in parts:
            if gain_idx is None:
                x = acc[:, lo:lo + width]
                if o_bf16 is not None:
                    outs[o_bf16][:, dst:dst + width] = x.astype(BF16)
                if o_f32 is not None:
                    outs[o_f32][:, dst:dst + width] = x
                continue
            for hd in range(width // HEAD_DIM):
                x = acc[:, lo + hd * HEAD_DIM:lo + (hd + 1) * HEAD_DIM]
                sl = slice(dst + hd * HEAD_DIM, dst + (hd + 1) * HEAD_DIM)
                ssq = jnp.dot((x * x).astype(BF16), ones, preferred_element_type=F32)
                y = x * lax.rsqrt(ssq * (1.0 / HEAD_DIM) + NORM_EPS) * g_ref[gain_idx:gain_idx + 1, :]
                if o_f32 is not None:
                    outs[o_f32][:, sl] = y
                if rope:
                    y = y * cos_ref[...] + pltpu.roll(y, HEAD_DIM // 2, 1) * sin_ref[...]
                outs[o_bf16][:, sl] = (y * scale).astype(BF16)

    for jv, parts in enumerate(tiles):
        pl.when(j == jv)(functools.partial(finish, parts))


def project(h, w_bf16, pieces, rope_tables=None):
    r, d = h.shape
    n = w_bf16.shape[1]
    tm, tn = 1024, PROJ_TILE
    n_tiles = n // tn
    gains = [p[1] for p in pieces if p[1] is not None]
    has_rope = any(p[3] for p in pieces)
    tiles = [[] for _ in range(n_tiles)]
    out_shape, out_specs, result = [], [], []
    col0 = 0
    for width, gain, scale, rope, want_bf16, want_f32 in pieces:
        block_w = min(width, tn)
        j0, n_blk = col0 // tn, max(width // tn, 1)
        slots = []
        for want, dt in ((want_bf16, BF16), (want_f32, F32)):
            if want:
                slots.append(len(out_shape))
                out_shape.append(jax.ShapeDtypeStruct((r, width), dt))
                out_specs.append(pl.BlockSpec(
                    (tm, block_w), lambda i, j, j0=j0, n_blk=n_blk: (i, jnp.clip(j - j0, 0, n_blk - 1))))
            else:
                slots.append(None)
        result.append(slots)
        gain_idx = None if gain is None else [g is gain for g in gains].index(True)
        for jv in range(j0, (col0 + width - 1) // tn + 1):
            start = max(col0, jv * tn)
            stop = min(col0 + width, (jv + 1) * tn)
            tiles[jv].append((start - jv * tn, stop - start, (start - col0) % block_w, gain_idx, scale, rope,
                              slots[0], slots[1]))
        col0 += width
    assert col0 == n
    in_specs = [pl.BlockSpec((tm, d), lambda i, j: (i, 0)),
                pl.BlockSpec((d, tn), lambda i, j: (0, j))]
    args = [h, w_bf16]
    if gains:
        in_specs.append(pl.BlockSpec((len(gains), HEAD_DIM), lambda i, j: (0, 0)))
        args.append(jnp.stack(gains))
    if has_rope:
        per = rope_tables[0].shape[0] // tm
        in_specs += [pl.BlockSpec((tm, HEAD_DIM), lambda i, j: (i % per, 0))] * 2
        args += list(rope_tables)
    out = pl.pallas_call(
        functools.partial(_proj_kernel, tiles=tiles, n_gain=len(gains), has_rope=has_rope),
        out_shape=out_shape,
        grid=(r // tm, n_tiles),
        in_specs=in_specs,
        out_specs=out_specs,
        compiler_params=_params(("parallel", "arbitrary"), 56),
        name="project",
    )(*args)
    return [tuple(None if s is None else out[s] for s in slots) for slots in result]


def _conv_gate_kernel(ab_ref, ac_ref, ah_ref, w_ref, b_ref, o_ref, *, seq):
    u = ac_ref[...] * ah_ref[...]
    rows = u.shape[0]
    pos = lax.broadcasted_iota(jnp.int32, u.shape, 0) % seq
    prev = jnp.where(pos == 0, 0.0, pltpu.roll(u, 1, 0))
    nxt = jnp.where(pos == seq - 1, 0.0, pltpu.roll(u, rows - 1, 0))
    conv = prev * w_ref[0:1, :] + u * w_ref[1:2, :] + nxt * w_ref[2:3, :] + b_ref[...]
    o_ref[...] = (ab_ref[...] * conv).astype(o_ref.dtype)


def conv_gate(proj, conv_w, conv_b, seq):
    r = proj.shape[0]
    tr = 4096
    nc = CONV_DIM // LANES
    return pl.pallas_call(
        functools.partial(_conv_gate_kernel, seq=seq),
        out_shape=jax.ShapeDtypeStruct((r, CONV_DIM), BF16),
        grid=(r // tr, nc),
        in_specs=[pl.BlockSpec((tr, LANES), lambda i, c: (i, c)),
                  pl.BlockSpec((tr, LANES), lambda i, c: (i, nc + c)),
                  pl.BlockSpec((tr, LANES), lambda i, c: (i, 2 * nc + c)),
                  pl.BlockSpec((3, LANES), lambda i, c: (0, c)),
                  pl.BlockSpec((1, LANES), lambda i, c: (0, c))],
        out_specs=pl.BlockSpec((tr, LANES), lambda i, c: (i, c)),
        compiler_params=_params(("parallel", "parallel"), 48),
        name="conv_gate",
    )(proj, proj, proj, conv_w, conv_b.reshape(1, CONV_DIM))


def _dense_attn_kernel(*refs, group, has_sink):
    if has_sink:
        sink_ref, q_ref, k_ref, v_ref, o_ref = refs
    else:
        q_ref, k_ref, v_ref, o_ref = refs
    qb = q_ref.shape[1]
    for hkv in range(k_ref.shape[2] // HEAD_DIM):
        head = lambda g: slice((hkv * group + g) * HEAD_DIM, (hkv * group + g + 1) * HEAD_DIM)
        kv = slice(hkv * HEAD_DIM, (hkv + 1) * HEAD_DIM)
        q = jnp.concatenate([q_ref[0, :, head(g)] for g in range(group)], axis=0)
        s = lax.dot_general(q, k_ref[0, :, kv], (((1,), (1,)), ((), ())), preferred_element_type=F32)
        m = jnp.max(s, axis=-1, keepdims=True)
        if has_sink:
            sk = jnp.concatenate([jnp.full((qb, 1), sink_ref[hkv * group + g], F32) for g in range(group)], axis=0)
            m = jnp.maximum(m, sk)
        p = jnp.exp(s - m)
        den = jnp.sum(p, axis=-1, keepdims=True)
        if has_sink:
            den = den + jnp.exp(sk - m)
        o = jnp.dot(p.astype(BF16), v_ref[0, :, kv], preferred_element_type=F32) / den
        for g in range(group):
            o_ref[0, :, head(g)] = o[g * qb:(g + 1) * qb].astype(o_ref.dtype)


def dense_attention(q, k, v, group, q_block, sink=None):
    b, nq, qw = q.shape
    nk = k.shape[1]
    kvw = k.shape[2]
    in_specs = [pl.BlockSpec((1, q_block, qw), lambda bi, qi: (bi, qi, 0)),
                pl.BlockSpec((1, nk, kvw), lambda bi, qi: (bi, 0, 0)),
                pl.BlockSpec((1, nk, kvw), lambda bi, qi: (bi, 0, 0))]
    args = [q, k, v]
    if sink is not None:
        in_specs = [pl.BlockSpec(memory_space=pltpu.SMEM)] + in_specs
        args = [sink] + args
    return pl.pallas_call(
        functools.partial(_dense_attn_kernel, group=group, has_sink=sink is not None),
        out_shape=jax.ShapeDtypeStruct((b, nq, qw), BF16),
        grid=(b, nq // q_block),
        in_specs=in_specs,
        out_specs=pl.BlockSpec((1, q_block, qw), lambda bi, qi: (bi, qi, 0)),
        compiler_params=_params(("parallel", "parallel"), 56),
        name="dense_attention",
    )(*args)


def _window_attn_kernel(sink_ref, q_ref, kp_ref, kc_ref, kn_ref, vp_ref, vc_ref, vn_ref, ck_ref, cv_ref, o_ref,
                        *, group, n_tokens):
    blk = pl.program_id(1)
    wb = WIN_BLOCK
    rows, cols = group * wb, 3 * wb + ck_ref.shape[1]
    q_pos = blk * wb + lax.broadcasted_iota(jnp.int32, (rows, cols), 0) % wb
    col = lax.broadcasted_iota(jnp.int32, (rows, cols), 1)
    k_pos = (blk - 1) * wb + col
    local_ok = (jnp.abs(k_pos - q_pos) <= WINDOW) & (k_pos >= 0) & (k_pos < n_tokens)
    visible = (col >= 3 * wb) | local_ok
    for hkv in range(kc_ref.shape[2] // HEAD_DIM):
        head = lambda g: slice((hkv * group + g) * HEAD_DIM, (hkv * group + g + 1) * HEAD_DIM)
        kv = slice(hkv * HEAD_DIM, (hkv + 1) * HEAD_DIM)
        q = jnp.concatenate([q_ref[0, :, head(g)] for g in range(group)], axis=0)
        keys = jnp.concatenate([kp_ref[0, :, kv], kc_ref[0, :, kv], kn_ref[0, :, kv], ck_ref[0, :, kv]], axis=0)
        vals = jnp.concatenate([vp_ref[0, :, kv], vc_ref[0, :, kv], vn_ref[0, :, kv], cv_ref[0, :, kv]], axis=0)
        s = lax.dot_general(q, keys, (((1,), (1,)), ((), ())), preferred_element_type=F32)
        s = jnp.where(visible, s, NEG_INF)
        sk = jnp.concatenate([jnp.full((wb, 1), sink_ref[hkv * group + g], F32) for g in range(group)], axis=0)
        m = jnp.maximum(jnp.max(s, axis=-1, keepdims=True), sk)
        p = jnp.exp(s - m)
        den = jnp.sum(p, axis=-1, keepdims=True) + jnp.exp(sk - m)
        o = jnp.dot(p.astype(BF16), vals, preferred_element_type=F32) / den
        for g in range(group):
            o_ref[0, :, head(g)] = o[g * wb:(g + 1) * wb].astype(o_ref.dtype)


def window_attention(q, k, v, ctx_k, ctx_v, sink, group):
    b, n, qw = q.shape
    kvw = k.shape[2]
    nb = n // WIN_BLOCK
    nctx = ctx_k.shape[1]
    prev_map = lambda bi, i: (bi, jnp.maximum(i - 1, 0), 0)
    cur_map = lambda bi, i: (bi, i, 0)
    next_map = lambda bi, i: (bi, jnp.minimum(i + 1, nb - 1), 0)
    ctx_map = lambda bi, i: (bi, 0, 0)
    kv_block = (1, WIN_BLOCK, kvw)
    return pl.pallas_call(
        functools.partial(_window_attn_kernel, group=group, n_tokens=n),
        out_shape=jax.ShapeDtypeStruct((b, n, qw), BF16),
        grid=(b, nb),
        in_specs=[pl.BlockSpec(memory_space=pltpu.SMEM),
                  pl.BlockSpec((1, WIN_BLOCK, qw), cur_map),
                  pl.BlockSpec(kv_block, prev_map), pl.BlockSpec(kv_block, cur_map), pl.BlockSpec(kv_block, next_map),
                  pl.BlockSpec(kv_block, prev_map), pl.BlockSpec(kv_block, cur_map), pl.BlockSpec(kv_block, next_map),
                  pl.BlockSpec((1, nctx, kvw), ctx_map), pl.BlockSpec((1, nctx, kvw), ctx_map)],
        out_specs=pl.BlockSpec((1, WIN_BLOCK, qw), cur_map),
        compiler_params=_params(("parallel", "parallel"), 32),
        name="window_attention",
    )(sink, q, k, k, k, v, v, v, ctx_k, ctx_v)


NAT_QROWS = 8
NAT_KROWS = 16


def _nat_bias_table(rpb):
    w = GRID_W
    c = np.arange(w)[:, None]
    kc = np.arange(w)[None, :]
    ws = np.clip(c - NAT_COLS // 2, 0, w - NAT_COLS)
    col_ok = (kc >= ws) & (kc < ws + NAT_COLS)
    col_idx = np.clip(kc - c + NAT_COLS - 1, 0, 2 * NAT_COLS - 2)
    t = jnp.where(col_ok[None, None], rpb[:, :, col_idx], NEG_INF)
    dead = jnp.full_like(t[:, :1], NEG_INF)
    t = jnp.concatenate([dead, t, dead], axis=1)
    return jnp.concatenate([t[:, :-1], t[:, 1:]], axis=-1)


def _nat_attn_kernel(q_ref, k_ref, v_ref, ck_ref, cv_ref, t_ref, o_ref, *, grid_rows):
    m = pl.program_id(2)
    w = GRID_W
    r0 = m * NAT_QROWS
    kr0 = jnp.clip(r0 - NAT_ROWS // 2, 0, grid_rows - NAT_KROWS)
    tok0 = pl.multiple_of(kr0 * w, 4 * w)
    lane_hi = lax.broadcasted_iota(jnp.int32, (w, 2 * w), 1) >= w
    layout = []
    for i in range(NAT_QROWS):
        r = r0 + i
        rs = jnp.clip(r - NAT_ROWS // 2, 0, grid_rows - NAT_ROWS)
        row = []
        for jj in range(NAT_KROWS // 2):
            kr = kr0 + 2 * jj
            d = kr - r + NAT_ROWS - 1
            ok_lo = ((kr >= rs) & (kr < rs + NAT_ROWS)).astype(jnp.int32)
            ok_hi = ((kr + 1 >= rs) & (kr + 1 < rs + NAT_ROWS)).astype(jnp.int32)
            row.append((jnp.clip(d, -1, 2 * NAT_ROWS - 2) + 1, jnp.where(lane_hi, ok_hi, ok_lo) > 0))
        layout.append(row)
    for hh in range(q_ref.shape[2] // HEAD_DIM):
        hs = slice(hh * HEAD_DIM, (hh + 1) * HEAD_DIM)
        keys = k_ref[0, pl.ds(tok0, NAT_KROWS * w), hs]
        vals = v_ref[0, pl.ds(tok0, NAT_KROWS * w), hs]
        q = q_ref[0, :, hs]
        s_loc = lax.dot_general(q, keys, (((1,), (1,)), ((), ())), preferred_element_type=F32)
        bias = jnp.concatenate(
            [jnp.concatenate([jnp.where(ok, t_ref[hh, entry], NEG_INF) for entry, ok in row], axis=1)
             for row in layout], axis=0)
        s_loc = s_loc + bias
        s_ctx = lax.dot_general(q, ck_ref[0, :, hs], (((1,), (1,)), ((), ())), preferred_element_type=F32)
        mx = jnp.maximum(jnp.max(s_loc, axis=-1, keepdims=True), jnp.max(s_ctx, axis=-1, keepdims=True))
        p_loc = jnp.exp(s_loc - mx)
        p_ctx = jnp.exp(s_ctx - mx)
        den = jnp.sum(p_loc, axis=-1, keepdims=True) + jnp.sum(p_ctx, axis=-1, keepdims=True)
        o = (jnp.dot(p_loc.astype(BF16), vals, preferred_element_type=F32)
             + jnp.dot(p_ctx.astype(BF16), cv_ref[0, :, hs], preferred_element_type=F32))
        o_ref[0, :, hs] = (o / den).astype(o_ref.dtype)


NAT_HEADS_PER_STEP = 2


def neighbourhood_attention(q, k, v, ctx_k, ctx_v, bias_table):
    b, n, hw = q.shape
    hp = NAT_HEADS_PER_STEP
    hpw = hp * HEAD_DIM
    grid_rows = n // GRID_W
    nctx = ctx_k.shape[1]
    tq = NAT_QROWS * GRID_W
    return pl.pallas_call(
        functools.partial(_nat_attn_kernel, grid_rows=grid_rows),
        out_shape=jax.ShapeDtypeStruct((b, n, hw), BF16),
        grid=(b, hw // hpw, grid_rows // NAT_QROWS),
        in_specs=[pl.BlockSpec((1, tq, hpw), lambda bi, h, m: (bi, m, h)),
                  pl.BlockSpec((1, n, hpw), lambda bi, h, m: (bi, 0, h)),
                  pl.BlockSpec((1, n, hpw), lambda bi, h, m: (bi, 0, h)),
                  pl.BlockSpec((1, nctx, hpw), lambda bi, h, m: (bi, 0, h)),
                  pl.BlockSpec((1, nctx, hpw), lambda bi, h, m: (bi, 0, h)),
                  pl.BlockSpec((hp, 2 * NAT_ROWS, GRID_W, 2 * GRID_W), lambda bi, h, m: (h, 0, 0, 0))],
        out_specs=pl.BlockSpec((1, tq, hpw), lambda bi, h, m: (bi, m, h)),
        compiler_params=_params(("parallel", "parallel", "arbitrary"), 40),
        name="neighbourhood_attention",
    )(q, k, v, ctx_k, ctx_v, bias_table)


def _out_proj_kernel(a_ref, b_ref, w_ref, x_ref, gate_ref, o_ref):
    ka = a_ref.shape[1]
    acc = jnp.dot(a_ref[...], w_ref[:ka, :], preferred_element_type=F32)
    acc = acc + jnp.dot(b_ref[...], w_ref[ka:, :], preferred_element_type=F32)
    o_ref[...] = x_ref[...] + gate_ref[0] * acc


def out_proj_residual(a, b_, w_bf16, x, gate):
    r, d = x.shape
    ka, kb = a.shape[1], b_.shape[1]
    nseg = gate.shape[0]
    tm, tn = 1024, 1024
    tiles_per_seg = (r // nseg) // tm
    return pl.pallas_call(
        _out_proj_kernel,
        out_shape=jax.ShapeDtypeStruct((r, d), F32),
        grid=(r // tm, d // tn),
        in_specs=[pl.BlockSpec((tm, ka), lambda i, j: (i, 0)),
                  pl.BlockSpec((tm, kb), lambda i, j: (i, 0)),
                  pl.BlockSpec((ka + kb, tn), lambda i, j: (0, j)),
                  pl.BlockSpec((tm, tn), lambda i, j: (i, j)),
                  pl.BlockSpec((1, 1, tn), lambda i, j: (i // tiles_per_seg, 0, j))],
        out_specs=pl.BlockSpec((tm, tn), lambda i, j: (i, j)),
        compiler_params=_params(("parallel", "parallel"), 56),
        name="out_proj_residual",
    )(a, b_, w_bf16, x, gate)


def _norm_router_kernel(x_ref, g_ref, shift_ref, scale_ref, whi_ref, wlo_ref, *out_refs):
    logit_ref = out_refs[-1]
    h = _mod_norm(x_ref[...], g_ref[...], shift_ref[0], scale_ref[0])
    h_hi = h.astype(BF16)
    h_lo = (h - h_hi.astype(F32)).astype(BF16)
    if len(out_refs) > 1:
        out_refs[0][...] = h_hi
    w_hi = whi_ref[...]
    logit_ref[...] = (jnp.dot(h_hi, w_hi, preferred_element_type=F32)
                      + jnp.dot(h_lo, w_hi, preferred_element_type=F32)
                      + jnp.dot(h_hi, wlo_ref[...], preferred_element_type=F32))


def norm_router(x, g, shift, scale, router_w, want_h):
    r, d = x.shape
    nseg = shift.shape[0]
    tm = 512
    tiles_per_seg = (r // nseg) // tm
    w_pad = jnp.zeros((d, LANES), F32).at[:, :N_EXPERTS].set(router_w)
    w_hi = w_pad.astype(BF16)
    w_lo = (w_pad - w_hi.astype(F32)).astype(BF16)
    out_shape = [jax.ShapeDtypeStruct((r, LANES), F32)]
    out_specs = [pl.BlockSpec((tm, LANES), lambda i: (i, 0))]
    if want_h:
        out_shape.insert(0, jax.ShapeDtypeStruct((r, d), BF16))
        out_specs.insert(0, pl.BlockSpec((tm, d), lambda i: (i, 0)))
    return pl.pallas_call(
        _norm_router_kernel,
        out_shape=out_shape,
        grid=(r // tm,),
        in_specs=[pl.BlockSpec((tm, d), lambda i: (i, 0)),
                  pl.BlockSpec((1, d), lambda i: (0, 0)),
                  pl.BlockSpec((1, 1, d), lambda i: (i // tiles_per_seg, 0, 0)),
                  pl.BlockSpec((1, 1, d), lambda i: (i // tiles_per_seg, 0, 0)),
                  pl.BlockSpec((d, LANES), lambda i: (0, 0)),
                  pl.BlockSpec((d, LANES), lambda i: (0, 0))],
        out_specs=out_specs,
        compiler_params=_params(("parallel",), 32),
        name="norm_router",
    )(x, g.reshape(1, d), shift, scale, w_hi, w_lo)


CUMSUM_BLOCK = 256


def _exclusive_cumsum_lanes(flags):
    e, n = flags.shape
    cb = min(CUMSUM_BLOCK, n)
    tri = (lax.broadcasted_iota(jnp.int32, (cb, cb), 0) < lax.broadcasted_iota(jnp.int32, (cb, cb), 1)).astype(BF16)
    carry = jnp.zeros((e, 1), F32)
    out = []
    for j in range(n // cb):
        blk = flags[:, j * cb:(j + 1) * cb]
        out.append(jnp.dot(blk.astype(BF16), tri, preferred_element_type=F32) + carry)
        carry = carry + jnp.sum(blk, axis=-1, keepdims=True)
    return jnp.concatenate(out, axis=1) if len(out) > 1 else out[0]


def _select_kernel(logit_ref, pos_ref, gate_ref, *idx_refs, cap):
    lg = logit_ref[...]
    sb, ne, n = lg.shape
    ex = jnp.exp(lg - jnp.max(lg, axis=1, keepdims=True))
    aff = (ex / jnp.sum(ex, axis=1, keepdims=True)).reshape(sb * ne, n)
    e = sb * ne

    def enough(t):
        return jnp.sum((aff >= t).astype(F32), axis=-1, keepdims=True) >= cap

    hi = jnp.full((e, 1), 2.0, F32)
    for step in (64, 32, 16, 8, 4, 2, 1):
        cand = hi * (2.0 ** -step)
        hi = jnp.where(enough(cand), hi, cand)
    lo = hi * 0.5
    lo = jnp.where(enough(lo), lo, 0.0)
    for _ in range(23):
        mid = (lo + hi) * 0.5
        ok = enough(mid)
        lo = jnp.where(ok, mid, lo)
        hi = jnp.where(ok, hi, mid)
    thr = lo
    above = (aff > thr).astype(F32)
    tied = (aff == thr).astype(F32)
    need = cap - jnp.sum(above, axis=-1, keepdims=True)
    take = above + tied * (_exclusive_cumsum_lanes(tied) < need).astype(F32)
    slot = _exclusive_cumsum_lanes(take)
    pos = jnp.where(take > 0, slot, -1.0).astype(jnp.int32)
    pos_ref[...] = pos.reshape(sb, ne, n)
    s_iota = lax.broadcasted_iota(jnp.int32, (cap, n), 0)
    token = lax.broadcasted_iota(jnp.int32, (1, n), 1)
    t_hi = (token // 64).astype(F32)
    t_lo = (token % 64).astype(F32)
    row = lax.broadcasted_iota(jnp.int32, (8, n), 0)
    for ei in range(e):
        a = aff[ei:ei + 1, :]
        a0 = a.astype(BF16).astype(F32)
        a1 = (a - a0).astype(BF16).astype(F32)
        a2 = a - a0 - a1
        vals = jnp.where(row == 0, a0, jnp.where(row == 1, a1, jnp.where(row == 2, a2,
                         jnp.where(row == 3, t_hi, jnp.where(row == 4, t_lo, 0.0)))))
        hit = jnp.where(pos[ei:ei + 1, :] == s_iota, 1.0, 0.0).astype(BF16)
        res = lax.dot_general(vals.astype(BF16), hit, (((1,), (1,)), ((), ())), preferred_element_type=F32)
        si, xi = divmod(ei, ne)
        gate_ref[si, xi:xi + 1, :] = res[0:1] + res[1:2] + res[2:3]
        if idx_refs:
            idx_refs[0][si, xi:xi + 1, :] = (res[3:4] * 64.0 + res[4:5]).astype(jnp.int32)


def select_tokens(logits_t, cap, want_idx):
    nseg, e, n = logits_t.shape
    sb = min(nseg, 8)
    slots = (nseg, e, cap)
    out_shape = [jax.ShapeDtypeStruct((nseg, e, n), jnp.int32), jax.ShapeDtypeStruct(slots, F32)]
    out_specs = [pl.BlockSpec((sb, e, n), lambda s: (s, 0, 0)), pl.BlockSpec((sb, e, cap), lambda s: (s, 0, 0))]
    if want_idx:
        out_shape.append(jax.ShapeDtypeStruct(slots, jnp.int32))
        out_specs.append(pl.BlockSpec((sb, e, cap), lambda s: (s, 0, 0)))
    return pl.pallas_call(
        functools.partial(_select_kernel, cap=cap),
        out_shape=out_shape,
        grid=(nseg // sb,),
        in_specs=[pl.BlockSpec((sb, e, n), lambda s: (s, 0, 0))],
        out_specs=out_specs,
        compiler_params=_params(("parallel",), 48),
        name="select_tokens",
    )(logits_t)


def _gather_rows_kernel(idx_ref, x_hbm, g_ref, shift_ref, scale_ref, o_ref, buf, sem, *, cap, n, experts):
    step = pl.program_id(0)
    slot = step % 2

    def row_copy(st, sl, k):
        row = (st // experts) * n + idx_ref[st * cap + k]
        return pltpu.make_async_copy(x_hbm.at[pl.ds(row, 1), :], buf.at[sl, pl.ds(k, 1), :], sem.at[sl])

    def issue(st, sl):
        def body(k, carry):
            row_copy(st, sl, k).start()
            return carry
        lax.fori_loop(0, cap, body, 0, unroll=8)

    @pl.when(step == 0)
    def _():
        issue(step, slot)

    @pl.when(step + 1 < pl.num_programs(0))
    def _():
        issue(step + 1, 1 - slot)

    pltpu.make_async_copy(x_hbm.at[pl.ds(0, cap), :], buf.at[slot], sem.at[slot]).wait()
    h = _mod_norm(buf[slot], g_ref[...], shift_ref[0], scale_ref[0])
    o_ref[0, 0] = h.astype(o_ref.dtype)


def gather_rows(idx, x, g, shift, scale, n):
    nseg, e, cap = idx.shape
    d = x.shape[1]
    return pl.pallas_call(
        functools.partial(_gather_rows_kernel, cap=cap, n=n, experts=e),
        out_shape=jax.ShapeDtypeStruct((nseg, e, cap, d), BF16),
        grid_spec=pltpu.PrefetchScalarGridSpec(
            num_scalar_prefetch=1,
            grid=(nseg * e,),
            in_specs=[pl.BlockSpec(memory_space=pl.ANY),
                      pl.BlockSpec((1, d), lambda st, idx_ref: (0, 0)),
                      pl.BlockSpec((1, 1, d), lambda st, idx_ref: (st // e, 0, 0)),
                      pl.BlockSpec((1, 1, d), lambda st, idx_ref: (st // e, 0, 0))],
            out_specs=pl.BlockSpec((1, 1, cap, d), lambda st, idx_ref: (st // e, st % e, 0, 0)),
            scratch_shapes=[pltpu.VMEM((2, cap, d), F32), pltpu.SemaphoreType.DMA((2,))]),
        compiler_params=_params(("arbitrary",), 32, disable_bounds_checks=True),
        name="gather_rows",
    )(idx.reshape(-1), x, g.reshape(1, d), shift, scale)


def _gather_kernel(pos_ref, h_ref, o_ref, acc_ref, *, cap):
    kt = pl.program_id(2)
    eb = pos_ref.shape[1]
    tk = pos_ref.shape[3]
    s_iota = lax.broadcasted_iota(jnp.int32, (cap, tk), 0)
    onehot = jnp.concatenate([(pos_ref[0, i] == s_iota).astype(BF16) for i in range(eb)], axis=0)
    part = jnp.dot(onehot, h_ref[0], preferred_element_type=F32)

    @pl.when(kt == 0)
    def _():
        acc_ref[...] = part

    @pl.when(kt > 0)
    def _():
        acc_ref[...] += part

    @pl.when(kt == pl.num_programs(2) - 1)
    def _():
        o_ref[0] = acc_ref[...].reshape(o_ref.shape[1:]).astype(o_ref.dtype)


def gather_tokens(pos, h, cap, experts_per_step, tk):
    nseg, e, n = pos.shape
    d = h.shape[2]
    eb = experts_per_step
    return pl.pallas_call(
        functools.partial(_gather_kernel, cap=cap),
        out_shape=jax.ShapeDtypeStruct((nseg, e, cap, d), BF16),
        grid=(nseg, e // eb, n // tk),
        in_specs=[pl.BlockSpec((1, eb, 1, tk), lambda s, ei, kt: (s, ei, 0, kt)),
                  pl.BlockSpec((1, tk, d), lambda s, ei, kt: (s, kt, 0))],
        out_specs=pl.BlockSpec((1, eb, cap, d), lambda s, ei, kt: (s, ei, 0, 0)),
        scratch_shapes=[pltpu.VMEM((eb * cap, d), F32)],
        compiler_params=_params(("parallel", "parallel", "arbitrary"), 48),
        name="gather_tokens",
    )(pos.reshape(nseg, e, 1, n), h)


def _expert_up_kernel(xc_ref, xl_ref, wg_ref, wu_ref, o_ref):
    d = xc_ref.shape[-1]
    wg = wg_ref[0, 0].astype(BF16)
    wu = wu_ref[0, 0].astype(BF16)
    row = 0
    for x_ref in (xc_ref, xl_ref):
        rows = x_ref.shape[0] * x_ref.shape[2]
        x = x_ref[...].reshape(rows, d)
        a = jnp.dot(x, wg, preferred_element_type=F32)
        b = jnp.dot(x, wu, preferred_element_type=F32)
        o_ref[0, row:row + rows, :] = (_silu(a) * b).astype(o_ref.dtype)
        row += rows


def expert_up(xg_ctx, xg_lat, w_gate, w_up, layer):
    _, e, d, f = w_gate.shape
    tf = 512
    sc, _, cc, _ = xg_ctx.shape
    sl, _, cl, _ = xg_lat.shape
    rows = sc * cc + sl * cl
    return pl.pallas_call(
        _expert_up_kernel,
        out_shape=jax.ShapeDtypeStruct((e, rows, f), BF16),
        grid=(e, f // tf),
        in_specs=[pl.BlockSpec((sc, 1, cc, d), lambda ei, j: (0, ei, 0, 0)),
                  pl.BlockSpec((sl, 1, cl, d), lambda ei, j: (0, ei, 0, 0)),
                  pl.BlockSpec((1, 1, d, tf), lambda ei, j: (layer, ei, 0, j)),
                  pl.BlockSpec((1, 1, d, tf), lambda ei, j: (layer, ei, 0, j))],
        out_specs=pl.BlockSpec((1, rows, tf), lambda ei, j: (ei, 0, j)),
        compiler_params=_params(("parallel", "arbitrary"), 56),
        name="expert_up",
    )(xg_ctx, xg_lat, w_gate, w_up)


def _expert_down_kernel(h_ref, w_ref, gate_ref, o_ref):
    y = jnp.dot(h_ref[0], w_ref[0, 0].astype(BF16), preferred_element_type=F32)
    o_ref[0] = (y * gate_ref[0]).astype(o_ref.dtype)


def expert_down(hid, w_down, gates, layer):
    e, rows, f = hid.shape
    d = w_down.shape[3]
    td = 512
    return pl.pallas_call(
        _expert_down_kernel,
        out_shape=jax.ShapeDtypeStruct((e, rows, d), BF16),
        grid=(e, d // td),
        in_specs=[pl.BlockSpec((1, rows, f), lambda ei, j: (ei, 0, 0)),
                  pl.BlockSpec((1, 1, f, td), lambda ei, j: (layer, ei, 0, j)),
                  pl.BlockSpec((1, rows, 1), lambda ei, j: (ei, 0, 0))],
        out_specs=pl.BlockSpec((1, rows, td), lambda ei, j: (ei, 0, j)),
        compiler_params=_params(("parallel", "arbitrary"), 48),
        name="expert_down",
    )(hid, w_down, gates)


def _combine_kernel(post_ref, y_ref, x_ref, gate_ref, o_ref, hit_ref, *, cap):
    ne = y_ref.shape[0]
    tt = post_ref.shape[1]

    @pl.when(pl.program_id(2) == 0)
    def _():
        post = post_ref[0]
        width = max(cap, LANES)
        per = width // cap
        lane = lax.broadcasted_iota(jnp.int32, (tt, width), 1)
        for blk in range(ne // per):
            hit = None
            for i in range(per):
                col = post[:, blk * per + i:blk * per + i + 1]
                h = jnp.where(col >= 0, col + i * cap, -1) == lane
                hit = h if hit is None else (hit | h)
            hit_ref[:, blk * width:(blk + 1) * width] = jnp.where(hit, 1.0, 0.0).astype(BF16)

    y = y_ref[...].reshape(ne * cap, y_ref.shape[-1])
    o_ref[...] = x_ref[...] + gate_ref[0] * jnp.dot(hit_ref[...], y, preferred_element_type=F32)


def combine_tokens(pos_t, y, x, gate, cap, row_offset, tt, td):
    nseg, n, e = pos_t.shape
    r, d = x.shape
    nt = n // tt
    rb = row_offset // cap
    return pl.pallas_call(
        functools.partial(_combine_kernel, cap=cap),
        out_shape=jax.ShapeDtypeStruct((r, d), F32),
        grid=(nseg, nt, d // td),
        in_specs=[pl.BlockSpec((1, tt, e), lambda s, t, j: (s, t, 0)),
                  pl.BlockSpec((e, cap, td), lambda s, t, j: (0, rb + s, j)),
                  pl.BlockSpec((tt, td), lambda s, t, j: (s * nt + t, j)),
                  pl.BlockSpec((1, 1, td), lambda s, t, j: (s * gate.shape[0] // nseg, 0, j))],
        out_specs=pl.BlockSpec((tt, td), lambda s, t, j: (s * nt + t, j)),
        scratch_shapes=[pltpu.VMEM((tt, e * cap), BF16)],
        compiler_params=_params(("parallel", "parallel", "arbitrary"), 56),
        name="combine_tokens",
    )(pos_t, y, x, gate)


COMBINE_WINDOW = 256
COMBINE_ALIGN = 64


def _combine_window_kernel(ws_ref, post_ref, *refs, experts):
    y_refs = refs[:experts]
    x_ref, gate_ref, o_ref, hit_ref = refs[experts:]
    s, t = pl.program_id(0), pl.program_id(1)
    tt = post_ref.shape[1]
    win = COMBINE_WINDOW

    @pl.when(pl.program_id(2) == 0)
    def _():
        post = post_ref[0]
        lane = lax.broadcasted_iota(jnp.int32, (tt, win), 1)
        for e in range(experts):
            start = ws_ref[(s * pl.num_programs(1) + t) * experts + e]
            col = post[:, e:e + 1]
            local = jnp.where(col >= 0, col - start, -1)
            hit_ref[:, e * win:(e + 1) * win] = jnp.where(local == lane, 1.0, 0.0).astype(BF16)

    acc = jnp.dot(hit_ref[:, 0:win], y_refs[0][0], preferred_element_type=F32)
    for e in range(1, experts):
        acc = acc + jnp.dot(hit_ref[:, e * win:(e + 1) * win], y_refs[e][0], preferred_element_type=F32)
    o_ref[...] = x_ref[...] + gate_ref[0] * acc


def combine_tokens_windowed(win_start, pos_t, y, x, gate, cap, row_offset, tt, td):
    nseg, n, e = pos_t.shape
    r, d = x.shape
    nt = n // tt

    def y_spec(ei):
        return pl.BlockSpec((pl.Element(1), pl.Element(COMBINE_WINDOW), pl.Element(td)),
                            lambda s, t, j, ws: (
                                ei,
                                pl.multiple_of(row_offset + s * cap + ws[(s * nt + t) * e + ei], COMBINE_ALIGN),
                                j * td))

    return pl.pallas_call(
        functools.partial(_combine_window_kernel, experts=e),
        out_shape=jax.ShapeDtypeStruct((r, d), F32),
        grid_spec=pltpu.PrefetchScalarGridSpec(
            num_scalar_prefetch=1,
            grid=(nseg, nt, d // td),
            in_specs=[pl.BlockSpec((1, tt, e), lambda s, t, j, ws: (s, t, 0))]
                     + [y_spec(ei) for ei in range(e)]
                     + [pl.BlockSpec((tt, td), lambda s, t, j, ws: (s * nt + t, j)),
                        pl.BlockSpec((1, 1, td), lambda s, t, j, ws: (s * gate.shape[0] // nseg, 0, j))],
            out_specs=pl.BlockSpec((tt, td), lambda s, t, j, ws: (s * nt + t, j)),
            scratch_shapes=[pltpu.VMEM((tt, e * COMBINE_WINDOW), BF16)]),
        compiler_params=_params(("parallel", "parallel", "arbitrary"), 56),
        name="combine_tokens_windowed",
    )(win_start.reshape(-1), pos_t, *([y] * e), x, gate)


def _combine_windows(pos, tt):
    nseg, e, n = pos.shape
    taken = (pos >= 0).reshape(nseg, e, n // tt, tt).sum(-1)
    first = jnp.cumsum(taken, axis=-1) - taken
    cap = taken.sum(-1, keepdims=True)
    start = jnp.minimum(first // COMBINE_ALIGN * COMBINE_ALIGN, cap - COMBINE_WINDOW)
    fits = jnp.all(first + taken <= start + COMBINE_WINDOW)
    return jnp.swapaxes(start, 1, 2).astype(jnp.int32), fits


def moe_layer(groups, g_ffn, router_w, w_gate, w_up, w_down, layer):
    sel = []
    for grp in groups:
        nseg, n = grp["nseg"], grp["n"]
        cap = (EC_CAPACITY * n) // N_EXPERTS
        dense = n <= 512
        routed = norm_router(grp["x"], g_ffn, grp["shift"], grp["scale"], router_w, want_h=dense)
        logits_t = jnp.swapaxes(routed[-1][:, :N_EXPERTS].reshape(nseg, n, N_EXPERTS), 1, 2)
        picked = select_tokens(logits_t, cap, want_idx=not dense)
        pos, gates = picked[0], picked[1]
        if dense:
            xg = gather_tokens(pos, routed[0].reshape(nseg, n, D_MODEL), cap, N_EXPERTS, n)
        else:
            xg = gather_rows(picked[2].reshape(nseg, N_EXPERTS, cap), grp["x"], g_ffn, grp["shift"], grp["scale"], n)
        sel.append(dict(cap=cap, pos=pos, xg=xg,
                        gates=jnp.swapaxes(gates, 0, 1).reshape(N_EXPERTS, nseg * cap, 1)))
    hid = expert_up(sel[0]["xg"], sel[1]["xg"], w_gate, w_up, layer)
    y = expert_down(hid, w_down, jnp.concatenate([s["gates"] for s in sel], axis=1), layer)
    out = []
    row_offset = 0
    for grp, s in zip(groups, sel):
        n, cap = grp["n"], s["cap"]
        pos_t = jnp.swapaxes(s["pos"], 1, 2)
        if n <= 512:
            out.append(combine_tokens(pos_t, y, grp["x"], grp["gate"], cap, row_offset, n, D_MODEL))
        else:
            tt, td = 1024, 512
            start, fits = _combine_windows(s["pos"], tt)
            args = (pos_t, y, grp["x"], grp["gate"], cap, row_offset, tt, td)
            out.append(lax.cond(fits, lambda a=args, w=start: combine_tokens_windowed(w, *a),
                                lambda a=args: combine_tokens(*a)))
        row_offset += grp["nseg"] * cap
    return out


def _rope_tables(n):
    t = jnp.arange(n)
    row = (t // GRID_W).astype(F32)
    col = (t % GRID_W).astype(F32)
    nf = HEAD_DIM // 4
    inv = ROPE_THETA ** (-jnp.arange(nf, dtype=F32) / nf)
    ang = jnp.concatenate([row[:, None] * inv, col[:, None] * inv], axis=-1)
    cos, sin = jnp.cos(ang), jnp.sin(ang)
    return jnp.concatenate([cos, cos], axis=-1), jnp.concatenate([-sin, sin], axis=-1)


def kernel(x_prompt, x_sample, cache_win_k, cache_win_v, cache_nat_k, cache_nat_v, cache_glob_k, cache_glob_v, c, c_ctx, mod_w, mod_b, norm_mix_w, norm_ffn_w, even_w_in, even_w_out, conv_w, conv_b, win_sink, win_q_norm, win_k_norm, odd_w_in, odd_w_out, nat_rpb, nat_q_norm, nat_k_norm, glob_q_norm, glob_k_norm, router_w, expert_w_gate, expert_w_up, expert_w_down):
    bp, sp, d = x_prompt.shape
    bs, ss, _ = x_sample.shape
    depth = mod_w.shape[0]
    rope = _rope_tables(ss)

    cond8 = jnp.zeros((8, d), F32).at[0].set(c_ctx).at[1:1 + bs].set(c)
    mods = modulation(cond8, mod_w, mod_b).reshape(depth, 8, N_MOD, d)

    xp = x_prompt.reshape(bp * sp, d)
    xs = x_sample.reshape(bs * ss, d)
    win_k, win_v, nat_k, nat_v, glob_k, glob_v = [], [], [], [], [], []

    def flat_cache(cache, i):
        return cache[:, i].reshape(bs, cache.shape[2], -1).astype(BF16)

    for layer in range(depth):
        i = layer // 2
        mp = [mods[layer, 0:1, k].reshape(1, 1, d) for k in range(N_MOD)]
        ms = [mods[layer, 1:1 + bs, k].reshape(bs, 1, d) for k in range(N_MOD)]
        if layer % 2 == 0:
            w_in = even_w_in[i].astype(BF16)
            w_out = even_w_out[i].astype(BF16)
            q0 = 3 * CONV_DIM
            qw = WIN_Q_HEADS * HEAD_DIM
            kw = WIN_KV_HEADS * HEAD_DIM
            group = WIN_Q_HEADS // WIN_KV_HEADS
            h = mod_norm(xp, norm_mix_w[layer], mp[0], mp[1])
            (_, conv_in), (q, _), (k, k32), (v, v32) = project(h, w_in, [
                (q0, None, 1.0, False, False, True), (qw, win_q_norm[i], ATTN_SCALE, False, True, False),
                (kw, win_k_norm[i], 1.0, False, True, True), (kw, None, 1.0, False, True, True)])
            ya = conv_gate(conv_in, conv_w[i], conv_b[i], sp)
            yb = dense_attention(q.reshape(bp, sp, qw), k.reshape(bp, sp, kw), v.reshape(bp, sp, kw),
                                 group, sp, sink=win_sink[i])
            xp = out_proj_residual(ya, yb.reshape(bp * sp, qw), w_out, xp, mp[2])
            win_k.append(k32.reshape(bp, sp, WIN_KV_HEADS, HEAD_DIM))
            win_v.append(v32.reshape(bp, sp, WIN_KV_HEADS, HEAD_DIM))
            h = mod_norm(xs, norm_mix_w[layer], ms[0], ms[1])
            (_, conv_in), (q, _), (k, _), (v, _) = project(h, w_in, [
                (q0, None, 1.0, False, False, True), (qw, win_q_norm[i], ATTN_SCALE, True, True, False),
                (kw, win_k_norm[i], 1.0, True, True, False), (kw, None, 1.0, False, True, False)], rope)
            ya = conv_gate(conv_in, conv_w[i], conv_b[i], ss)
            yb = window_attention(q.reshape(bs, ss, qw), k.reshape(bs, ss, kw), v.reshape(bs, ss, kw),
                                  flat_cache(cache_win_k, i), flat_cache(cache_win_v, i), win_sink[i], group)
            xs = out_proj_residual(ya, yb.reshape(bs * ss, qw), w_out, xs, ms[2])
        else:
            w_in = odd_w_in[i].astype(BF16)
            w_out = odd_w_out[i].astype(BF16)
            nw = NAT_HEADS * HEAD_DIM
            gqw = GLOB_Q_HEADS * HEAD_DIM
            gkw = GLOB_KV_HEADS * HEAD_DIM
            group = GLOB_Q_HEADS // GLOB_KV_HEADS
            kd0 = 3 * nw + gqw
            h = mod_norm(xp, norm_mix_w[layer], mp[0], mp[1])
            (qc, _), (kc, kc32), (vc, vc32), (qd, _), (kd, kd32), (vd, vd32) = project(h, w_in, [
                (nw, nat_q_norm[i], ATTN_SCALE, False, True, False), (nw, nat_k_norm[i], 1.0, False, True, True),
                (nw, None, 1.0, False, True, True), (gqw, glob_q_norm[i], ATTN_SCALE, False, True, False),
                (gkw, glob_k_norm[i], 1.0, False, True, True), (gkw, None, 1.0, False, True, True)])
            yc = dense_attention(qc.reshape(bp, sp, nw), kc.reshape(bp, sp, nw), vc.reshape(bp, sp, nw), 1, sp)
            yd = dense_attention(qd.reshape(bp, sp, gqw), kd.reshape(bp, sp, gkw), vd.reshape(bp, sp, gkw), group, sp)
            xp = out_proj_residual(yc.reshape(bp * sp, nw), yd.reshape(bp * sp, gqw), w_out, xp, mp[2])
            nat_k.append(kc32.reshape(bp, sp, NAT_HEADS, HEAD_DIM))
            nat_v.append(vc32.reshape(bp, sp, NAT_HEADS, HEAD_DIM))
            glob_k.append(kd32.reshape(bp, sp, GLOB_KV_HEADS, HEAD_DIM))
            glob_v.append(vd32.reshape(bp, sp, GLOB_KV_HEADS, HEAD_DIM))
            h = mod_norm(xs, norm_mix_w[layer], ms[0], ms[1])
            (qc, _), (kc, _), (vc, _), (qd, _), (kd, _), (vd, _) = project(h, w_in, [
                (nw, nat_q_norm[i], ATTN_SCALE, False, True, False), (nw, nat_k_norm[i], 1.0, False, True, False),
                (nw, None, 1.0, False, True, False), (gqw, glob_q_norm[i], ATTN_SCALE, True, True, False),
                (gkw, glob_k_norm[i], 1.0, True, True, False), (gkw, None, 1.0, False, True, False)], rope)
            yc = neighbourhood_attention(qc.reshape(bs, ss, nw), kc.reshape(bs, ss, nw), vc.reshape(bs, ss, nw),
                                         flat_cache(cache_nat_k, i), flat_cache(cache_nat_v, i),
                                         _nat_bias_table(nat_rpb[i]))
            k_all = jnp.concatenate([kd.reshape(bs, ss, gkw), flat_cache(cache_glob_k, i)], axis=1)
            v_all = jnp.concatenate([vd.reshape(bs, ss, gkw), flat_cache(cache_glob_v, i)], axis=1)
            yd = dense_attention(qd.reshape(bs, ss, gqw), k_all, v_all, group, 128)
            xs = out_proj_residual(yc.reshape(bs * ss, nw), yd.reshape(bs * ss, gqw), w_out, xs, ms[2])

        xp, xs = moe_layer(
            [dict(x=xp, shift=mp[3], scale=mp[4], gate=mp[5], nseg=bp, n=sp),
             dict(x=xs, shift=ms[3], scale=ms[4], gate=ms[5], nseg=bs, n=ss)],
            norm_ffn_w[layer], router_w[layer], expert_w_gate, expert_w_up, expert_w_down, layer)

    stack = lambda xs_: jnp.stack(xs_, axis=1)
    return (xp.reshape(bp, sp, d), xs.reshape(bs, ss, d), stack(win_k), stack(win_v), stack(nat_k),
            stack(nat_v), stack(glob_k), stack(glob_v))
```

```python
import functools

import numpy as np
import jax
import jax.numpy as jnp
from jax import lax
from jax.experimental import pallas as pl
from jax.experimental.pallas import tpu as pltpu

D_MODEL = 2048
HEAD_DIM = 128
GRID_W = 64
CONV_DIM = 512
WIN_Q_HEADS = 12
WIN_KV_HEADS = 4
WIN_BLOCK = 128
WINDOW = 128
NAT_HEADS = 8
NAT_ROWS = 8
NAT_COLS = 16
GLOB_Q_HEADS = 8
GLOB_KV_HEADS = 2
ROPE_THETA = 10000.0
N_EXPERTS = 16
EXPERT_FF = 2048
EC_CAPACITY = 2
N_MOD = 6
NORM_EPS = 1e-6
NEG_INF = -1e30
ATTN_SCALE = HEAD_DIM ** -0.5

LANES = 128
MIB = 1024 * 1024
BF16 = jnp.bfloat16
F32 = jnp.float32


def _params(semantics, vmem_mib, **extra):
    return pltpu.CompilerParams(dimension_semantics=semantics, vmem_limit_bytes=vmem_mib * MIB, **extra)


def _silu(x):
    return x * (1.0 / (1.0 + jnp.exp(-x)))


def _modulation_kernel(cond_ref, w_ref, b_ref, o_ref):
    s = _silu(cond_ref[...]).astype(BF16)
    o_ref[0] = jnp.dot(s, w_ref[0].astype(BF16), preferred_element_type=F32) + b_ref[0]


def modulation(cond8, mod_w, mod_b):
    depth, d, n = mod_w.shape
    tn = 1024
    return pl.pallas_call(
        _modulation_kernel,
        out_shape=jax.ShapeDtypeStruct((depth, 8, n), F32),
        grid=(depth, n // tn),
        in_specs=[pl.BlockSpec((8, d), lambda l, j: (0, 0)),
                  pl.BlockSpec((1, d, tn), lambda l, j: (l, 0, j)),
                  pl.BlockSpec((1, 1, tn), lambda l, j: (l, 0, j))],
        out_specs=pl.BlockSpec((1, 8, tn), lambda l, j: (l, 0, j)),
        compiler_params=_params(("parallel", "parallel"), 48),
        name="modulation",
    )(cond8, mod_w, mod_b.reshape(depth, 1, n))


def _mod_norm(x, g, shift, scale):
    ms = jnp.mean(x * x, axis=-1, keepdims=True)
    y = x * lax.rsqrt(ms + NORM_EPS) * g
    return y * (1.0 + scale) + shift


def _mod_norm_kernel(x_ref, g_ref, shift_ref, scale_ref, h_ref):
    h_ref[...] = _mod_norm(x_ref[...], g_ref[...], shift_ref[0], scale_ref[0]).astype(BF16)


def mod_norm(x, g, shift, scale):
    r, d = x.shape
    nseg = shift.shape[0]
    tm = 512
    tiles_per_seg = (r // nseg) // tm
    return pl.pallas_call(
        _mod_norm_kernel,
        out_shape=jax.ShapeDtypeStruct((r, d), BF16),
        grid=(r // tm,),
        in_specs=[pl.BlockSpec((tm, d), lambda i: (i, 0)),
                  pl.BlockSpec((1, d), lambda i: (0, 0)),
                  pl.BlockSpec((1, 1, d), lambda i: (i // tiles_per_seg, 0, 0)),
                  pl.BlockSpec((1, 1, d), lambda i: (i // tiles_per_seg, 0, 0))],
        out_specs=pl.BlockSpec((tm, d), lambda i: (i, 0)),
        compiler_params=_params(("parallel",), 32),
        name="mod_norm",
    )(x, g.reshape(1, d), shift, scale)


PROJ_TILE = 512


def _proj_kernel(*refs, tiles, n_gain, has_rope):
    h_ref, w_ref = refs[0], refs[1]
    k = 2
    if n_gain:
        g_ref = refs[k]
        k += 1
    if has_rope:
        cos_ref, sin_ref = refs[k], refs[k + 1]
        k += 2
    outs = refs[k:]
    j = pl.program_id(1)
    acc = jnp.dot(h_ref[...], w_ref[...], preferred_element_type=F32)
    ones = jnp.ones((HEAD_DIM, HEAD_DIM), BF16)

    def finish(parts):
        for lo, width, dst, gain_idx, scale, rope, o_bf16, o_f32 in parts:
            if gain_idx is None:
                x = acc[:, lo:lo + width]
                if o_bf16 is not None:
                    outs[o_bf16][:, dst:dst + width] = x.astype(BF16)
                if o_f32 is not None:
                    outs[o_f32][:, dst:dst + width] = x
                continue
            for hd in range(width // HEAD_DIM):
                x = acc[:, lo + hd * HEAD_DIM:lo + (hd + 1) * HEAD_DIM]
                sl = slice(dst + hd * HEAD_DIM, dst + (hd + 1) * HEAD_DIM)
                ssq = jnp.dot((x * x).astype(BF16), ones, preferred_element_type=F32)
                y = x * lax.rsqrt(ssq * (1.0 / HEAD_DIM) + NORM_EPS) * g_ref[gain_idx:gain_idx + 1, :]
                if o_f32 is not None:
                    outs[o_f32][:, sl] = y
                if rope:
                    y = y * cos_ref[...] + pltpu.roll(y, HEAD_DIM // 2, 1) * sin_ref[...]
                outs[o_bf16][:, sl] = (y * scale).astype(BF16)

    for jv, parts in enumerate(tiles):
        pl.when(j == jv)(functools.partial(finish, parts))


def project(h, w_bf16, pieces, rope_tables=None):
    r, d = h.shape
    n = w_bf16.shape[1]
    tm, tn = 1024, PROJ_TILE
    n_tiles = n // tn
    gains = [p[1] for p in pieces if p[1] is not None]
    has_rope = any(p[3] for p in pieces)
    tiles = [[] for _ in range(n_tiles)]
    out_shape, out_specs, result = [], [], []
    col0 = 0
    for width, gain, scale, rope, want_bf16, want_f32 in pieces:
        block_w = min(width, tn)
        j0, n_blk = col0 // tn, max(width // tn, 1)
        slots = []
        for want, dt in ((want_bf16, BF16), (want_f32, F32)):
            if want:
                slots.append(len(out_shape))
                out_shape.append(jax.ShapeDtypeStruct((r, width), dt))
                out_specs.append(pl.BlockSpec(
                    (tm, block_w), lambda i, j, j0=j0, n_blk=n_blk: (i, jnp.clip(j - j0, 0, n_blk - 1))))
            else:
                slots.append(None)
        result.append(slots)
        gain_idx = None if gain is None else [g is gain for g in gains].index(True)
        for jv in range(j0, (col0 + width - 1) // tn + 1):
            start = max(col0, jv * tn)
            stop = min(col0 + width, (jv + 1) * tn)
            tiles[jv].append((start - jv * tn, stop - start, (start - col0) % block_w, gain_idx, scale, rope,
                              slots[0], slots[1]))
        col0 += width
    assert col0 == n
    in_specs = [pl.BlockSpec((tm, d), lambda i, j: (i, 0)),
                pl.BlockSpec((d, tn), lambda i, j: (0, j))]
    args = [h, w_bf16]
    if gains:
        in_specs.append(pl.BlockSpec((len(gains), HEAD_DIM), lambda i, j: (0, 0)))
        args.append(jnp.stack(gains))
    if has_rope:
        per = rope_tables[0].shape[0] // tm
        in_specs += [pl.BlockSpec((tm, HEAD_DIM), lambda i, j: (i % per, 0))] * 2
        args += list(rope_tables)
    out = pl.pallas_call(
        functools.partial(_proj_kernel, tiles=tiles, n_gain=len(gains), has_rope=has_rope),
        out_shape=out_shape,
        grid=(r // tm, n_tiles),
        in_specs=in_specs,
        out_specs=out_specs,
        compiler_params=_params(("parallel", "arbitrary"), 56),
        name="project",
    )(*args)
    return [tuple(None if s is None else out[s] for s in slots) for slots in result]


def _conv_gate_kernel(ab_ref, ac_ref, ah_ref, w_ref, b_ref, o_ref, *, seq):
    u = ac_ref[...] * ah_ref[...]
    rows = u.shape[0]
    pos = lax.broadcasted_iota(jnp.int32, u.shape, 0) % seq
    prev = jnp.where(pos == 0, 0.0, pltpu.roll(u, 1, 0))
    nxt = jnp.where(pos == seq - 1, 0.0, pltpu.roll(u, rows - 1, 0))
    conv = prev * w_ref[0:1, :] + u * w_ref[1:2, :] + nxt * w_ref[2:3, :] + b_ref[...]
    o_ref[...] = (ab_ref[...] * conv).astype(o_ref.dtype)


def conv_gate(proj, conv_w, conv_b, seq):
    r = proj.shape[0]
    tr = 4096
    nc = CONV_DIM // LANES
    return pl.pallas_call(
        functools.partial(_conv_gate_kernel, seq=seq),
        out_shape=jax.ShapeDtypeStruct((r, CONV_DIM), BF16),
        grid=(r // tr, nc),
        in_specs=[pl.BlockSpec((tr, LANES), lambda i, c: (i, c)),
                  pl.BlockSpec((tr, LANES), lambda i, c: (i, nc + c)),
                  pl.BlockSpec((tr, LANES), lambda i, c: (i, 2 * nc + c)),
                  pl.BlockSpec((3, LANES), lambda i, c: (0, c)),
                  pl.BlockSpec((1, LANES), lambda i, c: (0, c))],
        out_specs=pl.BlockSpec((tr, LANES), lambda i, c: (i, c)),
        compiler_params=_params(("parallel", "parallel"), 48),
        name="conv_gate",
    )(proj, proj, proj, conv_w, conv_b.reshape(1, CONV_DIM))


def _dense_attn_kernel(*refs, group, has_sink):
    if has_sink:
        sink_ref, q_ref, k_ref, v_ref, o_ref = refs
    else:
        q_ref, k_ref, v_ref, o_ref = refs
    qb = q_ref.shape[1]
    for hkv in range(k_ref.shape[2] // HEAD_DIM):
        head = lambda g: slice((hkv * group + g) * HEAD_DIM, (hkv * group + g + 1) * HEAD_DIM)
        kv = slice(hkv * HEAD_DIM, (hkv + 1) * HEAD_DIM)
        q = jnp.concatenate([q_ref[0, :, head(g)] for g in range(group)], axis=0)
        s = lax.dot_general(q, k_ref[0, :, kv], (((1,), (1,)), ((), ())), preferred_element_type=F32)
        m = jnp.max(s, axis=-1, keepdims=True)
        if has_sink:
            sk = jnp.concatenate([jnp.full((qb, 1), sink_ref[hkv * group + g], F32) for g in range(group)], axis=0)
            m = jnp.maximum(m, sk)
        p = jnp.exp(s - m)
        den = jnp.sum(p, axis=-1, keepdims=True)
        if has_sink:
            den = den + jnp.exp(sk - m)
        o = jnp.dot(p.astype(BF16), v_ref[0, :, kv], preferred_element_type=F32) / den
        for g in range(group):
            o_ref[0, :, head(g)] = o[g * qb:(g + 1) * qb].astype(o_ref.dtype)


def dense_attention(q, k, v, group, q_block, sink=None):
    b, nq, qw = q.shape
    nk = k.shape[1]
    kvw = k.shape[2]
    in_specs = [pl.BlockSpec((1, q_block, qw), lambda bi, qi: (bi, qi, 0)),
                pl.BlockSpec((1, nk, kvw), lambda bi, qi: (bi, 0, 0)),
                pl.BlockSpec((1, nk, kvw), lambda bi, qi: (bi, 0, 0))]
    args = [q, k, v]
    if sink is not None:
        in_specs = [pl.BlockSpec(memory_space=pltpu.SMEM)] + in_specs
        args = [sink] + args
    return pl.pallas_call(
        functools.partial(_dense_attn_kernel, group=group, has_sink=sink is not None),
        out_shape=jax.ShapeDtypeStruct((b, nq, qw), BF16),
        grid=(b, nq // q_block),
        in_specs=in_specs,
        out_specs=pl.BlockSpec((1, q_block, qw), lambda bi, qi: (bi, qi, 0)),
        compiler_params=_params(("parallel", "parallel"), 56),
        name="dense_attention",
    )(*args)


def _window_attn_kernel(sink_ref, q_ref, kp_ref, kc_ref, kn_ref, vp_ref, vc_ref, vn_ref, ck_ref, cv_ref, o_ref,
                        *, group, n_tokens):
    blk = pl.program_id(1)
    wb = WIN_BLOCK
    rows, cols = group * wb, 3 * wb + ck_ref.shape[1]
    q_pos = blk * wb + lax.broadcasted_iota(jnp.int32, (rows, cols), 0) % wb
    col = lax.broadcasted_iota(jnp.int32, (rows, cols), 1)
    k_pos = (blk - 1) * wb + col
    local_ok = (jnp.abs(k_pos - q_pos) <= WINDOW) & (k_pos >= 0) & (k_pos < n_tokens)
    visible = (col >= 3 * wb) | local_ok
    for hkv in range(kc_ref.shape[2] // HEAD_DIM):
        head = lambda g: slice((hkv * group + g) * HEAD_DIM, (hkv * group + g + 1) * HEAD_DIM)
        kv = slice(hkv * HEAD_DIM, (hkv + 1) * HEAD_DIM)
        q = jnp.concatenate([q_ref[0, :, head(g)] for g in range(group)], axis=0)
        keys = jnp.concatenate([kp_ref[0, :, kv], kc_ref[0, :, kv], kn_ref[0, :, kv], ck_ref[0, :, kv]], axis=0)
        vals = jnp.concatenate([vp_ref[0, :, kv], vc_ref[0, :, kv], vn_ref[0, :, kv], cv_ref[0, :, kv]], axis=0)
        s = lax.dot_general(q, keys, (((1,), (1,)), ((), ())), preferred_element_type=F32)
        s = jnp.where(visible, s, NEG_INF)
        sk = jnp.concatenate([jnp.full((wb, 1), sink_ref[hkv * group + g], F32) for g in range(group)], axis=0)
        m = jnp.maximum(jnp.max(s, axis=-1, keepdims=True), sk)
        p = jnp.exp(s - m)
        den = jnp.sum(p, axis=-1, keepdims=True) + jnp.exp(sk - m)
        o = jnp.dot(p.astype(BF16), vals, preferred_element_type=F32) / den
        for g in range(group):
            o_ref[0, :, head(g)] = o[g * wb:(g + 1) * wb].astype(o_ref.dtype)


def window_attention(q, k, v, ctx_k, ctx_v, sink, group):
    b, n, qw = q.shape
    kvw = k.shape[2]
    nb = n // WIN_BLOCK
    nctx = ctx_k.shape[1]
    prev_map = lambda bi, i: (bi, jnp.maximum(i - 1, 0), 0)
    cur_map = lambda bi, i: (bi, i, 0)
    next_map = lambda bi, i: (bi, jnp.minimum(i + 1, nb - 1), 0)
    ctx_map = lambda bi, i: (bi, 0, 0)
    kv_block = (1, WIN_BLOCK, kvw)
    return pl.pallas_call(
        functools.partial(_window_attn_kernel, group=group, n_tokens=n),
        out_shape=jax.ShapeDtypeStruct((b, n, qw), BF16),
        grid=(b, nb),
        in_specs=[pl.BlockSpec(memory_space=pltpu.SMEM),
                  pl.BlockSpec((1, WIN_BLOCK, qw), cur_map),
                  pl.BlockSpec(kv_block, prev_map), pl.BlockSpec(kv_block, cur_map), pl.BlockSpec(kv_block, next_map),
                  pl.BlockSpec(kv_block, prev_map), pl.BlockSpec(kv_block, cur_map), pl.BlockSpec(kv_block, next_map),
                  pl.BlockSpec((1, nctx, kvw), ctx_map), pl.BlockSpec((1, nctx, kvw), ctx_map)],
        out_specs=pl.BlockSpec((1, WIN_BLOCK, qw), cur_map),
        compiler_params=_params(("parallel", "parallel"), 32),
        name="window_attention",
    )(sink, q, k, k, k, v, v, v, ctx_k, ctx_v)


NAT_QROWS = 8
NAT_KROWS = 16


def _nat_bias_table(rpb):
    w = GRID_W
    c = np.arange(w)[:, None]
    kc = np.arange(w)[None, :]
    ws = np.clip(c - NAT_COLS // 2, 0, w - NAT_COLS)
    col_ok = (kc >= ws) & (kc < ws + NAT_COLS)
    col_idx = np.clip(kc - c + NAT_COLS - 1, 0, 2 * NAT_COLS - 2)
    pick = (col_idx[..., None] == np.arange(2 * NAT_COLS - 1)).astype(np.float32)
    looked_up = jnp.einsum("hdj,ckj->hdck", rpb, pick, precision=lax.Precision.HIGHEST)
    t = jnp.where(col_ok[None, None], looked_up, NEG_INF)
    dead = jnp.full_like(t[:, :1], NEG_INF)
    t = jnp.concatenate([dead, t, dead], axis=1)
    return jnp.concatenate([t[:, :-1], t[:, 1:]], axis=-1)


def _nat_attn_kernel(q_ref, k_ref, v_ref, ck_ref, cv_ref, t_ref, o_ref, *, grid_rows):
    m = pl.program_id(2)
    w = GRID_W
    r0 = m * NAT_QROWS
    kr0 = jnp.clip(r0 - NAT_ROWS // 2, 0, grid_rows - NAT_KROWS)
    tok0 = pl.multiple_of(kr0 * w, 4 * w)
    lane_hi = lax.broadcasted_iota(jnp.int32, (w, 2 * w), 1) >= w
    layout = []
    for i in range(NAT_QROWS):
        r = r0 + i
        rs = jnp.clip(r - NAT_ROWS // 2, 0, grid_rows - NAT_ROWS)
        row = []
        for jj in range(NAT_KROWS // 2):
            kr = kr0 + 2 * jj
            d = kr - r + NAT_ROWS - 1
            ok_lo = ((kr >= rs) & (kr < rs + NAT_ROWS)).astype(jnp.int32)
            ok_hi = ((kr + 1 >= rs) & (kr + 1 < rs + NAT_ROWS)).astype(jnp.int32)
            row.append((jnp.clip(d, -1, 2 * NAT_ROWS - 2) + 1, jnp.where(lane_hi, ok_hi, ok_lo) > 0))
        layout.append(row)
    for hh in range(q_ref.shape[2] // HEAD_DIM):
        hs = slice(hh * HEAD_DIM, (hh + 1) * HEAD_DIM)
        keys = k_ref[0, pl.ds(tok0, NAT_KROWS * w), hs]
        vals = v_ref[0, pl.ds(tok0, NAT_KROWS * w), hs]
        q = q_ref[0, :, hs]
        s_loc = lax.dot_general(q, keys, (((1,), (1,)), ((), ())), preferred_element_type=F32)
        bias = jnp.concatenate(
            [jnp.concatenate([jnp.where(ok, t_ref[hh, entry], NEG_INF) for entry, ok in row], axis=1)
             for row in layout], axis=0)
        s_loc = s_loc + bias
        s_ctx = lax.dot_general(q, ck_ref[0, :, hs], (((1,), (1,)), ((), ())), preferred_element_type=F32)
        mx = jnp.maximum(jnp.max(s_loc, axis=-1, keepdims=True), jnp.max(s_ctx, axis=-1, keepdims=True))
        p_loc = jnp.exp(s_loc - mx)
        p_ctx = jnp.exp(s_ctx - mx)
        den = jnp.sum(p_loc, axis=-1, keepdims=True) + jnp.sum(p_ctx, axis=-1, keepdims=True)
        o = (jnp.dot(p_loc.astype(BF16), vals, preferred_element_type=F32)
             + jnp.dot(p_ctx.astype(BF16), cv_ref[0, :, hs], preferred_element_type=F32))
        o_ref[0, :, hs] = (o / den).astype(o_ref.dtype)


NAT_HEADS_PER_STEP = 2


def neighbourhood_attention(q, k, v, ctx_k, ctx_v, bias_table):
    b, n, hw = q.shape
    hp = NAT_HEADS_PER_STEP
    hpw = hp * HEAD_DIM
    grid_rows = n // GRID_W
    nctx = ctx_k.shape[1]
    tq = NAT_QROWS * GRID_W
    return pl.pallas_call(
        functools.partial(_nat_attn_kernel, grid_rows=grid_rows),
        out_shape=jax.ShapeDtypeStruct((b, n, hw), BF16),
        grid=(b, hw // hpw, grid_rows // NAT_QROWS),
        in_specs=[pl.BlockSpec((1, tq, hpw), lambda bi, h, m: (bi, m, h)),
                  pl.BlockSpec((1, n, hpw), lambda bi, h, m: (bi, 0, h)),
                  pl.BlockSpec((1, n, hpw), lambda bi, h, m: (bi, 0, h)),
                  pl.BlockSpec((1, nctx, hpw), lambda bi, h, m: (bi, 0, h)),
                  pl.BlockSpec((1, nctx, hpw), lambda bi, h, m: (bi, 0, h)),
                  pl.BlockSpec((hp, 2 * NAT_ROWS, GRID_W, 2 * GRID_W), lambda bi, h, m: (h, 0, 0, 0))],
        out_specs=pl.BlockSpec((1, tq, hpw), lambda bi, h, m: (bi, m, h)),
        compiler_params=_params(("parallel", "parallel", "arbitrary"), 40),
        name="neighbourhood_attention",
    )(q, k, v, ctx_k, ctx_v, bias_table)


def _out_proj_kernel(a_ref, b_ref, w_ref, x_ref, gate_ref, o_ref):
    ka = a_ref.shape[1]
    acc = jnp.dot(a_ref[...], w_ref[:ka, :], preferred_element_type=F32)
    acc = acc + jnp.dot(b_ref[...], w_ref[ka:, :], preferred_element_type=F32)
    o_ref[...] = x_ref[...] + gate_ref[0] * acc


def out_proj_residual(a, b_, w_bf16, x, gate):
    r, d = x.shape
    ka, kb = a.shape[1], b_.shape[1]
    nseg = gate.shape[0]
    tm, tn = 1024, 1024
    tiles_per_seg = (r // nseg) // tm
    return pl.pallas_call(
        _out_proj_kernel,
        out_shape=jax.ShapeDtypeStruct((r, d), F32),
        grid=(r // tm, d // tn),
        in_specs=[pl.BlockSpec((tm, ka), lambda i, j: (i, 0)),
                  pl.BlockSpec((tm, kb), lambda i, j: (i, 0)),
                  pl.BlockSpec((ka + kb, tn), lambda i, j: (0, j)),
                  pl.BlockSpec((tm, tn), lambda i, j: (i, j)),
                  pl.BlockSpec((1, 1, tn), lambda i, j: (i // tiles_per_seg, 0, j))],
        out_specs=pl.BlockSpec((tm, tn), lambda i, j: (i, j)),
        compiler_params=_params(("parallel", "parallel"), 56),
        name="out_proj_residual",
    )(a, b_, w_bf16, x, gate)


def _norm_router_kernel(x_ref, g_ref, shift_ref, scale_ref, whi_ref, wlo_ref, *out_refs):
    logit_ref = out_refs[-1]
    h = _mod_norm(x_ref[...], g_ref[...], shift_ref[0], scale_ref[0])
    h_hi = h.astype(BF16)
    h_lo = (h - h_hi.astype(F32)).astype(BF16)
    if len(out_refs) > 1:
        out_refs[0][...] = h_hi
    w_hi = whi_ref[...]
    logit_ref[...] = (jnp.dot(h_hi, w_hi, preferred_element_type=F32)
                      + jnp.dot(h_lo, w_hi, preferred_element_type=F32)
                      + jnp.dot(h_hi, wlo_ref[...], preferred_element_type=F32))


def norm_router(x, g, shift, scale, router_w, want_h):
    r, d = x.shape
    nseg = shift.shape[0]
    tm = 512
    tiles_per_seg = (r // nseg) // tm
    w_pad = jnp.zeros((d, LANES), F32).at[:, :N_EXPERTS].set(router_w)
    w_hi = w_pad.astype(BF16)
    w_lo = (w_pad - w_hi.astype(F32)).astype(BF16)
    out_shape = [jax.ShapeDtypeStruct((r, LANES), F32)]
    out_specs = [pl.BlockSpec((tm, LANES), lambda i: (i, 0))]
    if want_h:
        out_shape.insert(0, jax.ShapeDtypeStruct((r, d), BF16))
        out_specs.insert(0, pl.BlockSpec((tm, d), lambda i: (i, 0)))
    return pl.pallas_call(
        _norm_router_kernel,
        out_shape=out_shape,
        grid=(r // tm,),
        in_specs=[pl.BlockSpec((tm, d), lambda i: (i, 0)),
                  pl.BlockSpec((1, d), lambda i: (0, 0)),
                  pl.BlockSpec((1, 1, d), lambda i: (i // tiles_per_seg, 0, 0)),
                  pl.BlockSpec((1, 1, d), lambda i: (i // tiles_per_seg, 0, 0)),
                  pl.BlockSpec((d, LANES), lambda i: (0, 0)),
                  pl.BlockSpec((d, LANES), lambda i: (0, 0))],
        out_specs=out_specs,
        compiler_params=_params(("parallel",), 32),
        name="norm_router",
    )(x, g.reshape(1, d), shift, scale, w_hi, w_lo)


CUMSUM_BLOCK = 256


def _exclusive_cumsum_lanes(flags):
    e, n = flags.shape
    cb = min(CUMSUM_BLOCK, n)
    tri = (lax.broadcasted_iota(jnp.int32, (cb, cb), 0) < lax.broadcasted_iota(jnp.int32, (cb, cb), 1)).astype(BF16)
    carry = jnp.zeros((e, 1), F32)
    out = []
    for j in range(n // cb):
        blk = flags[:, j * cb:(j + 1) * cb]
        out.append(jnp.dot(blk.astype(BF16), tri, preferred_element_type=F32) + carry)
        carry = carry + jnp.sum(blk, axis=-1, keepdims=True)
    return jnp.concatenate(out, axis=1) if len(out) > 1 else out[0]


def _select_kernel(logit_ref, pos_ref, gate_ref, *idx_refs, cap):
    lg = logit_ref[...]
    sb, ne, n = lg.shape
    ex = jnp.exp(lg - jnp.max(lg, axis=1, keepdims=True))
    aff = (ex / jnp.sum(ex, axis=1, keepdims=True)).reshape(sb * ne, n)
    e = sb * ne

    def enough(t):
        return jnp.sum((aff >= t).astype(F32), axis=-1, keepdims=True) >= cap

    hi = jnp.full((e, 1), 2.0, F32)
    for step in (64, 32, 16, 8, 4, 2, 1):
        cand = hi * (2.0 ** -step)
        hi = jnp.where(enough(cand), hi, cand)
    lo = hi * 0.5
    lo = jnp.where(enough(lo), lo, 0.0)
    for _ in range(23):
        mid = (lo + hi) * 0.5
        ok = enough(mid)
        lo = jnp.where(ok, mid, lo)
        hi = jnp.where(ok, hi, mid)
    thr = lo
    above = (aff > thr).astype(F32)
    tied = (aff == thr).astype(F32)
    need = cap - jnp.sum(above, axis=-1, keepdims=True)
    take = above + tied * (_exclusive_cumsum_lanes(tied) < need).astype(F32)
    slot = _exclusive_cumsum_lanes(take)
    pos = jnp.where(take > 0, slot, -1.0).astype(jnp.int32)
    pos_ref[...] = pos.reshape(sb, ne, n)
    s_iota = lax.broadcasted_iota(jnp.int32, (cap, n), 0)
    token = lax.broadcasted_iota(jnp.int32, (1, n), 1)
    t_hi = (token // 64).astype(F32)
    t_lo = (token % 64).astype(F32)
    row = lax.broadcasted_iota(jnp.int32, (8, n), 0)
    for ei in range(e):
        a = aff[ei:ei + 1, :]
        a0 = a.astype(BF16).astype(F32)
        a1 = (a - a0).astype(BF16).astype(F32)
        a2 = a - a0 - a1
        vals = jnp.where(row == 0, a0, jnp.where(row == 1, a1, jnp.where(row == 2, a2,
                         jnp.where(row == 3, t_hi, jnp.where(row == 4, t_lo, 0.0)))))
        hit = jnp.where(pos[ei:ei + 1, :] == s_iota, 1.0, 0.0).astype(BF16)
        res = lax.dot_general(vals.astype(BF16), hit, (((1,), (1,)), ((), ())), preferred_element_type=F32)
        si, xi = divmod(ei, ne)
        gate_ref[si, xi:xi + 1, :] = res[0:1] + res[1:2] + res[2:3]
        if idx_refs:
            idx_refs[0][si, xi:xi + 1, :] = (res[3:4] * 64.0 + res[4:5]).astype(jnp.int32)


def select_tokens(logits_t, cap, want_idx):
    nseg, e, n = logits_t.shape
    sb = min(nseg, 8)
    slots = (nseg, e, cap)
    out_shape = [jax.ShapeDtypeStruct((nseg, e, n), jnp.int32), jax.ShapeDtypeStruct(slots, F32)]
    out_specs = [pl.BlockSpec((sb, e, n), lambda s: (s, 0, 0)), pl.BlockSpec((sb, e, cap), lambda s: (s, 0, 0))]
    if want_idx:
        out_shape.append(jax.ShapeDtypeStruct(slots, jnp.int32))
        out_specs.append(pl.BlockSpec((sb, e, cap), lambda s: (s, 0, 0)))
    return pl.pallas_call(
        functools.partial(_select_kernel, cap=cap),
        out_shape=out_shape,
        grid=(nseg // sb,),
        in_specs=[pl.BlockSpec((sb, e, n), lambda s: (s, 0, 0))],
        out_specs=out_specs,
        compiler_params=_params(("parallel",), 48),
        name="select_tokens",
    )(logits_t)


def _gather_rows_kernel(idx_ref, x_hbm, g_ref, shift_ref, scale_ref, o_ref, buf, sem, *, cap, n, experts):
    step = pl.program_id(0)
    slot = step % 2

    def row_copy(st, sl, k):
        row = (st // experts) * n + idx_ref[st * cap + k]
        return pltpu.make_async_copy(x_hbm.at[pl.ds(row, 1), :], buf.at[sl, pl.ds(k, 1), :], sem.at[sl])

    def issue(st, sl):
        def body(k, carry):
            row_copy(st, sl, k).start()
            return carry
        lax.fori_loop(0, cap, body, 0, unroll=8)

    @pl.when(step == 0)
    def _():
        issue(step, slot)

    @pl.when(step + 1 < pl.num_programs(0))
    def _():
        issue(step + 1, 1 - slot)

    pltpu.make_async_copy(x_hbm.at[pl.ds(0, cap), :], buf.at[slot], sem.at[slot]).wait()
    h = _mod_norm(buf[slot], g_ref[...], shift_ref[0], scale_ref[0])
    o_ref[0, 0] = h.astype(o_ref.dtype)


def gather_rows(idx, x, g, shift, scale, n):
    nseg, e, cap = idx.shape
    d = x.shape[1]
    return pl.pallas_call(
        functools.partial(_gather_rows_kernel, cap=cap, n=n, experts=e),
        out_shape=jax.ShapeDtypeStruct((nseg, e, cap, d), BF16),
        grid_spec=pltpu.PrefetchScalarGridSpec(
            num_scalar_prefetch=1,
            grid=(nseg * e,),
            in_specs=[pl.BlockSpec(memory_space=pl.ANY),
                      pl.BlockSpec((1, d), lambda st, idx_ref: (0, 0)),
                      pl.BlockSpec((1, 1, d), lambda st, idx_ref: (st // e, 0, 0)),
                      pl.BlockSpec((1, 1, d), lambda st, idx_ref: (st // e, 0, 0))],
            out_specs=pl.BlockSpec((1, 1, cap, d), lambda st, idx_ref: (st // e, st % e, 0, 0)),
            scratch_shapes=[pltpu.VMEM((2, cap, d), F32), pltpu.SemaphoreType.DMA((2,))]),
        compiler_params=_params(("arbitrary",), 32, disable_bounds_checks=True),
        name="gather_rows",
    )(idx.reshape(-1), x, g.reshape(1, d), shift, scale)


def _gather_kernel(pos_ref, h_ref, o_ref, acc_ref, *, cap):
    kt = pl.program_id(2)
    eb = pos_ref.shape[1]
    tk = pos_ref.shape[3]
    s_iota = lax.broadcasted_iota(jnp.int32, (cap, tk), 0)
    onehot = jnp.concatenate([(pos_ref[0, i] == s_iota).astype(BF16) for i in range(eb)], axis=0)
    part = jnp.dot(onehot, h_ref[0], preferred_element_type=F32)

    @pl.when(kt == 0)
    def _():
        acc_ref[...] = part

    @pl.when(kt > 0)
    def _():
        acc_ref[...] += part

    @pl.when(kt == pl.num_programs(2) - 1)
    def _():
        o_ref[0] = acc_ref[...].reshape(o_ref.shape[1:]).astype(o_ref.dtype)


def gather_tokens(pos, h, cap, experts_per_step, tk):
    nseg, e, n = pos.shape
    d = h.shape[2]
    eb = experts_per_step
    return pl.pallas_call(
        functools.partial(_gather_kernel, cap=cap),
        out_shape=jax.ShapeDtypeStruct((nseg, e, cap, d), BF16),
        grid=(nseg, e // eb, n // tk),
        in_specs=[pl.BlockSpec((1, eb, 1, tk), lambda s, ei, kt: (s, ei, 0, kt)),
                  pl.BlockSpec((1, tk, d), lambda s, ei, kt: (s, kt, 0))],
        out_specs=pl.BlockSpec((1, eb, cap, d), lambda s, ei, kt: (s, ei, 0, 0)),
        scratch_shapes=[pltpu.VMEM((eb * cap, d), F32)],
        compiler_params=_params(("parallel", "parallel", "arbitrary"), 48),
        name="gather_tokens",
    )(pos.reshape(nseg, e, 1, n), h)


def _expert_up_kernel(xc_ref, xl_ref, wg_ref, wu_ref, o_ref):
    d = xc_ref.shape[-1]
    wg = wg_ref[0, 0].astype(BF16)
    wu = wu_ref[0, 0].astype(BF16)
    row = 0
    for x_ref in (xc_ref, xl_ref):
        rows = x_ref.shape[0] * x_ref.shape[2]
        x = x_ref[...].reshape(rows, d)
        a = jnp.dot(x, wg, preferred_element_type=F32)
        b = jnp.dot(x, wu, preferred_element_type=F32)
        o_ref[0, row:row + rows, :] = (_silu(a) * b).astype(o_ref.dtype)
        row += rows


def expert_up(xg_ctx, xg_lat, w_gate, w_up, layer):
    _, e, d, f = w_gate.shape
    tf = 512
    sc, _, cc, _ = xg_ctx.shape
    sl, _, cl, _ = xg_lat.shape
    rows = sc * cc + sl * cl
    return pl.pallas_call(
        _expert_up_kernel,
        out_shape=jax.ShapeDtypeStruct((e, rows, f), BF16),
        grid=(e, f // tf),
        in_specs=[pl.BlockSpec((sc, 1, cc, d), lambda ei, j: (0, ei, 0, 0)),
                  pl.BlockSpec((sl, 1, cl, d), lambda ei, j: (0, ei, 0, 0)),
                  pl.BlockSpec((1, 1, d, tf), lambda ei, j: (layer, ei, 0, j)),
                  pl.BlockSpec((1, 1, d, tf), lambda ei, j: (layer, ei, 0, j))],
        out_specs=pl.BlockSpec((1, rows, tf), lambda ei, j: (ei, 0, j)),
        compiler_params=_params(("parallel", "arbitrary"), 56),
        name="expert_up",
    )(xg_ctx, xg_lat, w_gate, w_up)


def _expert_down_kernel(h_ref, w_ref, gate_ref, o_ref):
    y = jnp.dot(h_ref[0], w_ref[0, 0].astype(BF16), preferred_element_type=F32)
    o_ref[0] = (y * gate_ref[0]).astype(o_ref.dtype)


def expert_down(hid, w_down, gates, layer):
    e, rows, f = hid.shape
    d = w_down.shape[3]
    td = 512
    return pl.pallas_call(
        _expert_down_kernel,
        out_shape=jax.ShapeDtypeStruct((e, rows, d), BF16),
        grid=(e, d // td),
        in_specs=[pl.BlockSpec((1, rows, f), lambda ei, j: (ei, 0, 0)),
                  pl.BlockSpec((1, 1, f, td), lambda ei, j: (layer, ei, 0, j)),
                  pl.BlockSpec((1, rows, 1), lambda ei, j: (ei, 0, 0))],
        out_specs=pl.BlockSpec((1, rows, td), lambda ei, j: (ei, 0, j)),
        compiler_params=_params(("parallel", "arbitrary"), 48),
        name="expert_down",
    )(hid, w_down, gates)


def _combine_kernel(post_ref, y_ref, x_ref, gate_ref, o_ref, hit_ref, *, cap):
    ne = y_ref.shape[0]
    tt = post_ref.shape[1]

    @pl.when(pl.program_id(2) == 0)
    def _():
        post = post_ref[0]
        width = max(cap, LANES)
        per = width // cap
        lane = lax.broadcasted_iota(jnp.int32, (tt, width), 1)
        for blk in range(ne // per):
            hit = None
            for i in range(per):
                col = post[:, blk * per + i:blk * per + i + 1]
                h = jnp.where(col >= 0, col + i * cap, -1) == lane
                hit = h if hit is None else (hit | h)
            hit_ref[:, blk * width:(blk + 1) * width] = jnp.where(hit, 1.0, 0.0).astype(BF16)

    y = y_ref[...].reshape(ne * cap, y_ref.shape[-1])
    o_ref[...] = x_ref[...] + gate_ref[0] * jnp.dot(hit_ref[...], y, preferred_element_type=F32)


def combine_tokens(pos_t, y, x, gate, cap, row_offset, tt, td):
    nseg, n, e = pos_t.shape
    r, d = x.shape
    nt = n // tt
    rb = row_offset // cap
    return pl.pallas_call(
        functools.partial(_combine_kernel, cap=cap),
        out_shape=jax.ShapeDtypeStruct((r, d), F32),
        grid=(nseg, nt, d // td),
        in_specs=[pl.BlockSpec((1, tt, e), lambda s, t, j: (s, t, 0)),
                  pl.BlockSpec((e, cap, td), lambda s, t, j: (0, rb + s, j)),
                  pl.BlockSpec((tt, td), lambda s, t, j: (s * nt + t, j)),
                  pl.BlockSpec((1, 1, td), lambda s, t, j: (s * gate.shape[0] // nseg, 0, j))],
        out_specs=pl.BlockSpec((tt, td), lambda s, t, j: (s * nt + t, j)),
        scratch_shapes=[pltpu.VMEM((tt, e * cap), BF16)],
        compiler_params=_params(("parallel", "parallel", "arbitrary"), 56),
        name="combine_tokens",
    )(pos_t, y, x, gate)


COMBINE_WINDOW = 256
COMBINE_ALIGN = 64


def _combine_window_kernel(ws_ref, post_ref, *refs, experts):
    y_refs = refs[:experts]
    x_ref, gate_ref, o_ref, hit_ref = refs[experts:]
    s, t = pl.program_id(0), pl.program_id(1)
    tt = post_ref.shape[1]
    win = COMBINE_WINDOW

    @pl.when(pl.program_id(2) == 0)
    def _():
        post = post_ref[0]
        lane = lax.broadcasted_iota(jnp.int32, (tt, win), 1)
        for e in range(experts):
            start = ws_ref[(s * pl.num_programs(1) + t) * experts + e]
            col = post[:, e:e + 1]
            local = jnp.where(col >= 0, col - start, -1)
            hit_ref[:, e * win:(e + 1) * win] = jnp.where(local == lane, 1.0, 0.0).astype(BF16)

    acc = jnp.dot(hit_ref[:, 0:win], y_refs[0][0], preferred_element_type=F32)
    for e in range(1, experts):
        acc = acc + jnp.dot(hit_ref[:, e * win:(e + 1) * win], y_refs[e][0], preferred_element_type=F32)
    o_ref[...] = x_ref[...] + gate_ref[0] * acc


def combine_tokens_windowed(win_start, pos_t, y, x, gate, cap, row_offset, tt, td):
    nseg, n, e = pos_t.shape
    r, d = x.shape
    nt = n // tt

    def y_spec(ei):
        return pl.BlockSpec((pl.Element(1), pl.Element(COMBINE_WINDOW), pl.Element(td)),
                            lambda s, t, j, ws: (
                                ei,
                                pl.multiple_of(row_offset + s * cap + ws[(s * nt + t) * e + ei], COMBINE_ALIGN),
                                j * td))

    return pl.pallas_call(
        functools.partial(_combine_window_kernel, experts=e),
        out_shape=jax.ShapeDtypeStruct((r, d), F32),
        grid_spec=pltpu.PrefetchScalarGridSpec(
            num_scalar_prefetch=1,
            grid=(nseg, nt, d // td),
            in_specs=[pl.BlockSpec((1, tt, e), lambda s, t, j, ws: (s, t, 0))]
                     + [y_spec(ei) for ei in range(e)]
                     + [pl.BlockSpec((tt, td), lambda s, t, j, ws: (s * nt + t, j)),
                        pl.BlockSpec((1, 1, td), lambda s, t, j, ws: (s * gate.shape[0] // nseg, 0, j))],
            out_specs=pl.BlockSpec((tt, td), lambda s, t, j, ws: (s * nt + t, j)),
            scratch_shapes=[pltpu.VMEM((tt, e * COMBINE_WINDOW), BF16)]),
        compiler_params=_params(("parallel", "parallel", "arbitrary"), 56),
        name="combine_tokens_windowed",
    )(win_start.reshape(-1), pos_t, *([y] * e), x, gate)


def _combine_windows(pos, tt):
    nseg, e, n = pos.shape
    taken = (pos >= 0).reshape(nseg, e, n // tt, tt).sum(-1)
    first = jnp.cumsum(taken, axis=-1) - taken
    cap = taken.sum(-1, keepdims=True)
    start = jnp.minimum(first // COMBINE_ALIGN * COMBINE_ALIGN, cap - COMBINE_WINDOW)
    fits = jnp.all(first + taken <= start + COMBINE_WINDOW)
    return jnp.swapaxes(start, 1, 2).astype(jnp.int32), fits


def moe_layer(groups, g_ffn, router_w, w_gate, w_up, w_down, layer):
    sel = []
    for grp in groups:
        nseg, n = grp["nseg"], grp["n"]
        cap = (EC_CAPACITY * n) // N_EXPERTS
        dense = n <= 512
        routed = norm_router(grp["x"], g_ffn, grp["shift"], grp["scale"], router_w, want_h=dense)
        logits_t = jnp.swapaxes(routed[-1][:, :N_EXPERTS].reshape(nseg, n, N_EXPERTS), 1, 2)
        picked = select_tokens(logits_t, cap, want_idx=not dense)
        pos, gates = picked[0], picked[1]
        if dense:
            xg = gather_tokens(pos, routed[0].reshape(nseg, n, D_MODEL), cap, N_EXPERTS, n)
        else:
            xg = gather_rows(picked[2].reshape(nseg, N_EXPERTS, cap), grp["x"], g_ffn, grp["shift"], grp["scale"], n)
        sel.append(dict(cap=cap, pos=pos, xg=xg,
                        gates=jnp.swapaxes(gates, 0, 1).reshape(N_EXPERTS, nseg * cap, 1)))
    hid = expert_up(sel[0]["xg"], sel[1]["xg"], w_gate, w_up, layer)
    y = expert_down(hid, w_down, jnp.concatenate([s["gates"] for s in sel], axis=1), layer)
    out = []
    row_offset = 0
    for grp, s in zip(groups, sel):
        n, cap = grp["n"], s["cap"]
        pos_t = jnp.swapaxes(s["pos"], 1, 2)
        if n <= 512:
            out.append(combine_tokens(pos_t, y, grp["x"], grp["gate"], cap, row_offset, n, D_MODEL))
        else:
            tt, td = 1024, 512
            start, fits = _combine_windows(s["pos"], tt)
            args = (pos_t, y, grp["x"], grp["gate"], cap, row_offset, tt, td)
            out.append(lax.cond(fits, lambda a=args, w=start: combine_tokens_windowed(w, *a),
                                lambda a=args: combine_tokens(*a)))
        row_offset += grp["nseg"] * cap
    return out


def _rope_tables(n):
    t = jnp.arange(n)
    row = (t // GRID_W).astype(F32)
    col = (t % GRID_W).astype(F32)
    nf = HEAD_DIM // 4
    inv = ROPE_THETA ** (-jnp.arange(nf, dtype=F32) / nf)
    ang = jnp.concatenate([row[:, None] * inv, col[:, None] * inv], axis=-1)
    cos, sin = jnp.cos(ang), jnp.sin(ang)
    return jnp.concatenate([cos, cos], axis=-1), jnp.concatenate([-sin, sin], axis=-1)


def kernel(x_prompt, x_sample, cache_win_k, cache_win_v, cache_nat_k, cache_nat_v, cache_glob_k, cache_glob_v, c, c_ctx, mod_w, mod_b, norm_mix_w, norm_ffn_w, even_w_in, even_w_out, conv_w, conv_b, win_sink, win_q_norm, win_k_norm, odd_w_in, odd_w_out, nat_rpb, nat_q_norm, nat_k_norm, glob_q_norm, glob_k_norm, router_w, expert_w_gate, expert_w_up, expert_w_down):
    bp, sp, d = x_prompt.shape
    bs, ss, _ = x_sample.shape
    depth = mod_w.shape[0]
    rope = _rope_tables(ss)

    cond8 = jnp.zeros((8, d), F32).at[0].set(c_ctx).at[1:1 + bs].set(c)
    mods = modulation(cond8, mod_w, mod_b).reshape(depth, 8, N_MOD, d)

    xp = x_prompt.reshape(bp * sp, d)
    xs = x_sample.reshape(bs * ss, d)
    win_k, win_v, nat_k, nat_v, glob_k, glob_v = [], [], [], [], [], []

    def flat_cache(cache, i):
        return cache[:, i].reshape(bs, cache.shape[2], -1).astype(BF16)

    for layer in range(depth):
        i = layer // 2
        mp = [mods[layer, 0:1, k].reshape(1, 1, d) for k in range(N_MOD)]
        ms = [mods[layer, 1:1 + bs, k].reshape(bs, 1, d) for k in range(N_MOD)]
        if layer % 2 == 0:
            w_in = even_w_in[i].astype(BF16)
            w_out = even_w_out[i].astype(BF16)
            q0 = 3 * CONV_DIM
            qw = WIN_Q_HEADS * HEAD_DIM
            kw = WIN_KV_HEADS * HEAD_DIM
            group = WIN_Q_HEADS // WIN_KV_HEADS
            h = mod_norm(xp, norm_mix_w[layer], mp[0], mp[1])
            (_, conv_in), (q, _), (k, k32), (v, v32) = project(h, w_in, [
                (q0, None, 1.0, False, False, True), (qw, win_q_norm[i], ATTN_SCALE, False, True, False),
                (kw, win_k_norm[i], 1.0, False, True, True), (kw, None, 1.0, False, True, True)])
            ya = conv_gate(conv_in, conv_w[i], conv_b[i], sp)
            yb = dense_attention(q.reshape(bp, sp, qw), k.reshape(bp, sp, kw), v.reshape(bp, sp, kw),
                                 group, sp, sink=win_sink[i])
            xp = out_proj_residual(ya, yb.reshape(bp * sp, qw), w_out, xp, mp[2])
            win_k.append(k32.reshape(bp, sp, WIN_KV_HEADS, HEAD_DIM))
            win_v.append(v32.reshape(bp, sp, WIN_KV_HEADS, HEAD_DIM))
            h = mod_norm(xs, norm_mix_w[layer], ms[0], ms[1])
            (_, conv_in), (q, _), (k, _), (v, _) = project(h, w_in, [
                (q0, None, 1.0, False, False, True), (qw, win_q_norm[i], ATTN_SCALE, True, True, False),
                (kw, win_k_norm[i], 1.0, True, True, False), (kw, None, 1.0, False, True, False)], rope)
            ya = conv_gate(conv_in, conv_w[i], conv_b[i], ss)
            yb = window_attention(q.reshape(bs, ss, qw), k.reshape(bs, ss, kw), v.reshape(bs, ss, kw),
                                  flat_cache(cache_win_k, i), flat_cache(cache_win_v, i), win_sink[i], group)
            xs = out_proj_residual(ya, yb.reshape(bs * ss, qw), w_out, xs, ms[2])
        else:
            w_in = odd_w_in[i].astype(BF16)
            w_out = odd_w_out[i].astype(BF16)
            nw = NAT_HEADS * HEAD_DIM
            gqw = GLOB_Q_HEADS * HEAD_DIM
            gkw = GLOB_KV_HEADS * HEAD_DIM
            group = GLOB_Q_HEADS // GLOB_KV_HEADS
            kd0 = 3 * nw + gqw
            h = mod_norm(xp, norm_mix_w[layer], mp[0], mp[1])
            (qc, _), (kc, kc32), (vc, vc32), (qd, _), (kd, kd32), (vd, vd32) = project(h, w_in, [
                (nw, nat_q_norm[i], ATTN_SCALE, False, True, False), (nw, nat_k_norm[i], 1.0, False, True, True),
                (nw, None, 1.0, False, True, True), (gqw, glob_q_norm[i], ATTN_SCALE, False, True, False),
                (gkw, glob_k_norm[i], 1.0, False, True, True), (gkw, None, 1.0, False, True, True)])
            yc = dense_attention(qc.reshape(bp, sp, nw), kc.reshape(bp, sp, nw), vc.reshape(bp, sp, nw), 1, sp)
            yd = dense_attention(qd.reshape(bp, sp, gqw), kd.reshape(bp, sp, gkw), vd.reshape(bp, sp, gkw), group, sp)
            xp = out_proj_residual(yc.reshape(bp * sp, nw), yd.reshape(bp * sp, gqw), w_out, xp, mp[2])
            nat_k.append(kc32.reshape(bp, sp, NAT_HEADS, HEAD_DIM))
            nat_v.append(vc32.reshape(bp, sp, NAT_HEADS, HEAD_DIM))
            glob_k.append(kd32.reshape(bp, sp, GLOB_KV_HEADS, HEAD_DIM))
            glob_v.append(vd32.reshape(bp, sp, GLOB_KV_HEADS, HEAD_DIM))
            h = mod_norm(xs, norm_mix_w[layer], ms[0], ms[1])
            (qc, _), (kc, _), (vc, _), (qd, _), (kd, _), (vd, _) = project(h, w_in, [
                (nw, nat_q_norm[i], ATTN_SCALE, False, True, False), (nw, nat_k_norm[i], 1.0, False, True, False),
                (nw, None, 1.0, False, True, False), (gqw, glob_q_norm[i], ATTN_SCALE, True, True, False),
                (gkw, glob_k_norm[i], 1.0, True, True, False), (gkw, None, 1.0, False, True, False)], rope)
            yc = neighbourhood_attention(qc.reshape(bs, ss, nw), kc.reshape(bs, ss, nw), vc.reshape(bs, ss, nw),
                                         flat_cache(cache_nat_k, i), flat_cache(cache_nat_v, i),
                                         _nat_bias_table(nat_rpb[i]))
            k_all = jnp.concatenate([kd.reshape(bs, ss, gkw), flat_cache(cache_glob_k, i)], axis=1)
            v_all = jnp.concatenate([vd.reshape(bs, ss, gkw), flat_cache(cache_glob_v, i)], axis=1)
            yd = dense_attention(qd.reshape(bs, ss, gqw), k_all, v_all, group, 256)
            xs = out_proj_residual(yc.reshape(bs * ss, nw), yd.reshape(bs * ss, gqw), w_out, xs, ms[2])

        xp, xs = moe_layer(
            [dict(x=xp, shift=mp[3], scale=mp[4], gate=mp[5], nseg=bp, n=sp),
             dict(x=xs, shift=ms[3], scale=ms[4], gate=ms[5], nseg=bs, n=ss)],
            norm_ffn_w[layer], router_w[layer], expert_w_gate, expert_w_up, expert_w_down, layer)

    stack = lambda xs_: jnp.stack(xs_, axis=1)
    return (xp.reshape(bp, sp, d), xs.reshape(bs, ss, d), stack(win_k), stack(win_v), stack(nat_k),
            stack(nat_v), stack(glob_k), stack(glob_v))
```

```python
import functools

import numpy as np
import jax
import jax.numpy as jnp
from jax import lax
from jax.experimental import pallas as pl
from jax.experimental.pallas import tpu as pltpu

D_MODEL = 2048
HEAD_DIM = 128
GRID_W = 64
CONV_DIM = 512
WIN_Q_HEADS = 12
WIN_KV_HEADS = 4
WIN_BLOCK = 128
WINDOW = 128
NAT_HEADS = 8
NAT_ROWS = 8
NAT_COLS = 16
GLOB_Q_HEADS = 8
GLOB_KV_HEADS = 2
ROPE_THETA = 10000.0
N_EXPERTS = 16
EXPERT_FF = 2048
EC_CAPACITY = 2
N_MOD = 6
NORM_EPS = 1e-6
NEG_INF = -1e30
ATTN_SCALE = HEAD_DIM ** -0.5

LANES = 128
MIB = 1024 * 1024
BF16 = jnp.bfloat16
F32 = jnp.float32


def _params(semantics, vmem_mib, **extra):
    return pltpu.CompilerParams(dimension_semantics=semantics, vmem_limit_bytes=vmem_mib * MIB, **extra)


def _silu(x):
    return x * (1.0 / (1.0 + jnp.exp(-x)))


def _modulation_kernel(cond_ref, w_ref, b_ref, o_ref):
    s = _silu(cond_ref[...]).astype(BF16)
    o_ref[0] = jnp.dot(s, w_ref[0].astype(BF16), preferred_element_type=F32) + b_ref[0]


def modulation(cond8, mod_w, mod_b):
    depth, d, n = mod_w.shape
    tn = 1024
    return pl.pallas_call(
        _modulation_kernel,
        out_shape=jax.ShapeDtypeStruct((depth, 8, n), F32),
        grid=(depth, n // tn),
        in_specs=[pl.BlockSpec((8, d), lambda l, j: (0, 0)),
                  pl.BlockSpec((1, d, tn), lambda l, j: (l, 0, j)),
                  pl.BlockSpec((1, 1, tn), lambda l, j: (l, 0, j))],
        out_specs=pl.BlockSpec((1, 8, tn), lambda l, j: (l, 0, j)),
        compiler_params=_params(("parallel", "parallel"), 48),
        name="modulation",
    )(cond8, mod_w, mod_b.reshape(depth, 1, n))


def _mod_norm(x, g, shift, scale):
    ms = jnp.mean(x * x, axis=-1, keepdims=True)
    y = x * lax.rsqrt(ms + NORM_EPS) * g
    return y * (1.0 + scale) + shift


def _mod_norm_kernel(x_ref, g_ref, shift_ref, scale_ref, h_ref):
    h_ref[...] = _mod_norm(x_ref[...], g_ref[...], shift_ref[0], scale_ref[0]).astype(BF16)


def mod_norm(x, g, shift, scale):
    r, d = x.shape
    nseg = shift.shape[0]
    tm = 512
    tiles_per_seg = (r // nseg) // tm
    return pl.pallas_call(
        _mod_norm_kernel,
        out_shape=jax.ShapeDtypeStruct((r, d), BF16),
        grid=(r // tm,),
        in_specs=[pl.BlockSpec((tm, d), lambda i: (i, 0)),
                  pl.BlockSpec((1, d), lambda i: (0, 0)),
                  pl.BlockSpec((1, 1, d), lambda i: (i // tiles_per_seg, 0, 0)),
                  pl.BlockSpec((1, 1, d), lambda i: (i // tiles_per_seg, 0, 0))],
        out_specs=pl.BlockSpec((tm, d), lambda i: (i, 0)),
        compiler_params=_params(("parallel",), 32),
        name="mod_norm",
    )(x, g.reshape(1, d), shift, scale)


PROJ_TILE = 512


def _proj_kernel(*refs, tiles, n_gain, has_rope):
    h_ref, w_ref = refs[0], refs[1]
    k = 2
    if n_gain:
        g_ref = refs[k]
        k += 1
    if has_rope:
        cos_ref, sin_ref = refs[k], refs[k + 1]
        k += 2
    outs = refs[k:]
    j = pl.program_id(1)
    acc = jnp.dot(h_ref[...], w_ref[...], preferred_element_type=F32)
    ones = jnp.ones((HEAD_DIM, HEAD_DIM), BF16)

    def finish(parts):
        for lo, width, dst, gain_idx, scale, rope, o_bf16, o_f32 in parts:
            if gain_idx is None:
                x = acc[:, lo:lo + width]
                if o_bf16 is not None:
                    outs[o_bf16][:, dst:dst + width] = x.astype(BF16)
                if o_f32 is not None:
                    outs[o_f32][:, dst:dst + width] = x
                continue
            for hd in range(width // HEAD_DIM):
                x = acc[:, lo + hd * HEAD_DIM:lo + (hd + 1) * HEAD_DIM]
                sl = slice(dst + hd * HEAD_DIM, dst + (hd + 1) * HEAD_DIM)
                ssq = jnp.dot((x * x).astype(BF16), ones, preferred_element_type=F32)
                y = x * lax.rsqrt(ssq * (1.0 / HEAD_DIM) + NORM_EPS) * g_ref[gain_idx:gain_idx + 1, :]
                if o_f32 is not None:
                    outs[o_f32][:, sl] = y
                if rope:
                    y = y * cos_ref[...] + pltpu.roll(y, HEAD_DIM // 2, 1) * sin_ref[...]
                outs[o_bf16][:, sl] = (y * scale).astype(BF16)

    for jv, parts in enumerate(tiles):
        pl.when(j == jv)(functools.partial(finish, parts))


def project(h, w_bf16, pieces, rope_tables=None):
    r, d = h.shape
    n = w_bf16.shape[1]
    tm, tn = 1024, PROJ_TILE
    n_tiles = n // tn
    gains = [p[1] for p in pieces if p[1] is not None]
    has_rope = any(p[3] for p in pieces)
    tiles = [[] for _ in range(n_tiles)]
    out_shape, out_specs, result = [], [], []
    col0 = 0
    for width, gain, scale, rope, want_bf16, want_f32 in pieces:
        block_w = min(width, tn)
        j0, n_blk = col0 // tn, max(width // tn, 1)
        slots = []
        for want, dt in ((want_bf16, BF16), (want_f32, F32)):
            if want:
                slots.append(len(out_shape))
                out_shape.append(jax.ShapeDtypeStruct((r, width), dt))
                out_specs.append(pl.BlockSpec(
                    (tm, block_w), lambda i, j, j0=j0, n_blk=n_blk: (i, jnp.clip(j - j0, 0, n_blk - 1))))
            else:
                slots.append(None)
        result.append(slots)
        gain_idx = None if gain is None else [g is gain for g in gains].index(True)
        for jv in range(j0, (col0 + width - 1) // tn + 1):
            start = max(col0, jv * tn)
            stop = min(col0 + width, (jv + 1) * tn)
            tiles[jv].append((start - jv * tn, stop - start, (start - col0) % block_w, gain_idx, scale, rope,
                              slots[0], slots[1]))
        col0 += width
    assert col0 == n
    in_specs = [pl.BlockSpec((tm, d), lambda i, j: (i, 0)),
                pl.BlockSpec((d, tn), lambda i, j: (0, j))]
    args = [h, w_bf16]
    if gains:
        in_specs.append(pl.BlockSpec((len(gains), HEAD_DIM), lambda i, j: (0, 0)))
        args.append(jnp.stack(gains))
    if has_rope:
        per = rope_tables[0].shape[0] // tm
        in_specs += [pl.BlockSpec((tm, HEAD_DIM), lambda i, j: (i % per, 0))] * 2
        args += list(rope_tables)
    out = pl.pallas_call(
        functools.partial(_proj_kernel, tiles=tiles, n_gain=len(gains), has_rope=has_rope),
        out_shape=out_shape,
        grid=(r // tm, n_tiles),
        in_specs=in_specs,
        out_specs=out_specs,
        compiler_params=_params(("parallel", "arbitrary"), 56),
        name="project",
    )(*args)
    return [tuple(None if s is None else out[s] for s in slots) for slots in result]


def _conv_gate_kernel(ab_ref, ac_ref, ah_ref, w_ref, b_ref, o_ref, *, seq):
    u = ac_ref[...] * ah_ref[...]
    rows = u.shape[0]
    pos = lax.broadcasted_iota(jnp.int32, u.shape, 0) % seq
    prev = jnp.where(pos == 0, 0.0, pltpu.roll(u, 1, 0))
    nxt = jnp.where(pos == seq - 1, 0.0, pltpu.roll(u, rows - 1, 0))
    conv = prev * w_ref[0:1, :] + u * w_ref[1:2, :] + nxt * w_ref[2:3, :] + b_ref[...]
    o_ref[...] = (ab_ref[...] * conv).astype(o_ref.dtype)


def conv_gate(proj, conv_w, conv_b, seq):
    r = proj.shape[0]
    tr = 4096
    nc = CONV_DIM // LANES
    return pl.pallas_call(
        functools.partial(_conv_gate_kernel, seq=seq),
        out_shape=jax.ShapeDtypeStruct((r, CONV_DIM), BF16),
        grid=(r // tr, nc),
        in_specs=[pl.BlockSpec((tr, LANES), lambda i, c: (i, c)),
                  pl.BlockSpec((tr, LANES), lambda i, c: (i, nc + c)),
                  pl.BlockSpec((tr, LANES), lambda i, c: (i, 2 * nc + c)),
                  pl.BlockSpec((3, LANES), lambda i, c: (0, c)),
                  pl.BlockSpec((1, LANES), lambda i, c: (0, c))],
        out_specs=pl.BlockSpec((tr, LANES), lambda i, c: (i, c)),
        compiler_params=_params(("parallel", "parallel"), 48),
        name="conv_gate",
    )(proj, proj, proj, conv_w, conv_b.reshape(1, CONV_DIM))


def _with_ones(v):
    return jnp.concatenate([v, jnp.ones_like(v)], axis=1)


def _dense_attn_kernel(*refs, group, has_sink):
    if has_sink:
        sink_ref, q_ref, k_ref, v_ref, o_ref = refs
    else:
        q_ref, k_ref, v_ref, o_ref = refs
    qb = q_ref.shape[1]
    for hkv in range(k_ref.shape[2] // HEAD_DIM):
        head = lambda g: slice((hkv * group + g) * HEAD_DIM, (hkv * group + g + 1) * HEAD_DIM)
        kv = slice(hkv * HEAD_DIM, (hkv + 1) * HEAD_DIM)
        q = jnp.concatenate([q_ref[0, :, head(g)] for g in range(group)], axis=0)
        s = lax.dot_general(q, k_ref[0, :, kv], (((1,), (1,)), ((), ())), preferred_element_type=F32)
        m = jnp.max(s, axis=-1, keepdims=True)
        if has_sink:
            sk = jnp.concatenate([jnp.full((qb, 1), sink_ref[hkv * group + g], F32) for g in range(group)], axis=0)
            m = jnp.maximum(m, sk)
        p = jnp.exp(s - m)
        oa = jnp.dot(p.astype(BF16), _with_ones(v_ref[0, :, kv]), preferred_element_type=F32)
        den = oa[:, HEAD_DIM:]
        if has_sink:
            den = den + jnp.exp(sk - m)
        o = oa[:, :HEAD_DIM] / den
        for g in range(group):
            o_ref[0, :, head(g)] = o[g * qb:(g + 1) * qb].astype(o_ref.dtype)


def dense_attention(q, k, v, group, q_block, sink=None):
    b, nq, qw = q.shape
    nk = k.shape[1]
    kvw = k.shape[2]
    in_specs = [pl.BlockSpec((1, q_block, qw), lambda bi, qi: (bi, qi, 0)),
                pl.BlockSpec((1, nk, kvw), lambda bi, qi: (bi, 0, 0)),
                pl.BlockSpec((1, nk, kvw), lambda bi, qi: (bi, 0, 0))]
    args = [q, k, v]
    if sink is not None:
        in_specs = [pl.BlockSpec(memory_space=pltpu.SMEM)] + in_specs
        args = [sink] + args
    return pl.pallas_call(
        functools.partial(_dense_attn_kernel, group=group, has_sink=sink is not None),
        out_shape=jax.ShapeDtypeStruct((b, nq, qw), BF16),
        grid=(b, nq // q_block),
        in_specs=in_specs,
        out_specs=pl.BlockSpec((1, q_block, qw), lambda bi, qi: (bi, qi, 0)),
        compiler_params=_params(("parallel", "parallel"), 56),
        name="dense_attention",
    )(*args)


def _window_attn_kernel(sink_ref, q_ref, kp_ref, kc_ref, kn_ref, vp_ref, vc_ref, vn_ref, ck_ref, cv_ref, o_ref,
                        *, group, n_tokens):
    blk = pl.program_id(1)
    wb = WIN_BLOCK
    rows, cols = group * wb, 3 * wb + ck_ref.shape[1]
    q_pos = blk * wb + lax.broadcasted_iota(jnp.int32, (rows, cols), 0) % wb
    col = lax.broadcasted_iota(jnp.int32, (rows, cols), 1)
    k_pos = (blk - 1) * wb + col
    local_ok = (jnp.abs(k_pos - q_pos) <= WINDOW) & (k_pos >= 0) & (k_pos < n_tokens)
    visible = (col >= 3 * wb) | local_ok
    for hkv in range(kc_ref.shape[2] // HEAD_DIM):
        head = lambda g: slice((hkv * group + g) * HEAD_DIM, (hkv * group + g + 1) * HEAD_DIM)
        kv = slice(hkv * HEAD_DIM, (hkv + 1) * HEAD_DIM)
        q = jnp.concatenate([q_ref[0, :, head(g)] for g in range(group)], axis=0)
        keys = jnp.concatenate([kp_ref[0, :, kv], kc_ref[0, :, kv], kn_ref[0, :, kv], ck_ref[0, :, kv]], axis=0)
        vals = jnp.concatenate([vp_ref[0, :, kv], vc_ref[0, :, kv], vn_ref[0, :, kv], cv_ref[0, :, kv]], axis=0)
        s = lax.dot_general(q, keys, (((1,), (1,)), ((), ())), preferred_element_type=F32)
        s = jnp.where(visible, s, NEG_INF)
        sk = jnp.concatenate([jnp.full((wb, 1), sink_ref[hkv * group + g], F32) for g in range(group)], axis=0)
        m = jnp.maximum(jnp.max(s, axis=-1, keepdims=True), sk)
        p = jnp.exp(s - m)
        oa = jnp.dot(p.astype(BF16), _with_ones(vals), preferred_element_type=F32)
        o = oa[:, :HEAD_DIM] / (oa[:, HEAD_DIM:] + jnp.exp(sk - m))
        for g in range(group):
            o_ref[0, :, head(g)] = o[g * wb:(g + 1) * wb].astype(o_ref.dtype)


def window_attention(q, k, v, ctx_k, ctx_v, sink, group):
    b, n, qw = q.shape
    kvw = k.shape[2]
    nb = n // WIN_BLOCK
    nctx = ctx_k.shape[1]
    prev_map = lambda bi, i: (bi, jnp.maximum(i - 1, 0), 0)
    cur_map = lambda bi, i: (bi, i, 0)
    next_map = lambda bi, i: (bi, jnp.minimum(i + 1, nb - 1), 0)
    ctx_map = lambda bi, i: (bi, 0, 0)
    kv_block = (1, WIN_BLOCK, kvw)
    return pl.pallas_call(
        functools.partial(_window_attn_kernel, group=group, n_tokens=n),
        out_shape=jax.ShapeDtypeStruct((b, n, qw), BF16),
        grid=(b, nb),
        in_specs=[pl.BlockSpec(memory_space=pltpu.SMEM),
                  pl.BlockSpec((1, WIN_BLOCK, qw), cur_map),
                  pl.BlockSpec(kv_block, prev_map), pl.BlockSpec(kv_block, cur_map), pl.BlockSpec(kv_block, next_map),
                  pl.BlockSpec(kv_block, prev_map), pl.BlockSpec(kv_block, cur_map), pl.BlockSpec(kv_block, next_map),
                  pl.BlockSpec((1, nctx, kvw), ctx_map), pl.BlockSpec((1, nctx, kvw), ctx_map)],
        out_specs=pl.BlockSpec((1, WIN_BLOCK, qw), cur_map),
        compiler_params=_params(("parallel", "parallel"), 32),
        name="window_attention",
    )(sink, q, k, k, k, v, v, v, ctx_k, ctx_v)


NAT_QROWS = 8
NAT_KROWS = 16


def _nat_bias_table(rpb):
    w = GRID_W
    c = np.arange(w)[:, None]
    kc = np.arange(w)[None, :]
    ws = np.clip(c - NAT_COLS // 2, 0, w - NAT_COLS)
    col_ok = (kc >= ws) & (kc < ws + NAT_COLS)
    col_idx = np.clip(kc - c + NAT_COLS - 1, 0, 2 * NAT_COLS - 2)
    pick = (col_idx[..., None] == np.arange(2 * NAT_COLS - 1)).astype(np.float32)
    looked_up = jnp.einsum("hdj,ckj->hdck", rpb, pick, precision=lax.Precision.HIGHEST)
    t = jnp.where(col_ok[None, None], looked_up, NEG_INF)
    dead = jnp.full_like(t[:, :1], NEG_INF)
    t = jnp.concatenate([dead, t, dead], axis=1)
    return jnp.concatenate([t[:, :-1], t[:, 1:]], axis=-1)


def _nat_attn_kernel(q_ref, k_ref, v_ref, ck_ref, cv_ref, t_ref, o_ref, *, grid_rows):
    m = pl.program_id(2)
    w = GRID_W
    r0 = m * NAT_QROWS
    kr0 = jnp.clip(r0 - NAT_ROWS // 2, 0, grid_rows - NAT_KROWS)
    tok0 = pl.multiple_of(kr0 * w, 4 * w)
    lane_hi = lax.broadcasted_iota(jnp.int32, (w, 2 * w), 1) >= w
    layout = []
    for i in range(NAT_QROWS):
        r = r0 + i
        rs = jnp.clip(r - NAT_ROWS // 2, 0, grid_rows - NAT_ROWS)
        row = []
        for jj in range(NAT_KROWS // 2):
            kr = kr0 + 2 * jj
            d = kr - r + NAT_ROWS - 1
            ok_lo = ((kr >= rs) & (kr < rs + NAT_ROWS)).astype(jnp.int32)
            ok_hi = ((kr + 1 >= rs) & (kr + 1 < rs + NAT_ROWS)).astype(jnp.int32)
            row.append((jnp.clip(d, -1, 2 * NAT_ROWS - 2) + 1, jnp.where(lane_hi, ok_hi, ok_lo) > 0))
        layout.append(row)
    for hh in range(q_ref.shape[2] // HEAD_DIM):
        hs = slice(hh * HEAD_DIM, (hh + 1) * HEAD_DIM)
        keys = k_ref[0, pl.ds(tok0, NAT_KROWS * w), hs]
        vals = v_ref[0, pl.ds(tok0, NAT_KROWS * w), hs]
        q = q_ref[0, :, hs]
        s_loc = lax.dot_general(q, keys, (((1,), (1,)), ((), ())), preferred_element_type=F32)
        bias = jnp.concatenate(
            [jnp.concatenate([jnp.where(ok, t_ref[hh, entry], NEG_INF) for entry, ok in row], axis=1)
             for row in layout], axis=0)
        s_loc = s_loc + bias
        s_ctx = lax.dot_general(q, ck_ref[0, :, hs], (((1,), (1,)), ((), ())), preferred_element_type=F32)
        mx = jnp.maximum(jnp.max(s_loc, axis=-1, keepdims=True), jnp.max(s_ctx, axis=-1, keepdims=True))
        p_loc = jnp.exp(s_loc - mx)
        p_ctx = jnp.exp(s_ctx - mx)
        oa = (jnp.dot(p_loc.astype(BF16), _with_ones(vals), preferred_element_type=F32)
              + jnp.dot(p_ctx.astype(BF16), _with_ones(cv_ref[0, :, hs]), preferred_element_type=F32))
        o_ref[0, :, hs] = (oa[:, :HEAD_DIM] / oa[:, HEAD_DIM:]).astype(o_ref.dtype)


NAT_HEADS_PER_STEP = 2


def neighbourhood_attention(q, k, v, ctx_k, ctx_v, bias_table):
    b, n, hw = q.shape
    hp = NAT_HEADS_PER_STEP
    hpw = hp * HEAD_DIM
    grid_rows = n // GRID_W
    nctx = ctx_k.shape[1]
    tq = NAT_QROWS * GRID_W
    return pl.pallas_call(
        functools.partial(_nat_attn_kernel, grid_rows=grid_rows),
        out_shape=jax.ShapeDtypeStruct((b, n, hw), BF16),
        grid=(b, hw // hpw, grid_rows // NAT_QROWS),
        in_specs=[pl.BlockSpec((1, tq, hpw), lambda bi, h, m: (bi, m, h)),
                  pl.BlockSpec((1, n, hpw), lambda bi, h, m: (bi, 0, h)),
                  pl.BlockSpec((1, n, hpw), lambda bi, h, m: (bi, 0, h)),
                  pl.BlockSpec((1, nctx, hpw), lambda bi, h, m: (bi, 0, h)),
                  pl.BlockSpec((1, nctx, hpw), lambda bi, h, m: (bi, 0, h)),
                  pl.BlockSpec((hp, 2 * NAT_ROWS, GRID_W, 2 * GRID_W), lambda bi, h, m: (h, 0, 0, 0))],
        out_specs=pl.BlockSpec((1, tq, hpw), lambda bi, h, m: (bi, m, h)),
        compiler_params=_params(("parallel", "parallel", "arbitrary"), 40),
        name="neighbourhood_attention",
    )(q, k, v, ctx_k, ctx_v, bias_table)


def _out_proj_kernel(a_ref, b_ref, w_ref, x_ref, gate_ref, o_ref):
    ka = a_ref.shape[1]
    acc = jnp.dot(a_ref[...], w_ref[:ka, :], preferred_element_type=F32)
    acc = acc + jnp.dot(b_ref[...], w_ref[ka:, :], preferred_element_type=F32)
    o_ref[...] = x_ref[...] + gate_ref[0] * acc


def out_proj_residual(a, b_, w_bf16, x, gate):
    r, d = x.shape
    ka, kb = a.shape[1], b_.shape[1]
    nseg = gate.shape[0]
    tm, tn = 1024, 1024
    tiles_per_seg = (r // nseg) // tm
    return pl.pallas_call(
        _out_proj_kernel,
        out_shape=jax.ShapeDtypeStruct((r, d), F32),
        grid=(r // tm, d // tn),
        in_specs=[pl.BlockSpec((tm, ka), lambda i, j: (i, 0)),
                  pl.BlockSpec((tm, kb), lambda i, j: (i, 0)),
                  pl.BlockSpec((ka + kb, tn), lambda i, j: (0, j)),
                  pl.BlockSpec((tm, tn), lambda i, j: (i, j)),
                  pl.BlockSpec((1, 1, tn), lambda i, j: (i // tiles_per_seg, 0, j))],
        out_specs=pl.BlockSpec((tm, tn), lambda i, j: (i, j)),
        compiler_params=_params(("parallel", "parallel"), 56),
        name="out_proj_residual",
    )(a, b_, w_bf16, x, gate)


def _norm_router_kernel(x_ref, g_ref, shift_ref, scale_ref, whi_ref, wlo_ref, *out_refs):
    logit_ref = out_refs[-1]
    h = _mod_norm(x_ref[...], g_ref[...], shift_ref[0], scale_ref[0])
    h_hi = h.astype(BF16)
    h_lo = (h - h_hi.astype(F32)).astype(BF16)
    if len(out_refs) > 1:
        out_refs[0][...] = h_hi
    w_hi = whi_ref[...]
    logit_ref[...] = (jnp.dot(h_hi, w_hi, preferred_element_type=F32)
                      + jnp.dot(h_lo, w_hi, preferred_element_type=F32)
                      + jnp.dot(h_hi, wlo_ref[...], preferred_element_type=F32))


def norm_router(x, g, shift, scale, router_w, want_h):
    r, d = x.shape
    nseg = shift.shape[0]
    tm = 512
    tiles_per_seg = (r // nseg) // tm
    w_pad = jnp.zeros((d, LANES), F32).at[:, :N_EXPERTS].set(router_w)
    w_hi = w_pad.astype(BF16)
    w_lo = (w_pad - w_hi.astype(F32)).astype(BF16)
    out_shape = [jax.ShapeDtypeStruct((r, LANES), F32)]
    out_specs = [pl.BlockSpec((tm, LANES), lambda i: (i, 0))]
    if want_h:
        out_shape.insert(0, jax.ShapeDtypeStruct((r, d), BF16))
        out_specs.insert(0, pl.BlockSpec((tm, d), lambda i: (i, 0)))
    return pl.pallas_call(
        _norm_router_kernel,
        out_shape=out_shape,
        grid=(r // tm,),
        in_specs=[pl.BlockSpec((tm, d), lambda i: (i, 0)),
                  pl.BlockSpec((1, d), lambda i: (0, 0)),
                  pl.BlockSpec((1, 1, d), lambda i: (i // tiles_per_seg, 0, 0)),
                  pl.BlockSpec((1, 1, d), lambda i: (i // tiles_per_seg, 0, 0)),
                  pl.BlockSpec((d, LANES), lambda i: (0, 0)),
                  pl.BlockSpec((d, LANES), lambda i: (0, 0))],
        out_specs=out_specs,
        compiler_params=_params(("parallel",), 32),
        name="norm_router",
    )(x, g.reshape(1, d), shift, scale, w_hi, w_lo)


CUMSUM_BLOCK = 256


def _exclusive_cumsum_lanes(flags):
    e, n = flags.shape
    cb = min(CUMSUM_BLOCK, n)
    tri = (lax.broadcasted_iota(jnp.int32, (cb, cb), 0) < lax.broadcasted_iota(jnp.int32, (cb, cb), 1)).astype(BF16)
    carry = jnp.zeros((e, 1), F32)
    out = []
    for j in range(n // cb):
        blk = flags[:, j * cb:(j + 1) * cb]
        out.append(jnp.dot(blk.astype(BF16), tri, preferred_element_type=F32) + carry)
        carry = carry + jnp.sum(blk, axis=-1, keepdims=True)
    return jnp.concatenate(out, axis=1) if len(out) > 1 else out[0]


def _select_kernel(logit_ref, pos_ref, gate_ref, *idx_refs, cap):
    lg = logit_ref[...]
    sb, ne, n = lg.shape
    ex = jnp.exp(lg - jnp.max(lg, axis=1, keepdims=True))
    aff = (ex / jnp.sum(ex, axis=1, keepdims=True)).reshape(sb * ne, n)
    e = sb * ne

    def enough(t):
        return jnp.sum((aff >= t).astype(F32), axis=-1, keepdims=True) >= cap

    hi = jnp.full((e, 1), 2.0, F32)
    for step in (64, 32, 16, 8, 4, 2, 1):
        cand = hi * (2.0 ** -step)
        hi = jnp.where(enough(cand), hi, cand)
    lo = hi * 0.5
    lo = jnp.where(enough(lo), lo, 0.0)
    for _ in range(23):
        mid = (lo + hi) * 0.5
        ok = enough(mid)
        lo = jnp.where(ok, mid, lo)
        hi = jnp.where(ok, hi, mid)
    thr = lo
    above = (aff > thr).astype(F32)
    tied = (aff == thr).astype(F32)
    need = cap - jnp.sum(above, axis=-1, keepdims=True)
    take = above + tied * (_exclusive_cumsum_lanes(tied) < need).astype(F32)
    slot = _exclusive_cumsum_lanes(take)
    pos = jnp.where(take > 0, slot, -1.0).astype(jnp.int32)
    pos_ref[...] = pos.reshape(sb, ne, n)
    s_iota = lax.broadcasted_iota(jnp.int32, (cap, n), 0)
    token = lax.broadcasted_iota(jnp.int32, (1, n), 1)
    t_hi = (token // 64).astype(F32)
    t_lo = (token % 64).astype(F32)
    row = lax.broadcasted_iota(jnp.int32, (8, n), 0)
    for ei in range(e):
        a = aff[ei:ei + 1, :]
        a0 = a.astype(BF16).astype(F32)
        a1 = (a - a0).astype(BF16).astype(F32)
        a2 = a - a0 - a1
        vals = jnp.where(row == 0, a0, jnp.where(row == 1, a1, jnp.where(row == 2, a2,
                         jnp.where(row == 3, t_hi, jnp.where(row == 4, t_lo, 0.0)))))
        hit = jnp.where(pos[ei:ei + 1, :] == s_iota, 1.0, 0.0).astype(BF16)
        res = lax.dot_general(vals.astype(BF16), hit, (((1,), (1,)), ((), ())), preferred_element_type=F32)
        si, xi = divmod(ei, ne)
        gate_ref[si, xi:xi + 1, :] = res[0:1] + res[1:2] + res[2:3]
        if idx_refs:
            idx_refs[0][si, xi:xi + 1, :] = (res[3:4] * 64.0 + res[4:5]).astype(jnp.int32)


def select_tokens(logits_t, cap, want_idx):
    nseg, e, n = logits_t.shape
    sb = min(nseg, 8)
    slots = (nseg, e, cap)
    out_shape = [jax.ShapeDtypeStruct((nseg, e, n), jnp.int32), jax.ShapeDtypeStruct(slots, F32)]
    out_specs = [pl.BlockSpec((sb, e, n), lambda s: (s, 0, 0)), pl.BlockSpec((sb, e, cap), lambda s: (s, 0, 0))]
    if want_idx:
        out_shape.append(jax.ShapeDtypeStruct(slots, jnp.int32))
        out_specs.append(pl.BlockSpec((sb, e, cap), lambda s: (s, 0, 0)))
    return pl.pallas_call(
        functools.partial(_select_kernel, cap=cap),
        out_shape=out_shape,
        grid=(nseg // sb,),
        in_specs=[pl.BlockSpec((sb, e, n), lambda s: (s, 0, 0))],
        out_specs=out_specs,
        compiler_params=_params(("parallel",), 48),
        name="select_tokens",
    )(logits_t)


def _gather_rows_kernel(idx_ref, x_hbm, g_ref, shift_ref, scale_ref, o_ref, buf, sem, *, cap, n, experts):
    step = pl.program_id(0)
    slot = step % 2

    def row_copy(st, sl, k):
        row = (st // experts) * n + idx_ref[st * cap + k]
        return pltpu.make_async_copy(x_hbm.at[pl.ds(row, 1), :], buf.at[sl, pl.ds(k, 1), :], sem.at[sl])

    def issue(st, sl):
        def body(k, carry):
            row_copy(st, sl, k).start()
            return carry
        lax.fori_loop(0, cap, body, 0, unroll=8)

    @pl.when(step == 0)
    def _():
        issue(step, slot)

    @pl.when(step + 1 < pl.num_programs(0))
    def _():
        issue(step + 1, 1 - slot)

    pltpu.make_async_copy(x_hbm.at[pl.ds(0, cap), :], buf.at[slot], sem.at[slot]).wait()
    h = _mod_norm(buf[slot], g_ref[...], shift_ref[0], scale_ref[0])
    o_ref[0, 0] = h.astype(o_ref.dtype)


def gather_rows(idx, x, g, shift, scale, n):
    nseg, e, cap = idx.shape
    d = x.shape[1]
    return pl.pallas_call(
        functools.partial(_gather_rows_kernel, cap=cap, n=n, experts=e),
        out_shape=jax.ShapeDtypeStruct((nseg, e, cap, d), BF16),
        grid_spec=pltpu.PrefetchScalarGridSpec(
            num_scalar_prefetch=1,
            grid=(nseg * e,),
            in_specs=[pl.BlockSpec(memory_space=pl.ANY),
                      pl.BlockSpec((1, d), lambda st, idx_ref: (0, 0)),
                      pl.BlockSpec((1, 1, d), lambda st, idx_ref: (st // e, 0, 0)),
                      pl.BlockSpec((1, 1, d), lambda st, idx_ref: (st // e, 0, 0))],
            out_specs=pl.BlockSpec((1, 1, cap, d), lambda st, idx_ref: (st // e, st % e, 0, 0)),
            scratch_shapes=[pltpu.VMEM((2, cap, d), F32), pltpu.SemaphoreType.DMA((2,))]),
        compiler_params=_params(("arbitrary",), 32, disable_bounds_checks=True),
        name="gather_rows",
    )(idx.reshape(-1), x, g.reshape(1, d), shift, scale)


def _gather_kernel(pos_ref, h_ref, o_ref, acc_ref, *, cap):
    kt = pl.program_id(2)
    eb = pos_ref.shape[1]
    tk = pos_ref.shape[3]
    s_iota = lax.broadcasted_iota(jnp.int32, (cap, tk), 0)
    onehot = jnp.concatenate([(pos_ref[0, i] == s_iota).astype(BF16) for i in range(eb)], axis=0)
    part = jnp.dot(onehot, h_ref[0], preferred_element_type=F32)

    @pl.when(kt == 0)
    def _():
        acc_ref[...] = part

    @pl.when(kt > 0)
    def _():
        acc_ref[...] += part

    @pl.when(kt == pl.num_programs(2) - 1)
    def _():
        o_ref[0] = acc_ref[...].reshape(o_ref.shape[1:]).astype(o_ref.dtype)


def gather_tokens(pos, h, cap, experts_per_step, tk):
    nseg, e, n = pos.shape
    d = h.shape[2]
    eb = experts_per_step
    return pl.pallas_call(
        functools.partial(_gather_kernel, cap=cap),
        out_shape=jax.ShapeDtypeStruct((nseg, e, cap, d), BF16),
        grid=(nseg, e // eb, n // tk),
        in_specs=[pl.BlockSpec((1, eb, 1, tk), lambda s, ei, kt: (s, ei, 0, kt)),
                  pl.BlockSpec((1, tk, d), lambda s, ei, kt: (s, kt, 0))],
        out_specs=pl.BlockSpec((1, eb, cap, d), lambda s, ei, kt: (s, ei, 0, 0)),
        scratch_shapes=[pltpu.VMEM((eb * cap, d), F32)],
        compiler_params=_params(("parallel", "parallel", "arbitrary"), 48),
        name="gather_tokens",
    )(pos.reshape(nseg, e, 1, n), h)


def _expert_up_kernel(xc_ref, xl_ref, wg_ref, wu_ref, o_ref):
    d = xc_ref.shape[-1]
    wg = wg_ref[0, 0].astype(BF16)
    wu = wu_ref[0, 0].astype(BF16)
    row = 0
    for x_ref in (xc_ref, xl_ref):
        rows = x_ref.shape[0] * x_ref.shape[2]
        x = x_ref[...].reshape(rows, d)
        a = jnp.dot(x, wg, preferred_element_type=F32)
        b = jnp.dot(x, wu, preferred_element_type=F32)
        o_ref[0, row:row + rows, :] = (_silu(a) * b).astype(o_ref.dtype)
        row += rows


def expert_up(xg_ctx, xg_lat, w_gate, w_up, layer):
    _, e, d, f = w_gate.shape
    tf = 512
    sc, _, cc, _ = xg_ctx.shape
    sl, _, cl, _ = xg_lat.shape
    rows = sc * cc + sl * cl
    return pl.pallas_call(
        _expert_up_kernel,
        out_shape=jax.ShapeDtypeStruct((e, rows, f), BF16),
        grid=(e, f // tf),
        in_specs=[pl.BlockSpec((sc, 1, cc, d), lambda ei, j: (0, ei, 0, 0)),
                  pl.BlockSpec((sl, 1, cl, d), lambda ei, j: (0, ei, 0, 0)),
                  pl.BlockSpec((1, 1, d, tf), lambda ei, j: (layer, ei, 0, j)),
                  pl.BlockSpec((1, 1, d, tf), lambda ei, j: (layer, ei, 0, j))],
        out_specs=pl.BlockSpec((1, rows, tf), lambda ei, j: (ei, 0, j)),
        compiler_params=_params(("parallel", "arbitrary"), 56),
        name="expert_up",
    )(xg_ctx, xg_lat, w_gate, w_up)


def _expert_down_kernel(h_ref, w_ref, gate_ref, o_ref):
    y = jnp.dot(h_ref[0], w_ref[0, 0].astype(BF16), preferred_element_type=F32)
    o_ref[0] = (y * gate_ref[0]).astype(o_ref.dtype)


def expert_down(hid, w_down, gates, layer):
    e, rows, f = hid.shape
    d = w_down.shape[3]
    td = 512
    return pl.pallas_call(
        _expert_down_kernel,
        out_shape=jax.ShapeDtypeStruct((e, rows, d), BF16),
        grid=(e, d // td),
        in_specs=[pl.BlockSpec((1, rows, f), lambda ei, j: (ei, 0, 0)),
                  pl.BlockSpec((1, 1, f, td), lambda ei, j: (layer, ei, 0, j)),
                  pl.BlockSpec((1, rows, 1), lambda ei, j: (ei, 0, 0))],
        out_specs=pl.BlockSpec((1, rows, td), lambda ei, j: (ei, 0, j)),
        compiler_params=_params(("parallel", "arbitrary"), 48),
        name="expert_down",
    )(hid, w_down, gates)


def _combine_kernel(post_ref, y_ref, x_ref, gate_ref, o_ref, hit_ref, *, cap):
    ne = y_ref.shape[0]
    tt = post_ref.shape[1]

    @pl.when(pl.program_id(2) == 0)
    def _():
        post = post_ref[0]
        width = max(cap, LANES)
        per = width // cap
        lane = lax.broadcasted_iota(jnp.int32, (tt, width), 1)
        for blk in range(ne // per):
            hit = None
            for i in range(per):
                col = post[:, blk * per + i:blk * per + i + 1]
                h = jnp.where(col >= 0, col + i * cap, -1) == lane
                hit = h if hit is None else (hit | h)
            hit_ref[:, blk * width:(blk + 1) * width] = jnp.where(hit, 1.0, 0.0).astype(BF16)

    y = y_ref[...].reshape(ne * cap, y_ref.shape[-1])
    o_ref[...] = x_ref[...] + gate_ref[0] * jnp.dot(hit_ref[...], y, preferred_element_type=F32)


def combine_tokens(pos_t, y, x, gate, cap, row_offset, tt, td):
    nseg, n, e = pos_t.shape
    r, d = x.shape
    nt = n // tt
    rb = row_offset // cap
    return pl.pallas_call(
        functools.partial(_combine_kernel, cap=cap),
        out_shape=jax.ShapeDtypeStruct((r, d), F32),
        grid=(nseg, nt, d // td),
        in_specs=[pl.BlockSpec((1, tt, e), lambda s, t, j: (s, t, 0)),
                  pl.BlockSpec((e, cap, td), lambda s, t, j: (0, rb + s, j)),
                  pl.BlockSpec((tt, td), lambda s, t, j: (s * nt + t, j)),
                  pl.BlockSpec((1, 1, td), lambda s, t, j: (s * gate.shape[0] // nseg, 0, j))],
        out_specs=pl.BlockSpec((tt, td), lambda s, t, j: (s * nt + t, j)),
        scratch_shapes=[pltpu.VMEM((tt, e * cap), BF16)],
        compiler_params=_params(("parallel", "parallel", "arbitrary"), 56),
        name="combine_tokens",
    )(pos_t, y, x, gate)


COMBINE_WINDOW = 256
COMBINE_ALIGN = 64


def _combine_window_kernel(ws_ref, post_ref, *refs, experts):
    y_refs = refs[:experts]
    x_ref, gate_ref, o_ref, hit_ref = refs[experts:]
    s, t = pl.program_id(0), pl.program_id(1)
    tt = post_ref.shape[1]
    win = COMBINE_WINDOW

    @pl.when(pl.program_id(2) == 0)
    def _():
        post = post_ref[0]
        lane = lax.broadcasted_iota(jnp.int32, (tt, win), 1)
        for e in range(experts):
            start = ws_ref[(s * pl.num_programs(1) + t) * experts + e]
            col = post[:, e:e + 1]
            local = jnp.where(col >= 0, col - start, -1)
            hit_ref[:, e * win:(e + 1) * win] = jnp.where(local == lane, 1.0, 0.0).astype(BF16)

    acc = jnp.dot(hit_ref[:, 0:win], y_refs[0][0], preferred_element_type=F32)
    for e in range(1, experts):
        acc = acc + jnp.dot(hit_ref[:, e * win:(e + 1) * win], y_refs[e][0], preferred_element_type=F32)
    o_ref[...] = x_ref[...] + gate_ref[0] * acc


def combine_tokens_windowed(win_start, pos_t, y, x, gate, cap, row_offset, tt, td):
    nseg, n, e = pos_t.shape
    r, d = x.shape
    nt = n // tt

    def y_spec(ei):
        return pl.BlockSpec((pl.Element(1), pl.Element(COMBINE_WINDOW), pl.Element(td)),
                            lambda s, t, j, ws: (
                                ei,
                                pl.multiple_of(row_offset + s * cap + ws[(s * nt + t) * e + ei], COMBINE_ALIGN),
                                j * td))

    return pl.pallas_call(
        functools.partial(_combine_window_kernel, experts=e),
        out_shape=jax.ShapeDtypeStruct((r, d), F32),
        grid_spec=pltpu.PrefetchScalarGridSpec(
            num_scalar_prefetch=1,
            grid=(nseg, nt, d // td),
            in_specs=[pl.BlockSpec((1, tt, e), lambda s, t, j, ws: (s, t, 0))]
                     + [y_spec(ei) for ei in range(e)]
                     + [pl.BlockSpec((tt, td), lambda s, t, j, ws: (s * nt + t, j)),
                        pl.BlockSpec((1, 1, td), lambda s, t, j, ws: (s * gate.shape[0] // nseg, 0, j))],
            out_specs=pl.BlockSpec((tt, td), lambda s, t, j, ws: (s * nt + t, j)),
            scratch_shapes=[pltpu.VMEM((tt, e * COMBINE_WINDOW), BF16)]),
        compiler_params=_params(("parallel", "parallel", "arbitrary"), 56),
        name="combine_tokens_windowed",
    )(win_start.reshape(-1), pos_t, *([y] * e), x, gate)


def _combine_windows(pos, tt):
    nseg, e, n = pos.shape
    taken = (pos >= 0).reshape(nseg, e, n // tt, tt).sum(-1)
    first = jnp.cumsum(taken, axis=-1) - taken
    cap = taken.sum(-1, keepdims=True)
    start = jnp.minimum(first // COMBINE_ALIGN * COMBINE_ALIGN, cap - COMBINE_WINDOW)
    fits = jnp.all(first + taken <= start + COMBINE_WINDOW)
    return jnp.swapaxes(start, 1, 2).astype(jnp.int32), fits


def moe_layer(groups, g_ffn, router_w, w_gate, w_up, w_down, layer):
    sel = []
    for grp in groups:
        nseg, n = grp["nseg"], grp["n"]
        cap = (EC_CAPACITY * n) // N_EXPERTS
        dense = n <= 512
        routed = norm_router(grp["x"], g_ffn, grp["shift"], grp["scale"], router_w, want_h=dense)
        logits_t = jnp.swapaxes(routed[-1][:, :N_EXPERTS].reshape(nseg, n, N_EXPERTS), 1, 2)
        picked = select_tokens(logits_t, cap, want_idx=not dense)
        pos, gates = picked[0], picked[1]
        if dense:
            xg = gather_tokens(pos, routed[0].reshape(nseg, n, D_MODEL), cap, N_EXPERTS, n)
        else:
            xg = gather_rows(picked[2].reshape(nseg, N_EXPERTS, cap), grp["x"], g_ffn, grp["shift"], grp["scale"], n)
        sel.append(dict(cap=cap, pos=pos, xg=xg,
                        gates=jnp.swapaxes(gates, 0, 1).reshape(N_EXPERTS, nseg * cap, 1)))
    hid = expert_up(sel[0]["xg"], sel[1]["xg"], w_gate, w_up, layer)
    y = expert_down(hid, w_down, jnp.concatenate([s["gates"] for s in sel], axis=1), layer)
    out = []
    row_offset = 0
    for grp, s in zip(groups, sel):
        n, cap = grp["n"], s["cap"]
        pos_t = jnp.swapaxes(s["pos"], 1, 2)
        if n <= 512:
            out.append(combine_tokens(pos_t, y, grp["x"], grp["gate"], cap, row_offset, n, D_MODEL))
        else:
            tt, td = 1024, 512
            start, fits = _combine_windows(s["pos"], tt)
            args = (pos_t, y, grp["x"], grp["gate"], cap, row_offset, tt, td)
            out.append(lax.cond(fits, lambda a=args, w=start: combine_tokens_windowed(w, *a),
                                lambda a=args: combine_tokens(*a)))
        row_offset += grp["nseg"] * cap
    return out


def _rope_tables(n):
    t = jnp.arange(n)
    row = (t // GRID_W).astype(F32)
    col = (t % GRID_W).astype(F32)
    nf = HEAD_DIM // 4
    inv = ROPE_THETA ** (-jnp.arange(nf, dtype=F32) / nf)
    ang = jnp.concatenate([row[:, None] * inv, col[:, None] * inv], axis=-1)
    cos, sin = jnp.cos(ang), jnp.sin(ang)
    return jnp.concatenate([cos, cos], axis=-1), jnp.concatenate([-sin, sin], axis=-1)


def kernel(x_prompt, x_sample, cache_win_k, cache_win_v, cache_nat_k, cache_nat_v, cache_glob_k, cache_glob_v, c, c_ctx, mod_w, mod_b, norm_mix_w, norm_ffn_w, even_w_in, even_w_out, conv_w, conv_b, win_sink, win_q_norm, win_k_norm, odd_w_in, odd_w_out, nat_rpb, nat_q_norm, nat_k_norm, glob_q_norm, glob_k_norm, router_w, expert_w_gate, expert_w_up, expert_w_down):
    bp, sp, d = x_prompt.shape
    bs, ss, _ = x_sample.shape
    depth = mod_w.shape[0]
    rope = _rope_tables(ss)

    cond8 = jnp.zeros((8, d), F32).at[0].set(c_ctx).at[1:1 + bs].set(c)
    mods = modulation(cond8, mod_w, mod_b).reshape(depth, 8, N_MOD, d)

    xp = x_prompt.reshape(bp * sp, d)
    xs = x_sample.reshape(bs * ss, d)
    win_k, win_v, nat_k, nat_v, glob_k, glob_v = [], [], [], [], [], []

    def flat_cache(cache, i):
        return cache[:, i].reshape(bs, cache.shape[2], -1).astype(BF16)

    for layer in range(depth):
        i = layer // 2
        mp = [mods[layer, 0:1, k].reshape(1, 1, d) for k in range(N_MOD)]
        ms = [mods[layer, 1:1 + bs, k].reshape(bs, 1, d) for k in range(N_MOD)]
        if layer % 2 == 0:
            w_in = even_w_in[i].astype(BF16)
            w_out = even_w_out[i].astype(BF16)
            q0 = 3 * CONV_DIM
            qw = WIN_Q_HEADS * HEAD_DIM
            kw = WIN_KV_HEADS * HEAD_DIM
            group = WIN_Q_HEADS // WIN_KV_HEADS
            h = mod_norm(xp, norm_mix_w[layer], mp[0], mp[1])
            (_, conv_in), (q, _), (k, k32), (v, v32) = project(h, w_in, [
                (q0, None, 1.0, False, False, True), (qw, win_q_norm[i], ATTN_SCALE, False, True, False),
                (kw, win_k_norm[i], 1.0, False, True, True), (kw, None, 1.0, False, True, True)])
            ya = conv_gate(conv_in, conv_w[i], conv_b[i], sp)
            yb = dense_attention(q.reshape(bp, sp, qw), k.reshape(bp, sp, kw), v.reshape(bp, sp, kw),
                                 group, sp, sink=win_sink[i])
            xp = out_proj_residual(ya, yb.reshape(bp * sp, qw), w_out, xp, mp[2])
            win_k.append(k32.reshape(bp, sp, WIN_KV_HEADS, HEAD_DIM))
            win_v.append(v32.reshape(bp, sp, WIN_KV_HEADS, HEAD_DIM))
            h = mod_norm(xs, norm_mix_w[layer], ms[0], ms[1])
            (_, conv_in), (q, _), (k, _), (v, _) = project(h, w_in, [
                (q0, None, 1.0, False, False, True), (qw, win_q_norm[i], ATTN_SCALE, True, True, False),
                (kw, win_k_norm[i], 1.0, True, True, False), (kw, None, 1.0, False, True, False)], rope)
            ya = conv_gate(conv_in, conv_w[i], conv_b[i], ss)
            yb = window_attention(q.reshape(bs, ss, qw), k.reshape(bs, ss, kw), v.reshape(bs, ss, kw),
                                  flat_cache(cache_win_k, i), flat_cache(cache_win_v, i), win_sink[i], group)
            xs = out_proj_residual(ya, yb.reshape(bs * ss, qw), w_out, xs, ms[2])
        else:
            w_in = odd_w_in[i].astype(BF16)
            w_out = odd_w_out[i].astype(BF16)
            nw = NAT_HEADS * HEAD_DIM
            gqw = GLOB_Q_HEADS * HEAD_DIM
            gkw = GLOB_KV_HEADS * HEAD_DIM
            group = GLOB_Q_HEADS // GLOB_KV_HEADS
            kd0 = 3 * nw + gqw
            h = mod_norm(xp, norm_mix_w[layer], mp[0], mp[1])
            (qc, _), (kc, kc32), (vc, vc32), (qd, _), (kd, kd32), (vd, vd32) = project(h, w_in, [
                (nw, nat_q_norm[i], ATTN_SCALE, False, True, False), (nw, nat_k_norm[i], 1.0, False, True, True),
                (nw, None, 1.0, False, True, True), (gqw, glob_q_norm[i], ATTN_SCALE, False, True, False),
                (gkw, glob_k_norm[i], 1.0, False, True, True), (gkw, None, 1.0, False, True, True)])
            yc = dense_attention(qc.reshape(bp, sp, nw), kc.reshape(bp, sp, nw), vc.reshape(bp, sp, nw), 1, sp)
            yd = dense_attention(qd.reshape(bp, sp, gqw), kd.reshape(bp, sp, gkw), vd.reshape(bp, sp, gkw), group, sp)
            xp = out_proj_residual(yc.reshape(bp * sp, nw), yd.reshape(bp * sp, gqw), w_out, xp, mp[2])
            nat_k.append(kc32.reshape(bp, sp, NAT_HEADS, HEAD_DIM))
            nat_v.append(vc32.reshape(bp, sp, NAT_HEADS, HEAD_DIM))
            glob_k.append(kd32.reshape(bp, sp, GLOB_KV_HEADS, HEAD_DIM))
            glob_v.append(vd32.reshape(bp, sp, GLOB_KV_HEADS, HEAD_DIM))
            h = mod_norm(xs, norm_mix_w[layer], ms[0], ms[1])
            (qc, _), (kc, _), (vc, _), (qd, _), (kd, _), (vd, _) = project(h, w_in, [
                (nw, nat_q_norm[i], ATTN_SCALE, False, True, False), (nw, nat_k_norm[i], 1.0, False, True, False),
                (nw, None, 1.0, False, True, False), (gqw, glob_q_norm[i], ATTN_SCALE, True, True, False),
                (gkw, glob_k_norm[i], 1.0, True, True, False), (gkw, None, 1.0, False, True, False)], rope)
            yc = neighbourhood_attention(qc.reshape(bs, ss, nw), kc.reshape(bs, ss, nw), vc.reshape(bs, ss, nw),
                                         flat_cache(cache_nat_k, i), flat_cache(cache_nat_v, i),
                                         _nat_bias_table(nat_rpb[i]))
            k_all = jnp.concatenate([kd.reshape(bs, ss, gkw), flat_cache(cache_glob_k, i)], axis=1)
            v_all = jnp.concatenate([vd.reshape(bs, ss, gkw), flat_cache(cache_glob_v, i)], axis=1)
            yd = dense_attention(qd.reshape(bs, ss, gqw), k_all, v_all, group, 256)
            xs = out_proj_residual(yc.reshape(bs * ss, nw), yd.reshape(bs * ss, gqw), w_out, xs, ms[2])

        xp, xs = moe_layer(
            [dict(x=xp, shift=mp[3], scale=mp[4], gate=mp[5], nseg=bp, n=sp),
             dict(x=xs, shift=ms[3], scale=ms[4], gate=ms[5], nseg=bs, n=ss)],
            norm_ffn_w[layer], router_w[layer], expert_w_gate, expert_w_up, expert_w_down, layer)

    stack = lambda xs_: jnp.stack(xs_, axis=1)
    return (xp.reshape(bp, sp, d), xs.reshape(bs, ss, d), stack(win_k), stack(win_v), stack(nat_k),
            stack(nat_v), stack(glob_k), stack(glob_v))
```

```python
import functools

import numpy as np
import jax
import jax.numpy as jnp
from jax import lax
from jax.experimental import pallas as pl
from jax.experimental.pallas import tpu as pltpu

D_MODEL = 2048
HEAD_DIM = 128
GRID_W = 64
CONV_DIM = 512
WIN_Q_HEADS = 12
WIN_KV_HEADS = 4
WIN_BLOCK = 128
WINDOW = 128
NAT_HEADS = 8
NAT_ROWS = 8
NAT_COLS = 16
GLOB_Q_HEADS = 8
GLOB_KV_HEADS = 2
ROPE_THETA = 10000.0
N_EXPERTS = 16
EXPERT_FF = 2048
EC_CAPACITY = 2
N_MOD = 6
NORM_EPS = 1e-6
NEG_INF = -1e30
ATTN_SCALE = HEAD_DIM ** -0.5

LANES = 128
MIB = 1024 * 1024
BF16 = jnp.bfloat16
F32 = jnp.float32


def _params(semantics, vmem_mib, **extra):
    return pltpu.CompilerParams(dimension_semantics=semantics, vmem_limit_bytes=vmem_mib * MIB, **extra)


def _silu(x):
    return x * (1.0 / (1.0 + jnp.exp(-x)))


def _modulation_kernel(cond_ref, w_ref, b_ref, o_ref):
    s = _silu(cond_ref[...]).astype(BF16)
    o_ref[0] = jnp.dot(s, w_ref[0].astype(BF16), preferred_element_type=F32) + b_ref[0]


def modulation(cond8, mod_w, mod_b):
    depth, d, n = mod_w.shape
    tn = 1024
    return pl.pallas_call(
        _modulation_kernel,
        out_shape=jax.ShapeDtypeStruct((depth, 8, n), F32),
        grid=(depth, n // tn),
        in_specs=[pl.BlockSpec((8, d), lambda l, j: (0, 0)),
                  pl.BlockSpec((1, d, tn), lambda l, j: (l, 0, j)),
                  pl.BlockSpec((1, 1, tn), lambda l, j: (l, 0, j))],
        out_specs=pl.BlockSpec((1, 8, tn), lambda l, j: (l, 0, j)),
        compiler_params=_params(("parallel", "parallel"), 48),
        name="modulation",
    )(cond8, mod_w, mod_b.reshape(depth, 1, n))


def _mod_norm(x, g, shift, scale):
    ms = jnp.mean(x * x, axis=-1, keepdims=True)
    y = x * lax.rsqrt(ms + NORM_EPS) * g
    return y * (1.0 + scale) + shift


def _mod_norm_kernel(x_ref, g_ref, shift_ref, scale_ref, h_ref):
    h_ref[...] = _mod_norm(x_ref[...], g_ref[...], shift_ref[0], scale_ref[0]).astype(BF16)


def mod_norm(x, g, shift, scale):
    r, d = x.shape
    nseg = shift.shape[0]
    tm = 512
    tiles_per_seg = (r // nseg) // tm
    return pl.pallas_call(
        _mod_norm_kernel,
        out_shape=jax.ShapeDtypeStruct((r, d), BF16),
        grid=(r // tm,),
        in_specs=[pl.BlockSpec((tm, d), lambda i: (i, 0)),
                  pl.BlockSpec((1, d), lambda i: (0, 0)),
                  pl.BlockSpec((1, 1, d), lambda i: (i // tiles_per_seg, 0, 0)),
                  pl.BlockSpec((1, 1, d), lambda i: (i // tiles_per_seg, 0, 0))],
        out_specs=pl.BlockSpec((tm, d), lambda i: (i, 0)),
        compiler_params=_params(("parallel",), 32),
        name="mod_norm",
    )(x, g.reshape(1, d), shift, scale)


PROJ_TILE = 512


def _proj_kernel(*refs, tiles, n_gain, has_rope):
    h_ref, w_ref = refs[0], refs[1]
    k = 2
    if n_gain:
        g_ref = refs[k]
        k += 1
    if has_rope:
        cos_ref, sin_ref = refs[k], refs[k + 1]
        k += 2
    outs = refs[k:]
    j = pl.program_id(1)
    acc = jnp.dot(h_ref[...], w_ref[...], preferred_element_type=F32)
    ones = jnp.ones((HEAD_DIM, HEAD_DIM), BF16)

    def finish(parts):
        for lo, width, dst, gain_idx, scale, rope, o_bf16, o_f32 in parts:
            if gain_idx is None:
                x = acc[:, lo:lo + width]
                if o_bf16 is not None:
                    outs[o_bf16][:, dst:dst + width] = x.astype(BF16)
                if o_f32 is not None:
                    outs[o_f32][:, dst:dst + width] = x
                continue
            for hd in range(width // HEAD_DIM):
                x = acc[:, lo + hd * HEAD_DIM:lo + (hd + 1) * HEAD_DIM]
                sl = slice(dst + hd * HEAD_DIM, dst + (hd + 1) * HEAD_DIM)
                ssq = jnp.dot((x * x).astype(BF16), ones, preferred_element_type=F32)
                y = x * lax.rsqrt(ssq * (1.0 / HEAD_DIM) + NORM_EPS) * g_ref[gain_idx:gain_idx + 1, :]
                if o_f32 is not None:
                    outs[o_f32][:, sl] = y
                if rope:
                    y = y * cos_ref[...] + pltpu.roll(y, HEAD_DIM // 2, 1) * sin_ref[...]
                outs[o_bf16][:, sl] = (y * scale).astype(BF16)

    for jv, parts in enumerate(tiles):
        pl.when(j == jv)(functools.partial(finish, parts))


def project(h, w_bf16, pieces, rope_tables=None):
    r, d = h.shape
    n = w_bf16.shape[1]
    tm, tn = 1024, PROJ_TILE
    n_tiles = n // tn
    gains = [p[1] for p in pieces if p[1] is not None]
    has_rope = any(p[3] for p in pieces)
    tiles = [[] for _ in range(n_tiles)]
    out_shape, out_specs, result = [], [], []
    col0 = 0
    for width, gain, scale, rope, want_bf16, want_f32 in pieces:
        block_w = min(width, tn)
        j0, n_blk = col0 // tn, max(width // tn, 1)
        slots = []
        for want, dt in ((want_bf16, BF16), (want_f32, F32)):
            if want:
                slots.append(len(out_shape))
                out_shape.append(jax.ShapeDtypeStruct((r, width), dt))
                out_specs.append(pl.BlockSpec(
                    (tm, block_w), lambda i, j, j0=j0, n_blk=n_blk: (i, jnp.clip(j - j0, 0, n_blk - 1))))
            else:
                slots.append(None)
        result.append(slots)
        gain_idx = None if gain is None else [g is gain for g in gains].index(True)
        for jv in range(j0, (col0 + width - 1) // tn + 1):
            start = max(col0, jv * tn)
            stop = min(col0 + width, (jv + 1) * tn)
            tiles[jv].append((start - jv * tn, stop - start, (start - col0) % block_w, gain_idx, scale, rope,
                              slots[0], slots[1]))
        col0 += width
    assert col0 == n
    in_specs = [pl.BlockSpec((tm, d), lambda i, j: (i, 0)),
                pl.BlockSpec((d, tn), lambda i, j: (0, j))]
    args = [h, w_bf16]
    if gains:
        in_specs.append(pl.BlockSpec((len(gains), HEAD_DIM), lambda i, j: (0, 0)))
        args.append(jnp.stack(gains))
    if has_rope:
        per = rope_tables[0].shape[0] // tm
        in_specs += [pl.BlockSpec((tm, HEAD_DIM), lambda i, j: (i % per, 0))] * 2
        args += list(rope_tables)
    out = pl.pallas_call(
        functools.partial(_proj_kernel, tiles=tiles, n_gain=len(gains), has_rope=has_rope),
        out_shape=out_shape,
        grid=(r // tm, n_tiles),
        in_specs=in_specs,
        out_specs=out_specs,
        compiler_params=_params(("parallel", "arbitrary"), 56),
        name="project",
    )(*args)
    return [tuple(None if s is None else out[s] for s in slots) for slots in result]


def _conv_gate_kernel(ab_ref, ac_ref, ah_ref, w_ref, b_ref, o_ref, *, seq):
    u = ac_ref[...] * ah_ref[...]
    rows = u.shape[0]
    pos = lax.broadcasted_iota(jnp.int32, u.shape, 0) % seq
    prev = jnp.where(pos == 0, 0.0, pltpu.roll(u, 1, 0))
    nxt = jnp.where(pos == seq - 1, 0.0, pltpu.roll(u, rows - 1, 0))
    conv = prev * w_ref[0:1, :] + u * w_ref[1:2, :] + nxt * w_ref[2:3, :] + b_ref[...]
    o_ref[...] = (ab_ref[...] * conv).astype(o_ref.dtype)


def conv_gate(proj, conv_w, conv_b, seq):
    r = proj.shape[0]
    tr = 4096
    nc = CONV_DIM // LANES
    return pl.pallas_call(
        functools.partial(_conv_gate_kernel, seq=seq),
        out_shape=jax.ShapeDtypeStruct((r, CONV_DIM), BF16),
        grid=(r // tr, nc),
        in_specs=[pl.BlockSpec((tr, LANES), lambda i, c: (i, c)),
                  pl.BlockSpec((tr, LANES), lambda i, c: (i, nc + c)),
                  pl.BlockSpec((tr, LANES), lambda i, c: (i, 2 * nc + c)),
                  pl.BlockSpec((3, LANES), lambda i, c: (0, c)),
                  pl.BlockSpec((1, LANES), lambda i, c: (0, c))],
        out_specs=pl.BlockSpec((tr, LANES), lambda i, c: (i, c)),
        compiler_params=_params(("parallel", "parallel"), 48),
        name="conv_gate",
    )(proj, proj, proj, conv_w, conv_b.reshape(1, CONV_DIM))


def _with_ones(v):
    return jnp.concatenate([v, jnp.ones_like(v)], axis=1)


def _dense_attn_kernel(*refs, group, has_sink):
    if has_sink:
        sink_ref, q_ref, k_ref, v_ref, o_ref = refs
    else:
        q_ref, k_ref, v_ref, o_ref = refs
    qb = q_ref.shape[1]
    for hkv in range(k_ref.shape[2] // HEAD_DIM):
        head = lambda g: slice((hkv * group + g) * HEAD_DIM, (hkv * group + g + 1) * HEAD_DIM)
        kv = slice(hkv * HEAD_DIM, (hkv + 1) * HEAD_DIM)
        q = jnp.concatenate([q_ref[0, :, head(g)] for g in range(group)], axis=0)
        s = lax.dot_general(q, k_ref[0, :, kv], (((1,), (1,)), ((), ())), preferred_element_type=F32)
        m = jnp.max(s, axis=-1, keepdims=True)
        if has_sink:
            sk = jnp.concatenate([jnp.full((qb, 1), sink_ref[hkv * group + g], F32) for g in range(group)], axis=0)
            m = jnp.maximum(m, sk)
        p = jnp.exp(s - m)
        oa = jnp.dot(p.astype(BF16), _with_ones(v_ref[0, :, kv]), preferred_element_type=F32)
        den = oa[:, HEAD_DIM:]
        if has_sink:
            den = den + jnp.exp(sk - m)
        o = oa[:, :HEAD_DIM] / den
        for g in range(group):
            o_ref[0, :, head(g)] = o[g * qb:(g + 1) * qb].astype(o_ref.dtype)


def dense_attention(q, k, v, group, q_block, sink=None):
    b, nq, qw = q.shape
    nk = k.shape[1]
    kvw = k.shape[2]
    in_specs = [pl.BlockSpec((1, q_block, qw), lambda bi, qi: (bi, qi, 0)),
                pl.BlockSpec((1, nk, kvw), lambda bi, qi: (bi, 0, 0)),
                pl.BlockSpec((1, nk, kvw), lambda bi, qi: (bi, 0, 0))]
    args = [q, k, v]
    if sink is not None:
        in_specs = [pl.BlockSpec(memory_space=pltpu.SMEM)] + in_specs
        args = [sink] + args
    return pl.pallas_call(
        functools.partial(_dense_attn_kernel, group=group, has_sink=sink is not None),
        out_shape=jax.ShapeDtypeStruct((b, nq, qw), BF16),
        grid=(b, nq // q_block),
        in_specs=in_specs,
        out_specs=pl.BlockSpec((1, q_block, qw), lambda bi, qi: (bi, qi, 0)),
        compiler_params=_params(("parallel", "parallel"), 56),
        name="dense_attention",
    )(*args)


def _window_attn_kernel(sink_ref, q_ref, kp_ref, kc_ref, kn_ref, vp_ref, vc_ref, vn_ref, ck_ref, cv_ref, o_ref,
                        *, group, n_tokens):
    blk = pl.program_id(1)
    wb = WIN_BLOCK
    rows, cols = group * wb, 3 * wb + ck_ref.shape[1]
    q_pos = blk * wb + lax.broadcasted_iota(jnp.int32, (rows, cols), 0) % wb
    col = lax.broadcasted_iota(jnp.int32, (rows, cols), 1)
    k_pos = (blk - 1) * wb + col
    local_ok = (jnp.abs(k_pos - q_pos) <= WINDOW) & (k_pos >= 0) & (k_pos < n_tokens)
    visible = (col >= 3 * wb) | local_ok
    for hkv in range(kc_ref.shape[2] // HEAD_DIM):
        head = lambda g: slice((hkv * group + g) * HEAD_DIM, (hkv * group + g + 1) * HEAD_DIM)
        kv = slice(hkv * HEAD_DIM, (hkv + 1) * HEAD_DIM)
        q = jnp.concatenate([q_ref[0, :, head(g)] for g in range(group)], axis=0)
        keys = jnp.concatenate([kp_ref[0, :, kv], kc_ref[0, :, kv], kn_ref[0, :, kv], ck_ref[0, :, kv]], axis=0)
        vals = jnp.concatenate([vp_ref[0, :, kv], vc_ref[0, :, kv], vn_ref[0, :, kv], cv_ref[0, :, kv]], axis=0)
        s = lax.dot_general(q, keys, (((1,), (1,)), ((), ())), preferred_element_type=F32)
        s = jnp.where(visible, s, NEG_INF)
        sk = jnp.concatenate([jnp.full((wb, 1), sink_ref[hkv * group + g], F32) for g in range(group)], axis=0)
        m = jnp.maximum(jnp.max(s, axis=-1, keepdims=True), sk)
        p = jnp.exp(s - m)
        oa = jnp.dot(p.astype(BF16), _with_ones(vals), preferred_element_type=F32)
        o = oa[:, :HEAD_DIM] / (oa[:, HEAD_DIM:] + jnp.exp(sk - m))
        for g in range(group):
            o_ref[0, :, head(g)] = o[g * wb:(g + 1) * wb].astype(o_ref.dtype)


def window_attention(q, k, v, ctx_k, ctx_v, sink, group):
    b, n, qw = q.shape
    kvw = k.shape[2]
    nb = n // WIN_BLOCK
    nctx = ctx_k.shape[1]
    prev_map = lambda bi, i: (bi, jnp.maximum(i - 1, 0), 0)
    cur_map = lambda bi, i: (bi, i, 0)
    next_map = lambda bi, i: (bi, jnp.minimum(i + 1, nb - 1), 0)
    ctx_map = lambda bi, i: (bi, 0, 0)
    kv_block = (1, WIN_BLOCK, kvw)
    return pl.pallas_call(
        functools.partial(_window_attn_kernel, group=group, n_tokens=n),
        out_shape=jax.ShapeDtypeStruct((b, n, qw), BF16),
        grid=(b, nb),
        in_specs=[pl.BlockSpec(memory_space=pltpu.SMEM),
                  pl.BlockSpec((1, WIN_BLOCK, qw), cur_map),
                  pl.BlockSpec(kv_block, prev_map), pl.BlockSpec(kv_block, cur_map), pl.BlockSpec(kv_block, next_map),
                  pl.BlockSpec(kv_block, prev_map), pl.BlockSpec(kv_block, cur_map), pl.BlockSpec(kv_block, next_map),
                  pl.BlockSpec((1, nctx, kvw), ctx_map), pl.BlockSpec((1, nctx, kvw), ctx_map)],
        out_specs=pl.BlockSpec((1, WIN_BLOCK, qw), cur_map),
        compiler_params=_params(("parallel", "parallel"), 32),
        name="window_attention",
    )(sink, q, k, k, k, v, v, v, ctx_k, ctx_v)


NAT_QROWS = 8
NAT_KROWS = 16


def _nat_bias_table(rpb):
    w = GRID_W
    c = np.arange(w)[:, None]
    kc = np.arange(w)[None, :]
    ws = np.clip(c - NAT_COLS // 2, 0, w - NAT_COLS)
    col_ok = (kc >= ws) & (kc < ws + NAT_COLS)
    col_idx = np.clip(kc - c + NAT_COLS - 1, 0, 2 * NAT_COLS - 2)
    pick = (col_idx[..., None] == np.arange(2 * NAT_COLS - 1)).astype(np.float32)
    looked_up = jnp.einsum("hdj,ckj->hdck", rpb, pick, precision=lax.Precision.HIGHEST)
    t = jnp.where(col_ok[None, None], looked_up, NEG_INF)
    dead = jnp.full_like(t[:, :1], NEG_INF)
    t = jnp.concatenate([dead, t, dead], axis=1)
    return jnp.concatenate([t[:, :-1], t[:, 1:]], axis=-1)


def _nat_attn_kernel(q_ref, k_ref, v_ref, ck_ref, cv_ref, t_ref, o_ref, *, grid_rows):
    m = pl.program_id(2)
    w = GRID_W
    r0 = m * NAT_QROWS
    kr0 = jnp.clip(r0 - NAT_ROWS // 2, 0, grid_rows - NAT_KROWS)
    tok0 = pl.multiple_of(kr0 * w, 4 * w)
    lane_hi = lax.broadcasted_iota(jnp.int32, (w, 2 * w), 1) >= w
    layout = []
    for i in range(NAT_QROWS):
        r = r0 + i
        rs = jnp.clip(r - NAT_ROWS // 2, 0, grid_rows - NAT_ROWS)
        row = []
        for jj in range(NAT_KROWS // 2):
            kr = kr0 + 2 * jj
            d = kr - r + NAT_ROWS - 1
            ok_lo = ((kr >= rs) & (kr < rs + NAT_ROWS)).astype(jnp.int32)
            ok_hi = ((kr + 1 >= rs) & (kr + 1 < rs + NAT_ROWS)).astype(jnp.int32)
            row.append((jnp.clip(d, -1, 2 * NAT_ROWS - 2) + 1, jnp.where(lane_hi, ok_hi, ok_lo) > 0))
        layout.append(row)
    for hh in range(q_ref.shape[2] // HEAD_DIM):
        hs = slice(hh * HEAD_DIM, (hh + 1) * HEAD_DIM)
        keys = k_ref[0, pl.ds(tok0, NAT_KROWS * w), hs]
        vals = v_ref[0, pl.ds(tok0, NAT_KROWS * w), hs]
        q = q_ref[0, :, hs]
        s_loc = lax.dot_general(q, keys, (((1,), (1,)), ((), ())), preferred_element_type=F32)
        bias = jnp.concatenate(
            [jnp.concatenate([jnp.where(ok, t_ref[hh, entry], NEG_INF) for entry, ok in row], axis=1)
             for row in layout], axis=0)
        s_loc = s_loc + bias
        s_ctx = lax.dot_general(q, ck_ref[0, :, hs], (((1,), (1,)), ((), ())), preferred_element_type=F32)
        mx = jnp.maximum(jnp.max(s_loc, axis=-1, keepdims=True), jnp.max(s_ctx, axis=-1, keepdims=True))
        p_loc = jnp.exp(s_loc - mx)
        p_ctx = jnp.exp(s_ctx - mx)
        oa = (jnp.dot(p_loc.astype(BF16), _with_ones(vals), preferred_element_type=F32)
              + jnp.dot(p_ctx.astype(BF16), _with_ones(cv_ref[0, :, hs]), preferred_element_type=F32))
        o_ref[0, :, hs] = (oa[:, :HEAD_DIM] / oa[:, HEAD_DIM:]).astype(o_ref.dtype)


NAT_HEADS_PER_STEP = 2


def neighbourhood_attention(q, k, v, ctx_k, ctx_v, bias_table):
    b, n, hw = q.shape
    hp = NAT_HEADS_PER_STEP
    hpw = hp * HEAD_DIM
    grid_rows = n // GRID_W
    nctx = ctx_k.shape[1]
    tq = NAT_QROWS * GRID_W
    return pl.pallas_call(
        functools.partial(_nat_attn_kernel, grid_rows=grid_rows),
        out_shape=jax.ShapeDtypeStruct((b, n, hw), BF16),
        grid=(b, hw // hpw, grid_rows // NAT_QROWS),
        in_specs=[pl.BlockSpec((1, tq, hpw), lambda bi, h, m: (bi, m, h)),
                  pl.BlockSpec((1, n, hpw), lambda bi, h, m: (bi, 0, h)),
                  pl.BlockSpec((1, n, hpw), lambda bi, h, m: (bi, 0, h)),
                  pl.BlockSpec((1, nctx, hpw), lambda bi, h, m: (bi, 0, h)),
                  pl.BlockSpec((1, nctx, hpw), lambda bi, h, m: (bi, 0, h)),
                  pl.BlockSpec((hp, 2 * NAT_ROWS, GRID_W, 2 * GRID_W), lambda bi, h, m: (h, 0, 0, 0))],
        out_specs=pl.BlockSpec((1, tq, hpw), lambda bi, h, m: (bi, m, h)),
        compiler_params=_params(("parallel", "parallel", "arbitrary"), 40),
        name="neighbourhood_attention",
    )(q, k, v, ctx_k, ctx_v, bias_table)


def _out_proj_kernel(a_ref, b_ref, w_ref, x_ref, gate_ref, o_ref):
    ka = a_ref.shape[1]
    acc = jnp.dot(a_ref[...], w_ref[:ka, :], preferred_element_type=F32)
    acc = acc + jnp.dot(b_ref[...], w_ref[ka:, :], preferred_element_type=F32)
    o_ref[...] = x_ref[...] + gate_ref[0] * acc


def out_proj_residual(a, b_, w_bf16, x, gate):
    r, d = x.shape
    ka, kb = a.shape[1], b_.shape[1]
    nseg = gate.shape[0]
    tm, tn = 1024, 1024
    tiles_per_seg = (r // nseg) // tm
    return pl.pallas_call(
        _out_proj_kernel,
        out_shape=jax.ShapeDtypeStruct((r, d), F32),
        grid=(r // tm, d // tn),
        in_specs=[pl.BlockSpec((tm, ka), lambda i, j: (i, 0)),
                  pl.BlockSpec((tm, kb), lambda i, j: (i, 0)),
                  pl.BlockSpec((ka + kb, tn), lambda i, j: (0, j)),
                  pl.BlockSpec((tm, tn), lambda i, j: (i, j)),
                  pl.BlockSpec((1, 1, tn), lambda i, j: (i // tiles_per_seg, 0, j))],
        out_specs=pl.BlockSpec((tm, tn), lambda i, j: (i, j)),
        compiler_params=_params(("parallel", "parallel"), 56),
        name="out_proj_residual",
    )(a, b_, w_bf16, x, gate)


def _norm_router_kernel(x_ref, g_ref, shift_ref, scale_ref, whi_ref, wlo_ref, *out_refs):
    logit_ref = out_refs[-1]
    h = _mod_norm(x_ref[...], g_ref[...], shift_ref[0], scale_ref[0])
    h_hi = h.astype(BF16)
    h_lo = (h - h_hi.astype(F32)).astype(BF16)
    if len(out_refs) > 1:
        out_refs[0][...] = h_hi
    w_hi = whi_ref[...]
    logit_ref[...] = (jnp.dot(h_hi, w_hi, preferred_element_type=F32)
                      + jnp.dot(h_lo, w_hi, preferred_element_type=F32)
                      + jnp.dot(h_hi, wlo_ref[...], preferred_element_type=F32))


def norm_router(x, g, shift, scale, router_w, want_h):
    r, d = x.shape
    nseg = shift.shape[0]
    tm = 512
    tiles_per_seg = (r // nseg) // tm
    w_pad = jnp.zeros((d, LANES), F32).at[:, :N_EXPERTS].set(router_w)
    w_hi = w_pad.astype(BF16)
    w_lo = (w_pad - w_hi.astype(F32)).astype(BF16)
    out_shape = [jax.ShapeDtypeStruct((r, LANES), F32)]
    out_specs = [pl.BlockSpec((tm, LANES), lambda i: (i, 0))]
    if want_h:
        out_shape.insert(0, jax.ShapeDtypeStruct((r, d), BF16))
        out_specs.insert(0, pl.BlockSpec((tm, d), lambda i: (i, 0)))
    return pl.pallas_call(
        _norm_router_kernel,
        out_shape=out_shape,
        grid=(r // tm,),
        in_specs=[pl.BlockSpec((tm, d), lambda i: (i, 0)),
                  pl.BlockSpec((1, d), lambda i: (0, 0)),
                  pl.BlockSpec((1, 1, d), lambda i: (i // tiles_per_seg, 0, 0)),
                  pl.BlockSpec((1, 1, d), lambda i: (i // tiles_per_seg, 0, 0)),
                  pl.BlockSpec((d, LANES), lambda i: (0, 0)),
                  pl.BlockSpec((d, LANES), lambda i: (0, 0))],
        out_specs=out_specs,
        compiler_params=_params(("parallel",), 32),
        name="norm_router",
    )(x, g.reshape(1, d), shift, scale, w_hi, w_lo)


CUMSUM_BLOCK = 256


def _exclusive_cumsum_lanes(flags):
    e, n = flags.shape
    cb = min(CUMSUM_BLOCK, n)
    tri = (lax.broadcasted_iota(jnp.int32, (cb, cb), 0) < lax.broadcasted_iota(jnp.int32, (cb, cb), 1)).astype(BF16)
    carry = jnp.zeros((e, 1), F32)
    out = []
    for j in range(n // cb):
        blk = flags[:, j * cb:(j + 1) * cb]
        out.append(jnp.dot(blk.astype(BF16), tri, preferred_element_type=F32) + carry)
        carry = carry + jnp.sum(blk, axis=-1, keepdims=True)
    return jnp.concatenate(out, axis=1) if len(out) > 1 else out[0]


def _select_kernel(logit_ref, pos_ref, gate_ref, *idx_refs, cap):
    lg = logit_ref[...]
    sb, ne, n = lg.shape
    ex = jnp.exp(lg - jnp.max(lg, axis=1, keepdims=True))
    aff = (ex / jnp.sum(ex, axis=1, keepdims=True)).reshape(sb * ne, n)
    e = sb * ne

    def enough(t):
        return jnp.sum((aff >= t).astype(F32), axis=-1, keepdims=True) >= cap

    hi = jnp.full((e, 1), 2.0, F32)
    for step in (64, 32, 16, 8, 4, 2, 1):
        cand = hi * (2.0 ** -step)
        hi = jnp.where(enough(cand), hi, cand)
    lo = hi * 0.5
    lo = jnp.where(enough(lo), lo, 0.0)
    for _ in range(23):
        mid = (lo + hi) * 0.5
        ok = enough(mid)
        lo = jnp.where(ok, mid, lo)
        hi = jnp.where(ok, hi, mid)
    thr = lo
    above = (aff > thr).astype(F32)
    tied = (aff == thr).astype(F32)
    need = cap - jnp.sum(above, axis=-1, keepdims=True)
    take = above + tied * (_exclusive_cumsum_lanes(tied) < need).astype(F32)
    slot = _exclusive_cumsum_lanes(take)
    pos = jnp.where(take > 0, slot, -1.0).astype(jnp.int32)
    pos_ref[...] = pos.reshape(sb, ne, n)
    s_iota = lax.broadcasted_iota(jnp.int32, (cap, n), 0)
    token = lax.broadcasted_iota(jnp.int32, (1, n), 1)
    t_hi = (token // 64).astype(F32)
    t_lo = (token % 64).astype(F32)
    row = lax.broadcasted_iota(jnp.int32, (8, n), 0)
    for ei in range(e):
        a = aff[ei:ei + 1, :]
        a0 = a.astype(BF16).astype(F32)
        a1 = (a - a0).astype(BF16).astype(F32)
        a2 = a - a0 - a1
        vals = jnp.where(row == 0, a0, jnp.where(row == 1, a1, jnp.where(row == 2, a2,
                         jnp.where(row == 3, t_hi, jnp.where(row == 4, t_lo, 0.0)))))
        hit = jnp.where(pos[ei:ei + 1, :] == s_iota, 1.0, 0.0).astype(BF16)
        res = lax.dot_general(vals.astype(BF16), hit, (((1,), (1,)), ((), ())), preferred_element_type=F32)
        si, xi = divmod(ei, ne)
        gate_ref[si, xi:xi + 1, :] = res[0:1] + res[1:2] + res[2:3]
        if idx_refs:
            idx_refs[0][si, xi:xi + 1, :] = (res[3:4] * 64.0 + res[4:5]).astype(jnp.int32)


def select_tokens(logits_t, cap, want_idx):
    nseg, e, n = logits_t.shape
    sb = min(nseg, 8)
    slots = (nseg, e, cap)
    out_shape = [jax.ShapeDtypeStruct((nseg, e, n), jnp.int32), jax.ShapeDtypeStruct(slots, F32)]
    out_specs = [pl.BlockSpec((sb, e, n), lambda s: (s, 0, 0)), pl.BlockSpec((sb, e, cap), lambda s: (s, 0, 0))]
    if want_idx:
        out_shape.append(jax.ShapeDtypeStruct(slots, jnp.int32))
        out_specs.append(pl.BlockSpec((sb, e, cap), lambda s: (s, 0, 0)))
    return pl.pallas_call(
        functools.partial(_select_kernel, cap=cap),
        out_shape=out_shape,
        grid=(nseg // sb,),
        in_specs=[pl.BlockSpec((sb, e, n), lambda s: (s, 0, 0))],
        out_specs=out_specs,
        compiler_params=_params(("parallel",), 48),
        name="select_tokens",
    )(logits_t)


def _gather_rows_kernel(idx_ref, x_hbm, g_ref, shift_ref, scale_ref, o_ref, buf, sem, *, cap, n, experts):
    step = pl.program_id(0)
    slot = step % 2

    def row_copy(st, sl, k):
        row = (st // experts) * n + idx_ref[st * cap + k]
        return pltpu.make_async_copy(x_hbm.at[pl.ds(row, 1), :], buf.at[sl, pl.ds(k, 1), :], sem.at[sl])

    def issue(st, sl):
        def body(k, carry):
            row_copy(st, sl, k).start()
            return carry
        lax.fori_loop(0, cap, body, 0, unroll=16)

    @pl.when(step == 0)
    def _():
        issue(step, slot)

    @pl.when(step + 1 < pl.num_programs(0))
    def _():
        issue(step + 1, 1 - slot)

    pltpu.make_async_copy(x_hbm.at[pl.ds(0, cap), :], buf.at[slot], sem.at[slot]).wait()
    h = _mod_norm(buf[slot], g_ref[...], shift_ref[0], scale_ref[0])
    o_ref[0, 0] = h.astype(o_ref.dtype)


def gather_rows(idx, x, g, shift, scale, n):
    nseg, e, cap = idx.shape
    d = x.shape[1]
    return pl.pallas_call(
        functools.partial(_gather_rows_kernel, cap=cap, n=n, experts=e),
        out_shape=jax.ShapeDtypeStruct((nseg, e, cap, d), BF16),
        grid_spec=pltpu.PrefetchScalarGridSpec(
            num_scalar_prefetch=1,
            grid=(nseg * e,),
            in_specs=[pl.BlockSpec(memory_space=pl.ANY),
                      pl.BlockSpec((1, d), lambda st, idx_ref: (0, 0)),
                      pl.BlockSpec((1, 1, d), lambda st, idx_ref: (st // e, 0, 0)),
                      pl.BlockSpec((1, 1, d), lambda st, idx_ref: (st // e, 0, 0))],
            out_specs=pl.BlockSpec((1, 1, cap, d), lambda st, idx_ref: (st // e, st % e, 0, 0)),
            scratch_shapes=[pltpu.VMEM((2, cap, d), F32), pltpu.SemaphoreType.DMA((2,))]),
        compiler_params=_params(("arbitrary",), 32, disable_bounds_checks=True),
        name="gather_rows",
    )(idx.reshape(-1), x, g.reshape(1, d), shift, scale)


def _gather_kernel(pos_ref, h_ref, o_ref, acc_ref, *, cap):
    kt = pl.program_id(2)
    eb = pos_ref.shape[1]
    tk = pos_ref.shape[3]
    s_iota = lax.broadcasted_iota(jnp.int32, (cap, tk), 0)
    onehot = jnp.concatenate([(pos_ref[0, i] == s_iota).astype(BF16) for i in range(eb)], axis=0)
    part = jnp.dot(onehot, h_ref[0], preferred_element_type=F32)

    @pl.when(kt == 0)
    def _():
        acc_ref[...] = part

    @pl.when(kt > 0)
    def _():
        acc_ref[...] += part

    @pl.when(kt == pl.num_programs(2) - 1)
    def _():
        o_ref[0] = acc_ref[...].reshape(o_ref.shape[1:]).astype(o_ref.dtype)


def gather_tokens(pos, h, cap, experts_per_step, tk):
    nseg, e, n = pos.shape
    d = h.shape[2]
    eb = experts_per_step
    return pl.pallas_call(
        functools.partial(_gather_kernel, cap=cap),
        out_shape=jax.ShapeDtypeStruct((nseg, e, cap, d), BF16),
        grid=(nseg, e // eb, n // tk),
        in_specs=[pl.BlockSpec((1, eb, 1, tk), lambda s, ei, kt: (s, ei, 0, kt)),
                  pl.BlockSpec((1, tk, d), lambda s, ei, kt: (s, kt, 0))],
        out_specs=pl.BlockSpec((1, eb, cap, d), lambda s, ei, kt: (s, ei, 0, 0)),
        scratch_shapes=[pltpu.VMEM((eb * cap, d), F32)],
        compiler_params=_params(("parallel", "parallel", "arbitrary"), 48),
        name="gather_tokens",
    )(pos.reshape(nseg, e, 1, n), h)


def _expert_up_kernel(xc_ref, xl_ref, wg_ref, wu_ref, o_ref):
    d = xc_ref.shape[-1]
    wg = wg_ref[0, 0].astype(BF16)
    wu = wu_ref[0, 0].astype(BF16)
    row = 0
    for x_ref in (xc_ref, xl_ref):
        rows = x_ref.shape[0] * x_ref.shape[2]
        x = x_ref[...].reshape(rows, d)
        a = jnp.dot(x, wg, preferred_element_type=F32)
        b = jnp.dot(x, wu, preferred_element_type=F32)
        o_ref[0, row:row + rows, :] = (_silu(a) * b).astype(o_ref.dtype)
        row += rows


def expert_up(xg_ctx, xg_lat, w_gate, w_up, layer):
    _, e, d, f = w_gate.shape
    tf = 512
    sc, _, cc, _ = xg_ctx.shape
    sl, _, cl, _ = xg_lat.shape
    rows = sc * cc + sl * cl
    return pl.pallas_call(
        _expert_up_kernel,
        out_shape=jax.ShapeDtypeStruct((e, rows, f), BF16),
        grid=(e, f // tf),
        in_specs=[pl.BlockSpec((sc, 1, cc, d), lambda ei, j: (0, ei, 0, 0)),
                  pl.BlockSpec((sl, 1, cl, d), lambda ei, j: (0, ei, 0, 0)),
                  pl.BlockSpec((1, 1, d, tf), lambda ei, j: (layer, ei, 0, j)),
                  pl.BlockSpec((1, 1, d, tf), lambda ei, j: (layer, ei, 0, j))],
        out_specs=pl.BlockSpec((1, rows, tf), lambda ei, j: (ei, 0, j)),
        compiler_params=_params(("parallel", "arbitrary"), 56),
        name="expert_up",
    )(xg_ctx, xg_lat, w_gate, w_up)


def _expert_down_kernel(h_ref, w_ref, gate_ref, o_ref):
    y = jnp.dot(h_ref[0], w_ref[0, 0].astype(BF16), preferred_element_type=F32)
    o_ref[0] = (y * gate_ref[0]).astype(o_ref.dtype)


def expert_down(hid, w_down, gates, layer):
    e, rows, f = hid.shape
    d = w_down.shape[3]
    td = 512
    return pl.pallas_call(
        _expert_down_kernel,
        out_shape=jax.ShapeDtypeStruct((e, rows, d), BF16),
        grid=(e, d // td),
        in_specs=[pl.BlockSpec((1, rows, f), lambda ei, j: (ei, 0, 0)),
                  pl.BlockSpec((1, 1, f, td), lambda ei, j: (layer, ei, 0, j)),
                  pl.BlockSpec((1, rows, 1), lambda ei, j: (ei, 0, 0))],
        out_specs=pl.BlockSpec((1, rows, td), lambda ei, j: (ei, 0, j)),
        compiler_params=_params(("parallel", "arbitrary"), 48),
        name="expert_down",
    )(hid, w_down, gates)


def _combine_kernel(post_ref, y_ref, x_ref, gate_ref, o_ref, hit_ref, *, cap):
    ne = y_ref.shape[0]
    tt = post_ref.shape[1]

    @pl.when(pl.program_id(2) == 0)
    def _():
        post = post_ref[0]
        width = max(cap, LANES)
        per = width // cap
        lane = lax.broadcasted_iota(jnp.int32, (tt, width), 1)
        for blk in range(ne // per):
            hit = None
            for i in range(per):
                col = post[:, blk * per + i:blk * per + i + 1]
                h = jnp.where(col >= 0, col + i * cap, -1) == lane
                hit = h if hit is None else (hit | h)
            hit_ref[:, blk * width:(blk + 1) * width] = jnp.where(hit, 1.0, 0.0).astype(BF16)

    y = y_ref[...].reshape(ne * cap, y_ref.shape[-1])
    o_ref[...] = x_ref[...] + gate_ref[0] * jnp.dot(hit_ref[...], y, preferred_element_type=F32)


def combine_tokens(pos_t, y, x, gate, cap, row_offset, tt, td):
    nseg, n, e = pos_t.shape
    r, d = x.shape
    nt = n // tt
    rb = row_offset // cap
    return pl.pallas_call(
        functools.partial(_combine_kernel, cap=cap),
        out_shape=jax.ShapeDtypeStruct((r, d), F32),
        grid=(nseg, nt, d // td),
        in_specs=[pl.BlockSpec((1, tt, e), lambda s, t, j: (s, t, 0)),
                  pl.BlockSpec((e, cap, td), lambda s, t, j: (0, rb + s, j)),
                  pl.BlockSpec((tt, td), lambda s, t, j: (s * nt + t, j)),
                  pl.BlockSpec((1, 1, td), lambda s, t, j: (s * gate.shape[0] // nseg, 0, j))],
        out_specs=pl.BlockSpec((tt, td), lambda s, t, j: (s * nt + t, j)),
        scratch_shapes=[pltpu.VMEM((tt, e * cap), BF16)],
        compiler_params=_params(("parallel", "parallel", "arbitrary"), 56),
        name="combine_tokens",
    )(pos_t, y, x, gate)


COMBINE_WINDOW = 256
COMBINE_ALIGN = 64


def _combine_window_kernel(ws_ref, post_ref, *refs, experts):
    y_refs = refs[:experts]
    x_ref, gate_ref, o_ref, hit_ref = refs[experts:]
    s, t = pl.program_id(0), pl.program_id(1)
    tt = post_ref.shape[1]
    win = COMBINE_WINDOW

    @pl.when(pl.program_id(2) == 0)
    def _():
        post = post_ref[0]
        lane = lax.broadcasted_iota(jnp.int32, (tt, win), 1)
        for e in range(experts):
            start = ws_ref[(s * pl.num_programs(1) + t) * experts + e]
            col = post[:, e:e + 1]
            local = jnp.where(col >= 0, col - start, -1)
            hit_ref[:, e * win:(e + 1) * win] = jnp.where(local == lane, 1.0, 0.0).astype(BF16)

    acc = jnp.dot(hit_ref[:, 0:win], y_refs[0][0], preferred_element_type=F32)
    for e in range(1, experts):
        acc = acc + jnp.dot(hit_ref[:, e * win:(e + 1) * win], y_refs[e][0], preferred_element_type=F32)
    o_ref[...] = x_ref[...] + gate_ref[0] * acc


def combine_tokens_windowed(win_start, pos_t, y, x, gate, cap, row_offset, tt, td):
    nseg, n, e = pos_t.shape
    r, d = x.shape
    nt = n // tt

    def y_spec(ei):
        return pl.BlockSpec((pl.Element(1), pl.Element(COMBINE_WINDOW), pl.Element(td)),
                            lambda s, t, j, ws: (
                                ei,
                                pl.multiple_of(row_offset + s * cap + ws[(s * nt + t) * e + ei], COMBINE_ALIGN),
                                j * td))

    return pl.pallas_call(
        functools.partial(_combine_window_kernel, experts=e),
        out_shape=jax.ShapeDtypeStruct((r, d), F32),
        grid_spec=pltpu.PrefetchScalarGridSpec(
            num_scalar_prefetch=1,
            grid=(nseg, nt, d // td),
            in_specs=[pl.BlockSpec((1, tt, e), lambda s, t, j, ws: (s, t, 0))]
                     + [y_spec(ei) for ei in range(e)]
                     + [pl.BlockSpec((tt, td), lambda s, t, j, ws: (s * nt + t, j)),
                        pl.BlockSpec((1, 1, td), lambda s, t, j, ws: (s * gate.shape[0] // nseg, 0, j))],
            out_specs=pl.BlockSpec((tt, td), lambda s, t, j, ws: (s * nt + t, j)),
            scratch_shapes=[pltpu.VMEM((tt, e * COMBINE_WINDOW), BF16)]),
        compiler_params=_params(("parallel", "parallel", "arbitrary"), 56),
        name="combine_tokens_windowed",
    )(win_start.reshape(-1), pos_t, *([y] * e), x, gate)


def _combine_windows(pos, tt):
    nseg, e, n = pos.shape
    taken = (pos >= 0).reshape(nseg, e, n // tt, tt).sum(-1)
    first = jnp.cumsum(taken, axis=-1) - taken
    cap = taken.sum(-1, keepdims=True)
    start = jnp.minimum(first // COMBINE_ALIGN * COMBINE_ALIGN, cap - COMBINE_WINDOW)
    fits = jnp.all(first + taken <= start + COMBINE_WINDOW)
    return jnp.swapaxes(start, 1, 2).astype(jnp.int32), fits


def moe_layer(groups, g_ffn, router_w, w_gate, w_up, w_down, layer):
    sel = []
    for grp in groups:
        nseg, n = grp["nseg"], grp["n"]
        cap = (EC_CAPACITY * n) // N_EXPERTS
        dense = n <= 512
        routed = norm_router(grp["x"], g_ffn, grp["shift"], grp["scale"], router_w, want_h=dense)
        logits_t = jnp.swapaxes(routed[-1][:, :N_EXPERTS].reshape(nseg, n, N_EXPERTS), 1, 2)
        picked = select_tokens(logits_t, cap, want_idx=not dense)
        pos, gates = picked[0], picked[1]
        if dense:
            xg = gather_tokens(pos, routed[0].reshape(nseg, n, D_MODEL), cap, N_EXPERTS, n)
        else:
            xg = gather_rows(picked[2].reshape(nseg, N_EXPERTS, cap), grp["x"], g_ffn, grp["shift"], grp["scale"], n)
        sel.append(dict(cap=cap, pos=pos, xg=xg,
                        gates=jnp.swapaxes(gates, 0, 1).reshape(N_EXPERTS, nseg * cap, 1)))
    hid = expert_up(sel[0]["xg"], sel[1]["xg"], w_gate, w_up, layer)
    y = expert_down(hid, w_down, jnp.concatenate([s["gates"] for s in sel], axis=1), layer)
    out = []
    row_offset = 0
    for grp, s in zip(groups, sel):
        n, cap = grp["n"], s["cap"]
        pos_t = jnp.swapaxes(s["pos"], 1, 2)
        if n <= 512:
            out.append(combine_tokens(pos_t, y, grp["x"], grp["gate"], cap, row_offset, n, D_MODEL))
        else:
            tt = 1024
            start, fits = _combine_windows(s["pos"], tt)
            args = (pos_t, y, grp["x"], grp["gate"], cap, row_offset, tt)
            out.append(lax.cond(fits, lambda a=args, w=start: combine_tokens_windowed(w, *a, 1024),
                                lambda a=args: combine_tokens(*a, 512)))
        row_offset += grp["nseg"] * cap
    return out


def _rope_tables(n):
    t = jnp.arange(n)
    row = (t // GRID_W).astype(F32)
    col = (t % GRID_W).astype(F32)
    nf = HEAD_DIM // 4
    inv = ROPE_THETA ** (-jnp.arange(nf, dtype=F32) / nf)
    ang = jnp.concatenate([row[:, None] * inv, col[:, None] * inv], axis=-1)
    cos, sin = jnp.cos(ang), jnp.sin(ang)
    return jnp.concatenate([cos, cos], axis=-1), jnp.concatenate([-sin, sin], axis=-1)


def kernel(x_prompt, x_sample, cache_win_k, cache_win_v, cache_nat_k, cache_nat_v, cache_glob_k, cache_glob_v, c, c_ctx, mod_w, mod_b, norm_mix_w, norm_ffn_w, even_w_in, even_w_out, conv_w, conv_b, win_sink, win_q_norm, win_k_norm, odd_w_in, odd_w_out, nat_rpb, nat_q_norm, nat_k_norm, glob_q_norm, glob_k_norm, router_w, expert_w_gate, expert_w_up, expert_w_down):
    bp, sp, d = x_prompt.shape
    bs, ss, _ = x_sample.shape
    depth = mod_w.shape[0]
    rope = _rope_tables(ss)

    cond8 = jnp.zeros((8, d), F32).at[0].set(c_ctx).at[1:1 + bs].set(c)
    mods = modulation(cond8, mod_w, mod_b).reshape(depth, 8, N_MOD, d)

    xp = x_prompt.reshape(bp * sp, d)
    xs = x_sample.reshape(bs * ss, d)
    win_k, win_v, nat_k, nat_v, glob_k, glob_v = [], [], [], [], [], []

    def flat_cache(cache, i):
        return cache[:, i].reshape(bs, cache.shape[2], -1).astype(BF16)

    for layer in range(depth):
        i = layer // 2
        mp = [mods[layer, 0:1, k].reshape(1, 1, d) for k in range(N_MOD)]
        ms = [mods[layer, 1:1 + bs, k].reshape(bs, 1, d) for k in range(N_MOD)]
        if layer % 2 == 0:
            w_in = even_w_in[i].astype(BF16)
            w_out = even_w_out[i].astype(BF16)
            q0 = 3 * CONV_DIM
            qw = WIN_Q_HEADS * HEAD_DIM
            kw = WIN_KV_HEADS * HEAD_DIM
            group = WIN_Q_HEADS // WIN_KV_HEADS
            h = mod_norm(xp, norm_mix_w[layer], mp[0], mp[1])
            (_, conv_in), (q, _), (k, k32), (v, v32) = project(h, w_in, [
                (q0, None, 1.0, False, False, True), (qw, win_q_norm[i], ATTN_SCALE, False, True, False),
                (kw, win_k_norm[i], 1.0, False, True, True), (kw, None, 1.0, False, True, True)])
            ya = conv_gate(conv_in, conv_w[i], conv_b[i], sp)
            yb = dense_attention(q.reshape(bp, sp, qw), k.reshape(bp, sp, kw), v.reshape(bp, sp, kw),
                                 group, sp, sink=win_sink[i])
            xp = out_proj_residual(ya, yb.reshape(bp * sp, qw), w_out, xp, mp[2])
            win_k.append(k32.reshape(bp, sp, WIN_KV_HEADS, HEAD_DIM))
            win_v.append(v32.reshape(bp, sp, WIN_KV_HEADS, HEAD_DIM))
            h = mod_norm(xs, norm_mix_w[layer], ms[0], ms[1])
            (_, conv_in), (q, _), (k, _), (v, _) = project(h, w_in, [
                (q0, None, 1.0, False, False, True), (qw, win_q_norm[i], ATTN_SCALE, True, True, False),
                (kw, win_k_norm[i], 1.0, True, True, False), (kw, None, 1.0, False, True, False)], rope)
            ya = conv_gate(conv_in, conv_w[i], conv_b[i], ss)
            yb = window_attention(q.reshape(bs, ss, qw), k.reshape(bs, ss, kw), v.reshape(bs, ss, kw),
                                  flat_cache(cache_win_k, i), flat_cache(cache_win_v, i), win_sink[i], group)
            xs = out_proj_residual(ya, yb.reshape(bs * ss, qw), w_out, xs, ms[2])
        else:
            w_in = odd_w_in[i].astype(BF16)
            w_out = odd_w_out[i].astype(BF16)
            nw = NAT_HEADS * HEAD_DIM
            gqw = GLOB_Q_HEADS * HEAD_DIM
            gkw = GLOB_KV_HEADS * HEAD_DIM
            group = GLOB_Q_HEADS // GLOB_KV_HEADS
            kd0 = 3 * nw + gqw
            h = mod_norm(xp, norm_mix_w[layer], mp[0], mp[1])
            (qc, _), (kc, kc32), (vc, vc32), (qd, _), (kd, kd32), (vd, vd32) = project(h, w_in, [
                (nw, nat_q_norm[i], ATTN_SCALE, False, True, False), (nw, nat_k_norm[i], 1.0, False, True, True),
                (nw, None, 1.0, False, True, True), (gqw, glob_q_norm[i], ATTN_SCALE, False, True, False),
                (gkw, glob_k_norm[i], 1.0, False, True, True), (gkw, None, 1.0, False, True, True)])
            yc = dense_attention(qc.reshape(bp, sp, nw), kc.reshape(bp, sp, nw), vc.reshape(bp, sp, nw), 1, sp)
            yd = dense_attention(qd.reshape(bp, sp, gqw), kd.reshape(bp, sp, gkw), vd.reshape(bp, sp, gkw), group, sp)
            xp = out_proj_residual(yc.reshape(bp * sp, nw), yd.reshape(bp * sp, gqw), w_out, xp, mp[2])
            nat_k.append(kc32.reshape(bp, sp, NAT_HEADS, HEAD_DIM))
            nat_v.append(vc32.reshape(bp, sp, NAT_HEADS, HEAD_DIM))
            glob_k.append(kd32.reshape(bp, sp, GLOB_KV_HEADS, HEAD_DIM))
            glob_v.append(vd32.reshape(bp, sp, GLOB_KV_HEADS, HEAD_DIM))
            h = mod_norm(xs, norm_mix_w[layer], ms[0], ms[1])
            (qc, _), (kc, _), (vc, _), (qd, _), (kd, _), (vd, _) = project(h, w_in, [
                (nw, nat_q_norm[i], ATTN_SCALE, False, True, False), (nw, nat_k_norm[i], 1.0, False, True, False),
                (nw, None, 1.0, False, True, False), (gqw, glob_q_norm[i], ATTN_SCALE, True, True, False),
                (gkw, glob_k_norm[i], 1.0, True, True, False), (gkw, None, 1.0, False, True, False)], rope)
            yc = neighbourhood_attention(qc.reshape(bs, ss, nw), kc.reshape(bs, ss, nw), vc.reshape(bs, ss, nw),
                                         flat_cache(cache_nat_k, i), flat_cache(cache_nat_v, i),
                                         _nat_bias_table(nat_rpb[i]))
            k_all = jnp.concatenate([kd.reshape(bs, ss, gkw), flat_cache(cache_glob_k, i)], axis=1)
            v_all = jnp.concatenate([vd.reshape(bs, ss, gkw), flat_cache(cache_glob_v, i)], axis=1)
            yd = dense_attention(qd.reshape(bs, ss, gqw), k_all, v_all, group, 256)
            xs = out_proj_residual(yc.reshape(bs * ss, nw), yd.reshape(bs * ss, gqw), w_out, xs, ms[2])

        xp, xs = moe_layer(
            [dict(x=xp, shift=mp[3], scale=mp[4], gate=mp[5], nseg=bp, n=sp),
             dict(x=xs, shift=ms[3], scale=ms[4], gate=ms[5], nseg=bs, n=ss)],
            norm_ffn_w[layer], router_w[layer], expert_w_gate, expert_w_up, expert_w_down, layer)

    stack = lambda xs_: jnp.stack(xs_, axis=1)
    return (xp.reshape(bp, sp, d), xs.reshape(bs, ss, d), stack(win_k), stack(win_v), stack(nat_k),
            stack(nat_v), stack(glob_k), stack(glob_v))
```
